```python
import math
import jax
import jax.numpy as jnp
from jax import lax
import numpy as np

D_MODEL = 1024
BATCH = 8
SEQ = 2048
DEPTH = 2
DEC_BATCH = 128
DEC_SEQ = 8
PAST_LEN = 16384
PAGE_SIZE = 128

N_A_LAYERS = (DEPTH + 1) // 2
N_C_LAYERS = DEPTH // 2
SSM_HEAD_DIM = 64
SSM_HEADS = D_MODEL // SSM_HEAD_DIM
SSM_INNER = SSM_HEADS * SSM_HEAD_DIM
SSM_GROUPS = 2
SSM_GROUP_HEADS = SSM_HEADS // SSM_GROUPS
SSM_STATE = 128
CONV_W = 4
CONV_DIM = SSM_INNER + 2 * SSM_GROUPS * SSM_STATE
SSD_CHUNK = 128
ATT_HEAD_DIM = 64
ATT_HEADS = D_MODEL // ATT_HEAD_DIM
ATT_KV_HEADS = 4
ATT_GQA = ATT_HEADS // ATT_KV_HEADS
WINDOW = 128
ATT_BLOCK = WINDOW
REL_BUCKETS = 32
REL_MAX_DIST = 128
IN_PROJ = SSM_INNER + CONV_DIM + SSM_HEADS + ATT_HEADS * ATT_HEAD_DIM + 2 * ATT_KV_HEADS * ATT_HEAD_DIM
MIX_WIDTH = SSM_INNER + ATT_HEADS * ATT_HEAD_DIM
RWKV_HEAD = 64
RWKV_HEADS = D_MODEL // RWKV_HEAD
DECAY_LORA = 64
ICLR_LORA = 64
GATE_LORA = 160
PEER_HEADS = 8
N_KEYS = 128
N_EXPERTS = N_KEYS * N_KEYS
PEER_TOPK = 16
PEER_QDIM = 256
PEER_HALF = PEER_QDIM // 2
PEER_BLOCK = 256
EPS = 1e-5
GN_EPS = 64e-5

kernel_name = 'hybrid_ssd_swa_rwkv7_peer_step'


def rms_norm(x, g):
    xf = x.astype(jnp.float32)
    y = xf * lax.rsqrt(jnp.mean(xf * xf, axis=-1, keepdims=True) + EPS) * g.astype(jnp.float32)
    return y.astype(x.dtype)


def rel_bucket(dist):
    exact = REL_BUCKETS // 2
    d = jnp.maximum(dist, 0)
    large = exact + (jnp.log(jnp.maximum(d, 1).astype(jnp.float32) / exact)
                     / math.log(REL_MAX_DIST / exact) * (REL_BUCKETS - exact)).astype(jnp.int32)
    large = jnp.minimum(large, REL_BUCKETS - 1)
    return jnp.where(d < exact, d, large)


def ssd_scan(xdt, la, bm, cm, h0):
    b, L = xdt.shape[:2]
    q = min(SSD_CHUNK, L)
    pad = (-L) % q
    nc = (L + pad) // q

    def chunked(t):
        t = jnp.pad(t, [(0, 0), (0, pad)] + [(0, 0)] * (t.ndim - 2))
        return t.reshape((b, nc, q) + t.shape[2:])

    xdt, la, bm, cm = chunked(xdt), chunked(la), chunked(bm), chunked(cm)
    a_cs = jnp.cumsum(la, axis=2)
    seg = a_cs[:, :, :, None] - a_cs[:, :, None, :]
    causal = jnp.tril(jnp.ones((q, q), dtype=bool))[:, :, None, None]
    decay = jnp.exp(jnp.where(causal, seg, -jnp.inf))
    y_diag = jnp.einsum('bclgn,bcsgn,bclsgr,bcsgrp->bclgrp', cm, bm, decay, xdt)
    to_end = jnp.exp(a_cs[:, :, -1:] - a_cs)
    chunk_states = jnp.einsum('bclgn,bclgr,bclgrp->bcgrpn', bm, to_end, xdt)
    chunk_decay = jnp.exp(a_cs[:, :, -1])

    def carry(hs, inp):
        s_c, d_c = inp
        return hs * d_c[..., None, None] + s_c, hs

    h_fin, h_in = lax.scan(carry, h0, (jnp.moveaxis(chunk_states, 1, 0), jnp.moveaxis(chunk_decay, 1, 0)))
    y_off = jnp.einsum('bclgn,cbgrpn,bclgr->bclgrp', cm, h_in, jnp.exp(a_cs))
    y = (y_diag + y_off).reshape((b, nc * q) + y_diag.shape[3:])[:, :L]
    return y, h_fin


def swa_attention(q, k, v, k_prev, v_prev, start, sinks, rel_bias):
    b, L = q.shape[:2]
    k_ext = jnp.concatenate([k_prev.astype(k.dtype), k], axis=1)
    v_ext = jnp.concatenate([v_prev.astype(v.dtype), v], axis=1)
    new_k = k_ext[:, -WINDOW:]
    new_v = v_ext[:, -WINDOW:]
    pad = (-L) % ATT_BLOCK
    nb = (L + pad) // ATT_BLOCK
    q = jnp.pad(q, ((0, 0), (0, pad), (0, 0), (0, 0)))
    k_ext = jnp.pad(k_ext, ((0, 0), (0, pad), (0, 0), (0, 0)))
    v_ext = jnp.pad(v_ext, ((0, 0), (0, pad), (0, 0), (0, 0)))
    qb = q.reshape(b, nb, ATT_BLOCK, ATT_KV_HEADS, ATT_GQA, ATT_HEAD_DIM)
    kr = k_ext.reshape(b, nb + 1, ATT_BLOCK, ATT_KV_HEADS, ATT_HEAD_DIM)
    vr = v_ext.reshape(b, nb + 1, ATT_BLOCK, ATT_KV_HEADS, ATT_HEAD_DIM)
    kb = jnp.concatenate([kr[:, :-1], kr[:, 1:]], axis=2)
    vb = jnp.concatenate([vr[:, :-1], vr[:, 1:]], axis=2)
    s = jnp.einsum('bnqgrd,bnkgd->bngrqk', qb, kb).astype(jnp.float32) * (ATT_HEAD_DIM ** -0.5)
    qi = jnp.arange(ATT_BLOCK)[:, None]
    kj = jnp.arange(2 * ATT_BLOCK)[None, :]
    dist = qi + WINDOW - kj
    band = (dist >= 0) & (dist <= WINDOW)
    key_pos = start - WINDOW + jnp.arange(nb)[:, None, None] * ATT_BLOCK + kj[None]
    mask = band[None] & (key_pos >= 0)
    bias = rel_bias.astype(jnp.float32)[rel_bucket(dist)]
    bias = jnp.transpose(bias, (2, 0, 1)).reshape(ATT_KV_HEADS, ATT_GQA, ATT_BLOCK, 2 * ATT_BLOCK)
    logits = jnp.where(mask[None, :, None, None], s + bias[None, None], -jnp.inf)
    sink = sinks.astype(jnp.float32).reshape(ATT_KV_HEADS, ATT_GQA)[None, None, :, :, None, None]
    m = jnp.maximum(jnp.max(logits, axis=-1, keepdims=True), sink)
    p = jnp.exp(logits - m)
    denom = jnp.sum(p, axis=-1, keepdims=True) + jnp.exp(sink - m)
    o = jnp.einsum('bngrqk,bnkgd->bnqgrd', (p / denom).astype(vb.dtype), vb)
    o = o.reshape(b, nb * ATT_BLOCK, ATT_HEADS * ATT_HEAD_DIM)[:, :L]
    return o, new_k, new_v


def ssd_swa_mixer(h, conv_state, ssm_state, k_prev, v_prev, start, rel_bias, w_in, conv_w, conv_b,
                  dt_bias, a_log, d_skip, gnorm, sinks, w_out):
    f32 = jnp.float32
    b, L, _ = h.shape
    o1 = SSM_INNER
    o2 = o1 + CONV_DIM
    o3 = o2 + SSM_HEADS
    o4 = o3 + ATT_HEADS * ATT_HEAD_DIM
    o5 = o4 + ATT_KV_HEADS * ATT_HEAD_DIM
    z, xbc, dt, q, k, v = jnp.split(h @ w_in, [o1, o2, o3, o4, o5], axis=-1)
    xbc_ext = jnp.concatenate([conv_state.astype(xbc.dtype), xbc], axis=1)
    conv = conv_b
    for j in range(CONV_W):
        conv = conv + xbc_ext[:, j:j + L] * conv_w[j]
    new_conv = xbc_ext[:, -(CONV_W - 1):]
    xbc_c = jax.nn.silu(conv.astype(f32))
    xs, bm, cm = jnp.split(xbc_c, [SSM_INNER, SSM_INNER + SSM_GROUPS * SSM_STATE], axis=-1)
    xs = xs.reshape(b, L, SSM_GROUPS, SSM_GROUP_HEADS, SSM_HEAD_DIM)
    bm = bm.reshape(b, L, SSM_GROUPS, SSM_STATE)
    cm = cm.reshape(b, L, SSM_GROUPS, SSM_STATE)
    dt = jax.nn.softplus((dt + dt_bias).astype(f32)).reshape(b, L, SSM_GROUPS, SSM_GROUP_HEADS)
    a = -jnp.exp(a_log.astype(f32)).reshape(SSM_GROUPS, SSM_GROUP_HEADS)
    h0 = ssm_state.astype(f32).reshape(b, SSM_GROUPS, SSM_GROUP_HEADS, SSM_HEAD_DIM, SSM_STATE)
    y, h_fin = ssd_scan(xs * dt[..., None], dt * a, bm, cm, h0)
    y = y + d_skip.astype(f32).reshape(SSM_GROUPS, SSM_GROUP_HEADS)[:, :, None] * xs
    gs = SSM_INNER // SSM_GROUPS
    y = y.reshape(b, L, SSM_GROUPS, gs) * jax.nn.silu(z.astype(f32)).reshape(b, L, SSM_GROUPS, gs)
    y = y * lax.rsqrt(jnp.mean(y * y, axis=-1, keepdims=True) + EPS) * gnorm.astype(f32).reshape(SSM_GROUPS, gs)
    y_ssd = y.reshape(b, L, SSM_INNER).astype(h.dtype)
    o, new_k, new_v = swa_attention(q.reshape(b, L, ATT_HEADS, ATT_HEAD_DIM),
                                    k.reshape(b, L, ATT_KV_HEADS, ATT_HEAD_DIM),
                                    v.reshape(b, L, ATT_KV_HEADS, ATT_HEAD_DIM),
                                    k_prev, v_prev, start, sinks, rel_bias)
    out = jnp.concatenate([y_ssd, o.astype(h.dtype)], axis=-1) @ w_out
    new_ssm = h_fin.reshape(b, SSM_HEADS, SSM_HEAD_DIM, SSM_STATE).astype(ssm_state.dtype)
    return out, new_conv.astype(conv_state.dtype), new_ssm, new_k, new_v


def rwkv7_mixer(h, shift, wkv, mu, w0, w1, w2, a0, a1, a2, g1, g2, k_k, k_a, r_k, w_rkv, w_o, ln_w, ln_b):
    f32 = jnp.float32
    b, L, _ = h.shape
    hk = (RWKV_HEADS, RWKV_HEAD)
    prev = jnp.concatenate([shift[:, None].astype(h.dtype), h[:, :-1]], axis=1)
    xx = prev - h
    xr, xw, xk, xv, xa, xg = [h + xx * mu[j] for j in range(6)]
    r = (xr @ w_rkv[0]).astype(f32).reshape(b, L, *hk)
    k = (xk @ w_rkv[1]).astype(f32).reshape(b, L, *hk)
    v = (xv @ w_rkv[2]).astype(f32).reshape(b, L, *hk)
    w = -jax.nn.softplus(-(w0 + jnp.tanh(xw @ w1) @ w2).astype(f32)) - 0.5
    decay = jnp.exp(-jnp.exp(w)).reshape(b, L, *hk)
    iclr = jax.nn.sigmoid((a0 + (xa @ a1) @ a2).astype(f32)).reshape(b, L, *hk)
    g = jax.nn.sigmoid(xg @ g1) @ g2
    kk = k * k_k.astype(f32).reshape(hk)
    kk = kk / jnp.maximum(jnp.sqrt(jnp.sum(kk * kk, axis=-1, keepdims=True)), 1e-12)
    k = k * (1.0 + (iclr - 1.0) * k_a.astype(f32).reshape(hk))

    def step(S, inp):
        r_t, d_t, k_t, v_t, kk_t, a_t = inp
        sa = jnp.einsum('bhvk,bhk->bhv', S, -kk_t)
        S = S * d_t[:, :, None, :] + sa[..., None] * (kk_t * a_t)[:, :, None, :] + v_t[..., None] * k_t[:, :, None, :]
        return S, jnp.einsum('bhvk,bhk->bhv', S, r_t)

    seq = [jnp.swapaxes(t, 0, 1) for t in (r, decay, k, v, kk, iclr)]
    S_fin, y = lax.scan(step, wkv.astype(f32), seq)
    y = jnp.swapaxes(y, 0, 1)
    mean = jnp.mean(y, axis=-1, keepdims=True)
    var = jnp.mean(jnp.square(y - mean), axis=-1, keepdims=True)
    y = (y - mean) * lax.rsqrt(var + GN_EPS) * ln_w.astype(f32).reshape(hk) + ln_b.astype(f32).reshape(hk)
    y = y + jnp.sum(r * k * r_k.astype(f32), axis=-1, keepdims=True) * v
    out = (y.reshape(b, L, D_MODEL).astype(h.dtype) * g) @ w_o
    return out, h[:, -1].astype(shift.dtype), S_fin.astype(wkv.dtype)


def peer_ffn(x, w_q, sub_keys, u, v):
    shp = x.shape
    xt = x.reshape(-1, D_MODEL)
    n_tok = xt.shape[0]
    blk = min(PEER_BLOCK, n_tok)
    pad = (-n_tok) % blk
    xt = jnp.pad(xt, ((0, pad), (0, 0)))

    def block_fn(xb):
        q = (xb @ w_q).reshape(blk, PEER_HEADS, 2, PEER_HALF)
        s = jnp.einsum('thcd,hcnd->thcn', q, sub_keys).astype(jnp.float32)
        s_top, i_top = lax.top_k(s, PEER_TOPK)
        cand_s = (s_top[:, :, 0, :, None] + s_top[:, :, 1, None, :]).reshape(blk, PEER_HEADS, -1)
        cand_e = (i_top[:, :, 0, :, None] * N_KEYS + i_top[:, :, 1, None, :]).reshape(blk, PEER_HEADS, -1)
        best_s, best_pos = lax.top_k(cand_s, PEER_TOPK)
        expert = jnp.take_along_axis(cand_e, best_pos, axis=-1)
        gate = jax.nn.softmax(best_s, axis=-1)
        act = jax.nn.gelu(jnp.einsum('td,thkd->thk', xb, u[expert]).astype(jnp.float32), approximate=False)
        return jnp.einsum('thk,thkd->td', (gate * act).astype(xb.dtype), v[expert])

    y = lax.map(block_fn, xt.reshape(-1, blk, D_MODEL))
    return y.reshape(-1, D_MODEL)[:n_tok].reshape(shp)


def setup_inputs(seed: int = 0) -> dict:
    key = jax.random.key(seed)
    keys = jax.random.split(key, 48)
    kit = iter(range(48))
    f32 = jnp.float32

    def nrm(shape, scale):
        return scale * jax.random.normal(keys[next(kit)], shape, f32)

    def uni(shape, lo, hi):
        return jax.random.uniform(keys[next(kit)], shape, f32, lo, hi)

    NA, NC, D = N_A_LAYERS, N_C_LAYERS, D_MODEL
    dt0 = jnp.exp(uni((NA, SSM_HEADS), math.log(1e-3), math.log(1e-1)))
    return {
        'x_prompt': nrm((BATCH, SEQ, D), 1.0),
        'x_sample': nrm((DEC_BATCH, DEC_SEQ, D), 1.0),
        'state_ssm': nrm((NA, DEC_BATCH, SSM_HEADS, SSM_HEAD_DIM, SSM_STATE), 0.1),
        'state_conv': nrm((NA, DEC_BATCH, CONV_W - 1, CONV_DIM), 1.0),
        'cache_swa_k': nrm((NA, DEC_BATCH, WINDOW, ATT_KV_HEADS, ATT_HEAD_DIM), 1.0),
        'cache_swa_v': nrm((NA, DEC_BATCH, WINDOW, ATT_KV_HEADS, ATT_HEAD_DIM), 1.0),
        'state_wkv': nrm((NC, DEC_BATCH, RWKV_HEADS, RWKV_HEAD, RWKV_HEAD), 0.1),
        'state_shift': nrm((NC, DEC_BATCH, D), 1.0),
        'rel_bias': nrm((REL_BUCKETS, ATT_HEADS), 0.5),
        'norm_mix': 1.0 + nrm((DEPTH, D), 0.05),
        'norm_ffn': 1.0 + nrm((DEPTH, D), 0.05),
        'norm_final': 1.0 + nrm((D,), 0.05),
        'mix_w_in': nrm((NA, D, IN_PROJ), D ** -0.5),
        'ssd_conv_w': nrm((NA, CONV_W, CONV_DIM), CONV_W ** -0.5),
        'ssd_conv_b': nrm((NA, CONV_DIM), 0.02),
        'ssd_dt_bias': dt0 + jnp.log(-jnp.expm1(-dt0)),
        'ssd_a_log': jnp.log(uni((NA, SSM_HEADS), 1.0, 16.0)),
        'ssd_d_skip': 1.0 + nrm((NA, SSM_HEADS), 0.1),
        'ssd_gnorm': 1.0 + nrm((NA, SSM_INNER), 0.05),
        'attn_sinks': nrm((NA, ATT_HEADS), 1.0),
        'mix_w_out': nrm((NA, MIX_WIDTH, D), MIX_WIDTH ** -0.5),
        'rwkv_mu': uni((NC, 6, D), 0.0, 1.0),
        'rwkv_w0': uni((NC, D), -5.0, 1.0),
        'rwkv_w1': nrm((NC, D, DECAY_LORA), D ** -0.5),
        'rwkv_w2': nrm((NC, DECAY_LORA, D), 0.5 * DECAY_LORA ** -0.5),
        'rwkv_a0': nrm((NC, D), 0.1),
        'rwkv_a1': nrm((NC, D, ICLR_LORA), D ** -0.5),
        'rwkv_a2': nrm((NC, ICLR_LORA, D), 0.5 * ICLR_LORA ** -0.5),
        'rwkv_g1': nrm((NC, D, GATE_LORA), D ** -0.5),
        'rwkv_g2': nrm((NC, GATE_LORA, D), GATE_LORA ** -0.5),
        'rwkv_k_k': 0.85 + nrm((NC, D), 0.05),
        'rwkv_k_a': 1.0 + nrm((NC, D), 0.05),
        'rwkv_r_k': nrm((NC, RWKV_HEADS, RWKV_HEAD), 0.1),
        'rwkv_w_rkv': nrm((NC, 3, D, D), D ** -0.5),
        'rwkv_w_o': nrm((NC, D, D), D ** -0.5),
        'rwkv_ln_w': 1.0 + nrm((NC, D), 0.05),
        'rwkv_ln_b': nrm((NC, D), 0.02),
        'peer_w_q': nrm((DEPTH, D, PEER_HEADS * PEER_QDIM), D ** -0.5),
        'peer_sub_keys': nrm((DEPTH, PEER_HEADS, 2, N_KEYS, PEER_HALF), PEER_HALF ** -0.5),
        'peer_u': nrm((DEPTH, N_EXPERTS, D), D ** -0.5),
        'peer_v': nrm((DEPTH, N_EXPERTS, D), PEER_HEADS ** -0.5),
    }


def reference(x_prompt, x_sample, state_ssm, state_conv, cache_swa_k, cache_swa_v, state_wkv, state_shift,
              rel_bias, norm_mix, norm_ffn, norm_final,
              mix_w_in, ssd_conv_w, ssd_conv_b, ssd_dt_bias, ssd_a_log, ssd_d_skip, ssd_gnorm, attn_sinks, mix_w_out,
              rwkv_mu, rwkv_w0, rwkv_w1, rwkv_w2, rwkv_a0, rwkv_a1, rwkv_a2, rwkv_g1, rwkv_g2,
              rwkv_k_k, rwkv_k_a, rwkv_r_k, rwkv_w_rkv, rwkv_w_o, rwkv_ln_w, rwkv_ln_b,
              peer_w_q, peer_sub_keys, peer_u, peer_v):
    bp = x_prompt.shape[0]
    xs = [x_prompt, x_sample]
    starts = (0, PAST_LEN)

    def layer_state(grp, arr, i):
        return jnp.zeros((bp,) + arr.shape[2:], arr.dtype) if grp == 0 else arr[i]

    new = {n: ([], []) for n in ('ssm', 'conv', 'k', 'v', 'wkv', 'shift')}
    for layer in range(DEPTH):
        i = layer // 2
        for grp in range(2):
            h = rms_norm(xs[grp], norm_mix[layer])
            if layer % 2 == 0:
                mix, n_conv, n_ssm, n_k, n_v = ssd_swa_mixer(
                    h, layer_state(grp, state_conv, i), layer_state(grp, state_ssm, i),
                    layer_state(grp, cache_swa_k, i), layer_state(grp, cache_swa_v, i), starts[grp], rel_bias,
                    mix_w_in[i], ssd_conv_w[i], ssd_conv_b[i], ssd_dt_bias[i], ssd_a_log[i], ssd_d_skip[i],
                    ssd_gnorm[i], attn_sinks[i], mix_w_out[i])
                new['conv'][grp].append(n_conv)
                new['ssm'][grp].append(n_ssm)
                new['k'][grp].append(n_k)
                new['v'][grp].append(n_v)
            else:
                mix, n_shift, n_wkv = rwkv7_mixer(
                    h, layer_state(grp, state_shift, i), layer_state(grp, state_wkv, i),
                    rwkv_mu[i], rwkv_w0[i], rwkv_w1[i], rwkv_w2[i], rwkv_a0[i], rwkv_a1[i], rwkv_a2[i],
                    rwkv_g1[i], rwkv_g2[i], rwkv_k_k[i], rwkv_k_a[i], rwkv_r_k[i], rwkv_w_rkv[i], rwkv_w_o[i],
                    rwkv_ln_w[i], rwkv_ln_b[i])
                new['shift'][grp].append(n_shift)
                new['wkv'][grp].append(n_wkv)
            xs[grp] = xs[grp] + mix
            xs[grp] = xs[grp] + peer_ffn(rms_norm(xs[grp], norm_ffn[layer]), peer_w_q[layer],
                                         peer_sub_keys[layer], peer_u[layer], peer_v[layer])
    y_prompt = rms_norm(xs[0], norm_final)
    y_sample = rms_norm(xs[1], norm_final)
    p_ssm, s_ssm = jnp.stack(new['ssm'][0]), jnp.stack(new['ssm'][1])
    p_conv, s_conv = jnp.stack(new['conv'][0]), jnp.stack(new['conv'][1])
    p_k, s_k = jnp.stack(new['k'][0]), jnp.stack(new['k'][1])
    p_v, s_v = jnp.stack(new['v'][0]), jnp.stack(new['v'][1])
    p_wkv, s_wkv = jnp.stack(new['wkv'][0]), jnp.stack(new['wkv'][1])
    p_shift, s_shift = jnp.stack(new['shift'][0]), jnp.stack(new['shift'][1])
    return (y_prompt, y_sample, p_ssm, p_conv, p_k, p_v, p_wkv, p_shift,
            s_ssm, s_conv, s_k, s_v, s_wkv, s_shift)
```

```python
import functools
import math

import jax
import jax.numpy as jnp
from jax import lax
from jax.experimental import pallas as pl
from jax.experimental.pallas import tpu as pltpu

f32 = jnp.float32
bf16 = jnp.bfloat16

D_MODEL = 1024
SEQ = 2048
DEC_SEQ = 8
PAST_LEN = 16384
SSM_HEAD_DIM = 64
SSM_HEADS = 16
SSM_INNER = 1024
SSM_GROUPS = 2
SSM_GROUP_HEADS = 8
SSM_STATE = 128
CONV_W = 4
CONV_DIM = 1536
SSD_CHUNK = 128
ATT_HEAD_DIM = 64
ATT_HEADS = 16
ATT_KV_HEADS = 4
ATT_GQA = 4
WINDOW = 128
ATT_BLOCK = 128
REL_BUCKETS = 32
REL_MAX_DIST = 128
IN_PROJ = 4112
RWKV_HEAD = 64
RWKV_HEADS = 16
PEER_HEADS = 8
N_KEYS = 128
N_EXPERTS = N_KEYS * N_KEYS
PEER_TOPK = 16
PEER_HALF = 128
EPS = 1e-5
GN_EPS = 64e-5

LANE = 128
VMEM_LIMIT = 56 * 2 ** 20

_CAND_COUNTS = [PEER_TOPK // (a + 1) for a in range(PEER_TOPK)]
_N_CAND = sum(_CAND_COUNTS)
_N_CAND_PAD = -(-_N_CAND // 8) * 8
_BIG = 1e9


def _cparams(*sem):
    return pltpu.CompilerParams(dimension_semantics=sem, vmem_limit_bytes=VMEM_LIMIT)


def _rmsnorm_kernel(x_ref, g_ref, h_ref, *ht_ref):
    x = x_ref[...]
    y = x * lax.rsqrt(jnp.mean(x * x, axis=-1, keepdims=True) + EPS) * g_ref[...]
    h_ref[...] = y
    if ht_ref:
        ht_ref[0][...] = y.T.astype(bf16)


def rmsnorm(x, g, want_t=False, tm=512):
    t, d = x.shape
    out_shape = [jax.ShapeDtypeStruct((t, d), f32)]
    out_specs = [pl.BlockSpec((tm, d), lambda i: (i, 0))]
    if want_t:
        out_shape.append(jax.ShapeDtypeStruct((d, t), bf16))
        out_specs.append(pl.BlockSpec((d, tm), lambda i: (0, i)))
    res = pl.pallas_call(
        _rmsnorm_kernel,
        grid=(t // tm,),
        in_specs=[pl.BlockSpec((tm, d), lambda i: (i, 0)), pl.BlockSpec((1, d), lambda i: (0, 0))],
        out_specs=out_specs,
        out_shape=out_shape,
        compiler_params=_cparams("parallel"),
        name="rmsnorm_t" if want_t else "rmsnorm",
    )(x, g.reshape(1, d))
    return res if want_t else res[0]


def _matmul_kernel(a_ref, w_ref, *rest):
    o_ref = rest[-1]
    y = jnp.dot(a_ref[...].astype(bf16), w_ref[...], preferred_element_type=f32)
    if len(rest) == 2:
        y = y + rest[0][...]
    o_ref[...] = y


def matmul(a, w, res=None, name="matmul"):
    t, k = a.shape
    n = w.shape[1]
    tm = 256 if n > 2048 else 512
    in_specs = [pl.BlockSpec((tm, k), lambda i: (i, 0)), pl.BlockSpec((k, n), lambda i: (0, 0))]
    args = [a, w]
    if res is not None:
        in_specs.append(pl.BlockSpec((tm, n), lambda i: (i, 0)))
        args.append(res)
    return pl.pallas_call(
        _matmul_kernel,
        grid=(t // tm,),
        in_specs=in_specs,
        out_specs=pl.BlockSpec((tm, n), lambda i: (i, 0)),
        out_shape=jax.ShapeDtypeStruct((t, n), f32),
        compiler_params=_cparams("parallel"),
        name=name,
    )(*args)


def _peer_select_kernel(ht_ref, wqt_ref, sk_ref, pos_ref, b0_ref, r1_ref, e0_ref, e1_ref,
                        qt_ref, s_ref, vals_ref, idx_ref, cs_ref):
    tm = ht_ref.shape[1]
    qt_ref[...] = jnp.dot(wqt_ref[...], ht_ref[...], preferred_element_type=f32).astype(bf16)
    row = lax.broadcasted_iota(jnp.int32, (N_KEYS, tm), 0).astype(f32)
    pos = pos_ref[...]
    neg_inf = f32(-jnp.inf)

    def head_body(h, carry):
        for c in (0, 1):
            off = pl.multiple_of((h * 2 + c) * PEER_HALF, PEER_HALF)
            s = jnp.dot(sk_ref[h, c], qt_ref[pl.ds(off, PEER_HALF), :], preferred_element_type=f32)
            s_ref[c] = s

            def extract(k, sr, c=c):
                s, rank = sr
                m = jnp.max(s, axis=0, keepdims=True)
                idx = jnp.min(jnp.where(s == m, row, f32(N_KEYS)), axis=0, keepdims=True)
                hit = row == idx
                vals_ref[c, pl.ds(k, 1), :] = m
                idx_ref[c, pl.ds(k, 1), :] = idx
                return jnp.where(hit, neg_inf, s), jnp.where(hit, k.astype(f32), rank)

            _, rank = lax.fori_loop(0, PEER_TOPK, extract, (s, jnp.full((N_KEYS, tm), f32(PEER_TOPK))))
            if c == 1:
                r1_ref[h] = rank

        v0 = vals_ref[0]
        v1 = vals_ref[1]
        r = 0
        for a in range(PEER_TOPK):
            nb = _CAND_COUNTS[a]
            cs_ref[r:r + nb, :] = v0[a:a + 1, :] + v1[0:nb, :]
            r += nb
        cs_ref[_N_CAND:_N_CAND_PAD, :] = jnp.full((_N_CAND_PAD - _N_CAND, tm), neg_inf)
        m1 = v0[0:1, :] + v1[0:1, :]

        def extract2(k, st):
            cs, sel, z = st
            m = jnp.max(cs, axis=0, keepdims=True)
            p = jnp.min(jnp.where(cs == m, pos, f32(_BIG)), axis=0, keepdims=True)
            hit = pos == p
            return jnp.where(hit, neg_inf, cs), jnp.where(hit, f32(1.0), sel), z + jnp.exp(m - m1)

        _, sel, z = lax.fori_loop(0, PEER_TOPK, extract2,
                                  (cs_ref[...], jnp.zeros((_N_CAND_PAD, tm), f32), jnp.zeros((1, tm), f32)))
        idx0 = idx_ref[0]
        bound0 = jnp.zeros((N_KEYS, tm), f32)
        r = 0
        for a in range(PEER_TOPK):
            nb = _CAND_COUNTS[a]
            cnt = jnp.sum(sel[r:r + nb, :], axis=0, keepdims=True)
            bound0 = jnp.where(row == idx0[a:a + 1, :], cnt, bound0)
            r += nb
        b0_ref[h] = bound0
        e0_ref[h] = jnp.exp(s_ref[0] - v0[0:1, :]) / z
        e1_ref[h] = jnp.exp(s_ref[1] - v1[0:1, :])
        return carry

    lax.fori_loop(0, PEER_HEADS, head_body, 0)


def _peer_main_kernel(ht_ref, u_ref, v_ref, b0_ref, r1_ref, e0_ref, e1_ref, x_ref, o_ref,
                      at_ref, wt_ref, acc_ref, *, nblk):
    j = pl.program_id(1)

    @pl.when(j == 0)
    def _():
        acc_ref[...] = jnp.zeros_like(acc_ref)

    at_ref[...] = jnp.dot(u_ref[...], ht_ref[...], preferred_element_type=f32)

    def blk(b, carry):
        i = j * nblk + b
        off = pl.multiple_of(b * N_KEYS, N_KEYS)
        a = at_ref[pl.ds(off, N_KEYS), :]
        g = jnp.zeros_like(a)
        for h in range(PEER_HEADS):
            bnd = b0_ref[h, pl.ds(i, 1), :]
            e0 = e0_ref[h, pl.ds(i, 1), :]
            g = g + jnp.where(r1_ref[h] < bnd, e1_ref[h] * e0, f32(0.0))
        w = g * (0.5 * a * (1.0 + lax.erf(a * f32(math.sqrt(0.5)))))
        wt_ref[pl.ds(off, N_KEYS), :] = w.astype(bf16)
        return carry

    lax.fori_loop(0, nblk, blk, 0)
    acc_ref[...] += lax.dot_general(wt_ref[...], v_ref[...], (((0,), (0,)), ((), ())),
                                    preferred_element_type=f32)

    @pl.when(j == pl.num_programs(1) - 1)
    def _():
        o_ref[...] = x_ref[...] + acc_ref[...]


def peer_ffn_residual(x, ht, wqt, sk, u, v, tm_sel=256, tm=256, ce=1024):
    t = x.shape[0]
    d = D_MODEL
    pos_list = []
    for a in range(PEER_TOPK):
        pos_list += [a * PEER_TOPK + b for b in range(_CAND_COUNTS[a])]
    pos_list += [_BIG] * (_N_CAND_PAD - _N_CAND)
    pos = jnp.broadcast_to(jnp.asarray(pos_list, f32)[:, None], (_N_CAND_PAD, tm_sel))
    sel_shape = jax.ShapeDtypeStruct((PEER_HEADS, N_KEYS, t), f32)
    sel_spec = pl.BlockSpec((PEER_HEADS, N_KEYS, tm_sel), lambda i: (0, 0, i))
    b0, r1, e0, e1 = pl.pallas_call(
        _peer_select_kernel,
        grid=(t // tm_sel,),
        in_specs=[pl.BlockSpec((d, tm_sel), lambda i: (0, i)),
                  pl.BlockSpec((PEER_HEADS * 2 * PEER_HALF, d), lambda i: (0, 0)),
                  pl.BlockSpec((PEER_HEADS, 2, N_KEYS, PEER_HALF), lambda i: (0, 0, 0, 0)),
                  pl.BlockSpec((_N_CAND_PAD, tm_sel), lambda i: (0, 0))],
        out_specs=[sel_spec] * 4,
        out_shape=[sel_shape] * 4,
        scratch_shapes=[pltpu.VMEM((PEER_HEADS * 2 * PEER_HALF, tm_sel), bf16),
                        pltpu.VMEM((2, N_KEYS, tm_sel), f32),
                        pltpu.VMEM((2, PEER_TOPK, tm_sel), f32),
                        pltpu.VMEM((2, PEER_TOPK, tm_sel), f32),
                        pltpu.VMEM((_N_CAND_PAD, tm_sel), f32)],
        compiler_params=_cparams("parallel"),
        name="peer_select",
    )(ht, wqt, sk, pos)

    nblk = ce // N_KEYS
    sel_spec2 = pl.BlockSpec((PEER_HEADS, N_KEYS, tm), lambda i, j: (0, 0, i))
    return pl.pallas_call(
        functools.partial(_peer_main_kernel, nblk=nblk),
        grid=(t // tm, N_EXPERTS // ce),
        in_specs=[pl.BlockSpec((d, tm), lambda i, j: (0, i)),
                  pl.BlockSpec((ce, d), lambda i, j: (j, 0)),
                  pl.BlockSpec((ce, d), lambda i, j: (j, 0)),
                  sel_spec2, sel_spec2, sel_spec2, sel_spec2,
                  pl.BlockSpec((tm, d), lambda i, j: (i, 0))],
        out_specs=pl.BlockSpec((tm, d), lambda i, j: (i, 0)),
        out_shape=jax.ShapeDtypeStruct((t, d), f32),
        scratch_shapes=[pltpu.VMEM((ce, tm), f32), pltpu.VMEM((ce, tm), bf16), pltpu.VMEM((tm, d), f32)],
        compiler_params=_cparams("parallel", "arbitrary"),
        name="peer_main",
    )(ht, u, v, b0, r1, e0, e1, x)


def _rel_bucket(dist):
    exact = REL_BUCKETS // 2
    d = jnp.maximum(dist, 0)
    large = exact + (jnp.log(jnp.maximum(d, 1).astype(f32) / exact)
                     / math.log(REL_MAX_DIST / exact) * (REL_BUCKETS - exact)).astype(jnp.int32)
    large = jnp.minimum(large, REL_BUCKETS - 1)
    return jnp.where(d < exact, d, large)


def _ssd_scan(xdt, la, bm, cm, h0):
    b, L = xdt.shape[:2]
    q = min(SSD_CHUNK, L)
    nc = L // q

    def chunked(t):
        return t.reshape((b, nc, q) + t.shape[2:])

    xdt, la, bm, cm = chunked(xdt), chunked(la), chunked(bm), chunked(cm)
    a_cs = jnp.cumsum(la, axis=2)
    seg = a_cs[:, :, :, None] - a_cs[:, :, None, :]
    causal = jnp.tril(jnp.ones((q, q), dtype=bool))[:, :, None, None]
    decay = jnp.exp(jnp.where(causal, seg, -jnp.inf))
    y_diag = jnp.einsum('bclgn,bcsgn,bclsgr,bcsgrp->bclgrp', cm, bm, decay, xdt)
    to_end = jnp.exp(a_cs[:, :, -1:] - a_cs)
    chunk_states = jnp.einsum('bclgn,bclgr,bclgrp->bcgrpn', bm, to_end, xdt)
    chunk_decay = jnp.exp(a_cs[:, :, -1])

    def carry(hs, inp):
        s_c, d_c = inp
        return hs * d_c[..., None, None] + s_c, hs

    h_fin, h_in = lax.scan(carry, h0, (jnp.moveaxis(chunk_states, 1, 0), jnp.moveaxis(chunk_decay, 1, 0)))
    y_off = jnp.einsum('bclgn,cbgrpn,bclgr->bclgrp', cm, h_in, jnp.exp(a_cs))
    y = (y_diag + y_off).reshape((b, nc * q) + y_diag.shape[3:])
    return y, h_fin


def _swa_attention(q, k, v, k_prev, v_prev, start, sinks, rel_bias):
    b, L = q.shape[:2]
    k_ext = jnp.concatenate([k_prev, k], axis=1)
    v_ext = jnp.concatenate([v_prev, v], axis=1)
    new_k = k_ext[:, -WINDOW:]
    new_v = v_ext[:, -WINDOW:]
    pad = (-L) % ATT_BLOCK
    nb = (L + pad) // ATT_BLOCK
    q = jnp.pad(q, ((0, 0), (0, pad), (0, 0), (0, 0)))
    k_ext = jnp.pad(k_ext, ((0, 0), (0, pad), (0, 0), (0, 0)))
    v_ext = jnp.pad(v_ext, ((0, 0), (0, pad), (0, 0), (0, 0)))
    qb = q.reshape(b, nb, ATT_BLOCK, ATT_KV_HEADS, ATT_GQA, ATT_HEAD_DIM)
    kr = k_ext.reshape(b, nb + 1, ATT_BLOCK, ATT_KV_HEADS, ATT_HEAD_DIM)
    vr = v_ext.reshape(b, nb + 1, ATT_BLOCK, ATT_KV_HEADS, ATT_HEAD_DIM)
    kb = jnp.concatenate([kr[:, :-1], kr[:, 1:]], axis=2)
    vb = jnp.concatenate([vr[:, :-1], vr[:, 1:]], axis=2)
    s = jnp.einsum('bnqgrd,bnkgd->bngrqk', qb, kb).astype(f32) * (ATT_HEAD_DIM ** -0.5)
    qi = jnp.arange(ATT_BLOCK)[:, None]
    kj = jnp.arange(2 * ATT_BLOCK)[None, :]
    dist = qi + WINDOW - kj
    band = (dist >= 0) & (dist <= WINDOW)
    key_pos = start - WINDOW + jnp.arange(nb)[:, None, None] * ATT_BLOCK + kj[None]
    mask = band[None] & (key_pos >= 0)
    bias = rel_bias.astype(f32)[_rel_bucket(dist)]
    bias = jnp.transpose(bias, (2, 0, 1)).reshape(ATT_KV_HEADS, ATT_GQA, ATT_BLOCK, 2 * ATT_BLOCK)
    logits = jnp.where(mask[None, :, None, None], s + bias[None, None], -jnp.inf)
    sink = sinks.astype(f32).reshape(ATT_KV_HEADS, ATT_GQA)[None, None, :, :, None, None]
    m = jnp.maximum(jnp.max(logits, axis=-1, keepdims=True), sink)
    p = jnp.exp(logits - m)
    denom = jnp.sum(p, axis=-1, keepdims=True) + jnp.exp(sink - m)
    o = jnp.einsum('bngrqk,bnkgd->bnqgrd', p / denom, vb)
    o = o.reshape(b, nb * ATT_BLOCK, ATT_HEADS * ATT_HEAD_DIM)[:, :L]
    return o, new_k, new_v


def _ssd_swa_core(proj, conv_state, ssm_state, k_prev, v_prev, start, rel_bias, conv_w, conv_b,
                  dt_bias, a_log, d_skip, gnorm, sinks):
    b, L, _ = proj.shape
    o1 = SSM_INNER
    o2 = o1 + CONV_DIM
    o3 = o2 + SSM_HEADS
    o4 = o3 + ATT_HEADS * ATT_HEAD_DIM
    o5 = o4 + ATT_KV_HEADS * ATT_HEAD_DIM
    z, xbc, dt, q, k, v = jnp.split(proj[..., :IN_PROJ], [o1, o2, o3, o4, o5], axis=-1)
    xbc_ext = jnp.concatenate([conv_state, xbc], axis=1)
    conv = conv_b
    for j in range(CONV_W):
        conv = conv + xbc_ext[:, j:j + L] * conv_w[j]
    new_conv = xbc_ext[:, -(CONV_W - 1):]
    xbc_c = jax.nn.silu(conv)
    xs, bm, cm = jnp.split(xbc_c, [SSM_INNER, SSM_INNER + SSM_GROUPS * SSM_STATE], axis=-1)
    xs = xs.reshape(b, L, SSM_GROUPS, SSM_GROUP_HEADS, SSM_HEAD_DIM)
    bm = bm.reshape(b, L, SSM_GROUPS, SSM_STATE)
    cm = cm.reshape(b, L, SSM_GROUPS, SSM_STATE)
    dt = jax.nn.softplus(dt + dt_bias).reshape(b, L, SSM_GROUPS, SSM_GROUP_HEADS)
    a = -jnp.exp(a_log).reshape(SSM_GROUPS, SSM_GROUP_HEADS)
    h0 = ssm_state.reshape(b, SSM_GROUPS, SSM_GROUP_HEADS, SSM_HEAD_DIM, SSM_STATE)
    y, h_fin = _ssd_scan(xs * dt[..., None], dt * a, bm, cm, h0)
    y = y + d_skip.reshape(SSM_GROUPS, SSM_GROUP_HEADS)[:, :, None] * xs
    gs = SSM_INNER // SSM_GROUPS
    y = y.reshape(b, L, SSM_GROUPS, gs) * jax.nn.silu(z).reshape(b, L, SSM_GROUPS, gs)
    y = y * lax.rsqrt(jnp.mean(y * y, axis=-1, keepdims=True) + EPS) * gnorm.reshape(SSM_GROUPS, gs)
    y_ssd = y.reshape(b, L, SSM_INNER)
    o, new_k, new_v = _swa_attention(q.reshape(b, L, ATT_HEADS, ATT_HEAD_DIM),
                                     k.reshape(b, L, ATT_KV_HEADS, ATT_HEAD_DIM),
                                     v.reshape(b, L, ATT_KV_HEADS, ATT_HEAD_DIM),
                                     k_prev, v_prev, start, sinks, rel_bias)
    feats = jnp.concatenate([y_ssd, o], axis=-1)
    new_ssm = h_fin.reshape(b, SSM_HEADS, SSM_HEAD_DIM, SSM_STATE)
    return feats, new_conv, new_ssm, new_k, new_v


def _rwkv_core(r, k, v, w, a_pre, shift_unused, wkv, k_k, k_a, r_k, ln_w, ln_b):
    b, L, _ = r.shape
    hk = (RWKV_HEADS, RWKV_HEAD)
    r = r.reshape(b, L, *hk)
    k = k.reshape(b, L, *hk)
    v = v.reshape(b, L, *hk)
    w = -jax.nn.softplus(-w) - 0.5
    decay = jnp.exp(-jnp.exp(w)).reshape(b, L, *hk)
    iclr = jax.nn.sigmoid(a_pre).reshape(b, L, *hk)
    kk = k * k_k.reshape(hk)
    kk = kk / jnp.maximum(jnp.sqrt(jnp.sum(kk * kk, axis=-1, keepdims=True)), 1e-12)
    k = k * (1.0 + (iclr - 1.0) * k_a.reshape(hk))

    def step(S, inp):
        r_t, d_t, k_t, v_t, kk_t, a_t = inp
        sa = jnp.einsum('bhvk,bhk->bhv', S, -kk_t, precision=lax.Precision.HIGHEST)
        S = S * d_t[:, :, None, :] + sa[..., None] * (kk_t * a_t)[:, :, None, :] + v_t[..., None] * k_t[:, :, None, :]
        return S, jnp.einsum('bhvk,bhk->bhv', S, r_t, precision=lax.Precision.HIGHEST)

    seq = [jnp.swapaxes(t, 0, 1) for t in (r, decay, k, v, kk, iclr)]
    S_fin, y = lax.scan(step, wkv, seq)
    y = jnp.swapaxes(y, 0, 1)
    mean = jnp.mean(y, axis=-1, keepdims=True)
    var = jnp.mean(jnp.square(y - mean), axis=-1, keepdims=True)
    y = (y - mean) * lax.rsqrt(var + GN_EPS) * ln_w.reshape(hk) + ln_b.reshape(hk)
    y = y + jnp.sum(r * k * r_k, axis=-1, keepdims=True) * v
    return y.reshape(b, L, D_MODEL), S_fin


def kernel(x_prompt, x_sample, state_ssm, state_conv, cache_swa_k, cache_swa_v, state_wkv, state_shift, rel_bias, norm_mix, norm_ffn, norm_final, mix_w_in, ssd_conv_w, ssd_conv_b, ssd_dt_bias, ssd_a_log, ssd_d_skip, ssd_gnorm, attn_sinks, mix_w_out, rwkv_mu, rwkv_w0, rwkv_w1, rwkv_w2, rwkv_a0, rwkv_a1, rwkv_a2, rwkv_g1, rwkv_g2, rwkv_k_k, rwkv_k_a, rwkv_r_k, rwkv_w_rkv, rwkv_w_o, rwkv_ln_w, rwkv_ln_b, peer_w_q, peer_sub_keys, peer_u, peer_v):
    bp, lp, d = x_prompt.shape
    bs, ls, _ = x_sample.shape
    tp = bp * lp
    x = jnp.concatenate([x_prompt.reshape(tp, d), x_sample.reshape(bs * ls, d)], axis=0)

    def split(t):
        return t[:tp].reshape(bp, lp, -1), t[tp:].reshape(bs, ls, -1)

    def merge(p, s):
        return jnp.concatenate([p.reshape(tp, -1), s.reshape(bs * ls, -1)], axis=0)

    def peer(x, layer):
        _, ht = rmsnorm(x, norm_ffn[layer], want_t=True)
        wqt = peer_w_q[layer].T.astype(bf16)
        return peer_ffn_residual(x, ht, wqt, peer_sub_keys[layer].astype(bf16),
                                 peer_u[layer].astype(bf16), peer_v[layer].astype(bf16))

    h = rmsnorm(x, norm_mix[0])
    n_pad = -(-IN_PROJ // LANE) * LANE
    w_in = jnp.pad(mix_w_in[0], ((0, 0), (0, n_pad - IN_PROJ))).astype(bf16)
    proj_p, proj_s = split(matmul(h, w_in, name="in_proj"))
    args = (rel_bias, ssd_conv_w[0], ssd_conv_b[0], ssd_dt_bias[0], ssd_a_log[0], ssd_d_skip[0],
            ssd_gnorm[0], attn_sinks[0])
    zeros_like_state = lambda a: jnp.zeros((bp,) + a.shape[2:], a.dtype)
    f_p, conv_p, ssm_p, k_p, v_p = _ssd_swa_core(
        proj_p, zeros_like_state(state_conv), zeros_like_state(state_ssm), zeros_like_state(cache_swa_k),
        zeros_like_state(cache_swa_v), 0, *args)
    f_s, conv_s, ssm_s, k_s, v_s = _ssd_swa_core(
        proj_s, state_conv[0], state_ssm[0], cache_swa_k[0], cache_swa_v[0], PAST_LEN, *args)
    x = matmul(merge(f_p, f_s), mix_w_out[0].astype(bf16), res=x, name="out_proj")
    x = peer(x, 0)

    h = rmsnorm(x, norm_mix[1])
    h_p, h_s = split(h)
    prev_p = jnp.concatenate([jnp.zeros((bp, 1, d), f32), h_p[:, :-1]], axis=1)
    prev_s = jnp.concatenate([state_shift[0][:, None], h_s[:, :-1]], axis=1)
    xx = merge(prev_p, prev_s) - h
    mu = rwkv_mu[0]
    xr, xw, xk, xv, xa, xg = [h + xx * mu[j] for j in range(6)]
    w_rkv = rwkv_w_rkv[0].astype(bf16)
    r = matmul(xr, w_rkv[0], name="rwkv_r")
    k = matmul(xk, w_rkv[1], name="rwkv_k")
    v = matmul(xv, w_rkv[2], name="rwkv_v")

    def lora(xin, w1, w2, act, name):
        rank = w1.shape[1]
        rpad = -(-rank // LANE) * LANE
        w1p = jnp.pad(w1, ((0, 0), (0, rpad - rank))).astype(bf16)
        w2p = jnp.pad(w2, ((0, rpad - rank), (0, 0))).astype(bf16)
        return matmul(act(matmul(xin, w1p, name=name + "1")), w2p, name=name + "2")

    w = rwkv_w0[0] + lora(xw, rwkv_w1[0], rwkv_w2[0], jnp.tanh, "rwkv_w")
    a_pre = rwkv_a0[0] + lora(xa, rwkv_a1[0], rwkv_a2[0], lambda t: t, "rwkv_a")
    g = lora(xg, rwkv_g1[0], rwkv_g2[0], jax.nn.sigmoid, "rwkv_g")
    cargs = (rwkv_k_k[0], rwkv_k_a[0], rwkv_r_k[0], rwkv_ln_w[0], rwkv_ln_b[0])
    parts_p = [split(t)[0] for t in (r, k, v, w, a_pre)]
    parts_s = [split(t)[1] for t in (r, k, v, w, a_pre)]
    y_p, wkv_p = _rwkv_core(*parts_p, None, zeros_like_state(state_wkv), *cargs)
    y_s, wkv_s = _rwkv_core(*parts_s, None, state_wkv[0], *cargs)
    x = matmul(merge(y_p, y_s) * g, rwkv_w_o[0].astype(bf16), res=x, name="rwkv_o")
    shift_p, shift_s = h_p[:, -1], h_s[:, -1]
    x = peer(x, 1)

    y = rmsnorm(x, norm_final)
    y_p, y_s = split(y)
    return (y_p, y_s, ssm_p[None], conv_p[None], k_p[None], v_p[None], wkv_p[None], shift_p[None],
            ssm_s[None], conv_s[None], k_s[None], v_s[None], wkv_s[None], shift_s[None])
```

```python
import functools
import math

import jax
import jax.numpy as jnp
from jax import lax
from jax.experimental import pallas as pl
from jax.experimental.pallas import tpu as pltpu

f32 = jnp.float32
bf16 = jnp.bfloat16

D_MODEL = 1024
SEQ = 2048
DEC_SEQ = 8
PAST_LEN = 16384
SSM_HEAD_DIM = 64
SSM_HEADS = 16
SSM_INNER = 1024
SSM_GROUPS = 2
SSM_GROUP_HEADS = 8
SSM_STATE = 128
CONV_W = 4
CONV_DIM = 1536
SSD_CHUNK = 128
ATT_HEAD_DIM = 64
ATT_HEADS = 16
ATT_KV_HEADS = 4
ATT_GQA = 4
WINDOW = 128
ATT_BLOCK = 128
REL_BUCKETS = 32
REL_MAX_DIST = 128
IN_PROJ = 4112
RWKV_HEAD = 64
RWKV_HEADS = 16
PEER_HEADS = 8
N_KEYS = 128
N_EXPERTS = N_KEYS * N_KEYS
PEER_TOPK = 16
PEER_HALF = 128
EPS = 1e-5
GN_EPS = 64e-5

LANE = 128
VMEM_LIMIT = 56 * 2 ** 20

_CAND_COUNTS = [PEER_TOPK // (a + 1) for a in range(PEER_TOPK)]
_N_CAND = sum(_CAND_COUNTS)
_N_CAND_PAD = -(-_N_CAND // 8) * 8
_BIG = 1e9


def _cparams(*sem):
    return pltpu.CompilerParams(dimension_semantics=sem, vmem_limit_bytes=VMEM_LIMIT)


def _rmsnorm_kernel(x_ref, g_ref, h_ref, *ht_ref):
    x = x_ref[...]
    y = x * lax.rsqrt(jnp.mean(x * x, axis=-1, keepdims=True) + EPS) * g_ref[...]
    h_ref[...] = y
    if ht_ref:
        ht_ref[0][...] = y.T.astype(bf16)


def rmsnorm(x, g, want_t=False, tm=512):
    t, d = x.shape
    out_shape = [jax.ShapeDtypeStruct((t, d), f32)]
    out_specs = [pl.BlockSpec((tm, d), lambda i: (i, 0))]
    if want_t:
        out_shape.append(jax.ShapeDtypeStruct((d, t), bf16))
        out_specs.append(pl.BlockSpec((d, tm), lambda i: (0, i)))
    res = pl.pallas_call(
        _rmsnorm_kernel,
        grid=(t // tm,),
        in_specs=[pl.BlockSpec((tm, d), lambda i: (i, 0)), pl.BlockSpec((1, d), lambda i: (0, 0))],
        out_specs=out_specs,
        out_shape=out_shape,
        compiler_params=_cparams("parallel"),
        name="rmsnorm_t" if want_t else "rmsnorm",
    )(x, g.reshape(1, d))
    return res if want_t else res[0]


def _matmul_kernel(a_ref, w_ref, *rest):
    o_ref = rest[-1]
    y = jnp.dot(a_ref[...].astype(bf16), w_ref[...], preferred_element_type=f32)
    if len(rest) == 2:
        y = y + rest[0][...]
    o_ref[...] = y


def matmul(a, w, res=None, name="matmul"):
    t, k = a.shape
    n = w.shape[1]
    tm = 256 if n > 2048 else 512
    in_specs = [pl.BlockSpec((tm, k), lambda i: (i, 0)), pl.BlockSpec((k, n), lambda i: (0, 0))]
    args = [a, w]
    if res is not None:
        in_specs.append(pl.BlockSpec((tm, n), lambda i: (i, 0)))
        args.append(res)
    return pl.pallas_call(
        _matmul_kernel,
        grid=(t // tm,),
        in_specs=in_specs,
        out_specs=pl.BlockSpec((tm, n), lambda i: (i, 0)),
        out_shape=jax.ShapeDtypeStruct((t, n), f32),
        compiler_params=_cparams("parallel"),
        name=name,
    )(*args)


def _peer_select_kernel(ht_ref, wqt_ref, sk_ref, pos_ref, b0_ref, r1_ref, e0_ref, e1_ref,
                        qt_ref, s_ref, vals_ref, idx_ref, cs_ref):
    tm = ht_ref.shape[1]
    qt_ref[...] = jnp.dot(wqt_ref[...], ht_ref[...], preferred_element_type=f32).astype(bf16)
    row = lax.broadcasted_iota(jnp.int32, (N_KEYS, tm), 0).astype(f32)
    pos = pos_ref[...]
    neg_inf = f32(-jnp.inf)

    def head_body(h, carry):
        for c in (0, 1):
            off = pl.multiple_of((h * 2 + c) * PEER_HALF, PEER_HALF)
            s = jnp.dot(sk_ref[h, c], qt_ref[pl.ds(off, PEER_HALF), :], preferred_element_type=f32)
            s_ref[c] = s

            def extract(k, sr, c=c):
                s, rank = sr
                m = jnp.max(s, axis=0, keepdims=True)
                idx = jnp.min(jnp.where(s == m, row, f32(N_KEYS)), axis=0, keepdims=True)
                hit = row == idx
                vals_ref[c, pl.ds(k, 1), :] = m
                idx_ref[c, pl.ds(k, 1), :] = idx
                return jnp.where(hit, neg_inf, s), jnp.where(hit, k.astype(f32), rank)

            _, rank = lax.fori_loop(0, PEER_TOPK, extract, (s, jnp.full((N_KEYS, tm), f32(PEER_TOPK))))
            if c == 1:
                r1_ref[h] = rank

        v0 = vals_ref[0]
        v1 = vals_ref[1]
        r = 0
        for a in range(PEER_TOPK):
            nb = _CAND_COUNTS[a]
            cs_ref[r:r + nb, :] = v0[a:a + 1, :] + v1[0:nb, :]
            r += nb
        cs_ref[_N_CAND:_N_CAND_PAD, :] = jnp.full((_N_CAND_PAD - _N_CAND, tm), neg_inf)
        m1 = v0[0:1, :] + v1[0:1, :]

        def extract2(k, st):
            cs, sel, z = st
            m = jnp.max(cs, axis=0, keepdims=True)
            p = jnp.min(jnp.where(cs == m, pos, f32(_BIG)), axis=0, keepdims=True)
            hit = pos == p
            return jnp.where(hit, neg_inf, cs), jnp.where(hit, f32(1.0), sel), z + jnp.exp(m - m1)

        _, sel, z = lax.fori_loop(0, PEER_TOPK, extract2,
                                  (cs_ref[...], jnp.zeros((_N_CAND_PAD, tm), f32), jnp.zeros((1, tm), f32)))
        idx0 = idx_ref[0]
        bound0 = jnp.zeros((N_KEYS, tm), f32)
        r = 0
        for a in range(PEER_TOPK):
            nb = _CAND_COUNTS[a]
            cnt = jnp.sum(sel[r:r + nb, :], axis=0, keepdims=True)
            bound0 = jnp.where(row == idx0[a:a + 1, :], cnt, bound0)
            r += nb
        b0_ref[h] = bound0
        e0_ref[h] = jnp.exp(s_ref[0] - v0[0:1, :]) / z
        e1_ref[h] = jnp.exp(s_ref[1] - v1[0:1, :])
        return carry

    lax.fori_loop(0, PEER_HEADS, head_body, 0)


def _peer_main_kernel(ht_ref, u_ref, v_ref, b0_ref, r1_ref, e0_ref, e1_ref, x_ref, o_ref,
                      at_ref, wt_ref, acc_ref, *, nblk):
    j = pl.program_id(1)

    @pl.when(j == 0)
    def _():
        acc_ref[...] = jnp.zeros_like(acc_ref)

    at_ref[...] = jnp.dot(u_ref[...], ht_ref[...], preferred_element_type=f32)

    def blk(b, carry):
        i = j * nblk + b
        off = pl.multiple_of(b * N_KEYS, N_KEYS)
        a = at_ref[pl.ds(off, N_KEYS), :]
        g = jnp.zeros_like(a)
        for h in range(PEER_HEADS):
            bnd = b0_ref[h, pl.ds(i, 1), :]
            e0 = e0_ref[h, pl.ds(i, 1), :]
            g = g + jnp.where(r1_ref[h] < bnd, e1_ref[h] * e0, f32(0.0))
        w = g * (0.5 * a * (1.0 + lax.erf(a * f32(math.sqrt(0.5)))))
        wt_ref[pl.ds(off, N_KEYS), :] = w.astype(bf16)
        return carry

    lax.fori_loop(0, nblk, blk, 0)
    acc_ref[...] += lax.dot_general(wt_ref[...], v_ref[...], (((0,), (0,)), ((), ())),
                                    preferred_element_type=f32)

    @pl.when(j == pl.num_programs(1) - 1)
    def _():
        o_ref[...] = x_ref[...] + acc_ref[...]


def peer_ffn_residual(x, ht, wqt, sk, u, v, tm_sel=256, tm=256, ce=1024):
    t = x.shape[0]
    d = D_MODEL
    pos_list = []
    for a in range(PEER_TOPK):
        pos_list += [a * PEER_TOPK + b for b in range(_CAND_COUNTS[a])]
    pos_list += [_BIG] * (_N_CAND_PAD - _N_CAND)
    pos = jnp.broadcast_to(jnp.asarray(pos_list, f32)[:, None], (_N_CAND_PAD, tm_sel))
    sel_shape = jax.ShapeDtypeStruct((PEER_HEADS, N_KEYS, t), f32)
    sel_spec = pl.BlockSpec((PEER_HEADS, N_KEYS, tm_sel), lambda i: (0, 0, i))
    b0, r1, e0, e1 = pl.pallas_call(
        _peer_select_kernel,
        grid=(t // tm_sel,),
        in_specs=[pl.BlockSpec((d, tm_sel), lambda i: (0, i)),
                  pl.BlockSpec((PEER_HEADS * 2 * PEER_HALF, d), lambda i: (0, 0)),
                  pl.BlockSpec((PEER_HEADS, 2, N_KEYS, PEER_HALF), lambda i: (0, 0, 0, 0)),
                  pl.BlockSpec((_N_CAND_PAD, tm_sel), lambda i: (0, 0))],
        out_specs=[sel_spec] * 4,
        out_shape=[sel_shape] * 4,
        scratch_shapes=[pltpu.VMEM((PEER_HEADS * 2 * PEER_HALF, tm_sel), bf16),
                        pltpu.VMEM((2, N_KEYS, tm_sel), f32),
                        pltpu.VMEM((2, PEER_TOPK, tm_sel), f32),
                        pltpu.VMEM((2, PEER_TOPK, tm_sel), f32),
                        pltpu.VMEM((_N_CAND_PAD, tm_sel), f32)],
        compiler_params=_cparams("parallel"),
        name="peer_select",
    )(ht, wqt, sk, pos)

    nblk = ce // N_KEYS
    sel_spec2 = pl.BlockSpec((PEER_HEADS, N_KEYS, tm), lambda i, j: (0, 0, i))
    return pl.pallas_call(
        functools.partial(_peer_main_kernel, nblk=nblk),
        grid=(t // tm, N_EXPERTS // ce),
        in_specs=[pl.BlockSpec((d, tm), lambda i, j: (0, i)),
                  pl.BlockSpec((ce, d), lambda i, j: (j, 0)),
                  pl.BlockSpec((ce, d), lambda i, j: (j, 0)),
                  sel_spec2, sel_spec2, sel_spec2, sel_spec2,
                  pl.BlockSpec((tm, d), lambda i, j: (i, 0))],
        out_specs=pl.BlockSpec((tm, d), lambda i, j: (i, 0)),
        out_shape=jax.ShapeDtypeStruct((t, d), f32),
        scratch_shapes=[pltpu.VMEM((ce, tm), f32), pltpu.VMEM((ce, tm), bf16), pltpu.VMEM((tm, d), f32)],
        compiler_params=_cparams("parallel", "arbitrary"),
        name="peer_main",
    )(ht, u, v, b0, r1, e0, e1, x)


def _rel_bucket(dist):
    exact = REL_BUCKETS // 2
    d = jnp.maximum(dist, 0)
    large = exact + (jnp.log(jnp.maximum(d, 1).astype(f32) / exact)
                     / math.log(REL_MAX_DIST / exact) * (REL_BUCKETS - exact)).astype(jnp.int32)
    large = jnp.minimum(large, REL_BUCKETS - 1)
    return jnp.where(d < exact, d, large)


def _ssd_scan(xdt, la, bm, cm, h0):
    b, L = xdt.shape[:2]
    q = min(SSD_CHUNK, L)
    nc = L // q

    def chunked(t):
        return t.reshape((b, nc, q) + t.shape[2:])

    xdt, la, bm, cm = chunked(xdt), chunked(la), chunked(bm), chunked(cm)
    a_cs = jnp.cumsum(la, axis=2)
    seg = a_cs[:, :, :, None] - a_cs[:, :, None, :]
    causal = jnp.tril(jnp.ones((q, q), dtype=bool))[:, :, None, None]
    decay = jnp.exp(jnp.where(causal, seg, -jnp.inf))
    y_diag = jnp.einsum('bclgn,bcsgn,bclsgr,bcsgrp->bclgrp', cm, bm, decay, xdt)
    to_end = jnp.exp(a_cs[:, :, -1:] - a_cs)
    chunk_states = jnp.einsum('bclgn,bclgr,bclgrp->bcgrpn', bm, to_end, xdt)
    chunk_decay = jnp.exp(a_cs[:, :, -1])

    def carry(hs, inp):
        s_c, d_c = inp
        return hs * d_c[..., None, None] + s_c, hs

    h_fin, h_in = lax.scan(carry, h0, (jnp.moveaxis(chunk_states, 1, 0), jnp.moveaxis(chunk_decay, 1, 0)))
    y_off = jnp.einsum('bclgn,cbgrpn,bclgr->bclgrp', cm, h_in, jnp.exp(a_cs))
    y = (y_diag + y_off).reshape((b, nc * q) + y_diag.shape[3:])
    return y, h_fin


def _swa_attention(q, k, v, k_prev, v_prev, start, sinks, rel_bias):
    b, L = q.shape[:2]
    k_ext = jnp.concatenate([k_prev, k], axis=1)
    v_ext = jnp.concatenate([v_prev, v], axis=1)
    new_k = k_ext[:, -WINDOW:]
    new_v = v_ext[:, -WINDOW:]
    pad = (-L) % ATT_BLOCK
    nb = (L + pad) // ATT_BLOCK
    q = jnp.pad(q, ((0, 0), (0, pad), (0, 0), (0, 0)))
    k_ext = jnp.pad(k_ext, ((0, 0), (0, pad), (0, 0), (0, 0)))
    v_ext = jnp.pad(v_ext, ((0, 0), (0, pad), (0, 0), (0, 0)))
    qb = q.reshape(b, nb, ATT_BLOCK, ATT_KV_HEADS, ATT_GQA, ATT_HEAD_DIM)
    kr = k_ext.reshape(b, nb + 1, ATT_BLOCK, ATT_KV_HEADS, ATT_HEAD_DIM)
    vr = v_ext.reshape(b, nb + 1, ATT_BLOCK, ATT_KV_HEADS, ATT_HEAD_DIM)
    kb = jnp.concatenate([kr[:, :-1], kr[:, 1:]], axis=2)
    vb = jnp.concatenate([vr[:, :-1], vr[:, 1:]], axis=2)
    s = jnp.einsum('bnqgrd,bnkgd->bngrqk', qb, kb).astype(f32) * (ATT_HEAD_DIM ** -0.5)
    qi = jnp.arange(ATT_BLOCK)[:, None]
    kj = jnp.arange(2 * ATT_BLOCK)[None, :]
    dist = qi + WINDOW - kj
    band = (dist >= 0) & (dist <= WINDOW)
    key_pos = start - WINDOW + jnp.arange(nb)[:, None, None] * ATT_BLOCK + kj[None]
    mask = band[None] & (key_pos >= 0)
    bias = rel_bias.astype(f32)[_rel_bucket(dist)]
    bias = jnp.transpose(bias, (2, 0, 1)).reshape(ATT_KV_HEADS, ATT_GQA, ATT_BLOCK, 2 * ATT_BLOCK)
    logits = jnp.where(mask[None, :, None, None], s + bias[None, None], -jnp.inf)
    sink = sinks.astype(f32).reshape(ATT_KV_HEADS, ATT_GQA)[None, None, :, :, None, None]
    m = jnp.maximum(jnp.max(logits, axis=-1, keepdims=True), sink)
    p = jnp.exp(logits - m)
    denom = jnp.sum(p, axis=-1, keepdims=True) + jnp.exp(sink - m)
    o = jnp.einsum('bngrqk,bnkgd->bnqgrd', p / denom, vb)
    o = o.reshape(b, nb * ATT_BLOCK, ATT_HEADS * ATT_HEAD_DIM)[:, :L]
    return o, new_k, new_v


def _ssd_swa_core(proj, conv_state, ssm_state, k_prev, v_prev, start, rel_bias, conv_w, conv_b,
                  dt_bias, a_log, d_skip, gnorm, sinks):
    b, L, _ = proj.shape
    o1 = SSM_INNER
    o2 = o1 + CONV_DIM
    o3 = o2 + SSM_HEADS
    o4 = o3 + ATT_HEADS * ATT_HEAD_DIM
    o5 = o4 + ATT_KV_HEADS * ATT_HEAD_DIM
    z, xbc, dt, q, k, v = jnp.split(proj[..., :IN_PROJ], [o1, o2, o3, o4, o5], axis=-1)
    xbc_ext = jnp.concatenate([conv_state, xbc], axis=1)
    conv = conv_b
    for j in range(CONV_W):
        conv = conv + xbc_ext[:, j:j + L] * conv_w[j]
    new_conv = xbc_ext[:, -(CONV_W - 1):]
    xbc_c = jax.nn.silu(conv)
    xs, bm, cm = jnp.split(xbc_c, [SSM_INNER, SSM_INNER + SSM_GROUPS * SSM_STATE], axis=-1)
    xs = xs.reshape(b, L, SSM_GROUPS, SSM_GROUP_HEADS, SSM_HEAD_DIM)
    bm = bm.reshape(b, L, SSM_GROUPS, SSM_STATE)
    cm = cm.reshape(b, L, SSM_GROUPS, SSM_STATE)
    dt = jax.nn.softplus(dt + dt_bias).reshape(b, L, SSM_GROUPS, SSM_GROUP_HEADS)
    a = -jnp.exp(a_log).reshape(SSM_GROUPS, SSM_GROUP_HEADS)
    h0 = ssm_state.reshape(b, SSM_GROUPS, SSM_GROUP_HEADS, SSM_HEAD_DIM, SSM_STATE)
    y, h_fin = _ssd_scan(xs * dt[..., None], dt * a, bm, cm, h0)
    y = y + d_skip.reshape(SSM_GROUPS, SSM_GROUP_HEADS)[:, :, None] * xs
    gs = SSM_INNER // SSM_GROUPS
    y = y.reshape(b, L, SSM_GROUPS, gs) * jax.nn.silu(z).reshape(b, L, SSM_GROUPS, gs)
    y = y * lax.rsqrt(jnp.mean(y * y, axis=-1, keepdims=True) + EPS) * gnorm.reshape(SSM_GROUPS, gs)
    y_ssd = y.reshape(b, L, SSM_INNER)
    o, new_k, new_v = _swa_attention(q.reshape(b, L, ATT_HEADS, ATT_HEAD_DIM),
                                     k.reshape(b, L, ATT_KV_HEADS, ATT_HEAD_DIM),
                                     v.reshape(b, L, ATT_KV_HEADS, ATT_HEAD_DIM),
                                     k_prev, v_prev, start, sinks, rel_bias)
    feats = jnp.concatenate([y_ssd, o], axis=-1)
    new_ssm = h_fin.reshape(b, SSM_HEADS, SSM_HEAD_DIM, SSM_STATE)
    return feats, new_conv, new_ssm, new_k, new_v


RWKV_VGROUP = 8


def _rwkv_scan_kernel(r_ref, d_ref, k_ref, v_ref, kk_ref, a_ref, s0_ref, y_ref, sfin_ref, s_ref):
    tb = r_ref.shape[0]

    @pl.when(pl.program_id(1) == 0)
    def _():
        s_ref[...] = s0_ref[...]

    def step(t, carry):
        kk = kk_ref[t]
        d = d_ref[t]
        kv = k_ref[t]
        r = r_ref[t]
        b = kk * a_ref[t]

        def vgroup(g, c2):
            v0 = pl.multiple_of(g * RWKV_VGROUP, RWKV_VGROUP)
            vrows = v_ref[t, pl.ds(v0, RWKV_VGROUP), :]
            ys = []
            for vi in range(RWKV_VGROUP):
                sv = s_ref[v0 + vi]
                sa = -jnp.sum(sv * kk, axis=0, keepdims=True)
                sn = sv * d + sa * b + vrows[vi:vi + 1, :] * kv
                s_ref[v0 + vi] = sn
                ys.append(jnp.sum(sn * r, axis=0, keepdims=True))
            y_ref[t, pl.ds(v0, RWKV_VGROUP), :] = jnp.concatenate(ys, axis=0)
            return c2

        lax.fori_loop(0, RWKV_HEAD // RWKV_VGROUP, vgroup, 0)
        return carry

    lax.fori_loop(0, tb, step, 0)

    @pl.when(pl.program_id(1) == pl.num_programs(1) - 1)
    def _():
        sfin_ref[...] = s_ref[...]


def rwkv_scan(r, d, k, v, kk, a, s0):
    L, hd, c = r.shape
    tb = min(L, 16)
    seq_spec = pl.BlockSpec((tb, hd, LANE), lambda i, j: (j, 0, i))
    st_spec = pl.BlockSpec((hd, hd, LANE), lambda i, j: (0, 0, i))
    return pl.pallas_call(
        _rwkv_scan_kernel,
        grid=(c // LANE, L // tb),
        in_specs=[seq_spec] * 6 + [st_spec],
        out_specs=[seq_spec, st_spec],
        out_shape=[jax.ShapeDtypeStruct((L, hd, c), f32), jax.ShapeDtypeStruct((hd, hd, c), f32)],
        scratch_shapes=[pltpu.VMEM((hd, hd, LANE), f32)],
        compiler_params=_cparams("parallel", "arbitrary"),
        name="rwkv_scan",
    )(r, d, k, v, kk, a, s0)


def _rwkv_core(r, k, v, w, a_pre, shift_unused, wkv, k_k, k_a, r_k, ln_w, ln_b):
    b, L, _ = r.shape
    hk = (RWKV_HEADS, RWKV_HEAD)
    r = r.reshape(b, L, *hk)
    k = k.reshape(b, L, *hk)
    v = v.reshape(b, L, *hk)
    w = -jax.nn.softplus(-w) - 0.5
    decay = jnp.exp(-jnp.exp(w)).reshape(b, L, *hk)
    iclr = jax.nn.sigmoid(a_pre).reshape(b, L, *hk)
    kk = k * k_k.reshape(hk)
    kk = kk / jnp.maximum(jnp.sqrt(jnp.sum(kk * kk, axis=-1, keepdims=True)), 1e-12)
    k = k * (1.0 + (iclr - 1.0) * k_a.reshape(hk))

    def to_chain(t):
        return jnp.transpose(t, (1, 3, 0, 2)).reshape(L, RWKV_HEAD, b * RWKV_HEADS)

    s0 = jnp.transpose(wkv, (2, 3, 0, 1)).reshape(RWKV_HEAD, RWKV_HEAD, b * RWKV_HEADS)
    y, s_fin = rwkv_scan(*[to_chain(t) for t in (r, decay, k, v, kk, iclr)], s0)
    y = jnp.transpose(y.reshape(L, RWKV_HEAD, b, RWKV_HEADS), (2, 0, 3, 1))
    S_fin = jnp.transpose(s_fin.reshape(RWKV_HEAD, RWKV_HEAD, b, RWKV_HEADS), (2, 3, 0, 1))
    mean = jnp.mean(y, axis=-1, keepdims=True)
    var = jnp.mean(jnp.square(y - mean), axis=-1, keepdims=True)
    y = (y - mean) * lax.rsqrt(var + GN_EPS) * ln_w.reshape(hk) + ln_b.reshape(hk)
    y = y + jnp.sum(r * k * r_k, axis=-1, keepdims=True) * v
    return y.reshape(b, L, D_MODEL), S_fin


def kernel(x_prompt, x_sample, state_ssm, state_conv, cache_swa_k, cache_swa_v, state_wkv, state_shift, rel_bias, norm_mix, norm_ffn, norm_final, mix_w_in, ssd_conv_w, ssd_conv_b, ssd_dt_bias, ssd_a_log, ssd_d_skip, ssd_gnorm, attn_sinks, mix_w_out, rwkv_mu, rwkv_w0, rwkv_w1, rwkv_w2, rwkv_a0, rwkv_a1, rwkv_a2, rwkv_g1, rwkv_g2, rwkv_k_k, rwkv_k_a, rwkv_r_k, rwkv_w_rkv, rwkv_w_o, rwkv_ln_w, rwkv_ln_b, peer_w_q, peer_sub_keys, peer_u, peer_v):
    bp, lp, d = x_prompt.shape
    bs, ls, _ = x_sample.shape
    tp = bp * lp
    x = jnp.concatenate([x_prompt.reshape(tp, d), x_sample.reshape(bs * ls, d)], axis=0)

    def split(t):
        return t[:tp].reshape(bp, lp, -1), t[tp:].reshape(bs, ls, -1)

    def merge(p, s):
        return jnp.concatenate([p.reshape(tp, -1), s.reshape(bs * ls, -1)], axis=0)

    def peer(x, layer):
        _, ht = rmsnorm(x, norm_ffn[layer], want_t=True)
        wqt = peer_w_q[layer].T.astype(bf16)
        return peer_ffn_residual(x, ht, wqt, peer_sub_keys[layer].astype(bf16),
                                 peer_u[layer].astype(bf16), peer_v[layer].astype(bf16))

    h = rmsnorm(x, norm_mix[0])
    n_pad = -(-IN_PROJ // LANE) * LANE
    w_in = jnp.pad(mix_w_in[0], ((0, 0), (0, n_pad - IN_PROJ))).astype(bf16)
    proj_p, proj_s = split(matmul(h, w_in, name="in_proj"))
    args = (rel_bias, ssd_conv_w[0], ssd_conv_b[0], ssd_dt_bias[0], ssd_a_log[0], ssd_d_skip[0],
            ssd_gnorm[0], attn_sinks[0])
    zeros_like_state = lambda a: jnp.zeros((bp,) + a.shape[2:], a.dtype)
    f_p, conv_p, ssm_p, k_p, v_p = _ssd_swa_core(
        proj_p, zeros_like_state(state_conv), zeros_like_state(state_ssm), zeros_like_state(cache_swa_k),
        zeros_like_state(cache_swa_v), 0, *args)
    f_s, conv_s, ssm_s, k_s, v_s = _ssd_swa_core(
        proj_s, state_conv[0], state_ssm[0], cache_swa_k[0], cache_swa_v[0], PAST_LEN, *args)
    x = matmul(merge(f_p, f_s), mix_w_out[0].astype(bf16), res=x, name="out_proj")
    x = peer(x, 0)

    h = rmsnorm(x, norm_mix[1])
    h_p, h_s = split(h)
    prev_p = jnp.concatenate([jnp.zeros((bp, 1, d), f32), h_p[:, :-1]], axis=1)
    prev_s = jnp.concatenate([state_shift[0][:, None], h_s[:, :-1]], axis=1)
    xx = merge(prev_p, prev_s) - h
    mu = rwkv_mu[0]
    xr, xw, xk, xv, xa, xg = [h + xx * mu[j] for j in range(6)]
    w_rkv = rwkv_w_rkv[0].astype(bf16)
    r = matmul(xr, w_rkv[0], name="rwkv_r")
    k = matmul(xk, w_rkv[1], name="rwkv_k")
    v = matmul(xv, w_rkv[2], name="rwkv_v")

    def lora(xin, w1, w2, act, name):
        rank = w1.shape[1]
        rpad = -(-rank // LANE) * LANE
        w1p = jnp.pad(w1, ((0, 0), (0, rpad - rank))).astype(bf16)
        w2p = jnp.pad(w2, ((0, rpad - rank), (0, 0))).astype(bf16)
        return matmul(act(matmul(xin, w1p, name=name + "1")), w2p, name=name + "2")

    w = rwkv_w0[0] + lora(xw, rwkv_w1[0], rwkv_w2[0], jnp.tanh, "rwkv_w")
    a_pre = rwkv_a0[0] + lora(xa, rwkv_a1[0], rwkv_a2[0], lambda t: t, "rwkv_a")
    g = lora(xg, rwkv_g1[0], rwkv_g2[0], jax.nn.sigmoid, "rwkv_g")
    cargs = (rwkv_k_k[0], rwkv_k_a[0], rwkv_r_k[0], rwkv_ln_w[0], rwkv_ln_b[0])
    parts_p = [split(t)[0] for t in (r, k, v, w, a_pre)]
    parts_s = [split(t)[1] for t in (r, k, v, w, a_pre)]
    y_p, wkv_p = _rwkv_core(*parts_p, None, zeros_like_state(state_wkv), *cargs)
    y_s, wkv_s = _rwkv_core(*parts_s, None, state_wkv[0], *cargs)
    x = matmul(merge(y_p, y_s) * g, rwkv_w_o[0].astype(bf16), res=x, name="rwkv_o")
    shift_p, shift_s = h_p[:, -1], h_s[:, -1]
    x = peer(x, 1)

    y = rmsnorm(x, norm_final)
    y_p, y_s = split(y)
    return (y_p, y_s, ssm_p[None], conv_p[None], k_p[None], v_p[None], wkv_p[None], shift_p[None],
            ssm_s[None], conv_s[None], k_s[None], v_s[None], wkv_s[None], shift_s[None])
```

```python
import functools
import math

import jax
import jax.numpy as jnp
from jax import lax
from jax.experimental import pallas as pl
from jax.experimental.pallas import tpu as pltpu

f32 = jnp.float32
bf16 = jnp.bfloat16

D_MODEL = 1024
PAST_LEN = 16384
SSM_HEAD_DIM = 64
SSM_HEADS = 16
SSM_INNER = 1024
SSM_GROUPS = 2
SSM_STATE = 128
CONV_W = 4
CONV_DIM = 1536
SSD_CHUNK = 128
ATT_HEAD_DIM = 64
ATT_HEADS = 16
ATT_KV_HEADS = 4
ATT_GQA = 4
WINDOW = 128
ATT_BLOCK = 128
REL_BUCKETS = 32
REL_MAX_DIST = 128
RWKV_HEAD = 64
RWKV_HEADS = 16
PEER_HEADS = 8
N_KEYS = 128
N_EXPERTS = N_KEYS * N_KEYS
PEER_TOPK = 16
PEER_HALF = 128
EPS = 1e-5
GN_EPS = 64e-5

LANE = 128
SUBLANE = 8
HALF_LANE = LANE // 2
VMEM_LIMIT = 56 * 2 ** 20
HIGHEST = lax.Precision.HIGHEST
NT_DIMS = (((1,), (1,)), ((), ()))
TN_DIMS = (((0,), (0,)), ((), ()))

_CAND_COUNTS = [PEER_TOPK // (a + 1) for a in range(PEER_TOPK)]
_N_CAND = sum(_CAND_COUNTS)
_N_CAND_PAD = -(-_N_CAND // 8) * 8
_BIG = 1e9


def _cparams(*sem):
    return pltpu.CompilerParams(dimension_semantics=sem, vmem_limit_bytes=VMEM_LIMIT)


def _rms(x, g):
    return x * lax.rsqrt(jnp.mean(x * x, axis=-1, keepdims=True) + EPS) * g


def _silu(x):
    return x * jax.nn.sigmoid(x)


def _rmsnorm_kernel(x_ref, g_ref, h_ref, *ht_ref):
    y = _rms(x_ref[...], g_ref[...])
    h_ref[...] = y
    if ht_ref:
        ht_ref[0][...] = y.T.astype(bf16)


def rmsnorm(x, g, want_t=False, tm=512):
    t, d = x.shape
    out_shape = [jax.ShapeDtypeStruct((t, d), f32)]
    out_specs = [pl.BlockSpec((tm, d), lambda i: (i, 0))]
    if want_t:
        out_shape.append(jax.ShapeDtypeStruct((d, t), bf16))
        out_specs.append(pl.BlockSpec((d, tm), lambda i: (0, i)))
    res = pl.pallas_call(
        _rmsnorm_kernel,
        grid=(t // tm,),
        in_specs=[pl.BlockSpec((tm, d), lambda i: (i, 0)), pl.BlockSpec((1, d), lambda i: (0, 0))],
        out_specs=out_specs,
        out_shape=out_shape,
        compiler_params=_cparams("parallel"),
        name="rmsnorm_t" if want_t else "rmsnorm",
    )(x, g.reshape(1, d))
    return res if want_t else res[0]


def _matmul_kernel(a_ref, w_ref, *rest):
    o_ref = rest[-1]
    y = jnp.dot(a_ref[...].astype(bf16), w_ref[...], preferred_element_type=f32)
    if len(rest) == 2:
        y = y + rest[0][...]
    o_ref[...] = y


def matmul(a, w, res=None, name="matmul"):
    t, k = a.shape
    n = w.shape[1]
    tm = 256 if n > 2048 else 512
    in_specs = [pl.BlockSpec((tm, k), lambda i: (i, 0)), pl.BlockSpec((k, n), lambda i: (0, 0))]
    args = [a, w]
    if res is not None:
        in_specs.append(pl.BlockSpec((tm, n), lambda i: (i, 0)))
        args.append(res)
    return pl.pallas_call(
        _matmul_kernel,
        grid=(t // tm,),
        in_specs=in_specs,
        out_specs=pl.BlockSpec((tm, n), lambda i: (i, 0)),
        out_shape=jax.ShapeDtypeStruct((t, n), f32),
        compiler_params=_cparams("parallel"),
        name=name,
    )(*args)


def _peer_select_kernel(ht_ref, wqt_ref, sk_ref, pos_ref, b0_ref, r1_ref, e0_ref, e1_ref,
                        qt_ref, s_ref, vals_ref, idx_ref, cs_ref):
    tm = ht_ref.shape[1]
    qt_ref[...] = jnp.dot(wqt_ref[...], ht_ref[...], preferred_element_type=f32).astype(bf16)
    row = lax.broadcasted_iota(jnp.int32, (N_KEYS, tm), 0).astype(f32)
    pos = pos_ref[...]
    neg_inf = f32(-jnp.inf)

    def head_body(h, carry):
        for c in (0, 1):
            off = pl.multiple_of((h * 2 + c) * PEER_HALF, PEER_HALF)
            s = jnp.dot(sk_ref[h, c], qt_ref[pl.ds(off, PEER_HALF), :], preferred_element_type=f32)
            s_ref[c] = s

            def extract(k, sr, c=c):
                s, rank = sr
                m = jnp.max(s, axis=0, keepdims=True)
                idx = jnp.min(jnp.where(s == m, row, f32(N_KEYS)), axis=0, keepdims=True)
                hit = row == idx
                vals_ref[c, pl.ds(k, 1), :] = m
                idx_ref[c, pl.ds(k, 1), :] = idx
                return jnp.where(hit, neg_inf, s), jnp.where(hit, jnp.asarray(k, f32), rank)

            _, rank = lax.fori_loop(0, PEER_TOPK, extract, (s, jnp.full((N_KEYS, tm), f32(PEER_TOPK))))
            if c == 1:
                r1_ref[h] = rank

        v0 = vals_ref[0]
        v1 = vals_ref[1]
        r = 0
        for a in range(PEER_TOPK):
            nb = _CAND_COUNTS[a]
            cs_ref[r:r + nb, :] = v0[a:a + 1, :] + v1[0:nb, :]
            r += nb
        cs_ref[_N_CAND:_N_CAND_PAD, :] = jnp.full((_N_CAND_PAD - _N_CAND, tm), neg_inf)
        m1 = v0[0:1, :] + v1[0:1, :]

        def extract2(k, st):
            cs, sel, z = st
            m = jnp.max(cs, axis=0, keepdims=True)
            p = jnp.min(jnp.where(cs == m, pos, f32(_BIG)), axis=0, keepdims=True)
            hit = pos == p
            return jnp.where(hit, neg_inf, cs), jnp.where(hit, f32(1.0), sel), z + jnp.exp(m - m1)

        _, sel, z = lax.fori_loop(0, PEER_TOPK, extract2,
                                  (cs_ref[...], jnp.zeros((_N_CAND_PAD, tm), f32), jnp.zeros((1, tm), f32)))
        idx0 = idx_ref[0]
        bound0 = jnp.zeros((N_KEYS, tm), f32)
        r = 0
        for a in range(PEER_TOPK):
            nb = _CAND_COUNTS[a]
            cnt = jnp.sum(sel[r:r + nb, :], axis=0, keepdims=True)
            bound0 = jnp.where(row == idx0[a:a + 1, :], cnt, bound0)
            r += nb
        b0_ref[h] = bound0
        e0_ref[h] = jnp.exp(s_ref[0] - v0[0:1, :]) / z
        e1_ref[h] = jnp.exp(s_ref[1] - v1[0:1, :])
        return carry

    lax.fori_loop(0, PEER_HEADS, head_body, 0)


def _peer_main_kernel(ht_ref, u_ref, v_ref, b0_ref, r1_ref, e0_ref, e1_ref, x_ref, o_ref,
                      at_ref, wt_ref, acc_ref, *, nblk):
    j = pl.program_id(1)

    @pl.when(j == 0)
    def _():
        acc_ref[...] = jnp.zeros_like(acc_ref)

    at_ref[...] = jnp.dot(u_ref[...], ht_ref[...], preferred_element_type=f32)

    def blk(b, carry):
        i = j * nblk + b
        off = pl.multiple_of(b * N_KEYS, N_KEYS)
        a = at_ref[pl.ds(off, N_KEYS), :]
        g = jnp.zeros_like(a)
        for h in range(PEER_HEADS):
            bnd = b0_ref[h, pl.ds(i, 1), :]
            e0 = e0_ref[h, pl.ds(i, 1), :]
            g = g + jnp.where(r1_ref[h] < bnd, e1_ref[h] * e0, f32(0.0))
        w = g * (0.5 * a * (1.0 + lax.erf(a * f32(math.sqrt(0.5)))))
        wt_ref[pl.ds(off, N_KEYS), :] = w.astype(bf16)
        return carry

    lax.fori_loop(0, nblk, blk, 0)
    acc_ref[...] += lax.dot_general(wt_ref[...], v_ref[...], TN_DIMS, preferred_element_type=f32)

    @pl.when(j == pl.num_programs(1) - 1)
    def _():
        o_ref[...] = x_ref[...] + acc_ref[...]


def peer_ffn_residual(x, ht, wqt, sk, u, v, tm_sel=256, tm=256, ce=1024):
    t = x.shape[0]
    d = D_MODEL
    pos_list = []
    for a in range(PEER_TOPK):
        pos_list += [a * PEER_TOPK + b for b in range(_CAND_COUNTS[a])]
    pos_list += [_BIG] * (_N_CAND_PAD - _N_CAND)
    pos = jnp.broadcast_to(jnp.asarray(pos_list, f32)[:, None], (_N_CAND_PAD, tm_sel))
    sel_shape = jax.ShapeDtypeStruct((PEER_HEADS, N_KEYS, t), f32)
    sel_spec = pl.BlockSpec((PEER_HEADS, N_KEYS, tm_sel), lambda i: (0, 0, i))
    b0, r1, e0, e1 = pl.pallas_call(
        _peer_select_kernel,
        grid=(t // tm_sel,),
        in_specs=[pl.BlockSpec((d, tm_sel), lambda i: (0, i)),
                  pl.BlockSpec((PEER_HEADS * 2 * PEER_HALF, d), lambda i: (0, 0)),
                  pl.BlockSpec((PEER_HEADS, 2, N_KEYS, PEER_HALF), lambda i: (0, 0, 0, 0)),
                  pl.BlockSpec((_N_CAND_PAD, tm_sel), lambda i: (0, 0))],
        out_specs=[sel_spec] * 4,
        out_shape=[sel_shape] * 4,
        scratch_shapes=[pltpu.VMEM((PEER_HEADS * 2 * PEER_HALF, tm_sel), bf16),
                        pltpu.VMEM((2, N_KEYS, tm_sel), f32),
                        pltpu.VMEM((2, PEER_TOPK, tm_sel), f32),
                        pltpu.VMEM((2, PEER_TOPK, tm_sel), f32),
                        pltpu.VMEM((_N_CAND_PAD, tm_sel), f32)],
        compiler_params=_cparams("parallel"),
        name="peer_select",
    )(ht, wqt, sk, pos)

    nblk = ce // N_KEYS
    sel_spec2 = pl.BlockSpec((PEER_HEADS, N_KEYS, tm), lambda i, j: (0, 0, i))
    return pl.pallas_call(
        functools.partial(_peer_main_kernel, nblk=nblk),
        grid=(t // tm, N_EXPERTS // ce),
        in_specs=[pl.BlockSpec((d, tm), lambda i, j: (0, i)),
                  pl.BlockSpec((ce, d), lambda i, j: (j, 0)),
                  pl.BlockSpec((ce, d), lambda i, j: (j, 0)),
                  sel_spec2, sel_spec2, sel_spec2, sel_spec2,
                  pl.BlockSpec((tm, d), lambda i, j: (i, 0))],
        out_specs=pl.BlockSpec((tm, d), lambda i, j: (i, 0)),
        out_shape=jax.ShapeDtypeStruct((t, d), f32),
        scratch_shapes=[pltpu.VMEM((ce, tm), f32), pltpu.VMEM((ce, tm), bf16), pltpu.VMEM((tm, d), f32)],
        compiler_params=_cparams("parallel", "arbitrary"),
        name="peer_main",
    )(ht, u, v, b0, r1, e0, e1, x)


IN_WIDTHS = (SSM_INNER, CONV_DIM, ATT_HEADS * ATT_HEAD_DIM, 2 * ATT_KV_HEADS * ATT_HEAD_DIM, LANE)


def _in_proj_kernel(x_ref, g_ref, w_ref, *out_refs):
    h = _rms(x_ref[...], g_ref[...])
    y = jnp.dot(h.astype(bf16), w_ref[...], preferred_element_type=f32)
    o = 0
    for ref in out_refs:
        w = ref.shape[1]
        ref[...] = y[:, o:o + w]
        o += w


def in_proj(x, g, w_in, tm=256):
    t, d = x.shape
    o1 = SSM_INNER
    o2 = o1 + CONV_DIM
    o3 = o2 + SSM_HEADS
    w = jnp.concatenate([w_in[:, :o2], w_in[:, o3:], w_in[:, o2:o3],
                         jnp.zeros((d, LANE - SSM_HEADS), w_in.dtype)], axis=1).astype(bf16)
    n = w.shape[1]
    return pl.pallas_call(
        _in_proj_kernel,
        grid=(t // tm,),
        in_specs=[pl.BlockSpec((tm, d), lambda i: (i, 0)), pl.BlockSpec((1, d), lambda i: (0, 0)),
                  pl.BlockSpec((d, n), lambda i: (0, 0))],
        out_specs=[pl.BlockSpec((tm, wd), lambda i: (i, 0)) for wd in IN_WIDTHS],
        out_shape=[jax.ShapeDtypeStruct((t, wd), f32) for wd in IN_WIDTHS],
        compiler_params=_cparams("parallel"),
        name="in_proj",
    )(x, g.reshape(1, d), w)


N_PAIRS = SSM_HEADS // 2


def _ssd_kernel(*refs, lin, aliased):
    (z_ref, xbc_ref, dt_ref, cs_ref, h0_ref, cw_ref, cb_ref, dtb_ref, aneg_ref, dsk_ref, gn_ref, selh_ref) = refs[:12]
    refs = refs[12 + (1 if aliased else 0):]
    y_ref, hfin_ref, xe_ref, hp_ref, ys_ref = refs[:5]
    lc = SSD_CHUNK
    c = pl.program_id(1)

    @pl.when(c == 0)
    def _():
        xe_ref[0:SUBLANE, :] = cs_ref[0]
        hp_ref[...] = h0_ref[0]

    if lin == lc:
        xe_ref[SUBLANE:SUBLANE + lc, :] = xbc_ref[...]
        z = z_ref[...]
        dt_raw = dt_ref[...]
    else:
        zpad_ref, dtpad_ref = refs[5:7]
        xe_ref[SUBLANE:SUBLANE + lc, :] = jnp.zeros((lc, CONV_DIM), f32)
        xe_ref[SUBLANE:SUBLANE + lin, :] = xbc_ref[...]
        zpad_ref[...] = jnp.zeros_like(zpad_ref)
        zpad_ref[0:lin, :] = z_ref[...]
        dtpad_ref[...] = jnp.zeros_like(dtpad_ref)
        dtpad_ref[0:lin, :] = dt_ref[...]
        z = zpad_ref[...]
        dt_raw = dtpad_ref[...]

    conv = cb_ref[...]
    for j in range(CONV_W):
        o = SUBLANE - (CONV_W - 1) + j
        conv = conv + xe_ref[o:o + lc, :] * cw_ref[j:j + 1, :]
    if lin == lc:
        xe_ref[0:SUBLANE, :] = xe_ref[lc:lc + SUBLANE, :]
    xc = _silu(conv)
    xs = xc[:, :SSM_INNER]
    bm = xc[:, SSM_INNER:SSM_INNER + SSM_GROUPS * SSM_STATE]
    cm = xc[:, SSM_INNER + SSM_GROUPS * SSM_STATE:]

    row = lax.broadcasted_iota(jnp.int32, (lc, LANE), 0)
    col = lax.broadcasted_iota(jnp.int32, (lc, LANE), 1)
    causal = row >= col
    lane_lo = col < HALF_LANE
    neg_inf = f32(-jnp.inf)

    x = dt_raw + dtb_ref[...]
    dt = jnp.maximum(x, 0.0) + jnp.log(1.0 + jnp.exp(-jnp.abs(x)))
    if lin != lc:
        dt = jnp.where(row < lin, dt, 0.0)
    la = dt * aneg_ref[...]
    acs = jnp.dot(causal.astype(f32), la, precision=HIGHEST, preferred_element_type=f32)
    acs_t = acs.T
    selh = selh_ref[...]
    dt_exp = jnp.dot(dt, selh, precision=HIGHEST, preferred_element_type=f32)
    acs_exp = jnp.dot(acs, selh, precision=HIGHEST, preferred_element_type=f32)
    alast = acs[lc - 1:lc, :]
    xdt = xs * dt_exp
    eacs = jnp.exp(acs_exp)
    xdt_end = (xdt * jnp.exp(acs_exp[lc - 1:lc, :] - acs_exp)).astype(bf16)
    dsk = dsk_ref[...]

    for g in range(SSM_GROUPS):
        cmg = cm[:, g * SSM_STATE:(g + 1) * SSM_STATE].astype(bf16)
        bmg = bm[:, g * SSM_STATE:(g + 1) * SSM_STATE].astype(bf16)
        cb = lax.dot_general(cmg, bmg, NT_DIMS, preferred_element_type=f32)
        for jj in range(N_PAIRS // SSM_GROUPS):
            j = g * (N_PAIRS // SSM_GROUPS) + jj
            sl = slice(j * LANE, (j + 1) * LANE)
            xdt_pair = xdt[:, sl]
            ydiag = jnp.zeros((lc, LANE), f32)
            for half in (0, 1):
                h = 2 * j + half
                seg = acs[:, h:h + 1] - acs_t[h:h + 1, :]
                dec = jnp.exp(jnp.where(causal, seg, neg_inf))
                m = (cb * dec).astype(bf16)
                keep = lane_lo if half == 0 else jnp.logical_not(lane_lo)
                xd = jnp.where(keep, xdt_pair, 0.0).astype(bf16)
                ydiag = ydiag + jnp.dot(m, xd, preferred_element_type=f32)
            hpj = hp_ref[j]
            yoff = lax.dot_general(cmg, hpj.astype(bf16), NT_DIMS, preferred_element_type=f32) * eacs[:, sl]
            s_new = lax.dot_general(xdt_end[:, sl], bmg, TN_DIMS, preferred_element_type=f32)
            dl = jnp.where(row < SSM_HEAD_DIM, alast[:, 2 * j:2 * j + 1], alast[:, 2 * j + 1:2 * j + 2])
            hp_ref[j] = hpj * jnp.exp(dl) + s_new
            ys_ref[:, sl] = ydiag + yoff + dsk[:, sl] * xs[:, sl]

    y = ys_ref[...] * _silu(z)
    gs = SSM_INNER // SSM_GROUPS
    gn = gn_ref[...]
    for g in range(SSM_GROUPS):
        yg = y[:, g * gs:(g + 1) * gs]
        yg = yg * lax.rsqrt(jnp.mean(yg * yg, axis=-1, keepdims=True) + EPS) * gn[:, g * gs:(g + 1) * gs]
        y_ref[:, g * gs:(g + 1) * gs] = yg[0:lin, :]

    @pl.when(c == pl.num_programs(1) - 1)
    def _():
        hfin_ref[0] = hp_ref[...]


def ssd_mixer(z, xbc, dt, conv_state, ssm_state, params, *, batch, seq, row0, t_total, y_full=None):
    conv_w, conv_b, dt_bias, a_neg, d_skip, gnorm, selh = params
    lc = SSD_CHUNK
    lin = min(seq, lc)
    nc = seq // lin
    blk0 = row0 // lin
    cs = jnp.pad(conv_state, ((0, 0), (SUBLANE - (CONV_W - 1), 0), (0, 0)))
    h0 = ssm_state.reshape(batch, N_PAIRS, LANE, SSM_STATE)

    def rows(w):
        return pl.BlockSpec((lin, w), lambda b, c: (blk0 + b * nc + c, 0))

    def const(shape):
        return pl.BlockSpec(shape, lambda b, c: (0,) * len(shape))

    in_specs = [rows(SSM_INNER), rows(CONV_DIM), rows(LANE),
                pl.BlockSpec((1, SUBLANE, CONV_DIM), lambda b, c: (b, 0, 0)),
                pl.BlockSpec((1, N_PAIRS, LANE, SSM_STATE), lambda b, c: (b, 0, 0, 0)),
                const((CONV_W, CONV_DIM)), const((1, CONV_DIM)), const((1, LANE)), const((1, LANE)),
                const((1, SSM_INNER)), const((1, SSM_INNER)), const((LANE, SSM_INNER))]
    args = [z, xbc, dt, cs, h0, conv_w, conv_b, dt_bias, a_neg, d_skip, gnorm, selh]
    aliases = {}
    if y_full is not None:
        in_specs.append(pl.BlockSpec(memory_space=pl.ANY))
        args.append(y_full)
        aliases = {len(args) - 1: 0}
    scratch = [pltpu.VMEM((lc + 2 * SUBLANE, CONV_DIM), f32), pltpu.VMEM((N_PAIRS, LANE, SSM_STATE), f32),
               pltpu.VMEM((lc, SSM_INNER), f32)]
    if lin != lc:
        scratch += [pltpu.VMEM((lc, SSM_INNER), f32), pltpu.VMEM((lc, LANE), f32)]
    y, h_fin = pl.pallas_call(
        functools.partial(_ssd_kernel, lin=lin, aliased=y_full is not None),
        grid=(batch, nc),
        in_specs=in_specs,
        out_specs=[rows(SSM_INNER), pl.BlockSpec((1, N_PAIRS, LANE, SSM_STATE), lambda b, c: (b, 0, 0, 0))],
        out_shape=[jax.ShapeDtypeStruct((t_total, SSM_INNER), f32),
                   jax.ShapeDtypeStruct((batch, N_PAIRS, LANE, SSM_STATE), f32)],
        scratch_shapes=scratch,
        input_output_aliases=aliases,
        compiler_params=_cparams("parallel", "arbitrary"),
        name="ssd_mixer",
    )(*args)
    return y, h_fin.reshape(batch, SSM_HEADS, SSM_HEAD_DIM, SSM_STATE)


def _swa_kernel(*refs, lq, masked_first, aliased):
    q_ref, kp_ref, vp_ref, kc_ref, vc_ref, bias_ref, sink_ref = refs[:7]
    refs = refs[7 + (1 if aliased else 0):]
    o_ref = refs[0]
    n = pl.program_id(1)
    q = q_ref[...]
    if lq == ATT_BLOCK:
        kc = kc_ref[...]
        vc = vc_ref[...]
    else:
        kpad_ref, vpad_ref = refs[1:3]
        kpad_ref[...] = jnp.zeros_like(kpad_ref)
        vpad_ref[...] = jnp.zeros_like(vpad_ref)
        kpad_ref[0:lq, :] = kc_ref[...]
        vpad_ref[0:lq, :] = vc_ref[...]
        kc = kpad_ref[...]
        vc = vpad_ref[...]
    kp = kp_ref[...]
    vp = vp_ref[...]
    lane_lo = lax.broadcasted_iota(jnp.int32, (ATT_BLOCK, LANE), 1) < HALF_LANE
    lane_lo_q = lax.broadcasted_iota(jnp.int32, (lq, LANE), 1) < HALF_LANE
    neg_inf = f32(-jnp.inf)
    scale = f32(ATT_HEAD_DIM ** -0.5)

    for g in range(ATT_KV_HEADS):
        sl = slice((g // 2) * LANE, (g // 2 + 1) * LANE)
        odd = g % 2 == 1

        def kpad(k):
            pair = k[:, sl]
            if odd:
                pair = pltpu.roll(pair, HALF_LANE, 1)
            return jnp.where(lane_lo, pair, 0.0).astype(bf16)

        def vdup(v):
            pair = v[:, sl]
            rolled = pltpu.roll(pair, HALF_LANE, 1)
            return (jnp.where(lane_lo, rolled, pair) if odd else jnp.where(lane_lo, pair, rolled)).astype(bf16)

        qp0 = q[:, (2 * g) * LANE:(2 * g + 1) * LANE]
        qp1 = q[:, (2 * g + 1) * LANE:(2 * g + 2) * LANE]
        qg = jnp.concatenate([qp0, pltpu.roll(qp0, HALF_LANE, 1), qp1, pltpu.roll(qp1, HALF_LANE, 1)],
                             axis=0).astype(bf16)
        bias = bias_ref[g]
        sp = lax.dot_general(qg, kpad(kp), NT_DIMS, preferred_element_type=f32) * scale + bias[:, :WINDOW]
        sc = lax.dot_general(qg, kpad(kc), NT_DIMS, preferred_element_type=f32) * scale + bias[:, WINDOW:]
        if masked_first:
            sp = jnp.where(n > 0, sp, neg_inf)
        sink = sink_ref[g][:, 0:1]
        m = jnp.maximum(jnp.maximum(jnp.max(sp, axis=-1, keepdims=True), jnp.max(sc, axis=-1, keepdims=True)), sink)
        pp = jnp.exp(sp - m)
        pc = jnp.exp(sc - m)
        denom = jnp.sum(pp, axis=-1, keepdims=True) + jnp.sum(pc, axis=-1, keepdims=True) + jnp.exp(sink - m)
        og = (jnp.dot(pp.astype(bf16), vdup(vp), preferred_element_type=f32)
              + jnp.dot(pc.astype(bf16), vdup(vc), preferred_element_type=f32)) / denom
        o_ref[:, (2 * g) * LANE:(2 * g + 1) * LANE] = jnp.where(lane_lo_q, og[0:lq], og[lq:2 * lq])
        o_ref[:, (2 * g + 1) * LANE:(2 * g + 2) * LANE] = jnp.where(lane_lo_q, og[2 * lq:3 * lq], og[3 * lq:4 * lq])


def _rel_bucket(dist):
    exact = REL_BUCKETS // 2
    d = jnp.maximum(dist, 0)
    large = exact + (jnp.log(jnp.maximum(d, 1).astype(f32) / exact)
                     / math.log(REL_MAX_DIST / exact) * (REL_BUCKETS - exact)).astype(jnp.int32)
    large = jnp.minimum(large, REL_BUCKETS - 1)
    return jnp.where(d < exact, d, large)


def _swa_tables(rel_bias, sinks, lq):
    qi = jnp.arange(lq)[:, None]
    kj = jnp.arange(2 * ATT_BLOCK)[None, :]
    dist = qi + WINDOW - kj
    band = (dist >= 0) & (dist <= WINDOW)
    bias = jnp.where(band[..., None], rel_bias[_rel_bucket(dist)], -jnp.inf)
    bias = jnp.transpose(bias, (2, 0, 1)).reshape(ATT_KV_HEADS, ATT_GQA * lq, 2 * ATT_BLOCK)
    sink = jnp.broadcast_to(sinks.reshape(ATT_KV_HEADS, ATT_GQA, 1, 1), (ATT_KV_HEADS, ATT_GQA, lq, LANE))
    return bias, sink.reshape(ATT_KV_HEADS, ATT_GQA * lq, LANE)


def swa_mixer(q, kv, k_prev, v_prev, rel_bias, sinks, *, batch, seq, row0, t_total, o_full=None):
    lq = min(seq, ATT_BLOCK)
    nb = seq // lq
    blk0 = row0 // lq
    kvw = ATT_KV_HEADS * ATT_HEAD_DIM
    bias, sink = _swa_tables(rel_bias, sinks, lq)
    cur_k = pl.BlockSpec((lq, kvw), lambda b, n: (blk0 + b * nb + n, 0))
    cur_v = pl.BlockSpec((lq, kvw), lambda b, n: (blk0 + b * nb + n, 1))
    if k_prev is None:
        prev_k = pl.BlockSpec((WINDOW, kvw), lambda b, n: (blk0 + b * nb + jnp.maximum(n - 1, 0), 0))
        prev_v = pl.BlockSpec((WINDOW, kvw), lambda b, n: (blk0 + b * nb + jnp.maximum(n - 1, 0), 1))
        kp_arr, vp_arr = kv, kv
    else:
        prev_k = pl.BlockSpec((WINDOW, kvw), lambda b, n: (b, 0))
        prev_v = prev_k
        kp_arr = k_prev.reshape(batch * WINDOW, kvw)
        vp_arr = v_prev.reshape(batch * WINDOW, kvw)
    rows = pl.BlockSpec((lq, ATT_HEADS * ATT_HEAD_DIM), lambda b, n: (blk0 + b * nb + n, 0))
    in_specs = [rows, prev_k, prev_v, cur_k, cur_v,
                pl.BlockSpec(bias.shape, lambda b, n: (0, 0, 0)), pl.BlockSpec(sink.shape, lambda b, n: (0, 0, 0))]
    args = [q, kp_arr, vp_arr, kv, kv, bias, sink]
    aliases = {}
    if o_full is not None:
        in_specs.append(pl.BlockSpec(memory_space=pl.ANY))
        args.append(o_full)
        aliases = {len(args) - 1: 0}
    scratch = [] if lq == ATT_BLOCK else [pltpu.VMEM((ATT_BLOCK, kvw), f32), pltpu.VMEM((ATT_BLOCK, kvw), f32)]
    return pl.pallas_call(
        functools.partial(_swa_kernel, lq=lq, masked_first=k_prev is None, aliased=o_full is not None),
        grid=(batch, nb),
        in_specs=in_specs,
        out_specs=rows,
        out_shape=jax.ShapeDtypeStruct((t_total, ATT_HEADS * ATT_HEAD_DIM), f32),
        scratch_shapes=scratch,
        input_output_aliases=aliases,
        compiler_params=_cparams("parallel", "arbitrary"),
        name="swa_mixer",
    )(*args)


def _out_proj_kernel(ya_ref, yb_ref, w_ref, x_ref, o_ref):
    ka = ya_ref.shape[1]
    acc = jnp.dot(ya_ref[...].astype(bf16), w_ref[0:ka, :], preferred_element_type=f32)
    acc = acc + jnp.dot(yb_ref[...].astype(bf16), w_ref[ka:, :], preferred_element_type=f32)
    o_ref[...] = x_ref[...] + acc


def out_proj(ya, yb, w, x, tm=512):
    t, d = x.shape
    ka, kb = ya.shape[1], yb.shape[1]
    return pl.pallas_call(
        _out_proj_kernel,
        grid=(t // tm,),
        in_specs=[pl.BlockSpec((tm, ka), lambda i: (i, 0)), pl.BlockSpec((tm, kb), lambda i: (i, 0)),
                  pl.BlockSpec((ka + kb, d), lambda i: (0, 0)), pl.BlockSpec((tm, d), lambda i: (i, 0))],
        out_specs=pl.BlockSpec((tm, d), lambda i: (i, 0)),
        out_shape=jax.ShapeDtypeStruct((t, d), f32),
        compiler_params=_cparams("parallel"),
        name="out_proj",
    )(ya, yb, w, x)


RWKV_VGROUP = 8


def _rwkv_scan_kernel(r_ref, d_ref, k_ref, v_ref, kk_ref, a_ref, s0_ref, y_ref, sfin_ref, s_ref):
    tb = r_ref.shape[0]

    @pl.when(pl.program_id(1) == 0)
    def _():
        s_ref[...] = s0_ref[...]

    def step(t, carry):
        kk = kk_ref[t]
        d = d_ref[t]
        kv = k_ref[t]
        r = r_ref[t]
        b = kk * a_ref[t]

        def vgroup(g, c2):
            v0 = pl.multiple_of(g * RWKV_VGROUP, RWKV_VGROUP)
            vrows = v_ref[t, pl.ds(v0, RWKV_VGROUP), :]
            ys = []
            for vi in range(RWKV_VGROUP):
                sv = s_ref[v0 + vi]
                sa = -jnp.sum(sv * kk, axis=0, keepdims=True)
                sn = sv * d + sa * b + vrows[vi:vi + 1, :] * kv
                s_ref[v0 + vi] = sn
                ys.append(jnp.sum(sn * r, axis=0, keepdims=True))
            y_ref[t, pl.ds(v0, RWKV_VGROUP), :] = jnp.concatenate(ys, axis=0)
            return c2

        lax.fori_loop(0, RWKV_HEAD // RWKV_VGROUP, vgroup, 0)
        return carry

    lax.fori_loop(0, tb, step, 0)

    @pl.when(pl.program_id(1) == pl.num_programs(1) - 1)
    def _():
        sfin_ref[...] = s_ref[...]


def rwkv_scan(r, d, k, v, kk, a, s0):
    L, hd, c = r.shape
    tb = min(L, 16)
    seq_spec = pl.BlockSpec((tb, hd, LANE), lambda i, j: (j, 0, i))
    st_spec = pl.BlockSpec((hd, hd, LANE), lambda i, j: (0, 0, i))
    return pl.pallas_call(
        _rwkv_scan_kernel,
        grid=(c // LANE, L // tb),
        in_specs=[seq_spec] * 6 + [st_spec],
        out_specs=[seq_spec, st_spec],
        out_shape=[jax.ShapeDtypeStruct((L, hd, c), f32), jax.ShapeDtypeStruct((hd, hd, c), f32)],
        scratch_shapes=[pltpu.VMEM((hd, hd, LANE), f32)],
        compiler_params=_cparams("parallel", "arbitrary"),
        name="rwkv_scan",
    )(r, d, k, v, kk, a, s0)


def _rwkv_core(r, k, v, w, a_pre, wkv, k_k, k_a, r_k, ln_w, ln_b):
    b, L, _ = r.shape
    hk = (RWKV_HEADS, RWKV_HEAD)
    r = r.reshape(b, L, *hk)
    k = k.reshape(b, L, *hk)
    v = v.reshape(b, L, *hk)
    w = -jax.nn.softplus(-w) - 0.5
    decay = jnp.exp(-jnp.exp(w)).reshape(b, L, *hk)
    iclr = jax.nn.sigmoid(a_pre).reshape(b, L, *hk)
    kk = k * k_k.reshape(hk)
    kk = kk / jnp.maximum(jnp.sqrt(jnp.sum(kk * kk, axis=-1, keepdims=True)), 1e-12)
    k = k * (1.0 + (iclr - 1.0) * k_a.reshape(hk))

    def to_chain(t):
        return jnp.transpose(t, (1, 3, 0, 2)).reshape(L, RWKV_HEAD, b * RWKV_HEADS)

    s0 = jnp.transpose(wkv, (2, 3, 0, 1)).reshape(RWKV_HEAD, RWKV_HEAD, b * RWKV_HEADS)
    y, s_fin = rwkv_scan(*[to_chain(t) for t in (r, decay, k, v, kk, iclr)], s0)
    y = jnp.transpose(y.reshape(L, RWKV_HEAD, b, RWKV_HEADS), (2, 0, 3, 1))
    S_fin = jnp.transpose(s_fin.reshape(RWKV_HEAD, RWKV_HEAD, b, RWKV_HEADS), (2, 3, 0, 1))
    mean = jnp.mean(y, axis=-1, keepdims=True)
    var = jnp.mean(jnp.square(y - mean), axis=-1, keepdims=True)
    y = (y - mean) * lax.rsqrt(var + GN_EPS) * ln_w.reshape(hk) + ln_b.reshape(hk)
    y = y + jnp.sum(r * k * r_k, axis=-1, keepdims=True) * v
    return y.reshape(b, L, D_MODEL), S_fin


def kernel(x_prompt, x_sample, state_ssm, state_conv, cache_swa_k, cache_swa_v, state_wkv, state_shift, rel_bias, norm_mix, norm_ffn, norm_final, mix_w_in, ssd_conv_w, ssd_conv_b, ssd_dt_bias, ssd_a_log, ssd_d_skip, ssd_gnorm, attn_sinks, mix_w_out, rwkv_mu, rwkv_w0, rwkv_w1, rwkv_w2, rwkv_a0, rwkv_a1, rwkv_a2, rwkv_g1, rwkv_g2, rwkv_k_k, rwkv_k_a, rwkv_r_k, rwkv_w_rkv, rwkv_w_o, rwkv_ln_w, rwkv_ln_b, peer_w_q, peer_sub_keys, peer_u, peer_v):
    bp, lp, d = x_prompt.shape
    bs, ls, _ = x_sample.shape
    tp = bp * lp
    ts = bs * ls
    tt = tp + ts
    x = jnp.concatenate([x_prompt.reshape(tp, d), x_sample.reshape(ts, d)], axis=0)

    def split(t):
        return t[:tp].reshape(bp, lp, -1), t[tp:].reshape(bs, ls, -1)

    def merge(p, s):
        return jnp.concatenate([p.reshape(tp, -1), s.reshape(ts, -1)], axis=0)

    def zero_state(a):
        return jnp.zeros((bp,) + a.shape[2:], a.dtype)

    def peer(x, layer):
        _, ht = rmsnorm(x, norm_ffn[layer], want_t=True)
        wqt = peer_w_q[layer].T.astype(bf16)
        return peer_ffn_residual(x, ht, wqt, peer_sub_keys[layer].astype(bf16),
                                 peer_u[layer].astype(bf16), peer_v[layer].astype(bf16))

    z, xbc, q, kv, dt = in_proj(x, norm_mix[0], mix_w_in[0])
    pad16 = (0, LANE - SSM_HEADS)
    selh = (jnp.arange(LANE)[:, None] == jnp.arange(SSM_INNER)[None, :] // SSM_HEAD_DIM).astype(f32)
    ssd_params = (ssd_conv_w[0], ssd_conv_b[0].reshape(1, CONV_DIM),
                  jnp.pad(ssd_dt_bias[0], pad16).reshape(1, LANE),
                  jnp.pad(-jnp.exp(ssd_a_log[0]), pad16).reshape(1, LANE),
                  jnp.repeat(ssd_d_skip[0], SSM_HEAD_DIM).reshape(1, SSM_INNER),
                  ssd_gnorm[0].reshape(1, SSM_INNER), selh)
    y_ssd, ssm_p = ssd_mixer(z, xbc, dt, zero_state(state_conv), zero_state(state_ssm), ssd_params,
                             batch=bp, seq=lp, row0=0, t_total=tt)
    y_ssd, ssm_s = ssd_mixer(z, xbc, dt, state_conv[0], state_ssm[0], ssd_params,
                             batch=bs, seq=ls, row0=tp, t_total=tt, y_full=y_ssd)
    o_att = swa_mixer(q, kv, None, None, rel_bias, attn_sinks[0], batch=bp, seq=lp, row0=0, t_total=tt)
    o_att = swa_mixer(q, kv, cache_swa_k[0], cache_swa_v[0], rel_bias, attn_sinks[0],
                      batch=bs, seq=ls, row0=tp, t_total=tt, o_full=o_att)
    x = out_proj(y_ssd, o_att, mix_w_out[0].astype(bf16), x)
    xbc_p, xbc_s = split(xbc)
    conv_p = xbc_p[:, -(CONV_W - 1):]
    conv_s = jnp.concatenate([state_conv[0], xbc_s], axis=1)[:, -(CONV_W - 1):]
    kv_p, kv_s = split(kv)
    kvw = ATT_KV_HEADS * ATT_HEAD_DIM
    hshape = (ATT_KV_HEADS, ATT_HEAD_DIM)
    k_p = kv_p[:, -WINDOW:, :kvw].reshape(bp, WINDOW, *hshape)
    v_p = kv_p[:, -WINDOW:, kvw:].reshape(bp, WINDOW, *hshape)
    k_s = jnp.concatenate([cache_swa_k[0], kv_s[..., :kvw].reshape(bs, ls, *hshape)], axis=1)[:, -WINDOW:]
    v_s = jnp.concatenate([cache_swa_v[0], kv_s[..., kvw:].reshape(bs, ls, *hshape)], axis=1)[:, -WINDOW:]
    x = peer(x, 0)

    h = rmsnorm(x, norm_mix[1])
    h_p, h_s = split(h)
    prev_p = jnp.concatenate([jnp.zeros((bp, 1, d), f32), h_p[:, :-1]], axis=1)
    prev_s = jnp.concatenate([state_shift[0][:, None], h_s[:, :-1]], axis=1)
    xx = merge(prev_p, prev_s) - h
    mu = rwkv_mu[0]
    xr, xw, xk, xv, xa, xg = [h + xx * mu[j] for j in range(6)]
    w_rkv = rwkv_w_rkv[0].astype(bf16)
    r = matmul(xr, w_rkv[0], name="rwkv_r")
    k = matmul(xk, w_rkv[1], name="rwkv_k")
    v = matmul(xv, w_rkv[2], name="rwkv_v")

    def lora(xin, w1, w2, act, name):
        rank = w1.shape[1]
        rpad = -(-rank // LANE) * LANE
        w1p = jnp.pad(w1, ((0, 0), (0, rpad - rank))).astype(bf16)
        w2p = jnp.pad(w2, ((0, rpad - rank), (0, 0))).astype(bf16)
        return matmul(act(matmul(xin, w1p, name=name + "1")), w2p, name=name + "2")

    w = rwkv_w0[0] + lora(xw, rwkv_w1[0], rwkv_w2[0], jnp.tanh, "rwkv_w")
    a_pre = rwkv_a0[0] + lora(xa, rwkv_a1[0], rwkv_a2[0], lambda t: t, "rwkv_a")
    g = lora(xg, rwkv_g1[0], rwkv_g2[0], jax.nn.sigmoid, "rwkv_g")
    cargs = (rwkv_k_k[0], rwkv_k_a[0], rwkv_r_k[0], rwkv_ln_w[0], rwkv_ln_b[0])
    parts_p = [split(t)[0] for t in (r, k, v, w, a_pre)]
    parts_s = [split(t)[1] for t in (r, k, v, w, a_pre)]
    y_p, wkv_p = _rwkv_core(*parts_p, zero_state(state_wkv), *cargs)
    y_s, wkv_s = _rwkv_core(*parts_s, state_wkv[0], *cargs)
    x = matmul(merge(y_p, y_s) * g, rwkv_w_o[0].astype(bf16), res=x, name="rwkv_o")
    shift_p, shift_s = h_p[:, -1], h_s[:, -1]
    x = peer(x, 1)

    y = rmsnorm(x, norm_final)
    y_p, y_s = split(y)
    return (y_p, y_s, ssm_p[None], conv_p[None], k_p[None], v_p[None], wkv_p[None], shift_p[None],
            ssm_s[None], conv_s[None], k_s[None], v_s[None], wkv_s[None], shift_s[None])
```

```python
import functools
import math

import jax
import jax.numpy as jnp
from jax import lax
from jax.experimental import pallas as pl
from jax.experimental.pallas import tpu as pltpu

f32 = jnp.float32
bf16 = jnp.bfloat16

D_MODEL = 1024
PAST_LEN = 16384
SSM_HEAD_DIM = 64
SSM_HEADS = 16
SSM_INNER = 1024
SSM_GROUPS = 2
SSM_STATE = 128
CONV_W = 4
CONV_DIM = 1536
SSD_CHUNK = 128
ATT_HEAD_DIM = 64
ATT_HEADS = 16
ATT_KV_HEADS = 4
ATT_GQA = 4
WINDOW = 128
ATT_BLOCK = 128
REL_BUCKETS = 32
REL_MAX_DIST = 128
RWKV_HEAD = 64
RWKV_HEADS = 16
PEER_HEADS = 8
N_KEYS = 128
N_EXPERTS = N_KEYS * N_KEYS
PEER_TOPK = 16
PEER_HALF = 128
EPS = 1e-5
GN_EPS = 64e-5

LANE = 128
SUBLANE = 8
HALF_LANE = LANE // 2
VMEM_LIMIT = 56 * 2 ** 20
HIGHEST = lax.Precision.HIGHEST
NT_DIMS = (((1,), (1,)), ((), ()))
TN_DIMS = (((0,), (0,)), ((), ()))

_CAND_COUNTS = [PEER_TOPK // (a + 1) for a in range(PEER_TOPK)]
_N_CAND = sum(_CAND_COUNTS)
_N_CAND_PAD = -(-_N_CAND // 8) * 8
_BIG = 1e9


def _cparams(*sem):
    return pltpu.CompilerParams(dimension_semantics=sem, vmem_limit_bytes=VMEM_LIMIT)


def _rms(x, g):
    return x * lax.rsqrt(jnp.mean(x * x, axis=-1, keepdims=True) + EPS) * g


def _silu(x):
    return x * jax.nn.sigmoid(x)


def _rmsnorm_kernel(x_ref, g_ref, h_ref, *ht_ref):
    y = _rms(x_ref[...], g_ref[...])
    h_ref[...] = y
    if ht_ref:
        ht_ref[0][...] = y.T.astype(bf16)


def rmsnorm(x, g, want_t=False, tm=512):
    t, d = x.shape
    out_shape = [jax.ShapeDtypeStruct((t, d), f32)]
    out_specs = [pl.BlockSpec((tm, d), lambda i: (i, 0))]
    if want_t:
        out_shape.append(jax.ShapeDtypeStruct((d, t), bf16))
        out_specs.append(pl.BlockSpec((d, tm), lambda i: (0, i)))
    res = pl.pallas_call(
        _rmsnorm_kernel,
        grid=(t // tm,),
        in_specs=[pl.BlockSpec((tm, d), lambda i: (i, 0)), pl.BlockSpec((1, d), lambda i: (0, 0))],
        out_specs=out_specs,
        out_shape=out_shape,
        compiler_params=_cparams("parallel"),
        name="rmsnorm_t" if want_t else "rmsnorm",
    )(x, g.reshape(1, d))
    return res if want_t else res[0]


def _matmul_kernel(a_ref, w_ref, *rest):
    o_ref = rest[-1]
    y = jnp.dot(a_ref[...].astype(bf16), w_ref[...], preferred_element_type=f32)
    if len(rest) == 2:
        y = y + rest[0][...]
    o_ref[...] = y


def matmul(a, w, res=None, name="matmul"):
    t, k = a.shape
    n = w.shape[1]
    tm = 256 if n > 2048 else 512
    in_specs = [pl.BlockSpec((tm, k), lambda i: (i, 0)), pl.BlockSpec((k, n), lambda i: (0, 0))]
    args = [a, w]
    if res is not None:
        in_specs.append(pl.BlockSpec((tm, n), lambda i: (i, 0)))
        args.append(res)
    return pl.pallas_call(
        _matmul_kernel,
        grid=(t // tm,),
        in_specs=in_specs,
        out_specs=pl.BlockSpec((tm, n), lambda i: (i, 0)),
        out_shape=jax.ShapeDtypeStruct((t, n), f32),
        compiler_params=_cparams("parallel"),
        name=name,
    )(*args)


def _peer_select_kernel(ht_ref, wqt_ref, sk_ref, pos_ref, b0_ref, r1_ref, e0_ref, e1_ref,
                        qt_ref, s_ref, vals_ref, idx_ref, cs_ref):
    tm = ht_ref.shape[1]
    qt_ref[...] = jnp.dot(wqt_ref[...], ht_ref[...], preferred_element_type=f32).astype(bf16)
    row = lax.broadcasted_iota(jnp.int32, (N_KEYS, tm), 0).astype(f32)
    pos = pos_ref[...]
    neg_inf = f32(-jnp.inf)

    def head_body(h, carry):
        for c in (0, 1):
            off = pl.multiple_of((h * 2 + c) * PEER_HALF, PEER_HALF)
            s = jnp.dot(sk_ref[h, c], qt_ref[pl.ds(off, PEER_HALF), :], preferred_element_type=f32)
            s_ref[c] = s

            def extract(k, sr, c=c):
                s, rank = sr
                m = jnp.max(s, axis=0, keepdims=True)
                idx = jnp.min(jnp.where(s == m, row, f32(N_KEYS)), axis=0, keepdims=True)
                hit = row == idx
                vals_ref[c, pl.ds(k, 1), :] = m
                idx_ref[c, pl.ds(k, 1), :] = idx
                return jnp.where(hit, neg_inf, s), jnp.where(hit, jnp.asarray(k, f32), rank)

            _, rank = lax.fori_loop(0, PEER_TOPK, extract, (s, jnp.full((N_KEYS, tm), f32(PEER_TOPK))))
            if c == 1:
                r1_ref[h] = rank.astype(bf16)

        v0 = vals_ref[0]
        v1 = vals_ref[1]
        r = 0
        for a in range(PEER_TOPK):
            nb = _CAND_COUNTS[a]
            cs_ref[r:r + nb, :] = v0[a:a + 1, :] + v1[0:nb, :]
            r += nb
        cs_ref[_N_CAND:_N_CAND_PAD, :] = jnp.full((_N_CAND_PAD - _N_CAND, tm), neg_inf)
        m1 = v0[0:1, :] + v1[0:1, :]

        def extract2(k, st):
            cs, sel, z = st
            m = jnp.max(cs, axis=0, keepdims=True)
            p = jnp.min(jnp.where(cs == m, pos, f32(_BIG)), axis=0, keepdims=True)
            hit = pos == p
            return jnp.where(hit, neg_inf, cs), jnp.where(hit, f32(1.0), sel), z + jnp.exp(m - m1)

        _, sel, z = lax.fori_loop(0, PEER_TOPK, extract2,
                                  (cs_ref[...], jnp.zeros((_N_CAND_PAD, tm), f32), jnp.zeros((1, tm), f32)))
        idx0 = idx_ref[0]
        bound0 = jnp.zeros((N_KEYS, tm), f32)
        r = 0
        for a in range(PEER_TOPK):
            nb = _CAND_COUNTS[a]
            cnt = jnp.sum(sel[r:r + nb, :], axis=0, keepdims=True)
            bound0 = jnp.where(row == idx0[a:a + 1, :], cnt, bound0)
            r += nb
        b0_ref[h] = bound0
        e0_ref[h] = jnp.exp(s_ref[0] - v0[0:1, :]) / z
        e1_ref[h] = jnp.exp(s_ref[1] - v1[0:1, :]).astype(bf16)
        return carry

    lax.fori_loop(0, PEER_HEADS, head_body, 0)


def _peer_main_kernel(ht_ref, u_ref, v_ref, b0_ref, r1_ref, e0_ref, e1_ref, x_ref, o_ref,
                      acc_ref, *, nblk, npieces):
    j = pl.program_id(1)

    @pl.when(j == 0)
    def _():
        acc_ref[...] = jnp.zeros_like(acc_ref)

    ht = ht_ref[...]
    bpp = nblk // npieces
    pc = bpp * N_KEYS
    for p in range(npieces):
        at = jnp.dot(u_ref[p * pc:(p + 1) * pc, :], ht, preferred_element_type=f32)
        ws = []
        for b in range(bpp):
            i = j * nblk + p * bpp + b
            a = at[b * N_KEYS:(b + 1) * N_KEYS, :]
            g = jnp.zeros(a.shape, bf16)
            for h in range(PEER_HEADS):
                bnd = b0_ref[h, pl.ds(i, 1), :].astype(bf16)
                e0 = e0_ref[h, pl.ds(i, 1), :].astype(bf16)
                g = g + jnp.where(r1_ref[h] < bnd, e1_ref[h] * e0, bf16(0.0))
            gelu = 0.5 * a * (1.0 + lax.erf(a * f32(math.sqrt(0.5))))
            ws.append(g * gelu.astype(bf16))
        wt = jnp.concatenate(ws, axis=0) if bpp > 1 else ws[0]
        acc_ref[...] += lax.dot_general(wt, v_ref[p * pc:(p + 1) * pc, :], TN_DIMS,
                                        preferred_element_type=f32)

    @pl.when(j == pl.num_programs(1) - 1)
    def _():
        o_ref[...] = x_ref[...] + acc_ref[...]


def peer_ffn_residual(x, ht, wqt, sk, u, v, tm_sel=256, tm=256, ce=2048, npieces=8):
    t = x.shape[0]
    d = D_MODEL
    pos_list = []
    for a in range(PEER_TOPK):
        pos_list += [a * PEER_TOPK + b for b in range(_CAND_COUNTS[a])]
    pos_list += [_BIG] * (_N_CAND_PAD - _N_CAND)
    pos = jnp.broadcast_to(jnp.asarray(pos_list, f32)[:, None], (_N_CAND_PAD, tm_sel))
    sel_shape = jax.ShapeDtypeStruct((PEER_HEADS, N_KEYS, t), f32)
    sel_shape16 = jax.ShapeDtypeStruct((PEER_HEADS, N_KEYS, t), bf16)
    sel_spec = pl.BlockSpec((PEER_HEADS, N_KEYS, tm_sel), lambda i: (0, 0, i))
    b0, r1, e0, e1 = pl.pallas_call(
        _peer_select_kernel,
        grid=(t // tm_sel,),
        in_specs=[pl.BlockSpec((d, tm_sel), lambda i: (0, i)),
                  pl.BlockSpec((PEER_HEADS * 2 * PEER_HALF, d), lambda i: (0, 0)),
                  pl.BlockSpec((PEER_HEADS, 2, N_KEYS, PEER_HALF), lambda i: (0, 0, 0, 0)),
                  pl.BlockSpec((_N_CAND_PAD, tm_sel), lambda i: (0, 0))],
        out_specs=[sel_spec] * 4,
        out_shape=[sel_shape, sel_shape16, sel_shape, sel_shape16],
        scratch_shapes=[pltpu.VMEM((PEER_HEADS * 2 * PEER_HALF, tm_sel), bf16),
                        pltpu.VMEM((2, N_KEYS, tm_sel), f32),
                        pltpu.VMEM((2, PEER_TOPK, tm_sel), f32),
                        pltpu.VMEM((2, PEER_TOPK, tm_sel), f32),
                        pltpu.VMEM((_N_CAND_PAD, tm_sel), f32)],
        compiler_params=_cparams("parallel"),
        name="peer_select",
    )(ht, wqt, sk, pos)

    nblk = ce // N_KEYS
    sel_spec2 = pl.BlockSpec((PEER_HEADS, N_KEYS, tm), lambda i, j: (0, 0, i))
    return pl.pallas_call(
        functools.partial(_peer_main_kernel, nblk=nblk, npieces=npieces),
        grid=(t // tm, N_EXPERTS // ce),
        in_specs=[pl.BlockSpec((d, tm), lambda i, j: (0, i)),
                  pl.BlockSpec((ce, d), lambda i, j: (j, 0)),
                  pl.BlockSpec((ce, d), lambda i, j: (j, 0)),
                  sel_spec2, sel_spec2, sel_spec2, sel_spec2,
                  pl.BlockSpec((tm, d), lambda i, j: (i, 0))],
        out_specs=pl.BlockSpec((tm, d), lambda i, j: (i, 0)),
        out_shape=jax.ShapeDtypeStruct((t, d), f32),
        scratch_shapes=[pltpu.VMEM((tm, d), f32)],
        compiler_params=_cparams("parallel", "arbitrary"),
        name="peer_main",
    )(ht, u, v, b0, r1, e0, e1, x)


IN_WIDTHS = (SSM_INNER, CONV_DIM, ATT_HEADS * ATT_HEAD_DIM, 2 * ATT_KV_HEADS * ATT_HEAD_DIM, LANE)


def _in_proj_kernel(x_ref, g_ref, w_ref, *out_refs):
    h = _rms(x_ref[...], g_ref[...])
    y = jnp.dot(h.astype(bf16), w_ref[...], preferred_element_type=f32)
    o = 0
    for ref in out_refs:
        w = ref.shape[1]
        ref[...] = y[:, o:o + w]
        o += w


def in_proj(x, g, w_in, tm=256):
    t, d = x.shape
    o1 = SSM_INNER
    o2 = o1 + CONV_DIM
    o3 = o2 + SSM_HEADS
    w = jnp.concatenate([w_in[:, :o2], w_in[:, o3:], w_in[:, o2:o3],
                         jnp.zeros((d, LANE - SSM_HEADS), w_in.dtype)], axis=1).astype(bf16)
    n = w.shape[1]
    return pl.pallas_call(
        _in_proj_kernel,
        grid=(t // tm,),
        in_specs=[pl.BlockSpec((tm, d), lambda i: (i, 0)), pl.BlockSpec((1, d), lambda i: (0, 0)),
                  pl.BlockSpec((d, n), lambda i: (0, 0))],
        out_specs=[pl.BlockSpec((tm, wd), lambda i: (i, 0)) for wd in IN_WIDTHS],
        out_shape=[jax.ShapeDtypeStruct((t, wd), f32) for wd in IN_WIDTHS],
        compiler_params=_cparams("parallel"),
        name="in_proj",
    )(x, g.reshape(1, d), w)


N_PAIRS = SSM_HEADS // 2


def _ssd_kernel(*refs, lin, aliased):
    (z_ref, xbc_ref, dt_ref, cs_ref, h0_ref, cw_ref, cb_ref, dtb_ref, aneg_ref, dsk_ref, gn_ref, selh_ref) = refs[:12]
    refs = refs[12 + (1 if aliased else 0):]
    y_ref, hfin_ref, xe_ref, hp_ref, ys_ref = refs[:5]
    lc = SSD_CHUNK
    c = pl.program_id(1)

    @pl.when(c == 0)
    def _():
        xe_ref[0:SUBLANE, :] = cs_ref[0]
        hp_ref[...] = h0_ref[0]

    if lin == lc:
        xe_ref[SUBLANE:SUBLANE + lc, :] = xbc_ref[...]
        z = z_ref[...]
        dt_raw = dt_ref[...]
    else:
        zpad_ref, dtpad_ref = refs[5:7]
        xe_ref[SUBLANE:SUBLANE + lc, :] = jnp.zeros((lc, CONV_DIM), f32)
        xe_ref[SUBLANE:SUBLANE + lin, :] = xbc_ref[...]
        zpad_ref[...] = jnp.zeros_like(zpad_ref)
        zpad_ref[0:lin, :] = z_ref[...]
        dtpad_ref[...] = jnp.zeros_like(dtpad_ref)
        dtpad_ref[0:lin, :] = dt_ref[...]
        z = zpad_ref[...]
        dt_raw = dtpad_ref[...]

    conv = cb_ref[...]
    for j in range(CONV_W):
        o = SUBLANE - (CONV_W - 1) + j
        conv = conv + xe_ref[o:o + lc, :] * cw_ref[j:j + 1, :]
    if lin == lc:
        xe_ref[0:SUBLANE, :] = xe_ref[lc:lc + SUBLANE, :]
    xc = _silu(conv)
    xs = xc[:, :SSM_INNER]
    bm = xc[:, SSM_INNER:SSM_INNER + SSM_GROUPS * SSM_STATE]
    cm = xc[:, SSM_INNER + SSM_GROUPS * SSM_STATE:]

    row = lax.broadcasted_iota(jnp.int32, (lc, LANE), 0)
    col = lax.broadcasted_iota(jnp.int32, (lc, LANE), 1)
    causal = row >= col
    lane_lo = col < HALF_LANE
    neg_inf = f32(-jnp.inf)

    x = dt_raw + dtb_ref[...]
    dt = jnp.maximum(x, 0.0) + jnp.log(1.0 + jnp.exp(-jnp.abs(x)))
    if lin != lc:
        dt = jnp.where(row < lin, dt, 0.0)
    la = dt * aneg_ref[...]
    acs = jnp.dot(causal.astype(f32), la, precision=HIGHEST, preferred_element_type=f32)
    acs_t = acs.T
    selh = selh_ref[...]
    dt_exp = jnp.dot(dt, selh, precision=HIGHEST, preferred_element_type=f32)
    acs_exp = jnp.dot(acs, selh, precision=HIGHEST, preferred_element_type=f32)
    alast = acs[lc - 1:lc, :]
    xdt = xs * dt_exp
    eacs = jnp.exp(acs_exp)
    xdt_end = (xdt * jnp.exp(acs_exp[lc - 1:lc, :] - acs_exp)).astype(bf16)
    dsk = dsk_ref[...]

    for g in range(SSM_GROUPS):
        cmg = cm[:, g * SSM_STATE:(g + 1) * SSM_STATE].astype(bf16)
        bmg = bm[:, g * SSM_STATE:(g + 1) * SSM_STATE].astype(bf16)
        cb = lax.dot_general(cmg, bmg, NT_DIMS, preferred_element_type=f32)
        for jj in range(N_PAIRS // SSM_GROUPS):
            j = g * (N_PAIRS // SSM_GROUPS) + jj
            sl = slice(j * LANE, (j + 1) * LANE)
            xdt_pair = xdt[:, sl]
            ydiag = jnp.zeros((lc, LANE), f32)
            for half in (0, 1):
                h = 2 * j + half
                seg = acs[:, h:h + 1] - acs_t[h:h + 1, :]
                dec = jnp.exp(jnp.where(causal, seg, neg_inf))
                m = (cb * dec).astype(bf16)
                keep = lane_lo if half == 0 else jnp.logical_not(lane_lo)
                xd = jnp.where(keep, xdt_pair, 0.0).astype(bf16)
                ydiag = ydiag + jnp.dot(m, xd, preferred_element_type=f32)
            hpj = hp_ref[j]
            yoff = lax.dot_general(cmg, hpj.astype(bf16), NT_DIMS, preferred_element_type=f32) * eacs[:, sl]
            s_new = lax.dot_general(xdt_end[:, sl], bmg, TN_DIMS, preferred_element_type=f32)
            dl = jnp.where(row < SSM_HEAD_DIM, alast[:, 2 * j:2 * j + 1], alast[:, 2 * j + 1:2 * j + 2])
            hp_ref[j] = hpj * jnp.exp(dl) + s_new
            ys_ref[:, sl] = ydiag + yoff + dsk[:, sl] * xs[:, sl]

    y = ys_ref[...] * _silu(z)
    gs = SSM_INNER // SSM_GROUPS
    gn = gn_ref[...]
    for g in range(SSM_GROUPS):
        yg = y[:, g * gs:(g + 1) * gs]
        yg = yg * lax.rsqrt(jnp.mean(yg * yg, axis=-1, keepdims=True) + EPS) * gn[:, g * gs:(g + 1) * gs]
        y_ref[:, g * gs:(g + 1) * gs] = yg[0:lin, :]

    @pl.when(c == pl.num_programs(1) - 1)
    def _():
        hfin_ref[0] = hp_ref[...]


def ssd_mixer(z, xbc, dt, conv_state, ssm_state, params, *, batch, seq, row0, t_total, y_full=None):
    conv_w, conv_b, dt_bias, a_neg, d_skip, gnorm, selh = params
    lc = SSD_CHUNK
    lin = min(seq, lc)
    nc = seq // lin
    blk0 = row0 // lin
    cs = jnp.pad(conv_state, ((0, 0), (SUBLANE - (CONV_W - 1), 0), (0, 0)))
    h0 = ssm_state.reshape(batch, N_PAIRS, LANE, SSM_STATE)

    def rows(w):
        return pl.BlockSpec((lin, w), lambda b, c: (blk0 + b * nc + c, 0))

    def const(shape):
        return pl.BlockSpec(shape, lambda b, c: (0,) * len(shape))

    in_specs = [rows(SSM_INNER), rows(CONV_DIM), rows(LANE),
                pl.BlockSpec((1, SUBLANE, CONV_DIM), lambda b, c: (b, 0, 0)),
                pl.BlockSpec((1, N_PAIRS, LANE, SSM_STATE), lambda b, c: (b, 0, 0, 0)),
                const((CONV_W, CONV_DIM)), const((1, CONV_DIM)), const((1, LANE)), const((1, LANE)),
                const((1, SSM_INNER)), const((1, SSM_INNER)), const((LANE, SSM_INNER))]
    args = [z, xbc, dt, cs, h0, conv_w, conv_b, dt_bias, a_neg, d_skip, gnorm, selh]
    aliases = {}
    if y_full is not None:
        in_specs.append(pl.BlockSpec(memory_space=pl.ANY))
        args.append(y_full)
        aliases = {len(args) - 1: 0}
    scratch = [pltpu.VMEM((lc + 2 * SUBLANE, CONV_DIM), f32), pltpu.VMEM((N_PAIRS, LANE, SSM_STATE), f32),
               pltpu.VMEM((lc, SSM_INNER), f32)]
    if lin != lc:
        scratch += [pltpu.VMEM((lc, SSM_INNER), f32), pltpu.VMEM((lc, LANE), f32)]
    y, h_fin = pl.pallas_call(
        functools.partial(_ssd_kernel, lin=lin, aliased=y_full is not None),
        grid=(batch, nc),
        in_specs=in_specs,
        out_specs=[rows(SSM_INNER), pl.BlockSpec((1, N_PAIRS, LANE, SSM_STATE), lambda b, c: (b, 0, 0, 0))],
        out_shape=[jax.ShapeDtypeStruct((t_total, SSM_INNER), f32),
                   jax.ShapeDtypeStruct((batch, N_PAIRS, LANE, SSM_STATE), f32)],
        scratch_shapes=scratch,
        input_output_aliases=aliases,
        compiler_params=_cparams("parallel", "arbitrary"),
        name="ssd_mixer",
    )(*args)
    return y, h_fin.reshape(batch, SSM_HEADS, SSM_HEAD_DIM, SSM_STATE)


def _swa_kernel(*refs, lq, masked_first, aliased):
    q_ref, kp_ref, vp_ref, kc_ref, vc_ref, bias_ref, sink_ref = refs[:7]
    refs = refs[7 + (1 if aliased else 0):]
    o_ref = refs[0]
    n = pl.program_id(1)
    q = q_ref[...]
    if lq == ATT_BLOCK:
        kc = kc_ref[...]
        vc = vc_ref[...]
    else:
        kpad_ref, vpad_ref = refs[1:3]
        kpad_ref[...] = jnp.zeros_like(kpad_ref)
        vpad_ref[...] = jnp.zeros_like(vpad_ref)
        kpad_ref[0:lq, :] = kc_ref[...]
        vpad_ref[0:lq, :] = vc_ref[...]
        kc = kpad_ref[...]
        vc = vpad_ref[...]
    kp = kp_ref[...]
    vp = vp_ref[...]
    lane_lo = lax.broadcasted_iota(jnp.int32, (ATT_BLOCK, LANE), 1) < HALF_LANE
    lane_lo_q = lax.broadcasted_iota(jnp.int32, (lq, LANE), 1) < HALF_LANE
    neg_inf = f32(-jnp.inf)
    scale = f32(ATT_HEAD_DIM ** -0.5)

    for g in range(ATT_KV_HEADS):
        sl = slice((g // 2) * LANE, (g // 2 + 1) * LANE)
        odd = g % 2 == 1

        def kpad(k):
            pair = k[:, sl]
            if odd:
                pair = pltpu.roll(pair, HALF_LANE, 1)
            return jnp.where(lane_lo, pair, 0.0).astype(bf16)

        def vdup(v):
            pair = v[:, sl]
            rolled = pltpu.roll(pair, HALF_LANE, 1)
            return (jnp.where(lane_lo, rolled, pair) if odd else jnp.where(lane_lo, pair, rolled)).astype(bf16)

        qp0 = q[:, (2 * g) * LANE:(2 * g + 1) * LANE]
        qp1 = q[:, (2 * g + 1) * LANE:(2 * g + 2) * LANE]
        qg = jnp.concatenate([qp0, pltpu.roll(qp0, HALF_LANE, 1), qp1, pltpu.roll(qp1, HALF_LANE, 1)],
                             axis=0).astype(bf16)
        bias = bias_ref[g]
        sp = lax.dot_general(qg, kpad(kp), NT_DIMS, preferred_element_type=f32) * scale + bias[:, :WINDOW]
        sc = lax.dot_general(qg, kpad(kc), NT_DIMS, preferred_element_type=f32) * scale + bias[:, WINDOW:]
        if masked_first:
            sp = jnp.where(n > 0, sp, neg_inf)
        sink = sink_ref[g][:, 0:1]
        m = jnp.maximum(jnp.maximum(jnp.max(sp, axis=-1, keepdims=True), jnp.max(sc, axis=-1, keepdims=True)), sink)
        pp = jnp.exp(sp - m)
        pc = jnp.exp(sc - m)
        denom = jnp.sum(pp, axis=-1, keepdims=True) + jnp.sum(pc, axis=-1, keepdims=True) + jnp.exp(sink - m)
        og = (jnp.dot(pp.astype(bf16), vdup(vp), preferred_element_type=f32)
              + jnp.dot(pc.astype(bf16), vdup(vc), preferred_element_type=f32)) / denom
        o_ref[:, (2 * g) * LANE:(2 * g + 1) * LANE] = jnp.where(lane_lo_q, og[0:lq], og[lq:2 * lq])
        o_ref[:, (2 * g + 1) * LANE:(2 * g + 2) * LANE] = jnp.where(lane_lo_q, og[2 * lq:3 * lq], og[3 * lq:4 * lq])


def _rel_bucket(dist):
    exact = REL_BUCKETS // 2
    d = jnp.maximum(dist, 0)
    large = exact + (jnp.log(jnp.maximum(d, 1).astype(f32) / exact)
                     / math.log(REL_MAX_DIST / exact) * (REL_BUCKETS - exact)).astype(jnp.int32)
    large = jnp.minimum(large, REL_BUCKETS - 1)
    return jnp.where(d < exact, d, large)


def _swa_tables(rel_bias, sinks, lq):
    qi = jnp.arange(lq)[:, None]
    kj = jnp.arange(2 * ATT_BLOCK)[None, :]
    dist = qi + WINDOW - kj
    band = (dist >= 0) & (dist <= WINDOW)
    bias = jnp.where(band[..., None], rel_bias[_rel_bucket(dist)], -jnp.inf)
    bias = jnp.transpose(bias, (2, 0, 1)).reshape(ATT_KV_HEADS, ATT_GQA * lq, 2 * ATT_BLOCK)
    sink = jnp.broadcast_to(sinks.reshape(ATT_KV_HEADS, ATT_GQA, 1, 1), (ATT_KV_HEADS, ATT_GQA, lq, LANE))
    return bias, sink.reshape(ATT_KV_HEADS, ATT_GQA * lq, LANE)


def swa_mixer(q, kv, k_prev, v_prev, rel_bias, sinks, *, batch, seq, row0, t_total, o_full=None):
    lq = min(seq, ATT_BLOCK)
    nb = seq // lq
    blk0 = row0 // lq
    kvw = ATT_KV_HEADS * ATT_HEAD_DIM
    bias, sink = _swa_tables(rel_bias, sinks, lq)
    cur_k = pl.BlockSpec((lq, kvw), lambda b, n: (blk0 + b * nb + n, 0))
    cur_v = pl.BlockSpec((lq, kvw), lambda b, n: (blk0 + b * nb + n, 1))
    if k_prev is None:
        prev_k = pl.BlockSpec((WINDOW, kvw), lambda b, n: (blk0 + b * nb + jnp.maximum(n - 1, 0), 0))
        prev_v = pl.BlockSpec((WINDOW, kvw), lambda b, n: (blk0 + b * nb + jnp.maximum(n - 1, 0), 1))
        kp_arr, vp_arr = kv, kv
    else:
        prev_k = pl.BlockSpec((WINDOW, kvw), lambda b, n: (b, 0))
        prev_v = prev_k
        kp_arr = k_prev.reshape(batch * WINDOW, kvw)
        vp_arr = v_prev.reshape(batch * WINDOW, kvw)
    rows = pl.BlockSpec((lq, ATT_HEADS * ATT_HEAD_DIM), lambda b, n: (blk0 + b * nb + n, 0))
    in_specs = [rows, prev_k, prev_v, cur_k, cur_v,
                pl.BlockSpec(bias.shape, lambda b, n: (0, 0, 0)), pl.BlockSpec(sink.shape, lambda b, n: (0, 0, 0))]
    args = [q, kp_arr, vp_arr, kv, kv, bias, sink]
    aliases = {}
    if o_full is not None:
        in_specs.append(pl.BlockSpec(memory_space=pl.ANY))
        args.append(o_full)
        aliases = {len(args) - 1: 0}
    scratch = [] if lq == ATT_BLOCK else [pltpu.VMEM((ATT_BLOCK, kvw), f32), pltpu.VMEM((ATT_BLOCK, kvw), f32)]
    return pl.pallas_call(
        functools.partial(_swa_kernel, lq=lq, masked_first=k_prev is None, aliased=o_full is not None),
        grid=(batch, nb),
        in_specs=in_specs,
        out_specs=rows,
        out_shape=jax.ShapeDtypeStruct((t_total, ATT_HEADS * ATT_HEAD_DIM), f32),
        scratch_shapes=scratch,
        input_output_aliases=aliases,
        compiler_params=_cparams("parallel", "arbitrary"),
        name="swa_mixer",
    )(*args)


def _out_proj_kernel(ya_ref, yb_ref, w_ref, x_ref, o_ref):
    ka = ya_ref.shape[1]
    acc = jnp.dot(ya_ref[...].astype(bf16), w_ref[0:ka, :], preferred_element_type=f32)
    acc = acc + jnp.dot(yb_ref[...].astype(bf16), w_ref[ka:, :], preferred_element_type=f32)
    o_ref[...] = x_ref[...] + acc


def out_proj(ya, yb, w, x, tm=512):
    t, d = x.shape
    ka, kb = ya.shape[1], yb.shape[1]
    return pl.pallas_call(
        _out_proj_kernel,
        grid=(t // tm,),
        in_specs=[pl.BlockSpec((tm, ka), lambda i: (i, 0)), pl.BlockSpec((tm, kb), lambda i: (i, 0)),
                  pl.BlockSpec((ka + kb, d), lambda i: (0, 0)), pl.BlockSpec((tm, d), lambda i: (i, 0))],
        out_specs=pl.BlockSpec((tm, d), lambda i: (i, 0)),
        out_shape=jax.ShapeDtypeStruct((t, d), f32),
        compiler_params=_cparams("parallel"),
        name="out_proj",
    )(ya, yb, w, x)


RWKV_VGROUP = 8


def _rwkv_scan_kernel(r_ref, d_ref, k_ref, v_ref, kk_ref, a_ref, s0_ref, y_ref, sfin_ref, s_ref):
    tb = r_ref.shape[0]

    @pl.when(pl.program_id(1) == 0)
    def _():
        s_ref[...] = s0_ref[...]

    def step(t, carry):
        kk = kk_ref[t]
        d = d_ref[t]
        kv = k_ref[t]
        r = r_ref[t]
        b = kk * a_ref[t]

        def vgroup(g, c2):
            v0 = pl.multiple_of(g * RWKV_VGROUP, RWKV_VGROUP)
            vrows = v_ref[t, pl.ds(v0, RWKV_VGROUP), :]
            ys = []
            for vi in range(RWKV_VGROUP):
                sv = s_ref[v0 + vi]
                sa = -jnp.sum(sv * kk, axis=0, keepdims=True)
                sn = sv * d + sa * b + vrows[vi:vi + 1, :] * kv
                s_ref[v0 + vi] = sn
                ys.append(jnp.sum(sn * r, axis=0, keepdims=True))
            y_ref[t, pl.ds(v0, RWKV_VGROUP), :] = jnp.concatenate(ys, axis=0)
            return c2

        lax.fori_loop(0, RWKV_HEAD // RWKV_VGROUP, vgroup, 0)
        return carry

    lax.fori_loop(0, tb, step, 0)

    @pl.when(pl.program_id(1) == pl.num_programs(1) - 1)
    def _():
        sfin_ref[...] = s_ref[...]


def rwkv_scan(r, d, k, v, kk, a, s0):
    L, hd, c = r.shape
    tb = min(L, 16)
    seq_spec = pl.BlockSpec((tb, hd, LANE), lambda i, j: (j, 0, i))
    st_spec = pl.BlockSpec((hd, hd, LANE), lambda i, j: (0, 0, i))
    return pl.pallas_call(
        _rwkv_scan_kernel,
        grid=(c // LANE, L // tb),
        in_specs=[seq_spec] * 6 + [st_spec],
        out_specs=[seq_spec, st_spec],
        out_shape=[jax.ShapeDtypeStruct((L, hd, c), f32), jax.ShapeDtypeStruct((hd, hd, c), f32)],
        scratch_shapes=[pltpu.VMEM((hd, hd, LANE), f32)],
        compiler_params=_cparams("parallel", "arbitrary"),
        name="rwkv_scan",
    )(r, d, k, v, kk, a, s0)


def _rwkv_core(r, k, v, w, a_pre, wkv, k_k, k_a, r_k, ln_w, ln_b):
    b, L, _ = r.shape
    hk = (RWKV_HEADS, RWKV_HEAD)
    r = r.reshape(b, L, *hk)
    k = k.reshape(b, L, *hk)
    v = v.reshape(b, L, *hk)
    w = -jax.nn.softplus(-w) - 0.5
    decay = jnp.exp(-jnp.exp(w)).reshape(b, L, *hk)
    iclr = jax.nn.sigmoid(a_pre).reshape(b, L, *hk)
    kk = k * k_k.reshape(hk)
    kk = kk / jnp.maximum(jnp.sqrt(jnp.sum(kk * kk, axis=-1, keepdims=True)), 1e-12)
    k = k * (1.0 + (iclr - 1.0) * k_a.reshape(hk))

    def to_chain(t):
        return jnp.transpose(t, (1, 3, 0, 2)).reshape(L, RWKV_HEAD, b * RWKV_HEADS)

    s0 = jnp.transpose(wkv, (2, 3, 0, 1)).reshape(RWKV_HEAD, RWKV_HEAD, b * RWKV_HEADS)
    y, s_fin = rwkv_scan(*[to_chain(t) for t in (r, decay, k, v, kk, iclr)], s0)
    y = jnp.transpose(y.reshape(L, RWKV_HEAD, b, RWKV_HEADS), (2, 0, 3, 1))
    S_fin = jnp.transpose(s_fin.reshape(RWKV_HEAD, RWKV_HEAD, b, RWKV_HEADS), (2, 3, 0, 1))
    mean = jnp.mean(y, axis=-1, keepdims=True)
    var = jnp.mean(jnp.square(y - mean), axis=-1, keepdims=True)
    y = (y - mean) * lax.rsqrt(var + GN_EPS) * ln_w.reshape(hk) + ln_b.reshape(hk)
    y = y + jnp.sum(r * k * r_k, axis=-1, keepdims=True) * v
    return y.reshape(b, L, D_MODEL), S_fin


def kernel(x_prompt, x_sample, state_ssm, state_conv, cache_swa_k, cache_swa_v, state_wkv, state_shift, rel_bias, norm_mix, norm_ffn, norm_final, mix_w_in, ssd_conv_w, ssd_conv_b, ssd_dt_bias, ssd_a_log, ssd_d_skip, ssd_gnorm, attn_sinks, mix_w_out, rwkv_mu, rwkv_w0, rwkv_w1, rwkv_w2, rwkv_a0, rwkv_a1, rwkv_a2, rwkv_g1, rwkv_g2, rwkv_k_k, rwkv_k_a, rwkv_r_k, rwkv_w_rkv, rwkv_w_o, rwkv_ln_w, rwkv_ln_b, peer_w_q, peer_sub_keys, peer_u, peer_v):
    bp, lp, d = x_prompt.shape
    bs, ls, _ = x_sample.shape
    tp = bp * lp
    ts = bs * ls
    tt = tp + ts
    x = jnp.concatenate([x_prompt.reshape(tp, d), x_sample.reshape(ts, d)], axis=0)

    def split(t):
        return t[:tp].reshape(bp, lp, -1), t[tp:].reshape(bs, ls, -1)

    def merge(p, s):
        return jnp.concatenate([p.reshape(tp, -1), s.reshape(ts, -1)], axis=0)

    def zero_state(a):
        return jnp.zeros((bp,) + a.shape[2:], a.dtype)

    def peer(x, layer):
        _, ht = rmsnorm(x, norm_ffn[layer], want_t=True)
        wqt = peer_w_q[layer].T.astype(bf16)
        return peer_ffn_residual(x, ht, wqt, peer_sub_keys[layer].astype(bf16),
                                 peer_u[layer].astype(bf16), peer_v[layer].astype(bf16))

    z, xbc, q, kv, dt = in_proj(x, norm_mix[0], mix_w_in[0])
    pad16 = (0, LANE - SSM_HEADS)
    selh = (jnp.arange(LANE)[:, None] == jnp.arange(SSM_INNER)[None, :] // SSM_HEAD_DIM).astype(f32)
    ssd_params = (ssd_conv_w[0], ssd_conv_b[0].reshape(1, CONV_DIM),
                  jnp.pad(ssd_dt_bias[0], pad16).reshape(1, LANE),
                  jnp.pad(-jnp.exp(ssd_a_log[0]), pad16).reshape(1, LANE),
                  jnp.repeat(ssd_d_skip[0], SSM_HEAD_DIM).reshape(1, SSM_INNER),
                  ssd_gnorm[0].reshape(1, SSM_INNER), selh)
    y_ssd, ssm_p = ssd_mixer(z, xbc, dt, zero_state(state_conv), zero_state(state_ssm), ssd_params,
                             batch=bp, seq=lp, row0=0, t_total=tt)
    y_ssd, ssm_s = ssd_mixer(z, xbc, dt, state_conv[0], state_ssm[0], ssd_params,
                             batch=bs, seq=ls, row0=tp, t_total=tt, y_full=y_ssd)
    o_att = swa_mixer(q, kv, None, None, rel_bias, attn_sinks[0], batch=bp, seq=lp, row0=0, t_total=tt)
    o_att = swa_mixer(q, kv, cache_swa_k[0], cache_swa_v[0], rel_bias, attn_sinks[0],
                      batch=bs, seq=ls, row0=tp, t_total=tt, o_full=o_att)
    x = out_proj(y_ssd, o_att, mix_w_out[0].astype(bf16), x)
    xbc_p, xbc_s = split(xbc)
    conv_p = xbc_p[:, -(CONV_W - 1):]
    conv_s = jnp.concatenate([state_conv[0], xbc_s], axis=1)[:, -(CONV_W - 1):]
    kv_p, kv_s = split(kv)
    kvw = ATT_KV_HEADS * ATT_HEAD_DIM
    hshape = (ATT_KV_HEADS, ATT_HEAD_DIM)
    k_p = kv_p[:, -WINDOW:, :kvw].reshape(bp, WINDOW, *hshape)
    v_p = kv_p[:, -WINDOW:, kvw:].reshape(bp, WINDOW, *hshape)
    k_s = jnp.concatenate([cache_swa_k[0], kv_s[..., :kvw].reshape(bs, ls, *hshape)], axis=1)[:, -WINDOW:]
    v_s = jnp.concatenate([cache_swa_v[0], kv_s[..., kvw:].reshape(bs, ls, *hshape)], axis=1)[:, -WINDOW:]
    x = peer(x, 0)

    h = rmsnorm(x, norm_mix[1])
    h_p, h_s = split(h)
    prev_p = jnp.concatenate([jnp.zeros((bp, 1, d), f32), h_p[:, :-1]], axis=1)
    prev_s = jnp.concatenate([state_shift[0][:, None], h_s[:, :-1]], axis=1)
    xx = merge(prev_p, prev_s) - h
    mu = rwkv_mu[0]
    xr, xw, xk, xv, xa, xg = [h + xx * mu[j] for j in range(6)]
    w_rkv = rwkv_w_rkv[0].astype(bf16)
    r = matmul(xr, w_rkv[0], name="rwkv_r")
    k = matmul(xk, w_rkv[1], name="rwkv_k")
    v = matmul(xv, w_rkv[2], name="rwkv_v")

    def lora(xin, w1, w2, act, name):
        rank = w1.shape[1]
        rpad = -(-rank // LANE) * LANE
        w1p = jnp.pad(w1, ((0, 0), (0, rpad - rank))).astype(bf16)
        w2p = jnp.pad(w2, ((0, rpad - rank), (0, 0))).astype(bf16)
        return matmul(act(matmul(xin, w1p, name=name + "1")), w2p, name=name + "2")

    w = rwkv_w0[0] + lora(xw, rwkv_w1[0], rwkv_w2[0], jnp.tanh, "rwkv_w")
    a_pre = rwkv_a0[0] + lora(xa, rwkv_a1[0], rwkv_a2[0], lambda t: t, "rwkv_a")
    g = lora(xg, rwkv_g1[0], rwkv_g2[0], jax.nn.sigmoid, "rwkv_g")
    cargs = (rwkv_k_k[0], rwkv_k_a[0], rwkv_r_k[0], rwkv_ln_w[0], rwkv_ln_b[0])
    parts_p = [split(t)[0] for t in (r, k, v, w, a_pre)]
    parts_s = [split(t)[1] for t in (r, k, v, w, a_pre)]
    y_p, wkv_p = _rwkv_core(*parts_p, zero_state(state_wkv), *cargs)
    y_s, wkv_s = _rwkv_core(*parts_s, state_wkv[0], *cargs)
    x = matmul(merge(y_p, y_s) * g, rwkv_w_o[0].astype(bf16), res=x, name="rwkv_o")
    shift_p, shift_s = h_p[:, -1], h_s[:, -1]
    x = peer(x, 1)

    y = rmsnorm(x, norm_final)
    y_p, y_s = split(y)
    return (y_p, y_s, ssm_p[None], conv_p[None], k_p[None], v_p[None], wkv_p[None], shift_p[None],
            ssm_s[None], conv_s[None], k_s[None], v_s[None], wkv_s[None], shift_s[None])
```

```python
import functools
import math

import jax
import jax.numpy as jnp
from jax import lax
from jax.experimental import pallas as pl
from jax.experimental.pallas import tpu as pltpu

f32 = jnp.float32
bf16 = jnp.bfloat16

D_MODEL = 1024
PAST_LEN = 16384
SSM_HEAD_DIM = 64
SSM_HEADS = 16
SSM_INNER = 1024
SSM_GROUPS = 2
SSM_STATE = 128
CONV_W = 4
CONV_DIM = 1536
SSD_CHUNK = 128
ATT_HEAD_DIM = 64
ATT_HEADS = 16
ATT_KV_HEADS = 4
ATT_GQA = 4
WINDOW = 128
ATT_BLOCK = 128
REL_BUCKETS = 32
REL_MAX_DIST = 128
RWKV_HEAD = 64
RWKV_HEADS = 16
PEER_HEADS = 8
N_KEYS = 128
N_EXPERTS = N_KEYS * N_KEYS
PEER_TOPK = 16
PEER_HALF = 128
EPS = 1e-5
GN_EPS = 64e-5

LANE = 128
SUBLANE = 8
HALF_LANE = LANE // 2
VMEM_LIMIT = 56 * 2 ** 20
HIGHEST = lax.Precision.HIGHEST
NT_DIMS = (((1,), (1,)), ((), ()))
TN_DIMS = (((0,), (0,)), ((), ()))

_CAND_COUNTS = [PEER_TOPK // (a + 1) for a in range(PEER_TOPK)]
_N_CAND = sum(_CAND_COUNTS)
_N_CAND_PAD = -(-_N_CAND // 8) * 8
_BIG = 1e9


def _cparams(*sem):
    return pltpu.CompilerParams(dimension_semantics=sem, vmem_limit_bytes=VMEM_LIMIT)


def _rms(x, g):
    return x * lax.rsqrt(jnp.mean(x * x, axis=-1, keepdims=True) + EPS) * g


def _silu(x):
    return x * jax.nn.sigmoid(x)


def _rmsnorm_kernel(x_ref, g_ref, h_ref, *ht_ref):
    y = _rms(x_ref[...], g_ref[...])
    h_ref[...] = y
    if ht_ref:
        ht_ref[0][...] = y.T.astype(bf16)


def rmsnorm(x, g, want_t=False, tm=512):
    t, d = x.shape
    out_shape = [jax.ShapeDtypeStruct((t, d), f32)]
    out_specs = [pl.BlockSpec((tm, d), lambda i: (i, 0))]
    if want_t:
        out_shape.append(jax.ShapeDtypeStruct((d, t), bf16))
        out_specs.append(pl.BlockSpec((d, tm), lambda i: (0, i)))
    res = pl.pallas_call(
        _rmsnorm_kernel,
        grid=(t // tm,),
        in_specs=[pl.BlockSpec((tm, d), lambda i: (i, 0)), pl.BlockSpec((1, d), lambda i: (0, 0))],
        out_specs=out_specs,
        out_shape=out_shape,
        compiler_params=_cparams("parallel"),
        name="rmsnorm_t" if want_t else "rmsnorm",
    )(x, g.reshape(1, d))
    return res if want_t else res[0]


def _matmul_kernel(a_ref, w_ref, *rest):
    o_ref = rest[-1]
    y = jnp.dot(a_ref[...].astype(bf16), w_ref[...], preferred_element_type=f32)
    if len(rest) == 2:
        y = y + rest[0][...]
    o_ref[...] = y


def matmul(a, w, res=None, name="matmul"):
    t, k = a.shape
    n = w.shape[1]
    tm = 256 if n > 2048 else 512
    in_specs = [pl.BlockSpec((tm, k), lambda i: (i, 0)), pl.BlockSpec((k, n), lambda i: (0, 0))]
    args = [a, w]
    if res is not None:
        in_specs.append(pl.BlockSpec((tm, n), lambda i: (i, 0)))
        args.append(res)
    return pl.pallas_call(
        _matmul_kernel,
        grid=(t // tm,),
        in_specs=in_specs,
        out_specs=pl.BlockSpec((tm, n), lambda i: (i, 0)),
        out_shape=jax.ShapeDtypeStruct((t, n), f32),
        compiler_params=_cparams("parallel"),
        name=name,
    )(*args)


def _peer_select_kernel(ht_ref, wqt_ref, sk_ref, pos_ref, b0_ref, r1_ref, e0_ref, e1_ref,
                        qt_ref, s_ref, vals_ref, idx_ref, cs_ref):
    tm = ht_ref.shape[1]
    qt_ref[...] = jnp.dot(wqt_ref[...], ht_ref[...], preferred_element_type=f32).astype(bf16)
    row = lax.broadcasted_iota(jnp.int32, (N_KEYS, tm), 0).astype(f32)
    pos = pos_ref[...]
    neg_inf = f32(-jnp.inf)

    def head_compute(h, exact):
        bad = jnp.zeros((1, tm), f32)
        for c in (0, 1):
            off = pl.multiple_of((h * 2 + c) * PEER_HALF, PEER_HALF)
            s = jnp.dot(sk_ref[h, c], qt_ref[pl.ds(off, PEER_HALF), :], preferred_element_type=f32)
            s_ref[c] = s

            def extract(k, sr, c=c):
                s, rank = sr
                m = jnp.max(s, axis=0, keepdims=True)
                vals_ref[c, pl.ds(k, 1), :] = m
                if exact:
                    idx = jnp.min(jnp.where(s == m, row, f32(N_KEYS)), axis=0, keepdims=True)
                    idx_ref[c, pl.ds(k, 1), :] = idx
                    hit = row == idx
                else:
                    hit = s == m
                if c == 1:
                    rank = jnp.where(hit, jnp.asarray(k, f32), rank)
                return jnp.where(hit, neg_inf, s), rank

            rank0 = jnp.full((N_KEYS if c == 1 else SUBLANE, tm), f32(PEER_TOPK))
            s_fin, rank = lax.fori_loop(0, PEER_TOPK, extract, (s, rank0))
            if not exact:
                removed = jnp.sum(jnp.where(s_fin == neg_inf, f32(1.0), f32(0.0)), axis=0, keepdims=True)
                bad = bad + jnp.abs(removed - f32(PEER_TOPK))
            if c == 1:
                r1_ref[h] = rank.astype(bf16)

        v0 = vals_ref[0]
        v1 = vals_ref[1]
        r = 0
        for a in range(PEER_TOPK):
            nb = _CAND_COUNTS[a]
            cs_ref[r:r + nb, :] = v0[a:a + 1, :] + v1[0:nb, :]
            r += nb
        cs_ref[_N_CAND:_N_CAND_PAD, :] = jnp.full((_N_CAND_PAD - _N_CAND, tm), neg_inf)
        m1 = v0[0:1, :] + v1[0:1, :]

        def extract2(k, st):
            cs, sel, z = st
            m = jnp.max(cs, axis=0, keepdims=True)
            if exact:
                p = jnp.min(jnp.where(cs == m, pos, f32(_BIG)), axis=0, keepdims=True)
                hit = pos == p
            else:
                hit = cs == m
            return jnp.where(hit, neg_inf, cs), jnp.where(hit, f32(1.0), sel), z + jnp.exp(m - m1)

        _, sel, z = lax.fori_loop(0, PEER_TOPK, extract2,
                                  (cs_ref[...], jnp.zeros((_N_CAND_PAD, tm), f32), jnp.zeros((1, tm), f32)))
        if not exact:
            bad = bad + jnp.abs(jnp.sum(sel, axis=0, keepdims=True) - f32(PEER_TOPK))
        s0 = s_ref[0]
        idx0 = idx_ref[0]
        bound0 = jnp.zeros((N_KEYS, tm), f32)
        r = 0
        for a in range(PEER_TOPK):
            nb = _CAND_COUNTS[a]
            cnt = jnp.sum(sel[r:r + nb, :], axis=0, keepdims=True)
            is_a = (row == idx0[a:a + 1, :]) if exact else (s0 == v0[a:a + 1, :])
            bound0 = jnp.where(is_a, cnt, bound0)
            r += nb
        b0_ref[h] = bound0
        e0_ref[h] = jnp.exp(s0 - v0[0:1, :]) / z
        e1_ref[h] = jnp.exp(s_ref[1] - v1[0:1, :]).astype(bf16)
        return bad

    def head_body(h, carry):
        bad = head_compute(h, False)

        @pl.when(jnp.max(bad) > 0.0)
        def _():
            head_compute(h, True)

        return carry

    lax.fori_loop(0, PEER_HEADS, head_body, 0)


def _peer_main_kernel(ht_ref, u_ref, v_ref, b0_ref, r1_ref, e0_ref, e1_ref, x_ref, gn_ref, o_ref, hn_ref,
                      acc_ref, *, nblk, npieces):
    j = pl.program_id(1)

    @pl.when(j == 0)
    def _():
        acc_ref[...] = jnp.zeros_like(acc_ref)

    ht = ht_ref[...]
    bpp = nblk // npieces
    pc = bpp * N_KEYS
    for p in range(npieces):
        at = jnp.dot(u_ref[p * pc:(p + 1) * pc, :], ht, preferred_element_type=f32)
        ws = []
        for b in range(bpp):
            i = j * nblk + p * bpp + b
            a = at[b * N_KEYS:(b + 1) * N_KEYS, :]
            g = jnp.zeros(a.shape, bf16)
            for h in range(PEER_HEADS):
                bnd = b0_ref[h, pl.ds(i, 1), :].astype(bf16)
                e0 = e0_ref[h, pl.ds(i, 1), :].astype(bf16)
                g = g + jnp.where(r1_ref[h] < bnd, e1_ref[h] * e0, bf16(0.0))
            gelu = 0.5 * a * (1.0 + lax.erf(a * f32(math.sqrt(0.5))))
            ws.append(g * gelu.astype(bf16))
        wt = jnp.concatenate(ws, axis=0) if bpp > 1 else ws[0]
        acc_ref[...] += lax.dot_general(wt, v_ref[p * pc:(p + 1) * pc, :], TN_DIMS,
                                        preferred_element_type=f32)

    @pl.when(j == pl.num_programs(1) - 1)
    def _():
        xn = x_ref[...] + acc_ref[...]
        o_ref[...] = xn
        hn_ref[...] = _rms(xn, gn_ref[...])


def peer_ffn_residual(x, ht, wqt, sk, u, v, g_next, tm_sel=256, tm=256, ce=2048, npieces=8):
    t = x.shape[0]
    d = D_MODEL
    pos_list = []
    for a in range(PEER_TOPK):
        pos_list += [a * PEER_TOPK + b for b in range(_CAND_COUNTS[a])]
    pos_list += [_BIG] * (_N_CAND_PAD - _N_CAND)
    pos = jnp.broadcast_to(jnp.asarray(pos_list, f32)[:, None], (_N_CAND_PAD, tm_sel))
    sel_shape = jax.ShapeDtypeStruct((PEER_HEADS, N_KEYS, t), f32)
    sel_shape16 = jax.ShapeDtypeStruct((PEER_HEADS, N_KEYS, t), bf16)
    sel_spec = pl.BlockSpec((PEER_HEADS, N_KEYS, tm_sel), lambda i: (0, 0, i))
    b0, r1, e0, e1 = pl.pallas_call(
        _peer_select_kernel,
        grid=(t // tm_sel,),
        in_specs=[pl.BlockSpec((d, tm_sel), lambda i: (0, i)),
                  pl.BlockSpec((PEER_HEADS * 2 * PEER_HALF, d), lambda i: (0, 0)),
                  pl.BlockSpec((PEER_HEADS, 2, N_KEYS, PEER_HALF), lambda i: (0, 0, 0, 0)),
                  pl.BlockSpec((_N_CAND_PAD, tm_sel), lambda i: (0, 0))],
        out_specs=[sel_spec] * 4,
        out_shape=[sel_shape, sel_shape16, sel_shape, sel_shape16],
        scratch_shapes=[pltpu.VMEM((PEER_HEADS * 2 * PEER_HALF, tm_sel), bf16),
                        pltpu.VMEM((2, N_KEYS, tm_sel), f32),
                        pltpu.VMEM((2, PEER_TOPK, tm_sel), f32),
                        pltpu.VMEM((2, PEER_TOPK, tm_sel), f32),
                        pltpu.VMEM((_N_CAND_PAD, tm_sel), f32)],
        compiler_params=_cparams("parallel"),
        name="peer_select",
    )(ht, wqt, sk, pos)

    nblk = ce // N_KEYS
    sel_spec2 = pl.BlockSpec((PEER_HEADS, N_KEYS, tm), lambda i, j: (0, 0, i))
    return pl.pallas_call(
        functools.partial(_peer_main_kernel, nblk=nblk, npieces=npieces),
        grid=(t // tm, N_EXPERTS // ce),
        in_specs=[pl.BlockSpec((d, tm), lambda i, j: (0, i)),
                  pl.BlockSpec((ce, d), lambda i, j: (j, 0)),
                  pl.BlockSpec((ce, d), lambda i, j: (j, 0)),
                  sel_spec2, sel_spec2, sel_spec2, sel_spec2,
                  pl.BlockSpec((tm, d), lambda i, j: (i, 0)), pl.BlockSpec((1, d), lambda i, j: (0, 0))],
        out_specs=[pl.BlockSpec((tm, d), lambda i, j: (i, 0))] * 2,
        out_shape=[jax.ShapeDtypeStruct((t, d), f32)] * 2,
        scratch_shapes=[pltpu.VMEM((tm, d), f32)],
        compiler_params=_cparams("parallel", "arbitrary"),
        name="peer_main",
    )(ht, u, v, b0, r1, e0, e1, x, g_next.reshape(1, d))


IN_WIDTHS = (SSM_INNER, CONV_DIM, ATT_HEADS * ATT_HEAD_DIM, 2 * ATT_KV_HEADS * ATT_HEAD_DIM, LANE)


def _in_proj_kernel(x_ref, g_ref, w_ref, *out_refs):
    h = _rms(x_ref[...], g_ref[...])
    y = jnp.dot(h.astype(bf16), w_ref[...], preferred_element_type=f32)
    o = 0
    for ref in out_refs:
        w = ref.shape[1]
        ref[...] = y[:, o:o + w]
        o += w


def in_proj(x, g, w_in, tm=256):
    t, d = x.shape
    o1 = SSM_INNER
    o2 = o1 + CONV_DIM
    o3 = o2 + SSM_HEADS
    w = jnp.concatenate([w_in[:, :o2], w_in[:, o3:], w_in[:, o2:o3],
                         jnp.zeros((d, LANE - SSM_HEADS), w_in.dtype)], axis=1).astype(bf16)
    n = w.shape[1]
    return pl.pallas_call(
        _in_proj_kernel,
        grid=(t // tm,),
        in_specs=[pl.BlockSpec((tm, d), lambda i: (i, 0)), pl.BlockSpec((1, d), lambda i: (0, 0)),
                  pl.BlockSpec((d, n), lambda i: (0, 0))],
        out_specs=[pl.BlockSpec((tm, wd), lambda i: (i, 0)) for wd in IN_WIDTHS],
        out_shape=[jax.ShapeDtypeStruct((t, wd), f32) for wd in IN_WIDTHS],
        compiler_params=_cparams("parallel"),
        name="in_proj",
    )(x, g.reshape(1, d), w)


N_PAIRS = SSM_HEADS // 2


def _ssd_kernel(*refs, lin):
    (z_ref, xbc_ref, dt_ref, cs_ref, h0_ref, cw_ref, cb_ref, dtb_ref, aneg_ref, dsk_ref, gn_ref, selh_ref) = refs[:12]
    refs = refs[12:]
    y_ref, hfin_ref, xe_ref, hp_ref, ys_ref = refs[:5]
    lc = SSD_CHUNK
    c = pl.program_id(1)

    @pl.when(c == 0)
    def _():
        xe_ref[0:SUBLANE, :] = cs_ref[0]
        hp_ref[...] = h0_ref[0]

    if lin == lc:
        xe_ref[SUBLANE:SUBLANE + lc, :] = xbc_ref[...]
        z = z_ref[...]
        dt_raw = dt_ref[...]
    else:
        zpad_ref, dtpad_ref = refs[5:7]
        xe_ref[SUBLANE:SUBLANE + lc, :] = jnp.zeros((lc, CONV_DIM), f32)
        xe_ref[SUBLANE:SUBLANE + lin, :] = xbc_ref[...]
        zpad_ref[...] = jnp.zeros_like(zpad_ref)
        zpad_ref[0:lin, :] = z_ref[...]
        dtpad_ref[...] = jnp.zeros_like(dtpad_ref)
        dtpad_ref[0:lin, :] = dt_ref[...]
        z = zpad_ref[...]
        dt_raw = dtpad_ref[...]

    conv = cb_ref[...]
    for j in range(CONV_W):
        o = SUBLANE - (CONV_W - 1) + j
        conv = conv + xe_ref[o:o + lc, :] * cw_ref[j:j + 1, :]
    if lin == lc:
        xe_ref[0:SUBLANE, :] = xe_ref[lc:lc + SUBLANE, :]
    xc = _silu(conv)
    xs = xc[:, :SSM_INNER]
    bm = xc[:, SSM_INNER:SSM_INNER + SSM_GROUPS * SSM_STATE]
    cm = xc[:, SSM_INNER + SSM_GROUPS * SSM_STATE:]

    row = lax.broadcasted_iota(jnp.int32, (lc, LANE), 0)
    col = lax.broadcasted_iota(jnp.int32, (lc, LANE), 1)
    causal = row >= col
    lane_lo = col < HALF_LANE
    neg_inf = f32(-jnp.inf)

    x = dt_raw + dtb_ref[...]
    dt = jnp.maximum(x, 0.0) + jnp.log(1.0 + jnp.exp(-jnp.abs(x)))
    if lin != lc:
        dt = jnp.where(row < lin, dt, 0.0)
    la = dt * aneg_ref[...]
    acs = jnp.dot(causal.astype(f32), la, precision=HIGHEST, preferred_element_type=f32)
    acs_t = acs.T
    selh = selh_ref[...]
    dt_exp = jnp.dot(dt, selh, precision=HIGHEST, preferred_element_type=f32)
    acs_exp = jnp.dot(acs, selh, precision=HIGHEST, preferred_element_type=f32)
    alast = acs[lc - 1:lc, :]
    xdt = xs * dt_exp
    eacs = jnp.exp(acs_exp)
    xdt_end = (xdt * jnp.exp(acs_exp[lc - 1:lc, :] - acs_exp)).astype(bf16)
    dsk = dsk_ref[...]

    for g in range(SSM_GROUPS):
        cmg = cm[:, g * SSM_STATE:(g + 1) * SSM_STATE].astype(bf16)
        bmg = bm[:, g * SSM_STATE:(g + 1) * SSM_STATE].astype(bf16)
        cb = lax.dot_general(cmg, bmg, NT_DIMS, preferred_element_type=f32)
        for jj in range(N_PAIRS // SSM_GROUPS):
            j = g * (N_PAIRS // SSM_GROUPS) + jj
            sl = slice(j * LANE, (j + 1) * LANE)
            xdt_pair = xdt[:, sl]
            ydiag = jnp.zeros((lc, LANE), f32)
            for half in (0, 1):
                h = 2 * j + half
                seg = acs[:, h:h + 1] - acs_t[h:h + 1, :]
                dec = jnp.exp(jnp.where(causal, seg, neg_inf))
                m = (cb * dec).astype(bf16)
                keep = lane_lo if half == 0 else jnp.logical_not(lane_lo)
                xd = jnp.where(keep, xdt_pair, 0.0).astype(bf16)
                ydiag = ydiag + jnp.dot(m, xd, preferred_element_type=f32)
            hpj = hp_ref[j]
            yoff = lax.dot_general(cmg, hpj.astype(bf16), NT_DIMS, preferred_element_type=f32) * eacs[:, sl]
            s_new = lax.dot_general(xdt_end[:, sl], bmg, TN_DIMS, preferred_element_type=f32)
            dl = jnp.where(row < SSM_HEAD_DIM, alast[:, 2 * j:2 * j + 1], alast[:, 2 * j + 1:2 * j + 2])
            hp_ref[j] = hpj * jnp.exp(dl) + s_new
            ys_ref[:, sl] = ydiag + yoff + dsk[:, sl] * xs[:, sl]

    y = ys_ref[...] * _silu(z)
    gs = SSM_INNER // SSM_GROUPS
    gn = gn_ref[...]
    for g in range(SSM_GROUPS):
        yg = y[:, g * gs:(g + 1) * gs]
        yg = yg * lax.rsqrt(jnp.mean(yg * yg, axis=-1, keepdims=True) + EPS) * gn[:, g * gs:(g + 1) * gs]
        y_ref[:, g * gs:(g + 1) * gs] = yg[0:lin, :]

    @pl.when(c == pl.num_programs(1) - 1)
    def _():
        hfin_ref[0] = hp_ref[...]


def ssd_mixer(z, xbc, dt, conv_state, ssm_state, params, *, batch, seq, row0):
    conv_w, conv_b, dt_bias, a_neg, d_skip, gnorm, selh = params
    lc = SSD_CHUNK
    lin = min(seq, lc)
    nc = seq // lin
    blk0 = row0 // lin
    cs = jnp.pad(conv_state, ((0, 0), (SUBLANE - (CONV_W - 1), 0), (0, 0)))
    h0 = ssm_state.reshape(batch, N_PAIRS, LANE, SSM_STATE)

    def rows(w):
        return pl.BlockSpec((lin, w), lambda b, c: (blk0 + b * nc + c, 0))

    def const(shape):
        return pl.BlockSpec(shape, lambda b, c: (0,) * len(shape))

    in_specs = [rows(SSM_INNER), rows(CONV_DIM), rows(LANE),
                pl.BlockSpec((1, SUBLANE, CONV_DIM), lambda b, c: (b, 0, 0)),
                pl.BlockSpec((1, N_PAIRS, LANE, SSM_STATE), lambda b, c: (b, 0, 0, 0)),
                const((CONV_W, CONV_DIM)), const((1, CONV_DIM)), const((1, LANE)), const((1, LANE)),
                const((1, SSM_INNER)), const((1, SSM_INNER)), const((LANE, SSM_INNER))]
    args = [z, xbc, dt, cs, h0, conv_w, conv_b, dt_bias, a_neg, d_skip, gnorm, selh]
    scratch = [pltpu.VMEM((lc + 2 * SUBLANE, CONV_DIM), f32), pltpu.VMEM((N_PAIRS, LANE, SSM_STATE), f32),
               pltpu.VMEM((lc, SSM_INNER), f32)]
    if lin != lc:
        scratch += [pltpu.VMEM((lc, SSM_INNER), f32), pltpu.VMEM((lc, LANE), f32)]
    y, h_fin = pl.pallas_call(
        functools.partial(_ssd_kernel, lin=lin),
        grid=(batch, nc),
        in_specs=in_specs,
        out_specs=[pl.BlockSpec((lin, SSM_INNER), lambda b, c: (b * nc + c, 0)),
                   pl.BlockSpec((1, N_PAIRS, LANE, SSM_STATE), lambda b, c: (b, 0, 0, 0))],
        out_shape=[jax.ShapeDtypeStruct((batch * seq, SSM_INNER), f32),
                   jax.ShapeDtypeStruct((batch, N_PAIRS, LANE, SSM_STATE), f32)],
        scratch_shapes=scratch,
        compiler_params=_cparams("parallel", "arbitrary"),
        name="ssd_mixer",
    )(*args)
    return y, h_fin.reshape(batch, SSM_HEADS, SSM_HEAD_DIM, SSM_STATE)


def _swa_kernel(*refs, lq, masked_first):
    q_ref, kp_ref, vp_ref, kc_ref, vc_ref, bias_ref, sink_ref = refs[:7]
    refs = refs[7:]
    o_ref = refs[0]
    n = pl.program_id(1)
    q = q_ref[...]
    if lq == ATT_BLOCK:
        kc = kc_ref[...]
        vc = vc_ref[...]
    else:
        kpad_ref, vpad_ref = refs[1:3]
        kpad_ref[...] = jnp.zeros_like(kpad_ref)
        vpad_ref[...] = jnp.zeros_like(vpad_ref)
        kpad_ref[0:lq, :] = kc_ref[...]
        vpad_ref[0:lq, :] = vc_ref[...]
        kc = kpad_ref[...]
        vc = vpad_ref[...]
    kp = kp_ref[...]
    vp = vp_ref[...]
    lane_lo = lax.broadcasted_iota(jnp.int32, (ATT_BLOCK, LANE), 1) < HALF_LANE
    lane_lo_q = lax.broadcasted_iota(jnp.int32, (lq, LANE), 1) < HALF_LANE
    neg_inf = f32(-jnp.inf)
    scale = f32(ATT_HEAD_DIM ** -0.5)

    for g in range(ATT_KV_HEADS):
        sl = slice((g // 2) * LANE, (g // 2 + 1) * LANE)
        odd = g % 2 == 1

        def kpad(k):
            pair = k[:, sl]
            if odd:
                pair = pltpu.roll(pair, HALF_LANE, 1)
            return jnp.where(lane_lo, pair, 0.0).astype(bf16)

        def vdup(v):
            pair = v[:, sl]
            rolled = pltpu.roll(pair, HALF_LANE, 1)
            return (jnp.where(lane_lo, rolled, pair) if odd else jnp.where(lane_lo, pair, rolled)).astype(bf16)

        qp0 = q[:, (2 * g) * LANE:(2 * g + 1) * LANE]
        qp1 = q[:, (2 * g + 1) * LANE:(2 * g + 2) * LANE]
        qg = jnp.concatenate([qp0, pltpu.roll(qp0, HALF_LANE, 1), qp1, pltpu.roll(qp1, HALF_LANE, 1)],
                             axis=0).astype(bf16)
        bias = bias_ref[g]
        sp = lax.dot_general(qg, kpad(kp), NT_DIMS, preferred_element_type=f32) * scale + bias[:, :WINDOW]
        sc = lax.dot_general(qg, kpad(kc), NT_DIMS, preferred_element_type=f32) * scale + bias[:, WINDOW:]
        if masked_first:
            sp = jnp.where(n > 0, sp, neg_inf)
        sink = sink_ref[g][:, 0:1]
        m = jnp.maximum(jnp.maximum(jnp.max(sp, axis=-1, keepdims=True), jnp.max(sc, axis=-1, keepdims=True)), sink)
        pp = jnp.exp(sp - m)
        pc = jnp.exp(sc - m)
        denom = jnp.sum(pp, axis=-1, keepdims=True) + jnp.sum(pc, axis=-1, keepdims=True) + jnp.exp(sink - m)
        og = (jnp.dot(pp.astype(bf16), vdup(vp), preferred_element_type=f32)
              + jnp.dot(pc.astype(bf16), vdup(vc), preferred_element_type=f32)) / denom
        o_ref[:, (2 * g) * LANE:(2 * g + 1) * LANE] = jnp.where(lane_lo_q, og[0:lq], og[lq:2 * lq])
        o_ref[:, (2 * g + 1) * LANE:(2 * g + 2) * LANE] = jnp.where(lane_lo_q, og[2 * lq:3 * lq], og[3 * lq:4 * lq])


def _rel_bucket(dist):
    exact = REL_BUCKETS // 2
    d = jnp.maximum(dist, 0)
    large = exact + (jnp.log(jnp.maximum(d, 1).astype(f32) / exact)
                     / math.log(REL_MAX_DIST / exact) * (REL_BUCKETS - exact)).astype(jnp.int32)
    large = jnp.minimum(large, REL_BUCKETS - 1)
    return jnp.where(d < exact, d, large)


def _swa_tables(rel_bias, sinks, lq):
    qi = jnp.arange(lq)[:, None]
    kj = jnp.arange(2 * ATT_BLOCK)[None, :]
    dist = qi + WINDOW - kj
    band = (dist >= 0) & (dist <= WINDOW)
    onehot = (_rel_bucket(dist)[..., None] == jnp.arange(REL_BUCKETS)).astype(f32)
    bias = jnp.einsum('qkb,bh->qkh', onehot, rel_bias, precision=HIGHEST)
    bias = jnp.where(band[..., None], bias, -jnp.inf)
    bias = jnp.transpose(bias, (2, 0, 1)).reshape(ATT_KV_HEADS, ATT_GQA * lq, 2 * ATT_BLOCK)
    sink = jnp.broadcast_to(sinks.reshape(ATT_KV_HEADS, ATT_GQA, 1, 1), (ATT_KV_HEADS, ATT_GQA, lq, LANE))
    return bias, sink.reshape(ATT_KV_HEADS, ATT_GQA * lq, LANE)


def swa_mixer(q, kv, k_prev, v_prev, rel_bias, sinks, *, batch, seq, row0):
    lq = min(seq, ATT_BLOCK)
    nb = seq // lq
    blk0 = row0 // lq
    kvw = ATT_KV_HEADS * ATT_HEAD_DIM
    bias, sink = _swa_tables(rel_bias, sinks, lq)
    cur_k = pl.BlockSpec((lq, kvw), lambda b, n: (blk0 + b * nb + n, 0))
    cur_v = pl.BlockSpec((lq, kvw), lambda b, n: (blk0 + b * nb + n, 1))
    if k_prev is None:
        prev_k = pl.BlockSpec((WINDOW, kvw), lambda b, n: (blk0 + b * nb + jnp.maximum(n - 1, 0), 0))
        prev_v = pl.BlockSpec((WINDOW, kvw), lambda b, n: (blk0 + b * nb + jnp.maximum(n - 1, 0), 1))
        kp_arr, vp_arr = kv, kv
    else:
        prev_k = pl.BlockSpec((WINDOW, kvw), lambda b, n: (b, 0))
        prev_v = prev_k
        kp_arr = k_prev.reshape(batch * WINDOW, kvw)
        vp_arr = v_prev.reshape(batch * WINDOW, kvw)
    rows = pl.BlockSpec((lq, ATT_HEADS * ATT_HEAD_DIM), lambda b, n: (blk0 + b * nb + n, 0))
    in_specs = [rows, prev_k, prev_v, cur_k, cur_v,
                pl.BlockSpec(bias.shape, lambda b, n: (0, 0, 0)), pl.BlockSpec(sink.shape, lambda b, n: (0, 0, 0))]
    args = [q, kp_arr, vp_arr, kv, kv, bias, sink]
    scratch = [] if lq == ATT_BLOCK else [pltpu.VMEM((ATT_BLOCK, kvw), f32), pltpu.VMEM((ATT_BLOCK, kvw), f32)]
    return pl.pallas_call(
        functools.partial(_swa_kernel, lq=lq, masked_first=k_prev is None),
        grid=(batch, nb),
        in_specs=in_specs,
        out_specs=pl.BlockSpec((lq, ATT_HEADS * ATT_HEAD_DIM), lambda b, n: (b * nb + n, 0)),
        out_shape=jax.ShapeDtypeStruct((batch * seq, ATT_HEADS * ATT_HEAD_DIM), f32),
        scratch_shapes=scratch,
        compiler_params=_cparams("parallel", "arbitrary"),
        name="swa_mixer",
    )(*args)


def _out_proj_kernel(yap_ref, ybp_ref, yas_ref, ybs_ref, w_ref, x_ref, g_ref, o_ref, ht_ref, *, n_prompt):
    ka = yap_ref.shape[1]

    def body(ya_ref, yb_ref):
        acc = jnp.dot(ya_ref[...].astype(bf16), w_ref[0:ka, :], preferred_element_type=f32)
        acc = acc + jnp.dot(yb_ref[...].astype(bf16), w_ref[ka:, :], preferred_element_type=f32)
        xn = x_ref[...] + acc
        o_ref[...] = xn
        ht_ref[...] = _rms(xn, g_ref[...]).T.astype(bf16)

    @pl.when(pl.program_id(0) < n_prompt)
    def _():
        body(yap_ref, ybp_ref)

    @pl.when(pl.program_id(0) >= n_prompt)
    def _():
        body(yas_ref, ybs_ref)


def out_proj(ya_p, yb_p, ya_s, yb_s, w, x, g, tm=512):
    t, d = x.shape
    ka, kb = ya_p.shape[1], yb_p.shape[1]
    n_p = ya_p.shape[0] // tm
    n_s = ya_s.shape[0] // tm

    def p_rows(k):
        return pl.BlockSpec((tm, k), lambda i: (jnp.minimum(i, n_p - 1), 0))

    def s_rows(k):
        return pl.BlockSpec((tm, k), lambda i: (jnp.maximum(i - n_p, 0), 0))

    return pl.pallas_call(
        functools.partial(_out_proj_kernel, n_prompt=n_p),
        grid=(n_p + n_s,),
        in_specs=[p_rows(ka), p_rows(kb), s_rows(ka), s_rows(kb),
                  pl.BlockSpec((ka + kb, d), lambda i: (0, 0)), pl.BlockSpec((tm, d), lambda i: (i, 0)),
                  pl.BlockSpec((1, d), lambda i: (0, 0))],
        out_specs=[pl.BlockSpec((tm, d), lambda i: (i, 0)), pl.BlockSpec((d, tm), lambda i: (0, i))],
        out_shape=[jax.ShapeDtypeStruct((t, d), f32), jax.ShapeDtypeStruct((d, t), bf16)],
        compiler_params=_cparams("arbitrary"),
        name="out_proj",
    )(ya_p, yb_p, ya_s, yb_s, w, x, g.reshape(1, d))


def _gated_out_kernel(y_ref, g_ref, w_ref, x_ref, gn_ref, o_ref, ht_ref):
    a = (y_ref[...] * g_ref[...]).astype(bf16)
    xn = x_ref[...] + jnp.dot(a, w_ref[...], preferred_element_type=f32)
    o_ref[...] = xn
    ht_ref[...] = _rms(xn, gn_ref[...]).T.astype(bf16)


def gated_out_proj(y, g, w, x, gn, tm=512):
    t, d = x.shape
    rows = pl.BlockSpec((tm, d), lambda i: (i, 0))
    return pl.pallas_call(
        _gated_out_kernel,
        grid=(t // tm,),
        in_specs=[rows, rows, pl.BlockSpec((d, d), lambda i: (0, 0)), rows, pl.BlockSpec((1, d), lambda i: (0, 0))],
        out_specs=[rows, pl.BlockSpec((d, tm), lambda i: (0, i))],
        out_shape=[jax.ShapeDtypeStruct((t, d), f32), jax.ShapeDtypeStruct((d, t), bf16)],
        compiler_params=_cparams("parallel"),
        name="rwkv_out",
    )(y, g, w, x, gn.reshape(1, d))


RWKV_VGROUP = 8


def _rwkv_scan_kernel(r_ref, d_ref, k_ref, v_ref, kk_ref, a_ref, s0_ref, y_ref, sfin_ref, s_ref):
    tb = r_ref.shape[0]

    @pl.when(pl.program_id(1) == 0)
    def _():
        s_ref[...] = s0_ref[...]

    def step(t, carry):
        kk = kk_ref[t]
        d = d_ref[t]
        kv = k_ref[t]
        r = r_ref[t]
        b = kk * a_ref[t]

        def vgroup(g, c2):
            v0 = pl.multiple_of(g * RWKV_VGROUP, RWKV_VGROUP)
            vrows = v_ref[t, pl.ds(v0, RWKV_VGROUP), :]
            ys = []
            for vi in range(RWKV_VGROUP):
                sv = s_ref[v0 + vi]
                sa = -jnp.sum(sv * kk, axis=0, keepdims=True)
                sn = sv * d + sa * b + vrows[vi:vi + 1, :] * kv
                s_ref[v0 + vi] = sn
                ys.append(jnp.sum(sn * r, axis=0, keepdims=True))
            y_ref[t, pl.ds(v0, RWKV_VGROUP), :] = jnp.concatenate(ys, axis=0)
            return c2

        lax.fori_loop(0, RWKV_HEAD // RWKV_VGROUP, vgroup, 0)
        return carry

    lax.fori_loop(0, tb, step, 0)

    @pl.when(pl.program_id(1) == pl.num_programs(1) - 1)
    def _():
        sfin_ref[...] = s_ref[...]


def rwkv_scan(r, d, k, v, kk, a, s0):
    L, hd, c = r.shape
    tb = min(L, 16)
    seq_spec = pl.BlockSpec((tb, hd, LANE), lambda i, j: (j, 0, i))
    st_spec = pl.BlockSpec((hd, hd, LANE), lambda i, j: (0, 0, i))
    return pl.pallas_call(
        _rwkv_scan_kernel,
        grid=(c // LANE, L // tb),
        in_specs=[seq_spec] * 6 + [st_spec],
        out_specs=[seq_spec, st_spec],
        out_shape=[jax.ShapeDtypeStruct((L, hd, c), f32), jax.ShapeDtypeStruct((hd, hd, c), f32)],
        scratch_shapes=[pltpu.VMEM((hd, hd, LANE), f32)],
        compiler_params=_cparams("parallel", "arbitrary"),
        name="rwkv_scan",
    )(r, d, k, v, kk, a, s0)


def _rwkv_core(r, k, v, w, a_pre, wkv, k_k, k_a, r_k, ln_w, ln_b):
    b, L, _ = r.shape
    hk = (RWKV_HEADS, RWKV_HEAD)
    r = r.reshape(b, L, *hk)
    k = k.reshape(b, L, *hk)
    v = v.reshape(b, L, *hk)
    w = -jax.nn.softplus(-w) - 0.5
    decay = jnp.exp(-jnp.exp(w)).reshape(b, L, *hk)
    iclr = jax.nn.sigmoid(a_pre).reshape(b, L, *hk)
    kk = k * k_k.reshape(hk)
    kk = kk / jnp.maximum(jnp.sqrt(jnp.sum(kk * kk, axis=-1, keepdims=True)), 1e-12)
    k = k * (1.0 + (iclr - 1.0) * k_a.reshape(hk))

    def to_chain(t):
        return jnp.transpose(t, (1, 3, 0, 2)).reshape(L, RWKV_HEAD, b * RWKV_HEADS)

    s0 = jnp.transpose(wkv, (2, 3, 0, 1)).reshape(RWKV_HEAD, RWKV_HEAD, b * RWKV_HEADS)
    y, s_fin = rwkv_scan(*[to_chain(t) for t in (r, decay, k, v, kk, iclr)], s0)
    y = jnp.transpose(y.reshape(L, RWKV_HEAD, b, RWKV_HEADS), (2, 0, 3, 1))
    S_fin = jnp.transpose(s_fin.reshape(RWKV_HEAD, RWKV_HEAD, b, RWKV_HEADS), (2, 3, 0, 1))
    mean = jnp.mean(y, axis=-1, keepdims=True)
    var = jnp.mean(jnp.square(y - mean), axis=-1, keepdims=True)
    y = (y - mean) * lax.rsqrt(var + GN_EPS) * ln_w.reshape(hk) + ln_b.reshape(hk)
    y = y + jnp.sum(r * k * r_k, axis=-1, keepdims=True) * v
    return y.reshape(b, L, D_MODEL), S_fin


def kernel(x_prompt, x_sample, state_ssm, state_conv, cache_swa_k, cache_swa_v, state_wkv, state_shift, rel_bias, norm_mix, norm_ffn, norm_final, mix_w_in, ssd_conv_w, ssd_conv_b, ssd_dt_bias, ssd_a_log, ssd_d_skip, ssd_gnorm, attn_sinks, mix_w_out, rwkv_mu, rwkv_w0, rwkv_w1, rwkv_w2, rwkv_a0, rwkv_a1, rwkv_a2, rwkv_g1, rwkv_g2, rwkv_k_k, rwkv_k_a, rwkv_r_k, rwkv_w_rkv, rwkv_w_o, rwkv_ln_w, rwkv_ln_b, peer_w_q, peer_sub_keys, peer_u, peer_v):
    bp, lp, d = x_prompt.shape
    bs, ls, _ = x_sample.shape
    tp = bp * lp
    ts = bs * ls
    tt = tp + ts
    x = jnp.concatenate([x_prompt.reshape(tp, d), x_sample.reshape(ts, d)], axis=0)

    def split(t):
        return t[:tp].reshape(bp, lp, -1), t[tp:].reshape(bs, ls, -1)

    def merge(p, s):
        return jnp.concatenate([p.reshape(tp, -1), s.reshape(ts, -1)], axis=0)

    def zero_state(a):
        return jnp.zeros((bp,) + a.shape[2:], a.dtype)

    def peer(x, ht, layer, g_next):
        wqt = peer_w_q[layer].T.astype(bf16)
        return peer_ffn_residual(x, ht, wqt, peer_sub_keys[layer].astype(bf16),
                                 peer_u[layer].astype(bf16), peer_v[layer].astype(bf16), g_next)

    z, xbc, q, kv, dt = in_proj(x, norm_mix[0], mix_w_in[0])
    pad16 = (0, LANE - SSM_HEADS)
    selh = (jnp.arange(LANE)[:, None] == jnp.arange(SSM_INNER)[None, :] // SSM_HEAD_DIM).astype(f32)
    ssd_params = (ssd_conv_w[0], ssd_conv_b[0].reshape(1, CONV_DIM),
                  jnp.pad(ssd_dt_bias[0], pad16).reshape(1, LANE),
                  jnp.pad(-jnp.exp(ssd_a_log[0]), pad16).reshape(1, LANE),
                  jnp.repeat(ssd_d_skip[0], SSM_HEAD_DIM).reshape(1, SSM_INNER),
                  ssd_gnorm[0].reshape(1, SSM_INNER), selh)
    y_ssd_p, ssm_p = ssd_mixer(z, xbc, dt, zero_state(state_conv), zero_state(state_ssm), ssd_params,
                               batch=bp, seq=lp, row0=0)
    y_ssd_s, ssm_s = ssd_mixer(z, xbc, dt, state_conv[0], state_ssm[0], ssd_params, batch=bs, seq=ls, row0=tp)
    o_att_p = swa_mixer(q, kv, None, None, rel_bias, attn_sinks[0], batch=bp, seq=lp, row0=0)
    o_att_s = swa_mixer(q, kv, cache_swa_k[0], cache_swa_v[0], rel_bias, attn_sinks[0], batch=bs, seq=ls, row0=tp)
    x, ht = out_proj(y_ssd_p, o_att_p, y_ssd_s, o_att_s, mix_w_out[0].astype(bf16), x, norm_ffn[0])
    xbc_p, xbc_s = split(xbc)
    conv_p = xbc_p[:, -(CONV_W - 1):]
    conv_s = jnp.concatenate([state_conv[0], xbc_s], axis=1)[:, -(CONV_W - 1):]
    kv_p, kv_s = split(kv)
    kvw = ATT_KV_HEADS * ATT_HEAD_DIM
    hshape = (ATT_KV_HEADS, ATT_HEAD_DIM)
    k_p = kv_p[:, -WINDOW:, :kvw].reshape(bp, WINDOW, *hshape)
    v_p = kv_p[:, -WINDOW:, kvw:].reshape(bp, WINDOW, *hshape)
    k_s = jnp.concatenate([cache_swa_k[0], kv_s[..., :kvw].reshape(bs, ls, *hshape)], axis=1)[:, -WINDOW:]
    v_s = jnp.concatenate([cache_swa_v[0], kv_s[..., kvw:].reshape(bs, ls, *hshape)], axis=1)[:, -WINDOW:]
    x, h = peer(x, ht, 0, norm_mix[1])

    h_p, h_s = split(h)
    prev_p = jnp.concatenate([jnp.zeros((bp, 1, d), f32), h_p[:, :-1]], axis=1)
    prev_s = jnp.concatenate([state_shift[0][:, None], h_s[:, :-1]], axis=1)
    xx = merge(prev_p, prev_s) - h
    mu = rwkv_mu[0]
    xr, xw, xk, xv, xa, xg = [h + xx * mu[j] for j in range(6)]
    w_rkv = rwkv_w_rkv[0].astype(bf16)
    r = matmul(xr, w_rkv[0], name="rwkv_r")
    k = matmul(xk, w_rkv[1], name="rwkv_k")
    v = matmul(xv, w_rkv[2], name="rwkv_v")

    def lora(xin, w1, w2, act, name):
        rank = w1.shape[1]
        rpad = -(-rank // LANE) * LANE
        w1p = jnp.pad(w1, ((0, 0), (0, rpad - rank))).astype(bf16)
        w2p = jnp.pad(w2, ((0, rpad - rank), (0, 0))).astype(bf16)
        return matmul(act(matmul(xin, w1p, name=name + "1")), w2p, name=name + "2")

    w = rwkv_w0[0] + lora(xw, rwkv_w1[0], rwkv_w2[0], jnp.tanh, "rwkv_w")
    a_pre = rwkv_a0[0] + lora(xa, rwkv_a1[0], rwkv_a2[0], lambda t: t, "rwkv_a")
    g = lora(xg, rwkv_g1[0], rwkv_g2[0], jax.nn.sigmoid, "rwkv_g")
    cargs = (rwkv_k_k[0], rwkv_k_a[0], rwkv_r_k[0], rwkv_ln_w[0], rwkv_ln_b[0])
    parts_p = [split(t)[0] for t in (r, k, v, w, a_pre)]
    parts_s = [split(t)[1] for t in (r, k, v, w, a_pre)]
    y_p, wkv_p = _rwkv_core(*parts_p, zero_state(state_wkv), *cargs)
    y_s, wkv_s = _rwkv_core(*parts_s, state_wkv[0], *cargs)
    x, ht = gated_out_proj(merge(y_p, y_s), g, rwkv_w_o[0].astype(bf16), x, norm_ffn[1])
    shift_p, shift_s = h_p[:, -1], h_s[:, -1]
    _, y = peer(x, ht, 1, norm_final)
    y_p, y_s = split(y)
    return (y_p, y_s, ssm_p[None], conv_p[None], k_p[None], v_p[None], wkv_p[None], shift_p[None],
            ssm_s[None], conv_s[None], k_s[None], v_s[None], wkv_s[None], shift_s[None])
```

```python
import functools
import math

import jax
import jax.numpy as jnp
from jax import lax
from jax.experimental import pallas as pl
from jax.experimental.pallas import tpu as pltpu

f32 = jnp.float32
bf16 = jnp.bfloat16

D_MODEL = 1024
PAST_LEN = 16384
SSM_HEAD_DIM = 64
SSM_HEADS = 16
SSM_INNER = 1024
SSM_GROUPS = 2
SSM_STATE = 128
CONV_W = 4
CONV_DIM = 1536
SSD_CHUNK = 128
ATT_HEAD_DIM = 64
ATT_HEADS = 16
ATT_KV_HEADS = 4
ATT_GQA = 4
WINDOW = 128
ATT_BLOCK = 128
REL_BUCKETS = 32
REL_MAX_DIST = 128
RWKV_HEAD = 64
RWKV_HEADS = 16
PEER_HEADS = 8
N_KEYS = 128
N_EXPERTS = N_KEYS * N_KEYS
PEER_TOPK = 16
PEER_HALF = 128
EPS = 1e-5
GN_EPS = 64e-5

LANE = 128
SUBLANE = 8
HALF_LANE = LANE // 2
VMEM_LIMIT = 56 * 2 ** 20
HIGHEST = lax.Precision.HIGHEST
NT_DIMS = (((1,), (1,)), ((), ()))
TN_DIMS = (((0,), (0,)), ((), ()))

_CAND_COUNTS = [PEER_TOPK // (a + 1) for a in range(PEER_TOPK)]
_N_CAND = sum(_CAND_COUNTS)
_N_CAND_PAD = -(-_N_CAND // 8) * 8
_BIG = 1e9


def _cparams(*sem):
    return pltpu.CompilerParams(dimension_semantics=sem, vmem_limit_bytes=VMEM_LIMIT)


def _rms(x, g):
    return x * lax.rsqrt(jnp.mean(x * x, axis=-1, keepdims=True) + EPS) * g


def _silu(x):
    return x * jax.nn.sigmoid(x)


def _peer_select_kernel(ht_ref, wqt_ref, sk_ref, pos_ref, b0_ref, r1_ref, e0_ref, e1_ref,
                        qt_ref, s_ref, vals_ref, idx_ref, cs_ref):
    tm = ht_ref.shape[1]
    qt_ref[...] = jnp.dot(wqt_ref[...], ht_ref[...], preferred_element_type=f32).astype(bf16)
    row = lax.broadcasted_iota(jnp.int32, (N_KEYS, tm), 0).astype(f32)
    pos = pos_ref[...]
    neg_inf = f32(-jnp.inf)

    def head_compute(h, exact):
        bad = jnp.zeros((1, tm), f32)
        for c in (0, 1):
            off = pl.multiple_of((h * 2 + c) * PEER_HALF, PEER_HALF)
            s = jnp.dot(sk_ref[h, c], qt_ref[pl.ds(off, PEER_HALF), :], preferred_element_type=f32)
            s_ref[c] = s

            def extract(k, sr, c=c):
                s, rank = sr
                m = jnp.max(s, axis=0, keepdims=True)
                vals_ref[c, pl.ds(k, 1), :] = m
                if exact:
                    idx = jnp.min(jnp.where(s == m, row, f32(N_KEYS)), axis=0, keepdims=True)
                    idx_ref[c, pl.ds(k, 1), :] = idx
                    hit = row == idx
                else:
                    hit = s == m
                if c == 1:
                    rank = jnp.where(hit, jnp.asarray(k, f32), rank)
                return jnp.where(hit, neg_inf, s), rank

            rank0 = jnp.full((N_KEYS if c == 1 else SUBLANE, tm), f32(PEER_TOPK))
            s_fin, rank = lax.fori_loop(0, PEER_TOPK, extract, (s, rank0))
            if not exact:
                removed = jnp.sum(jnp.where(s_fin == neg_inf, f32(1.0), f32(0.0)), axis=0, keepdims=True)
                bad = bad + jnp.abs(removed - f32(PEER_TOPK))
            if c == 1:
                r1_ref[h] = rank.astype(bf16)

        v0 = vals_ref[0]
        v1 = vals_ref[1]
        r = 0
        for a in range(PEER_TOPK):
            nb = _CAND_COUNTS[a]
            cs_ref[r:r + nb, :] = v0[a:a + 1, :] + v1[0:nb, :]
            r += nb
        cs_ref[_N_CAND:_N_CAND_PAD, :] = jnp.full((_N_CAND_PAD - _N_CAND, tm), neg_inf)
        m1 = v0[0:1, :] + v1[0:1, :]

        def extract2(k, st):
            cs, sel, z = st
            m = jnp.max(cs, axis=0, keepdims=True)
            if exact:
                p = jnp.min(jnp.where(cs == m, pos, f32(_BIG)), axis=0, keepdims=True)
                hit = pos == p
            else:
                hit = cs == m
            return jnp.where(hit, neg_inf, cs), jnp.where(hit, f32(1.0), sel), z + jnp.exp(m - m1)

        _, sel, z = lax.fori_loop(0, PEER_TOPK, extract2,
                                  (cs_ref[...], jnp.zeros((_N_CAND_PAD, tm), f32), jnp.zeros((1, tm), f32)))
        if not exact:
            bad = bad + jnp.abs(jnp.sum(sel, axis=0, keepdims=True) - f32(PEER_TOPK))
        s0 = s_ref[0]
        idx0 = idx_ref[0]
        bound0 = jnp.zeros((N_KEYS, tm), f32)
        r = 0
        for a in range(PEER_TOPK):
            nb = _CAND_COUNTS[a]
            cnt = jnp.sum(sel[r:r + nb, :], axis=0, keepdims=True)
            is_a = (row == idx0[a:a + 1, :]) if exact else (s0 == v0[a:a + 1, :])
            bound0 = jnp.where(is_a, cnt, bound0)
            r += nb
        b0_ref[h] = bound0
        e0_ref[h] = jnp.exp(s0 - v0[0:1, :]) / z
        e1_ref[h] = jnp.exp(s_ref[1] - v1[0:1, :]).astype(bf16)
        return bad

    def head_body(h, carry):
        bad = head_compute(h, False)

        @pl.when(jnp.max(bad) > 0.0)
        def _():
            head_compute(h, True)

        return carry

    lax.fori_loop(0, PEER_HEADS, head_body, 0)


def _peer_main_kernel(ht_ref, u_ref, v_ref, b0_ref, r1_ref, e0_ref, e1_ref, x_ref, gn_ref, o_ref, hn_ref,
                      acc_ref, *, nblk, npieces):
    j = pl.program_id(1)

    @pl.when(j == 0)
    def _():
        acc_ref[...] = jnp.zeros_like(acc_ref)

    ht = ht_ref[...]
    bpp = nblk // npieces
    pc = bpp * N_KEYS
    for p in range(npieces):
        at = jnp.dot(u_ref[p * pc:(p + 1) * pc, :], ht, preferred_element_type=f32)
        ws = []
        for b in range(bpp):
            i = j * nblk + p * bpp + b
            a = at[b * N_KEYS:(b + 1) * N_KEYS, :]
            g = jnp.zeros(a.shape, bf16)
            for h in range(PEER_HEADS):
                bnd = b0_ref[h, pl.ds(i, 1), :].astype(bf16)
                e0 = e0_ref[h, pl.ds(i, 1), :].astype(bf16)
                g = g + jnp.where(r1_ref[h] < bnd, e1_ref[h] * e0, bf16(0.0))
            gelu = 0.5 * a * (1.0 + lax.erf(a * f32(math.sqrt(0.5))))
            ws.append(g * gelu.astype(bf16))
        wt = jnp.concatenate(ws, axis=0) if bpp > 1 else ws[0]
        acc_ref[...] += lax.dot_general(wt, v_ref[p * pc:(p + 1) * pc, :], TN_DIMS,
                                        preferred_element_type=f32)

    @pl.when(j == pl.num_programs(1) - 1)
    def _():
        xn = x_ref[...] + acc_ref[...]
        o_ref[...] = xn
        hn_ref[...] = _rms(xn, gn_ref[...])


def peer_ffn_residual(x, ht, wqt, sk, u, v, g_next, tm_sel=256, tm=256, ce=2048, npieces=8):
    t = x.shape[0]
    d = D_MODEL
    pos_list = []
    for a in range(PEER_TOPK):
        pos_list += [a * PEER_TOPK + b for b in range(_CAND_COUNTS[a])]
    pos_list += [_BIG] * (_N_CAND_PAD - _N_CAND)
    pos = jnp.broadcast_to(jnp.asarray(pos_list, f32)[:, None], (_N_CAND_PAD, tm_sel))
    sel_shape = jax.ShapeDtypeStruct((PEER_HEADS, N_KEYS, t), f32)
    sel_shape16 = jax.ShapeDtypeStruct((PEER_HEADS, N_KEYS, t), bf16)
    sel_spec = pl.BlockSpec((PEER_HEADS, N_KEYS, tm_sel), lambda i: (0, 0, i))
    b0, r1, e0, e1 = pl.pallas_call(
        _peer_select_kernel,
        grid=(t // tm_sel,),
        in_specs=[pl.BlockSpec((d, tm_sel), lambda i: (0, i)),
                  pl.BlockSpec((PEER_HEADS * 2 * PEER_HALF, d), lambda i: (0, 0)),
                  pl.BlockSpec((PEER_HEADS, 2, N_KEYS, PEER_HALF), lambda i: (0, 0, 0, 0)),
                  pl.BlockSpec((_N_CAND_PAD, tm_sel), lambda i: (0, 0))],
        out_specs=[sel_spec] * 4,
        out_shape=[sel_shape, sel_shape16, sel_shape, sel_shape16],
        scratch_shapes=[pltpu.VMEM((PEER_HEADS * 2 * PEER_HALF, tm_sel), bf16),
                        pltpu.VMEM((2, N_KEYS, tm_sel), f32),
                        pltpu.VMEM((2, PEER_TOPK, tm_sel), f32),
                        pltpu.VMEM((2, PEER_TOPK, tm_sel), f32),
                        pltpu.VMEM((_N_CAND_PAD, tm_sel), f32)],
        compiler_params=_cparams("parallel"),
        name="peer_select",
    )(ht, wqt, sk, pos)

    nblk = ce // N_KEYS
    sel_spec2 = pl.BlockSpec((PEER_HEADS, N_KEYS, tm), lambda i, j: (0, 0, i))
    return pl.pallas_call(
        functools.partial(_peer_main_kernel, nblk=nblk, npieces=npieces),
        grid=(t // tm, N_EXPERTS // ce),
        in_specs=[pl.BlockSpec((d, tm), lambda i, j: (0, i)),
                  pl.BlockSpec((ce, d), lambda i, j: (j, 0)),
                  pl.BlockSpec((ce, d), lambda i, j: (j, 0)),
                  sel_spec2, sel_spec2, sel_spec2, sel_spec2,
                  pl.BlockSpec((tm, d), lambda i, j: (i, 0)), pl.BlockSpec((1, d), lambda i, j: (0, 0))],
        out_specs=[pl.BlockSpec((tm, d), lambda i, j: (i, 0))] * 2,
        out_shape=[jax.ShapeDtypeStruct((t, d), f32)] * 2,
        scratch_shapes=[pltpu.VMEM((tm, d), f32)],
        compiler_params=_cparams("parallel", "arbitrary"),
        name="peer_main",
    )(ht, u, v, b0, r1, e0, e1, x, g_next.reshape(1, d))


IN_WIDTHS = (SSM_INNER, CONV_DIM, ATT_HEADS * ATT_HEAD_DIM, 2 * ATT_KV_HEADS * ATT_HEAD_DIM, LANE)


def _in_proj_kernel(x_ref, g_ref, w_ref, *out_refs):
    h = _rms(x_ref[...], g_ref[...])
    y = jnp.dot(h.astype(bf16), w_ref[...], preferred_element_type=f32)
    o = 0
    for ref in out_refs:
        w = ref.shape[1]
        ref[...] = y[:, o:o + w]
        o += w


def in_proj(x, g, w_in, tm=256):
    t, d = x.shape
    o1 = SSM_INNER
    o2 = o1 + CONV_DIM
    o3 = o2 + SSM_HEADS
    w = jnp.concatenate([w_in[:, :o2], w_in[:, o3:], w_in[:, o2:o3],
                         jnp.zeros((d, LANE - SSM_HEADS), w_in.dtype)], axis=1).astype(bf16)
    n = w.shape[1]
    return pl.pallas_call(
        _in_proj_kernel,
        grid=(t // tm,),
        in_specs=[pl.BlockSpec((tm, d), lambda i: (i, 0)), pl.BlockSpec((1, d), lambda i: (0, 0)),
                  pl.BlockSpec((d, n), lambda i: (0, 0))],
        out_specs=[pl.BlockSpec((tm, wd), lambda i: (i, 0)) for wd in IN_WIDTHS],
        out_shape=[jax.ShapeDtypeStruct((t, wd), f32) for wd in IN_WIDTHS],
        compiler_params=_cparams("parallel"),
        name="in_proj",
    )(x, g.reshape(1, d), w)


N_PAIRS = SSM_HEADS // 2


def _ssd_kernel(*refs, lin):
    (z_ref, xbc_ref, dt_ref, cs_ref, h0_ref, cw_ref, cb_ref, dtb_ref, aneg_ref, dsk_ref, gn_ref, selh_ref) = refs[:12]
    refs = refs[12:]
    y_ref, hfin_ref, xe_ref, hp_ref, ys_ref = refs[:5]
    lc = SSD_CHUNK
    c = pl.program_id(1)

    @pl.when(c == 0)
    def _():
        xe_ref[0:SUBLANE, :] = cs_ref[0]
        hp_ref[...] = h0_ref[0]

    if lin == lc:
        xe_ref[SUBLANE:SUBLANE + lc, :] = xbc_ref[...]
        z = z_ref[...]
        dt_raw = dt_ref[...]
    else:
        zpad_ref, dtpad_ref = refs[5:7]
        xe_ref[SUBLANE:SUBLANE + lc, :] = jnp.zeros((lc, CONV_DIM), f32)
        xe_ref[SUBLANE:SUBLANE + lin, :] = xbc_ref[...]
        zpad_ref[...] = jnp.zeros_like(zpad_ref)
        zpad_ref[0:lin, :] = z_ref[...]
        dtpad_ref[...] = jnp.zeros_like(dtpad_ref)
        dtpad_ref[0:lin, :] = dt_ref[...]
        z = zpad_ref[...]
        dt_raw = dtpad_ref[...]

    conv = cb_ref[...]
    for j in range(CONV_W):
        o = SUBLANE - (CONV_W - 1) + j
        conv = conv + xe_ref[o:o + lc, :] * cw_ref[j:j + 1, :]
    if lin == lc:
        xe_ref[0:SUBLANE, :] = xe_ref[lc:lc + SUBLANE, :]
    xc = _silu(conv)
    xs = xc[:, :SSM_INNER]
    bm = xc[:, SSM_INNER:SSM_INNER + SSM_GROUPS * SSM_STATE]
    cm = xc[:, SSM_INNER + SSM_GROUPS * SSM_STATE:]

    row = lax.broadcasted_iota(jnp.int32, (lc, LANE), 0)
    col = lax.broadcasted_iota(jnp.int32, (lc, LANE), 1)
    causal = row >= col
    lane_lo = col < HALF_LANE
    neg_inf = f32(-jnp.inf)

    x = dt_raw + dtb_ref[...]
    dt = jnp.maximum(x, 0.0) + jnp.log(1.0 + jnp.exp(-jnp.abs(x)))
    if lin != lc:
        dt = jnp.where(row < lin, dt, 0.0)
    la = dt * aneg_ref[...]
    acs = jnp.dot(causal.astype(f32), la, precision=HIGHEST, preferred_element_type=f32)
    acs_t = acs.T
    selh = selh_ref[...]
    dt_exp = jnp.dot(dt, selh, precision=HIGHEST, preferred_element_type=f32)
    acs_exp = jnp.dot(acs, selh, precision=HIGHEST, preferred_element_type=f32)
    alast = acs[lc - 1:lc, :]
    xdt = xs * dt_exp
    eacs = jnp.exp(acs_exp)
    xdt_end = (xdt * jnp.exp(acs_exp[lc - 1:lc, :] - acs_exp)).astype(bf16)
    dsk = dsk_ref[...]

    for g in range(SSM_GROUPS):
        cmg = cm[:, g * SSM_STATE:(g + 1) * SSM_STATE].astype(bf16)
        bmg = bm[:, g * SSM_STATE:(g + 1) * SSM_STATE].astype(bf16)
        cb = lax.dot_general(cmg, bmg, NT_DIMS, preferred_element_type=f32)
        for jj in range(N_PAIRS // SSM_GROUPS):
            j = g * (N_PAIRS // SSM_GROUPS) + jj
            sl = slice(j * LANE, (j + 1) * LANE)
            xdt_pair = xdt[:, sl]
            ydiag = jnp.zeros((lc, LANE), f32)
            for half in (0, 1):
                h = 2 * j + half
                seg = acs[:, h:h + 1] - acs_t[h:h + 1, :]
                dec = jnp.exp(jnp.where(causal, seg, neg_inf))
                m = (cb * dec).astype(bf16)
                keep = lane_lo if half == 0 else jnp.logical_not(lane_lo)
                xd = jnp.where(keep, xdt_pair, 0.0).astype(bf16)
                ydiag = ydiag + jnp.dot(m, xd, preferred_element_type=f32)
            hpj = hp_ref[j]
            yoff = lax.dot_general(cmg, hpj.astype(bf16), NT_DIMS, preferred_element_type=f32) * eacs[:, sl]
            s_new = lax.dot_general(xdt_end[:, sl], bmg, TN_DIMS, preferred_element_type=f32)
            dl = jnp.where(row < SSM_HEAD_DIM, alast[:, 2 * j:2 * j + 1], alast[:, 2 * j + 1:2 * j + 2])
            hp_ref[j] = hpj * jnp.exp(dl) + s_new
            ys_ref[:, sl] = ydiag + yoff + dsk[:, sl] * xs[:, sl]

    y = ys_ref[...] * _silu(z)
    gs = SSM_INNER // SSM_GROUPS
    gn = gn_ref[...]
    for g in range(SSM_GROUPS):
        yg = y[:, g * gs:(g + 1) * gs]
        yg = yg * lax.rsqrt(jnp.mean(yg * yg, axis=-1, keepdims=True) + EPS) * gn[:, g * gs:(g + 1) * gs]
        y_ref[:, g * gs:(g + 1) * gs] = yg[0:lin, :]

    @pl.when(c == pl.num_programs(1) - 1)
    def _():
        hfin_ref[0] = hp_ref[...]


def ssd_mixer(z, xbc, dt, conv_state, ssm_state, params, *, batch, seq, row0):
    conv_w, conv_b, dt_bias, a_neg, d_skip, gnorm, selh = params
    lc = SSD_CHUNK
    lin = min(seq, lc)
    nc = seq // lin
    blk0 = row0 // lin
    cs = jnp.pad(conv_state, ((0, 0), (SUBLANE - (CONV_W - 1), 0), (0, 0)))
    h0 = ssm_state.reshape(batch, N_PAIRS, LANE, SSM_STATE)

    def rows(w):
        return pl.BlockSpec((lin, w), lambda b, c: (blk0 + b * nc + c, 0))

    def const(shape):
        return pl.BlockSpec(shape, lambda b, c: (0,) * len(shape))

    in_specs = [rows(SSM_INNER), rows(CONV_DIM), rows(LANE),
                pl.BlockSpec((1, SUBLANE, CONV_DIM), lambda b, c: (b, 0, 0)),
                pl.BlockSpec((1, N_PAIRS, LANE, SSM_STATE), lambda b, c: (b, 0, 0, 0)),
                const((CONV_W, CONV_DIM)), const((1, CONV_DIM)), const((1, LANE)), const((1, LANE)),
                const((1, SSM_INNER)), const((1, SSM_INNER)), const((LANE, SSM_INNER))]
    args = [z, xbc, dt, cs, h0, conv_w, conv_b, dt_bias, a_neg, d_skip, gnorm, selh]
    scratch = [pltpu.VMEM((lc + 2 * SUBLANE, CONV_DIM), f32), pltpu.VMEM((N_PAIRS, LANE, SSM_STATE), f32),
               pltpu.VMEM((lc, SSM_INNER), f32)]
    if lin != lc:
        scratch += [pltpu.VMEM((lc, SSM_INNER), f32), pltpu.VMEM((lc, LANE), f32)]
    y, h_fin = pl.pallas_call(
        functools.partial(_ssd_kernel, lin=lin),
        grid=(batch, nc),
        in_specs=in_specs,
        out_specs=[pl.BlockSpec((lin, SSM_INNER), lambda b, c: (b * nc + c, 0)),
                   pl.BlockSpec((1, N_PAIRS, LANE, SSM_STATE), lambda b, c: (b, 0, 0, 0))],
        out_shape=[jax.ShapeDtypeStruct((batch * seq, SSM_INNER), f32),
                   jax.ShapeDtypeStruct((batch, N_PAIRS, LANE, SSM_STATE), f32)],
        scratch_shapes=scratch,
        compiler_params=_cparams("parallel", "arbitrary"),
        name="ssd_mixer",
    )(*args)
    return y, h_fin.reshape(batch, SSM_HEADS, SSM_HEAD_DIM, SSM_STATE)


def _swa_kernel(*refs, lq, masked_first):
    q_ref, kp_ref, vp_ref, kc_ref, vc_ref, bias_ref, sink_ref = refs[:7]
    refs = refs[7:]
    o_ref = refs[0]
    n = pl.program_id(1)
    q = q_ref[...]
    if lq == ATT_BLOCK:
        kc = kc_ref[...]
        vc = vc_ref[...]
    else:
        kpad_ref, vpad_ref = refs[1:3]
        kpad_ref[...] = jnp.zeros_like(kpad_ref)
        vpad_ref[...] = jnp.zeros_like(vpad_ref)
        kpad_ref[0:lq, :] = kc_ref[...]
        vpad_ref[0:lq, :] = vc_ref[...]
        kc = kpad_ref[...]
        vc = vpad_ref[...]
    kp = kp_ref[...]
    vp = vp_ref[...]
    lane_lo = lax.broadcasted_iota(jnp.int32, (ATT_BLOCK, LANE), 1) < HALF_LANE
    lane_lo_q = lax.broadcasted_iota(jnp.int32, (lq, LANE), 1) < HALF_LANE
    neg_inf = f32(-jnp.inf)
    scale = f32(ATT_HEAD_DIM ** -0.5)

    for g in range(ATT_KV_HEADS):
        sl = slice((g // 2) * LANE, (g // 2 + 1) * LANE)
        odd = g % 2 == 1

        def kpad(k):
            pair = k[:, sl]
            if odd:
                pair = pltpu.roll(pair, HALF_LANE, 1)
            return jnp.where(lane_lo, pair, 0.0).astype(bf16)

        def vdup(v):
            pair = v[:, sl]
            rolled = pltpu.roll(pair, HALF_LANE, 1)
            return (jnp.where(lane_lo, rolled, pair) if odd else jnp.where(lane_lo, pair, rolled)).astype(bf16)

        qp0 = q[:, (2 * g) * LANE:(2 * g + 1) * LANE]
        qp1 = q[:, (2 * g + 1) * LANE:(2 * g + 2) * LANE]
        qg = jnp.concatenate([qp0, pltpu.roll(qp0, HALF_LANE, 1), qp1, pltpu.roll(qp1, HALF_LANE, 1)],
                             axis=0).astype(bf16)
        bias = bias_ref[g]
        sp = lax.dot_general(qg, kpad(kp), NT_DIMS, preferred_element_type=f32) * scale + bias[:, :WINDOW]
        sc = lax.dot_general(qg, kpad(kc), NT_DIMS, preferred_element_type=f32) * scale + bias[:, WINDOW:]
        if masked_first:
            sp = jnp.where(n > 0, sp, neg_inf)
        sink = sink_ref[g][:, 0:1]
        m = jnp.maximum(jnp.maximum(jnp.max(sp, axis=-1, keepdims=True), jnp.max(sc, axis=-1, keepdims=True)), sink)
        pp = jnp.exp(sp - m)
        pc = jnp.exp(sc - m)
        denom = jnp.sum(pp, axis=-1, keepdims=True) + jnp.sum(pc, axis=-1, keepdims=True) + jnp.exp(sink - m)
        og = (jnp.dot(pp.astype(bf16), vdup(vp), preferred_element_type=f32)
              + jnp.dot(pc.astype(bf16), vdup(vc), preferred_element_type=f32)) / denom
        o_ref[:, (2 * g) * LANE:(2 * g + 1) * LANE] = jnp.where(lane_lo_q, og[0:lq], og[lq:2 * lq])
        o_ref[:, (2 * g + 1) * LANE:(2 * g + 2) * LANE] = jnp.where(lane_lo_q, og[2 * lq:3 * lq], og[3 * lq:4 * lq])


def _rel_bucket(dist):
    exact = REL_BUCKETS // 2
    d = jnp.maximum(dist, 0)
    large = exact + (jnp.log(jnp.maximum(d, 1).astype(f32) / exact)
                     / math.log(REL_MAX_DIST / exact) * (REL_BUCKETS - exact)).astype(jnp.int32)
    large = jnp.minimum(large, REL_BUCKETS - 1)
    return jnp.where(d < exact, d, large)


def _swa_tables(rel_bias, sinks, lq):
    qi = jnp.arange(lq)[:, None]
    kj = jnp.arange(2 * ATT_BLOCK)[None, :]
    dist = qi + WINDOW - kj
    band = (dist >= 0) & (dist <= WINDOW)
    onehot = (_rel_bucket(dist)[..., None] == jnp.arange(REL_BUCKETS)).astype(f32)
    bias = jnp.einsum('qkb,bh->qkh', onehot, rel_bias, precision=HIGHEST)
    bias = jnp.where(band[..., None], bias, -jnp.inf)
    bias = jnp.transpose(bias, (2, 0, 1)).reshape(ATT_KV_HEADS, ATT_GQA * lq, 2 * ATT_BLOCK)
    sink = jnp.broadcast_to(sinks.reshape(ATT_KV_HEADS, ATT_GQA, 1, 1), (ATT_KV_HEADS, ATT_GQA, lq, LANE))
    return bias, sink.reshape(ATT_KV_HEADS, ATT_GQA * lq, LANE)


def swa_mixer(q, kv, k_prev, v_prev, rel_bias, sinks, *, batch, seq, row0):
    lq = min(seq, ATT_BLOCK)
    nb = seq // lq
    blk0 = row0 // lq
    kvw = ATT_KV_HEADS * ATT_HEAD_DIM
    bias, sink = _swa_tables(rel_bias, sinks, lq)
    cur_k = pl.BlockSpec((lq, kvw), lambda b, n: (blk0 + b * nb + n, 0))
    cur_v = pl.BlockSpec((lq, kvw), lambda b, n: (blk0 + b * nb + n, 1))
    if k_prev is None:
        prev_k = pl.BlockSpec((WINDOW, kvw), lambda b, n: (blk0 + b * nb + jnp.maximum(n - 1, 0), 0))
        prev_v = pl.BlockSpec((WINDOW, kvw), lambda b, n: (blk0 + b * nb + jnp.maximum(n - 1, 0), 1))
        kp_arr, vp_arr = kv, kv
    else:
        prev_k = pl.BlockSpec((WINDOW, kvw), lambda b, n: (b, 0))
        prev_v = prev_k
        kp_arr = k_prev.reshape(batch * WINDOW, kvw)
        vp_arr = v_prev.reshape(batch * WINDOW, kvw)
    rows = pl.BlockSpec((lq, ATT_HEADS * ATT_HEAD_DIM), lambda b, n: (blk0 + b * nb + n, 0))
    in_specs = [rows, prev_k, prev_v, cur_k, cur_v,
                pl.BlockSpec(bias.shape, lambda b, n: (0, 0, 0)), pl.BlockSpec(sink.shape, lambda b, n: (0, 0, 0))]
    args = [q, kp_arr, vp_arr, kv, kv, bias, sink]
    scratch = [] if lq == ATT_BLOCK else [pltpu.VMEM((ATT_BLOCK, kvw), f32), pltpu.VMEM((ATT_BLOCK, kvw), f32)]
    return pl.pallas_call(
        functools.partial(_swa_kernel, lq=lq, masked_first=k_prev is None),
        grid=(batch, nb),
        in_specs=in_specs,
        out_specs=pl.BlockSpec((lq, ATT_HEADS * ATT_HEAD_DIM), lambda b, n: (b * nb + n, 0)),
        out_shape=jax.ShapeDtypeStruct((batch * seq, ATT_HEADS * ATT_HEAD_DIM), f32),
        scratch_shapes=scratch,
        compiler_params=_cparams("parallel", "arbitrary"),
        name="swa_mixer",
    )(*args)


def _out_proj_kernel(yap_ref, ybp_ref, yas_ref, ybs_ref, w_ref, x_ref, g_ref, o_ref, ht_ref, *, n_prompt):
    ka = yap_ref.shape[1]

    def body(ya_ref, yb_ref):
        acc = jnp.dot(ya_ref[...].astype(bf16), w_ref[0:ka, :], preferred_element_type=f32)
        acc = acc + jnp.dot(yb_ref[...].astype(bf16), w_ref[ka:, :], preferred_element_type=f32)
        xn = x_ref[...] + acc
        o_ref[...] = xn
        ht_ref[...] = _rms(xn, g_ref[...]).T.astype(bf16)

    @pl.when(pl.program_id(0) < n_prompt)
    def _():
        body(yap_ref, ybp_ref)

    @pl.when(pl.program_id(0) >= n_prompt)
    def _():
        body(yas_ref, ybs_ref)


def out_proj(ya_p, yb_p, ya_s, yb_s, w, x, g, tm=512):
    t, d = x.shape
    ka, kb = ya_p.shape[1], yb_p.shape[1]
    n_p = ya_p.shape[0] // tm
    n_s = ya_s.shape[0] // tm

    def p_rows(k):
        return pl.BlockSpec((tm, k), lambda i: (jnp.minimum(i, n_p - 1), 0))

    def s_rows(k):
        return pl.BlockSpec((tm, k), lambda i: (jnp.maximum(i - n_p, 0), 0))

    return pl.pallas_call(
        functools.partial(_out_proj_kernel, n_prompt=n_p),
        grid=(n_p + n_s,),
        in_specs=[p_rows(ka), p_rows(kb), s_rows(ka), s_rows(kb),
                  pl.BlockSpec((ka + kb, d), lambda i: (0, 0)), pl.BlockSpec((tm, d), lambda i: (i, 0)),
                  pl.BlockSpec((1, d), lambda i: (0, 0))],
        out_specs=[pl.BlockSpec((tm, d), lambda i: (i, 0)), pl.BlockSpec((d, tm), lambda i: (0, i))],
        out_shape=[jax.ShapeDtypeStruct((t, d), f32), jax.ShapeDtypeStruct((d, t), bf16)],
        compiler_params=_cparams("arbitrary"),
        name="out_proj",
    )(ya_p, yb_p, ya_s, yb_s, w, x, g.reshape(1, d))


def _gated_out_kernel(yp_ref, ys_ref, g_ref, w_ref, x_ref, gn_ref, o_ref, ht_ref, *, n_prompt):
    def body(y_ref):
        a = (y_ref[...] * g_ref[...]).astype(bf16)
        xn = x_ref[...] + jnp.dot(a, w_ref[...], preferred_element_type=f32)
        o_ref[...] = xn
        ht_ref[...] = _rms(xn, gn_ref[...]).T.astype(bf16)

    @pl.when(pl.program_id(0) < n_prompt)
    def _():
        body(yp_ref)

    @pl.when(pl.program_id(0) >= n_prompt)
    def _():
        body(ys_ref)


def gated_out_proj(y_p, y_s, g, w, x, gn, tm=512):
    t, d = x.shape
    n_p = y_p.shape[0] // tm
    n_s = y_s.shape[0] // tm
    rows = pl.BlockSpec((tm, d), lambda i: (i, 0))
    return pl.pallas_call(
        functools.partial(_gated_out_kernel, n_prompt=n_p),
        grid=(n_p + n_s,),
        in_specs=[pl.BlockSpec((tm, d), lambda i: (jnp.minimum(i, n_p - 1), 0)),
                  pl.BlockSpec((tm, d), lambda i: (jnp.maximum(i - n_p, 0), 0)),
                  rows, pl.BlockSpec((d, d), lambda i: (0, 0)), rows, pl.BlockSpec((1, d), lambda i: (0, 0))],
        out_specs=[rows, pl.BlockSpec((d, tm), lambda i: (0, i))],
        out_shape=[jax.ShapeDtypeStruct((t, d), f32), jax.ShapeDtypeStruct((d, t), bf16)],
        compiler_params=_cparams("arbitrary"),
        name="rwkv_out",
    )(y_p, y_s, g, w, x, gn.reshape(1, d))


LORA_PAD = 2 * LANE


def _softplus(x):
    return jnp.maximum(x, 0.0) + jnp.log(1.0 + jnp.exp(-jnp.abs(x)))


def _rwkv_pre_kernel(h_ref, p_ref, mu_ref, wrkv_ref, l1_ref, l2_ref, vec_ref,
                     r_ref, d_ref, k_ref, v_ref, kk_ref, a_ref, g_ref):
    h = h_ref[...]
    xx = p_ref[...] - h

    def mix(j):
        return (h + xx * mu_ref[j:j + 1, :]).astype(bf16)

    def mm(a, w):
        return jnp.dot(a, w, preferred_element_type=f32)

    r_ref[...] = mm(mix(0), wrkv_ref[0])
    k = mm(mix(2), wrkv_ref[1])
    v_ref[...] = mm(mix(3), wrkv_ref[2])
    wl = vec_ref[0:1, :] + mm(jnp.tanh(mm(mix(1), l1_ref[0])).astype(bf16), l2_ref[0])
    al = vec_ref[1:2, :] + mm(mm(mix(4), l1_ref[1]).astype(bf16), l2_ref[1])
    g_ref[...] = mm(jax.nn.sigmoid(mm(mix(5), l1_ref[2])).astype(bf16), l2_ref[2])
    w = -_softplus(-wl) - 0.5
    d_ref[...] = jnp.exp(-jnp.exp(w))
    a = jax.nn.sigmoid(al)
    a_ref[...] = a
    kk_ref[...] = k * vec_ref[2:3, :]
    k_ref[...] = k * (1.0 + (a - 1.0) * vec_ref[3:4, :])


def rwkv_pre(h, prev, mu, w_rkv, lora1, lora2, vecs, tm=256):
    t, d = h.shape
    rows = pl.BlockSpec((tm, d), lambda i: (i, 0))

    def const(a):
        return pl.BlockSpec(a.shape, lambda i: (0,) * a.ndim)

    return pl.pallas_call(
        _rwkv_pre_kernel,
        grid=(t // tm,),
        in_specs=[rows, rows, const(mu), const(w_rkv), const(lora1), const(lora2), const(vecs)],
        out_specs=[rows] * 7,
        out_shape=[jax.ShapeDtypeStruct((t, d), f32)] * 7,
        compiler_params=_cparams("parallel"),
        name="rwkv_pre",
    )(h, prev, mu, w_rkv, lora1, lora2, vecs)


RWKV_VGROUP = 8


def _rwkv_scan_kernel(r_ref, d_ref, k_ref, v_ref, kk_ref, a_ref, s0_ref, tab_ref, y_ref, sfin_ref, s_ref):
    tb = r_ref.shape[0]

    @pl.when(pl.program_id(1) == 0)
    def _():
        s_ref[...] = s0_ref[...]

    def step(t, carry):
        kkr = kk_ref[t]
        nrm = jnp.sqrt(jnp.sum(kkr * kkr, axis=0, keepdims=True))
        kk = kkr / jnp.maximum(nrm, 1e-12)
        d = d_ref[t]
        kv = k_ref[t]
        r = r_ref[t]
        b = kk * a_ref[t]

        def vgroup(g, c2):
            v0 = pl.multiple_of(g * RWKV_VGROUP, RWKV_VGROUP)
            vrows = v_ref[t, pl.ds(v0, RWKV_VGROUP), :]
            ys = []
            for vi in range(RWKV_VGROUP):
                sv = s_ref[v0 + vi]
                sa = -jnp.sum(sv * kk, axis=0, keepdims=True)
                sn = sv * d + sa * b + vrows[vi:vi + 1, :] * kv
                s_ref[v0 + vi] = sn
                ys.append(jnp.sum(sn * r, axis=0, keepdims=True))
            y_ref[t, pl.ds(v0, RWKV_VGROUP), :] = jnp.concatenate(ys, axis=0)
            return c2

        lax.fori_loop(0, RWKV_HEAD // RWKV_VGROUP, vgroup, 0)
        y = y_ref[t]
        mean = jnp.mean(y, axis=0, keepdims=True)
        yc = y - mean
        var = jnp.mean(yc * yc, axis=0, keepdims=True)
        bonus = jnp.sum(r * kv * tab_ref[0], axis=0, keepdims=True)
        y_ref[t] = yc * lax.rsqrt(var + GN_EPS) * tab_ref[1] + tab_ref[2] + bonus * v_ref[t]
        return carry

    lax.fori_loop(0, tb, step, 0)

    @pl.when(pl.program_id(1) == pl.num_programs(1) - 1)
    def _():
        sfin_ref[...] = s_ref[...]


def rwkv_scan(r, d, k, v, kk, a, s0, tab):
    L, hd, c = r.shape
    tb = min(L, 16)
    seq_spec = pl.BlockSpec((tb, hd, LANE), lambda i, j: (j, 0, i))
    st_spec = pl.BlockSpec((hd, hd, LANE), lambda i, j: (0, 0, i))
    return pl.pallas_call(
        _rwkv_scan_kernel,
        grid=(c // LANE, L // tb),
        in_specs=[seq_spec] * 6 + [st_spec, pl.BlockSpec(tab.shape, lambda i, j: (0, 0, 0))],
        out_specs=[seq_spec, st_spec],
        out_shape=[jax.ShapeDtypeStruct((L, hd, c), f32), jax.ShapeDtypeStruct((hd, hd, c), f32)],
        scratch_shapes=[pltpu.VMEM((hd, hd, LANE), f32)],
        compiler_params=_cparams("parallel", "arbitrary"),
        name="rwkv_scan",
    )(r, d, k, v, kk, a, s0, tab)


def _rwkv_core(parts, wkv, tab):
    b, L, _ = parts[0].shape

    def to_chain(t):
        return jnp.transpose(t.reshape(b, L, RWKV_HEADS, RWKV_HEAD), (1, 3, 0, 2)).reshape(L, RWKV_HEAD, b * RWKV_HEADS)

    s0 = jnp.transpose(wkv, (2, 3, 0, 1)).reshape(RWKV_HEAD, RWKV_HEAD, b * RWKV_HEADS)
    y, s_fin = rwkv_scan(*[to_chain(t) for t in parts], s0, tab)
    y = jnp.transpose(y.reshape(L, RWKV_HEAD, b, RWKV_HEADS), (2, 0, 3, 1)).reshape(b * L, D_MODEL)
    s_fin = jnp.transpose(s_fin.reshape(RWKV_HEAD, RWKV_HEAD, b, RWKV_HEADS), (2, 3, 0, 1))
    return y, s_fin


def kernel(x_prompt, x_sample, state_ssm, state_conv, cache_swa_k, cache_swa_v, state_wkv, state_shift, rel_bias, norm_mix, norm_ffn, norm_final, mix_w_in, ssd_conv_w, ssd_conv_b, ssd_dt_bias, ssd_a_log, ssd_d_skip, ssd_gnorm, attn_sinks, mix_w_out, rwkv_mu, rwkv_w0, rwkv_w1, rwkv_w2, rwkv_a0, rwkv_a1, rwkv_a2, rwkv_g1, rwkv_g2, rwkv_k_k, rwkv_k_a, rwkv_r_k, rwkv_w_rkv, rwkv_w_o, rwkv_ln_w, rwkv_ln_b, peer_w_q, peer_sub_keys, peer_u, peer_v):
    bp, lp, d = x_prompt.shape
    bs, ls, _ = x_sample.shape
    tp = bp * lp
    ts = bs * ls
    tt = tp + ts
    x = jnp.concatenate([x_prompt.reshape(tp, d), x_sample.reshape(ts, d)], axis=0)

    def split(t):
        return t[:tp].reshape(bp, lp, -1), t[tp:].reshape(bs, ls, -1)

    def merge(p, s):
        return jnp.concatenate([p.reshape(tp, -1), s.reshape(ts, -1)], axis=0)

    def zero_state(a):
        return jnp.zeros((bp,) + a.shape[2:], a.dtype)

    def peer(x, ht, layer, g_next):
        wqt = peer_w_q[layer].T.astype(bf16)
        return peer_ffn_residual(x, ht, wqt, peer_sub_keys[layer].astype(bf16),
                                 peer_u[layer].astype(bf16), peer_v[layer].astype(bf16), g_next)

    z, xbc, q, kv, dt = in_proj(x, norm_mix[0], mix_w_in[0])
    pad16 = (0, LANE - SSM_HEADS)
    selh = (jnp.arange(LANE)[:, None] == jnp.arange(SSM_INNER)[None, :] // SSM_HEAD_DIM).astype(f32)
    ssd_params = (ssd_conv_w[0], ssd_conv_b[0].reshape(1, CONV_DIM),
                  jnp.pad(ssd_dt_bias[0], pad16).reshape(1, LANE),
                  jnp.pad(-jnp.exp(ssd_a_log[0]), pad16).reshape(1, LANE),
                  jnp.repeat(ssd_d_skip[0], SSM_HEAD_DIM).reshape(1, SSM_INNER),
                  ssd_gnorm[0].reshape(1, SSM_INNER), selh)
    y_ssd_p, ssm_p = ssd_mixer(z, xbc, dt, zero_state(state_conv), zero_state(state_ssm), ssd_params,
                               batch=bp, seq=lp, row0=0)
    y_ssd_s, ssm_s = ssd_mixer(z, xbc, dt, state_conv[0], state_ssm[0], ssd_params, batch=bs, seq=ls, row0=tp)
    o_att_p = swa_mixer(q, kv, None, None, rel_bias, attn_sinks[0], batch=bp, seq=lp, row0=0)
    o_att_s = swa_mixer(q, kv, cache_swa_k[0], cache_swa_v[0], rel_bias, attn_sinks[0], batch=bs, seq=ls, row0=tp)
    x, ht = out_proj(y_ssd_p, o_att_p, y_ssd_s, o_att_s, mix_w_out[0].astype(bf16), x, norm_ffn[0])
    xbc_p, xbc_s = split(xbc)
    conv_p = xbc_p[:, -(CONV_W - 1):]
    conv_s = jnp.concatenate([state_conv[0], xbc_s], axis=1)[:, -(CONV_W - 1):]
    kv_p, kv_s = split(kv)
    kvw = ATT_KV_HEADS * ATT_HEAD_DIM
    hshape = (ATT_KV_HEADS, ATT_HEAD_DIM)
    k_p = kv_p[:, -WINDOW:, :kvw].reshape(bp, WINDOW, *hshape)
    v_p = kv_p[:, -WINDOW:, kvw:].reshape(bp, WINDOW, *hshape)
    k_s = jnp.concatenate([cache_swa_k[0], kv_s[..., :kvw].reshape(bs, ls, *hshape)], axis=1)[:, -WINDOW:]
    v_s = jnp.concatenate([cache_swa_v[0], kv_s[..., kvw:].reshape(bs, ls, *hshape)], axis=1)[:, -WINDOW:]
    x, h = peer(x, ht, 0, norm_mix[1])

    h_p, h_s = split(h)
    prev_p = jnp.concatenate([jnp.zeros((bp, 1, d), f32), h_p[:, :-1]], axis=1)
    prev_s = jnp.concatenate([state_shift[0][:, None], h_s[:, :-1]], axis=1)
    prev = merge(prev_p, prev_s)

    def lora_pair(w1, w2):
        pad = LORA_PAD - w1.shape[1]
        return jnp.pad(w1, ((0, 0), (0, pad))), jnp.pad(w2, ((0, pad), (0, 0)))

    pairs = [lora_pair(rwkv_w1[0], rwkv_w2[0]), lora_pair(rwkv_a1[0], rwkv_a2[0]), lora_pair(rwkv_g1[0], rwkv_g2[0])]
    lora1 = jnp.stack([p[0] for p in pairs]).astype(bf16)
    lora2 = jnp.stack([p[1] for p in pairs]).astype(bf16)
    vecs = jnp.stack([rwkv_w0[0], rwkv_a0[0], rwkv_k_k[0], rwkv_k_a[0]])
    *parts, g = rwkv_pre(h, prev, rwkv_mu[0], rwkv_w_rkv[0].astype(bf16), lora1, lora2, vecs)

    def chain_table(p):
        return jnp.tile(p.reshape(RWKV_HEADS, RWKV_HEAD).T, (1, LANE // RWKV_HEADS))

    tab = jnp.stack([chain_table(rwkv_r_k[0].reshape(-1)), chain_table(rwkv_ln_w[0]), chain_table(rwkv_ln_b[0])])
    y_p, wkv_p = _rwkv_core([split(t)[0] for t in parts], zero_state(state_wkv), tab)
    y_s, wkv_s = _rwkv_core([split(t)[1] for t in parts], state_wkv[0], tab)
    x, ht = gated_out_proj(y_p, y_s, g, rwkv_w_o[0].astype(bf16), x, norm_ffn[1])
    shift_p, shift_s = h_p[:, -1], h_s[:, -1]
    _, y = peer(x, ht, 1, norm_final)
    y_p, y_s = split(y)
    return (y_p, y_s, ssm_p[None], conv_p[None], k_p[None], v_p[None], wkv_p[None], shift_p[None],
            ssm_s[None], conv_s[None], k_s[None], v_s[None], wkv_s[None], shift_s[None])
```

```python
import functools
import math

import jax
import jax.numpy as jnp
from jax import lax
from jax.experimental import pallas as pl
from jax.experimental.pallas import tpu as pltpu

f32 = jnp.float32
bf16 = jnp.bfloat16

D_MODEL = 1024
PAST_LEN = 16384
SSM_HEAD_DIM = 64
SSM_HEADS = 16
SSM_INNER = 1024
SSM_GROUPS = 2
SSM_STATE = 128
CONV_W = 4
CONV_DIM = 1536
SSD_CHUNK = 128
ATT_HEAD_DIM = 64
ATT_HEADS = 16
ATT_KV_HEADS = 4
ATT_GQA = 4
WINDOW = 128
ATT_BLOCK = 128
REL_BUCKETS = 32
REL_MAX_DIST = 128
RWKV_HEAD = 64
RWKV_HEADS = 16
PEER_HEADS = 8
N_KEYS = 128
N_EXPERTS = N_KEYS * N_KEYS
PEER_TOPK = 16
PEER_HALF = 128
EPS = 1e-5
GN_EPS = 64e-5

LANE = 128
SUBLANE = 8
HALF_LANE = LANE // 2
VMEM_LIMIT = 56 * 2 ** 20
HIGHEST = lax.Precision.HIGHEST
NT_DIMS = (((1,), (1,)), ((), ()))
TN_DIMS = (((0,), (0,)), ((), ()))

_CAND_COUNTS = [PEER_TOPK // (a + 1) for a in range(PEER_TOPK)]
_N_CAND = sum(_CAND_COUNTS)
_N_CAND_PAD = -(-_N_CAND // 8) * 8
_BIG = 1e9


def _cparams(*sem):
    return pltpu.CompilerParams(dimension_semantics=sem, vmem_limit_bytes=VMEM_LIMIT)


def _rms(x, g):
    return x * lax.rsqrt(jnp.mean(x * x, axis=-1, keepdims=True) + EPS) * g


def _silu(x):
    return x * jax.nn.sigmoid(x)


BF16_ROWS = 2 * SUBLANE


def _dup_bf16(v):
    u = pltpu.bitcast(v.astype(bf16).astype(f32), jnp.int32)
    return u | lax.shift_right_logical(u, jnp.int32(16))


def _row_as_bf16_tile(ref, h, i):
    row = ref[h, pl.ds(i, 1), :]
    tile = pltpu.bitcast(jnp.broadcast_to(row, (SUBLANE, row.shape[1])), bf16)
    return pltpu.repeat(tile, N_KEYS // BF16_ROWS, axis=0)


def _peer_select_kernel(ht_ref, wqt_ref, sk_ref, pos_ref, b0_ref, r1_ref, e0_ref, e1_ref,
                        qt_ref, s_ref, vals_ref, idx_ref, cs_ref):
    tm = ht_ref.shape[1]
    qt_ref[...] = jnp.dot(wqt_ref[...], ht_ref[...], preferred_element_type=f32).astype(bf16)
    row = lax.broadcasted_iota(jnp.int32, (N_KEYS, tm), 0).astype(f32)
    pos = pos_ref[...]
    neg_inf = f32(-jnp.inf)

    def head_compute(h, exact):
        bad = jnp.zeros((1, tm), f32)
        for c in (0, 1):
            off = pl.multiple_of((h * 2 + c) * PEER_HALF, PEER_HALF)
            s_ref[c] = jnp.dot(sk_ref[h, c], qt_ref[pl.ds(off, PEER_HALF), :], preferred_element_type=f32)

        bad_tiles = []
        for lt in range(tm // LANE):
            ls = slice(lt * LANE, (lt + 1) * LANE)
            row_t = row[:, :LANE]

            def remove_max(k, s, c, lt=lt, row_t=row_t):
                m = jnp.max(s, axis=0, keepdims=True)
                vals_ref[c, lt, pl.ds(k, 1), :] = m
                if exact:
                    idx = jnp.min(jnp.where(s == m, row_t, f32(N_KEYS)), axis=0, keepdims=True)
                    idx_ref[c, lt, pl.ds(k, 1), :] = idx
                    hit = row_t == idx
                else:
                    hit = s == m
                return jnp.where(hit, neg_inf, s), hit

            def extract(k, st, remove_max=remove_max):
                sa, sb, rank = st
                sa, _ = remove_max(k, sa, 0)
                sb, hit = remove_max(k, sb, 1)
                return sa, sb, jnp.where(hit, jnp.asarray(k, f32), rank)

            sa_fin, sb_fin, rank = lax.fori_loop(
                0, PEER_TOPK, extract, (s_ref[0, :, ls], s_ref[1, :, ls], jnp.full((N_KEYS, LANE), f32(PEER_TOPK))))
            r1_ref[h, :, ls] = rank.astype(bf16)
            if not exact:
                bad_t = jnp.zeros((1, LANE), f32)
                for s_fin in (sa_fin, sb_fin):
                    removed = jnp.sum(jnp.where(s_fin == neg_inf, f32(1.0), f32(0.0)), axis=0, keepdims=True)
                    bad_t = bad_t + jnp.abs(removed - f32(PEER_TOPK))
                bad_tiles.append(bad_t)
        if not exact:
            bad = bad + jnp.concatenate(bad_tiles, axis=1)

        def lane_tiles(ref, c):
            return jnp.concatenate([ref[c, lt] for lt in range(tm // LANE)], axis=1)

        v0 = lane_tiles(vals_ref, 0)
        v1 = lane_tiles(vals_ref, 1)
        r = 0
        for a in range(PEER_TOPK):
            nb = _CAND_COUNTS[a]
            cs_ref[r:r + nb, :] = v0[a:a + 1, :] + v1[0:nb, :]
            r += nb
        cs_ref[_N_CAND:_N_CAND_PAD, :] = jnp.full((_N_CAND_PAD - _N_CAND, tm), neg_inf)
        m1 = v0[0:1, :] + v1[0:1, :]

        cs = cs_ref[...]
        if exact:
            def extract2(k, st):
                cs, sel, z = st
                m = jnp.max(cs, axis=0, keepdims=True)
                p = jnp.min(jnp.where(cs == m, pos, f32(_BIG)), axis=0, keepdims=True)
                hit = pos == p
                return jnp.where(hit, neg_inf, cs), jnp.where(hit, f32(1.0), sel), z + jnp.exp(m - m1)

            _, sel, z = lax.fori_loop(0, PEER_TOPK, extract2,
                                      (cs, jnp.zeros((_N_CAND_PAD, tm), f32), jnp.zeros((1, tm), f32)))
        else:
            larger = jnp.zeros((_N_CAND_PAD, tm), f32)
            for r in range(_N_CAND):
                larger = larger + jnp.where(cs[r:r + 1, :] > cs, f32(1.0), f32(0.0))
            sel = jnp.where((larger < f32(PEER_TOPK)) & (cs > neg_inf), f32(1.0), f32(0.0))
            z = jnp.sum(sel * jnp.exp(cs - m1), axis=0, keepdims=True)
            bad = bad + jnp.abs(jnp.sum(sel, axis=0, keepdims=True) - f32(PEER_TOPK))
        s0 = s_ref[0]
        idx0 = lane_tiles(idx_ref, 0)
        bound0 = jnp.zeros((N_KEYS, tm), f32)
        r = 0
        for a in range(PEER_TOPK):
            nb = _CAND_COUNTS[a]
            cnt = jnp.sum(sel[r:r + nb, :], axis=0, keepdims=True)
            is_a = (row == idx0[a:a + 1, :]) if exact else (s0 == v0[a:a + 1, :])
            bound0 = jnp.where(is_a, cnt, bound0)
            r += nb
        b0_ref[h] = _dup_bf16(bound0)
        e0_ref[h] = _dup_bf16(jnp.exp(s0 - v0[0:1, :]) / z)
        e1_ref[h] = jnp.exp(s_ref[1] - v1[0:1, :]).astype(bf16)
        return bad

    def head_body(h, carry):
        bad = head_compute(h, False)

        @pl.when(jnp.max(bad) > 0.0)
        def _():
            head_compute(h, True)

        return carry

    lax.fori_loop(0, PEER_HEADS, head_body, 0)


def _peer_main_kernel(ht_ref, u_ref, v_ref, b0_ref, r1_ref, e0_ref, e1_ref, x_ref, gn_ref, o_ref, hn_ref,
                      acc_ref, *, nblk, npieces):
    j = pl.program_id(1)

    @pl.when(j == 0)
    def _():
        acc_ref[...] = jnp.zeros_like(acc_ref)

    ht = ht_ref[...]
    bpp = nblk // npieces
    pc = bpp * N_KEYS
    for p in range(npieces):
        at = jnp.dot(u_ref[p * pc:(p + 1) * pc, :], ht, preferred_element_type=f32)
        ws = []
        for b in range(bpp):
            i = j * nblk + p * bpp + b
            a = at[b * N_KEYS:(b + 1) * N_KEYS, :]
            g = jnp.zeros(a.shape, bf16)
            for h in range(PEER_HEADS):
                bnd = _row_as_bf16_tile(b0_ref, h, i)
                e0 = _row_as_bf16_tile(e0_ref, h, i)
                g = g + jnp.where(r1_ref[h] < bnd, e1_ref[h] * e0, bf16(0.0))
            gelu = 0.5 * a * (1.0 + lax.erf(a * f32(math.sqrt(0.5))))
            ws.append(g * gelu.astype(bf16))
        wt = jnp.concatenate(ws, axis=0) if bpp > 1 else ws[0]
        acc_ref[...] += lax.dot_general(wt, v_ref[p * pc:(p + 1) * pc, :], TN_DIMS,
                                        preferred_element_type=f32)

    @pl.when(j == pl.num_programs(1) - 1)
    def _():
        xn = x_ref[...] + acc_ref[...]
        o_ref[...] = xn
        hn_ref[...] = _rms(xn, gn_ref[...])


def peer_ffn_residual(x, ht, wqt, sk, u, v, g_next, tm_sel=256, tm=256, ce=2048, npieces=8):
    t = x.shape[0]
    d = D_MODEL
    pos_list = []
    for a in range(PEER_TOPK):
        pos_list += [a * PEER_TOPK + b for b in range(_CAND_COUNTS[a])]
    pos_list += [_BIG] * (_N_CAND_PAD - _N_CAND)
    pos = jnp.broadcast_to(jnp.asarray(pos_list, f32)[:, None], (_N_CAND_PAD, tm_sel))
    sel_shape = jax.ShapeDtypeStruct((PEER_HEADS, N_KEYS, t), jnp.int32)
    sel_shape16 = jax.ShapeDtypeStruct((PEER_HEADS, N_KEYS, t), bf16)
    sel_spec = pl.BlockSpec((PEER_HEADS, N_KEYS, tm_sel), lambda i: (0, 0, i))
    b0, r1, e0, e1 = pl.pallas_call(
        _peer_select_kernel,
        grid=(t // tm_sel,),
        in_specs=[pl.BlockSpec((d, tm_sel), lambda i: (0, i)),
                  pl.BlockSpec((PEER_HEADS * 2 * PEER_HALF, d), lambda i: (0, 0)),
                  pl.BlockSpec((PEER_HEADS, 2, N_KEYS, PEER_HALF), lambda i: (0, 0, 0, 0)),
                  pl.BlockSpec((_N_CAND_PAD, tm_sel), lambda i: (0, 0))],
        out_specs=[sel_spec] * 4,
        out_shape=[sel_shape, sel_shape16, sel_shape, sel_shape16],
        scratch_shapes=[pltpu.VMEM((PEER_HEADS * 2 * PEER_HALF, tm_sel), bf16),
                        pltpu.VMEM((2, N_KEYS, tm_sel), f32),
                        pltpu.VMEM((2, tm_sel // LANE, PEER_TOPK, LANE), f32),
                        pltpu.VMEM((2, tm_sel // LANE, PEER_TOPK, LANE), f32),
                        pltpu.VMEM((_N_CAND_PAD, tm_sel), f32)],
        compiler_params=_cparams("parallel"),
        name="peer_select",
    )(ht, wqt, sk, pos)

    nblk = ce // N_KEYS
    sel_spec2 = pl.BlockSpec((PEER_HEADS, N_KEYS, tm), lambda i, j: (0, 0, i))
    return pl.pallas_call(
        functools.partial(_peer_main_kernel, nblk=nblk, npieces=npieces),
        grid=(t // tm, N_EXPERTS // ce),
        in_specs=[pl.BlockSpec((d, tm), lambda i, j: (0, i)),
                  pl.BlockSpec((ce, d), lambda i, j: (j, 0)),
                  pl.BlockSpec((ce, d), lambda i, j: (j, 0)),
                  sel_spec2, sel_spec2, sel_spec2, sel_spec2,
                  pl.BlockSpec((tm, d), lambda i, j: (i, 0)), pl.BlockSpec((1, d), lambda i, j: (0, 0))],
        out_specs=[pl.BlockSpec((tm, d), lambda i, j: (i, 0))] * 2,
        out_shape=[jax.ShapeDtypeStruct((t, d), f32)] * 2,
        scratch_shapes=[pltpu.VMEM((tm, d), f32)],
        compiler_params=_cparams("parallel", "arbitrary"),
        name="peer_main",
    )(ht, u, v, b0, r1, e0, e1, x, g_next.reshape(1, d))


IN_WIDTHS = (SSM_INNER, CONV_DIM, ATT_HEADS * ATT_HEAD_DIM, 2 * ATT_KV_HEADS * ATT_HEAD_DIM, LANE)


def _in_proj_kernel(x_ref, g_ref, w_ref, *out_refs):
    h = _rms(x_ref[...], g_ref[...])
    y = jnp.dot(h.astype(bf16), w_ref[...], preferred_element_type=f32)
    o = 0
    for ref in out_refs:
        w = ref.shape[1]
        ref[...] = y[:, o:o + w]
        o += w


def in_proj(x, g, w_in, tm=256):
    t, d = x.shape
    o1 = SSM_INNER
    o2 = o1 + CONV_DIM
    o3 = o2 + SSM_HEADS
    w = jnp.concatenate([w_in[:, :o2], w_in[:, o3:], w_in[:, o2:o3],
                         jnp.zeros((d, LANE - SSM_HEADS), w_in.dtype)], axis=1).astype(bf16)
    n = w.shape[1]
    return pl.pallas_call(
        _in_proj_kernel,
        grid=(t // tm,),
        in_specs=[pl.BlockSpec((tm, d), lambda i: (i, 0)), pl.BlockSpec((1, d), lambda i: (0, 0)),
                  pl.BlockSpec((d, n), lambda i: (0, 0))],
        out_specs=[pl.BlockSpec((tm, wd), lambda i: (i, 0)) for wd in IN_WIDTHS],
        out_shape=[jax.ShapeDtypeStruct((t, wd), f32) for wd in IN_WIDTHS],
        compiler_params=_cparams("parallel"),
        name="in_proj",
    )(x, g.reshape(1, d), w)


N_PAIRS = SSM_HEADS // 2


def _ssd_kernel(*refs, lin):
    (z_ref, xbc_ref, dt_ref, cs_ref, h0_ref, cw_ref, cb_ref, dtb_ref, aneg_ref, dsk_ref, gn_ref, selh_ref) = refs[:12]
    refs = refs[12:]
    y_ref, hfin_ref, xe_ref, hp_ref, ys_ref = refs[:5]
    lc = SSD_CHUNK
    c = pl.program_id(1)

    @pl.when(c == 0)
    def _():
        xe_ref[0:SUBLANE, :] = cs_ref[0]
        hp_ref[...] = h0_ref[0]

    if lin == lc:
        xe_ref[SUBLANE:SUBLANE + lc, :] = xbc_ref[...]
        z = z_ref[...]
        dt_raw = dt_ref[...]
    else:
        zpad_ref, dtpad_ref = refs[5:7]
        xe_ref[SUBLANE:SUBLANE + lc, :] = jnp.zeros((lc, CONV_DIM), f32)
        xe_ref[SUBLANE:SUBLANE + lin, :] = xbc_ref[...]
        zpad_ref[...] = jnp.zeros_like(zpad_ref)
        zpad_ref[0:lin, :] = z_ref[...]
        dtpad_ref[...] = jnp.zeros_like(dtpad_ref)
        dtpad_ref[0:lin, :] = dt_ref[...]
        z = zpad_ref[...]
        dt_raw = dtpad_ref[...]

    conv = cb_ref[...]
    for j in range(CONV_W):
        o = SUBLANE - (CONV_W - 1) + j
        conv = conv + xe_ref[o:o + lc, :] * cw_ref[j:j + 1, :]
    if lin == lc:
        xe_ref[0:SUBLANE, :] = xe_ref[lc:lc + SUBLANE, :]
    xc = _silu(conv)
    xs = xc[:, :SSM_INNER]
    bm = xc[:, SSM_INNER:SSM_INNER + SSM_GROUPS * SSM_STATE]
    cm = xc[:, SSM_INNER + SSM_GROUPS * SSM_STATE:]

    row = lax.broadcasted_iota(jnp.int32, (lc, LANE), 0)
    col = lax.broadcasted_iota(jnp.int32, (lc, LANE), 1)
    causal = row >= col
    lane_lo = col < HALF_LANE
    neg_inf = f32(-jnp.inf)

    x = dt_raw + dtb_ref[...]
    dt = jnp.maximum(x, 0.0) + jnp.log(1.0 + jnp.exp(-jnp.abs(x)))
    if lin != lc:
        dt = jnp.where(row < lin, dt, 0.0)
    la = dt * aneg_ref[...]
    acs = jnp.dot(causal.astype(f32), la, precision=HIGHEST, preferred_element_type=f32)
    acs_t = acs.T
    selh = selh_ref[...]
    dt_exp = jnp.dot(dt, selh, precision=HIGHEST, preferred_element_type=f32)
    acs_exp = jnp.dot(acs, selh, precision=HIGHEST, preferred_element_type=f32)
    alast = acs[lc - 1:lc, :]
    xdt = xs * dt_exp
    eacs = jnp.exp(acs_exp)
    xdt_end = (xdt * jnp.exp(acs_exp[lc - 1:lc, :] - acs_exp)).astype(bf16)
    dsk = dsk_ref[...]

    for g in range(SSM_GROUPS):
        cmg = cm[:, g * SSM_STATE:(g + 1) * SSM_STATE].astype(bf16)
        bmg = bm[:, g * SSM_STATE:(g + 1) * SSM_STATE].astype(bf16)
        cb = lax.dot_general(cmg, bmg, NT_DIMS, preferred_element_type=f32)
        for jj in range(N_PAIRS // SSM_GROUPS):
            j = g * (N_PAIRS // SSM_GROUPS) + jj
            sl = slice(j * LANE, (j + 1) * LANE)
            xdt_pair = xdt[:, sl]
            ydiag = jnp.zeros((lc, LANE), f32)
            for half in (0, 1):
                h = 2 * j + half
                seg = acs[:, h:h + 1] - acs_t[h:h + 1, :]
                dec = jnp.exp(jnp.where(causal, seg, neg_inf))
                m = (cb * dec).astype(bf16)
                keep = lane_lo if half == 0 else jnp.logical_not(lane_lo)
                xd = jnp.where(keep, xdt_pair, 0.0).astype(bf16)
                ydiag = ydiag + jnp.dot(m, xd, preferred_element_type=f32)
            hpj = hp_ref[j]
            yoff = lax.dot_general(cmg, hpj.astype(bf16), NT_DIMS, preferred_element_type=f32) * eacs[:, sl]
            s_new = lax.dot_general(xdt_end[:, sl], bmg, TN_DIMS, preferred_element_type=f32)
            dl = jnp.where(row < SSM_HEAD_DIM, alast[:, 2 * j:2 * j + 1], alast[:, 2 * j + 1:2 * j + 2])
            hp_ref[j] = hpj * jnp.exp(dl) + s_new
            ys_ref[:, sl] = ydiag + yoff + dsk[:, sl] * xs[:, sl]

    y = ys_ref[...] * _silu(z)
    gs = SSM_INNER // SSM_GROUPS
    gn = gn_ref[...]
    for g in range(SSM_GROUPS):
        yg = y[:, g * gs:(g + 1) * gs]
        yg = yg * lax.rsqrt(jnp.mean(yg * yg, axis=-1, keepdims=True) + EPS) * gn[:, g * gs:(g + 1) * gs]
        y_ref[:, g * gs:(g + 1) * gs] = yg[0:lin, :]

    @pl.when(c == pl.num_programs(1) - 1)
    def _():
        hfin_ref[0] = hp_ref[...]


def ssd_mixer(z, xbc, dt, conv_state, ssm_state, params, *, batch, seq, row0):
    conv_w, conv_b, dt_bias, a_neg, d_skip, gnorm, selh = params
    lc = SSD_CHUNK
    lin = min(seq, lc)
    nc = seq // lin
    blk0 = row0 // lin
    cs = jnp.pad(conv_state, ((0, 0), (SUBLANE - (CONV_W - 1), 0), (0, 0)))
    h0 = ssm_state.reshape(batch, N_PAIRS, LANE, SSM_STATE)

    def rows(w):
        return pl.BlockSpec((lin, w), lambda b, c: (blk0 + b * nc + c, 0))

    def const(shape):
        return pl.BlockSpec(shape, lambda b, c: (0,) * len(shape))

    in_specs = [rows(SSM_INNER), rows(CONV_DIM), rows(LANE),
                pl.BlockSpec((1, SUBLANE, CONV_DIM), lambda b, c: (b, 0, 0)),
                pl.BlockSpec((1, N_PAIRS, LANE, SSM_STATE), lambda b, c: (b, 0, 0, 0)),
                const((CONV_W, CONV_DIM)), const((1, CONV_DIM)), const((1, LANE)), const((1, LANE)),
                const((1, SSM_INNER)), const((1, SSM_INNER)), const((LANE, SSM_INNER))]
    args = [z, xbc, dt, cs, h0, conv_w, conv_b, dt_bias, a_neg, d_skip, gnorm, selh]
    scratch = [pltpu.VMEM((lc + 2 * SUBLANE, CONV_DIM), f32), pltpu.VMEM((N_PAIRS, LANE, SSM_STATE), f32),
               pltpu.VMEM((lc, SSM_INNER), f32)]
    if lin != lc:
        scratch += [pltpu.VMEM((lc, SSM_INNER), f32), pltpu.VMEM((lc, LANE), f32)]
    y, h_fin = pl.pallas_call(
        functools.partial(_ssd_kernel, lin=lin),
        grid=(batch, nc),
        in_specs=in_specs,
        out_specs=[pl.BlockSpec((lin, SSM_INNER), lambda b, c: (b * nc + c, 0)),
                   pl.BlockSpec((1, N_PAIRS, LANE, SSM_STATE), lambda b, c: (b, 0, 0, 0))],
        out_shape=[jax.ShapeDtypeStruct((batch * seq, SSM_INNER), f32),
                   jax.ShapeDtypeStruct((batch, N_PAIRS, LANE, SSM_STATE), f32)],
        scratch_shapes=scratch,
        compiler_params=_cparams("parallel", "arbitrary"),
        name="ssd_mixer",
    )(*args)
    return y, h_fin.reshape(batch, SSM_HEADS, SSM_HEAD_DIM, SSM_STATE)


def _swa_kernel(*refs, lq, masked_first):
    q_ref, kp_ref, vp_ref, kc_ref, vc_ref, bias_ref, sink_ref = refs[:7]
    refs = refs[7:]
    o_ref = refs[0]
    n = pl.program_id(1)
    q = q_ref[...]
    if lq == ATT_BLOCK:
        kc = kc_ref[...]
        vc = vc_ref[...]
    else:
        kpad_ref, vpad_ref = refs[1:3]
        kpad_ref[...] = jnp.zeros_like(kpad_ref)
        vpad_ref[...] = jnp.zeros_like(vpad_ref)
        kpad_ref[0:lq, :] = kc_ref[...]
        vpad_ref[0:lq, :] = vc_ref[...]
        kc = kpad_ref[...]
        vc = vpad_ref[...]
    kp = kp_ref[...]
    vp = vp_ref[...]
    lane_lo = lax.broadcasted_iota(jnp.int32, (ATT_BLOCK, LANE), 1) < HALF_LANE
    lane_lo_q = lax.broadcasted_iota(jnp.int32, (lq, LANE), 1) < HALF_LANE
    neg_inf = f32(-jnp.inf)
    scale = f32(ATT_HEAD_DIM ** -0.5)

    for g in range(ATT_KV_HEADS):
        sl = slice((g // 2) * LANE, (g // 2 + 1) * LANE)
        odd = g % 2 == 1

        def kpad(k):
            pair = k[:, sl]
            if odd:
                pair = pltpu.roll(pair, HALF_LANE, 1)
            return jnp.where(lane_lo, pair, 0.0).astype(bf16)

        def vdup(v):
            pair = v[:, sl]
            rolled = pltpu.roll(pair, HALF_LANE, 1)
            return (jnp.where(lane_lo, rolled, pair) if odd else jnp.where(lane_lo, pair, rolled)).astype(bf16)

        qp0 = q[:, (2 * g) * LANE:(2 * g + 1) * LANE]
        qp1 = q[:, (2 * g + 1) * LANE:(2 * g + 2) * LANE]
        qg = jnp.concatenate([qp0, pltpu.roll(qp0, HALF_LANE, 1), qp1, pltpu.roll(qp1, HALF_LANE, 1)],
                             axis=0).astype(bf16)
        bias = bias_ref[g]
        sp = lax.dot_general(qg, kpad(kp), NT_DIMS, preferred_element_type=f32) * scale + bias[:, :WINDOW]
        sc = lax.dot_general(qg, kpad(kc), NT_DIMS, preferred_element_type=f32) * scale + bias[:, WINDOW:]
        if masked_first:
            sp = jnp.where(n > 0, sp, neg_inf)
        sink = sink_ref[g][:, 0:1]
        m = jnp.maximum(jnp.maximum(jnp.max(sp, axis=-1, keepdims=True), jnp.max(sc, axis=-1, keepdims=True)), sink)
        pp = jnp.exp(sp - m)
        pc = jnp.exp(sc - m)
        denom = jnp.sum(pp, axis=-1, keepdims=True) + jnp.sum(pc, axis=-1, keepdims=True) + jnp.exp(sink - m)
        og = (jnp.dot(pp.astype(bf16), vdup(vp), preferred_element_type=f32)
              + jnp.dot(pc.astype(bf16), vdup(vc), preferred_element_type=f32)) / denom
        o_ref[:, (2 * g) * LANE:(2 * g + 1) * LANE] = jnp.where(lane_lo_q, og[0:lq], og[lq:2 * lq])
        o_ref[:, (2 * g + 1) * LANE:(2 * g + 2) * LANE] = jnp.where(lane_lo_q, og[2 * lq:3 * lq], og[3 * lq:4 * lq])


def _rel_bucket(dist):
    exact = REL_BUCKETS // 2
    d = jnp.maximum(dist, 0)
    large = exact + (jnp.log(jnp.maximum(d, 1).astype(f32) / exact)
                     / math.log(REL_MAX_DIST / exact) * (REL_BUCKETS - exact)).astype(jnp.int32)
    large = jnp.minimum(large, REL_BUCKETS - 1)
    return jnp.where(d < exact, d, large)


def _swa_tables(rel_bias, sinks, lq):
    qi = jnp.arange(lq)[:, None]
    kj = jnp.arange(2 * ATT_BLOCK)[None, :]
    dist = qi + WINDOW - kj
    band = (dist >= 0) & (dist <= WINDOW)
    onehot = (_rel_bucket(dist)[..., None] == jnp.arange(REL_BUCKETS)).astype(f32)
    bias = jnp.einsum('qkb,bh->qkh', onehot, rel_bias, precision=HIGHEST)
    bias = jnp.where(band[..., None], bias, -jnp.inf)
    bias = jnp.transpose(bias, (2, 0, 1)).reshape(ATT_KV_HEADS, ATT_GQA * lq, 2 * ATT_BLOCK)
    sink = jnp.broadcast_to(sinks.reshape(ATT_KV_HEADS, ATT_GQA, 1, 1), (ATT_KV_HEADS, ATT_GQA, lq, LANE))
    return bias, sink.reshape(ATT_KV_HEADS, ATT_GQA * lq, LANE)


def swa_mixer(q, kv, k_prev, v_prev, rel_bias, sinks, *, batch, seq, row0):
    lq = min(seq, ATT_BLOCK)
    nb = seq // lq
    blk0 = row0 // lq
    kvw = ATT_KV_HEADS * ATT_HEAD_DIM
    bias, sink = _swa_tables(rel_bias, sinks, lq)
    cur_k = pl.BlockSpec((lq, kvw), lambda b, n: (blk0 + b * nb + n, 0))
    cur_v = pl.BlockSpec((lq, kvw), lambda b, n: (blk0 + b * nb + n, 1))
    if k_prev is None:
        prev_k = pl.BlockSpec((WINDOW, kvw), lambda b, n: (blk0 + b * nb + jnp.maximum(n - 1, 0), 0))
        prev_v = pl.BlockSpec((WINDOW, kvw), lambda b, n: (blk0 + b * nb + jnp.maximum(n - 1, 0), 1))
        kp_arr, vp_arr = kv, kv
    else:
        prev_k = pl.BlockSpec((WINDOW, kvw), lambda b, n: (b, 0))
        prev_v = prev_k
        kp_arr = k_prev.reshape(batch * WINDOW, kvw)
        vp_arr = v_prev.reshape(batch * WINDOW, kvw)
    rows = pl.BlockSpec((lq, ATT_HEADS * ATT_HEAD_DIM), lambda b, n: (blk0 + b * nb + n, 0))
    in_specs = [rows, prev_k, prev_v, cur_k, cur_v,
                pl.BlockSpec(bias.shape, lambda b, n: (0, 0, 0)), pl.BlockSpec(sink.shape, lambda b, n: (0, 0, 0))]
    args = [q, kp_arr, vp_arr, kv, kv, bias, sink]
    scratch = [] if lq == ATT_BLOCK else [pltpu.VMEM((ATT_BLOCK, kvw), f32), pltpu.VMEM((ATT_BLOCK, kvw), f32)]
    return pl.pallas_call(
        functools.partial(_swa_kernel, lq=lq, masked_first=k_prev is None),
        grid=(batch, nb),
        in_specs=in_specs,
        out_specs=pl.BlockSpec((lq, ATT_HEADS * ATT_HEAD_DIM), lambda b, n: (b * nb + n, 0)),
        out_shape=jax.ShapeDtypeStruct((batch * seq, ATT_HEADS * ATT_HEAD_DIM), f32),
        scratch_shapes=scratch,
        compiler_params=_cparams("parallel", "arbitrary"),
        name="swa_mixer",
    )(*args)


def _out_proj_kernel(yap_ref, ybp_ref, yas_ref, ybs_ref, w_ref, x_ref, g_ref, o_ref, ht_ref, *, n_prompt):
    ka = yap_ref.shape[1]

    def body(ya_ref, yb_ref):
        acc = jnp.dot(ya_ref[...].astype(bf16), w_ref[0:ka, :], preferred_element_type=f32)
        acc = acc + jnp.dot(yb_ref[...].astype(bf16), w_ref[ka:, :], preferred_element_type=f32)
        xn = x_ref[...] + acc
        o_ref[...] = xn
        ht_ref[...] = _rms(xn, g_ref[...]).T.astype(bf16)

    @pl.when(pl.program_id(0) < n_prompt)
    def _():
        body(yap_ref, ybp_ref)

    @pl.when(pl.program_id(0) >= n_prompt)
    def _():
        body(yas_ref, ybs_ref)


def out_proj(ya_p, yb_p, ya_s, yb_s, w, x, g, tm=512):
    t, d = x.shape
    ka, kb = ya_p.shape[1], yb_p.shape[1]
    n_p = ya_p.shape[0] // tm
    n_s = ya_s.shape[0] // tm

    def p_rows(k):
        return pl.BlockSpec((tm, k), lambda i: (jnp.minimum(i, n_p - 1), 0))

    def s_rows(k):
        return pl.BlockSpec((tm, k), lambda i: (jnp.maximum(i - n_p, 0), 0))

    return pl.pallas_call(
        functools.partial(_out_proj_kernel, n_prompt=n_p),
        grid=(n_p + n_s,),
        in_specs=[p_rows(ka), p_rows(kb), s_rows(ka), s_rows(kb),
                  pl.BlockSpec((ka + kb, d), lambda i: (0, 0)), pl.BlockSpec((tm, d), lambda i: (i, 0)),
                  pl.BlockSpec((1, d), lambda i: (0, 0))],
        out_specs=[pl.BlockSpec((tm, d), lambda i: (i, 0)), pl.BlockSpec((d, tm), lambda i: (0, i))],
        out_shape=[jax.ShapeDtypeStruct((t, d), f32), jax.ShapeDtypeStruct((d, t), bf16)],
        compiler_params=_cparams("arbitrary"),
        name="out_proj",
    )(ya_p, yb_p, ya_s, yb_s, w, x, g.reshape(1, d))


def _gated_out_kernel(yp_ref, gp_ref, ys_ref, gs_ref, w_ref, x_ref, gn_ref, o_ref, ht_ref, *, n_prompt):
    def body(y_ref, g_ref):
        a = (y_ref[...] * g_ref[...]).astype(bf16)
        xn = x_ref[...] + jnp.dot(a, w_ref[...], preferred_element_type=f32)
        o_ref[...] = xn
        ht_ref[...] = _rms(xn, gn_ref[...]).T.astype(bf16)

    @pl.when(pl.program_id(0) < n_prompt)
    def _():
        body(yp_ref, gp_ref)

    @pl.when(pl.program_id(0) >= n_prompt)
    def _():
        body(ys_ref, gs_ref)


def gated_out_proj(y_p, g_p, y_s, g_s, w, x, gn, tm=512):
    t, d = x.shape
    n_p = y_p.shape[0] // tm
    n_s = y_s.shape[0] // tm
    rows = pl.BlockSpec((tm, d), lambda i: (i, 0))
    p_rows = pl.BlockSpec((tm, d), lambda i: (jnp.minimum(i, n_p - 1), 0))
    s_rows = pl.BlockSpec((tm, d), lambda i: (jnp.maximum(i - n_p, 0), 0))
    return pl.pallas_call(
        functools.partial(_gated_out_kernel, n_prompt=n_p),
        grid=(n_p + n_s,),
        in_specs=[p_rows, p_rows, s_rows, s_rows,
                  pl.BlockSpec((d, d), lambda i: (0, 0)), rows, pl.BlockSpec((1, d), lambda i: (0, 0))],
        out_specs=[rows, pl.BlockSpec((d, tm), lambda i: (0, i))],
        out_shape=[jax.ShapeDtypeStruct((t, d), f32), jax.ShapeDtypeStruct((d, t), bf16)],
        compiler_params=_cparams("arbitrary"),
        name="rwkv_out",
    )(y_p, g_p, y_s, g_s, w, x, gn.reshape(1, d))


LORA_PAD = 2 * LANE


def _softplus(x):
    return jnp.maximum(x, 0.0) + jnp.log(1.0 + jnp.exp(-jnp.abs(x)))


def _rwkv_pre_kernel(h_ref, p_ref, mu_ref, wrkv_ref, l1_ref, l2_ref, vec_ref,
                     r_ref, d_ref, k_ref, v_ref, kk_ref, a_ref, g_ref):
    h = h_ref[...]
    xx = p_ref[...] - h

    def mix(j):
        return (h + xx * mu_ref[j:j + 1, :]).astype(bf16)

    def mm(a, w):
        return jnp.dot(a, w, preferred_element_type=f32)

    r_ref[...] = mm(mix(0), wrkv_ref[0])
    k = mm(mix(2), wrkv_ref[1])
    v_ref[...] = mm(mix(3), wrkv_ref[2])
    wl = vec_ref[0:1, :] + mm(jnp.tanh(mm(mix(1), l1_ref[0])).astype(bf16), l2_ref[0])
    al = vec_ref[1:2, :] + mm(mm(mix(4), l1_ref[1]).astype(bf16), l2_ref[1])
    g_ref[...] = mm(jax.nn.sigmoid(mm(mix(5), l1_ref[2])).astype(bf16), l2_ref[2])
    w = -_softplus(-wl) - 0.5
    d_ref[...] = jnp.exp(-jnp.exp(w))
    a = jax.nn.sigmoid(al)
    a_ref[...] = a
    kk_ref[...] = k * vec_ref[2:3, :]
    k_ref[...] = k * (1.0 + (a - 1.0) * vec_ref[3:4, :])


def rwkv_pre(h, prev, mu, w_rkv, lora1, lora2, vecs, *, row0, tm=256):
    t, d = prev.shape
    blk0 = row0 // tm
    rows = pl.BlockSpec((tm, d), lambda i: (i, 0))

    def const(a):
        return pl.BlockSpec(a.shape, lambda i: (0,) * a.ndim)

    return pl.pallas_call(
        _rwkv_pre_kernel,
        grid=(t // tm,),
        in_specs=[pl.BlockSpec((tm, d), lambda i: (blk0 + i, 0)), rows,
                  const(mu), const(w_rkv), const(lora1), const(lora2), const(vecs)],
        out_specs=[rows] * 7,
        out_shape=[jax.ShapeDtypeStruct((t, d), f32)] * 7,
        compiler_params=_cparams("parallel"),
        name="rwkv_pre",
    )(h, prev, mu, w_rkv, lora1, lora2, vecs)


RWKV_VGROUP = 8


def _rwkv_scan_kernel(r_ref, d_ref, k_ref, v_ref, kk_ref, a_ref, s0_ref, tab_ref, y_ref, sfin_ref, s_ref):
    tb = r_ref.shape[0]

    @pl.when(pl.program_id(1) == 0)
    def _():
        s_ref[...] = s0_ref[...]

    def step(t, carry):
        kkr = kk_ref[t]
        nrm = jnp.sqrt(jnp.sum(kkr * kkr, axis=0, keepdims=True))
        kk = kkr / jnp.maximum(nrm, 1e-12)
        d = d_ref[t]
        kv = k_ref[t]
        r = r_ref[t]
        b = kk * a_ref[t]

        def vgroup(g, c2):
            v0 = pl.multiple_of(g * RWKV_VGROUP, RWKV_VGROUP)
            vrows = v_ref[t, pl.ds(v0, RWKV_VGROUP), :]
            ys = []
            for vi in range(RWKV_VGROUP):
                sv = s_ref[v0 + vi]
                sa = -jnp.sum(sv * kk, axis=0, keepdims=True)
                sn = sv * d + sa * b + vrows[vi:vi + 1, :] * kv
                s_ref[v0 + vi] = sn
                ys.append(jnp.sum(sn * r, axis=0, keepdims=True))
            y_ref[t, pl.ds(v0, RWKV_VGROUP), :] = jnp.concatenate(ys, axis=0)
            return c2

        lax.fori_loop(0, RWKV_HEAD // RWKV_VGROUP, vgroup, 0)
        y = y_ref[t]
        mean = jnp.mean(y, axis=0, keepdims=True)
        yc = y - mean
        var = jnp.mean(yc * yc, axis=0, keepdims=True)
        bonus = jnp.sum(r * kv * tab_ref[0], axis=0, keepdims=True)
        y_ref[t] = yc * lax.rsqrt(var + GN_EPS) * tab_ref[1] + tab_ref[2] + bonus * v_ref[t]
        return carry

    lax.fori_loop(0, tb, step, 0)

    @pl.when(pl.program_id(1) == pl.num_programs(1) - 1)
    def _():
        sfin_ref[...] = s_ref[...]


def rwkv_scan(r, d, k, v, kk, a, s0, tab):
    L, hd, c = r.shape
    tb = min(L, 16)
    seq_spec = pl.BlockSpec((tb, hd, LANE), lambda i, j: (j, 0, i))
    st_spec = pl.BlockSpec((hd, hd, LANE), lambda i, j: (0, 0, i))
    return pl.pallas_call(
        _rwkv_scan_kernel,
        grid=(c // LANE, L // tb),
        in_specs=[seq_spec] * 6 + [st_spec, pl.BlockSpec(tab.shape, lambda i, j: (0, 0, 0))],
        out_specs=[seq_spec, st_spec],
        out_shape=[jax.ShapeDtypeStruct((L, hd, c), f32), jax.ShapeDtypeStruct((hd, hd, c), f32)],
        scratch_shapes=[pltpu.VMEM((hd, hd, LANE), f32)],
        compiler_params=_cparams("parallel", "arbitrary"),
        name="rwkv_scan",
    )(r, d, k, v, kk, a, s0, tab)


def _rwkv_core(parts, wkv, tab):
    b, L, _ = parts[0].shape

    def to_chain(t):
        return jnp.transpose(t.reshape(b, L, RWKV_HEADS, RWKV_HEAD), (1, 3, 0, 2)).reshape(L, RWKV_HEAD, b * RWKV_HEADS)

    s0 = jnp.transpose(wkv, (2, 3, 0, 1)).reshape(RWKV_HEAD, RWKV_HEAD, b * RWKV_HEADS)
    y, s_fin = rwkv_scan(*[to_chain(t) for t in parts], s0, tab)
    y = jnp.transpose(y.reshape(L, RWKV_HEAD, b, RWKV_HEADS), (2, 0, 3, 1)).reshape(b * L, D_MODEL)
    s_fin = jnp.transpose(s_fin.reshape(RWKV_HEAD, RWKV_HEAD, b, RWKV_HEADS), (2, 3, 0, 1))
    return y, s_fin


def kernel(x_prompt, x_sample, state_ssm, state_conv, cache_swa_k, cache_swa_v, state_wkv, state_shift, rel_bias, norm_mix, norm_ffn, norm_final, mix_w_in, ssd_conv_w, ssd_conv_b, ssd_dt_bias, ssd_a_log, ssd_d_skip, ssd_gnorm, attn_sinks, mix_w_out, rwkv_mu, rwkv_w0, rwkv_w1, rwkv_w2, rwkv_a0, rwkv_a1, rwkv_a2, rwkv_g1, rwkv_g2, rwkv_k_k, rwkv_k_a, rwkv_r_k, rwkv_w_rkv, rwkv_w_o, rwkv_ln_w, rwkv_ln_b, peer_w_q, peer_sub_keys, peer_u, peer_v):
    bp, lp, d = x_prompt.shape
    bs, ls, _ = x_sample.shape
    tp = bp * lp
    ts = bs * ls
    tt = tp + ts
    x = jnp.concatenate([x_prompt.reshape(tp, d), x_sample.reshape(ts, d)], axis=0)

    def split(t):
        return t[:tp].reshape(bp, lp, -1), t[tp:].reshape(bs, ls, -1)

    def zero_state(a):
        return jnp.zeros((bp,) + a.shape[2:], a.dtype)

    def peer(x, ht, layer, g_next):
        wqt = peer_w_q[layer].T.astype(bf16)
        return peer_ffn_residual(x, ht, wqt, peer_sub_keys[layer].astype(bf16),
                                 peer_u[layer].astype(bf16), peer_v[layer].astype(bf16), g_next)

    z, xbc, q, kv, dt = in_proj(x, norm_mix[0], mix_w_in[0])
    pad16 = (0, LANE - SSM_HEADS)
    selh = (jnp.arange(LANE)[:, None] == jnp.arange(SSM_INNER)[None, :] // SSM_HEAD_DIM).astype(f32)
    ssd_params = (ssd_conv_w[0], ssd_conv_b[0].reshape(1, CONV_DIM),
                  jnp.pad(ssd_dt_bias[0], pad16).reshape(1, LANE),
                  jnp.pad(-jnp.exp(ssd_a_log[0]), pad16).reshape(1, LANE),
                  jnp.repeat(ssd_d_skip[0], SSM_HEAD_DIM).reshape(1, SSM_INNER),
                  ssd_gnorm[0].reshape(1, SSM_INNER), selh)
    y_ssd_p, ssm_p = ssd_mixer(z, xbc, dt, zero_state(state_conv), zero_state(state_ssm), ssd_params,
                               batch=bp, seq=lp, row0=0)
    y_ssd_s, ssm_s = ssd_mixer(z, xbc, dt, state_conv[0], state_ssm[0], ssd_params, batch=bs, seq=ls, row0=tp)
    o_att_p = swa_mixer(q, kv, None, None, rel_bias, attn_sinks[0], batch=bp, seq=lp, row0=0)
    o_att_s = swa_mixer(q, kv, cache_swa_k[0], cache_swa_v[0], rel_bias, attn_sinks[0], batch=bs, seq=ls, row0=tp)
    x, ht = out_proj(y_ssd_p, o_att_p, y_ssd_s, o_att_s, mix_w_out[0].astype(bf16), x, norm_ffn[0])
    xbc_p, xbc_s = split(xbc)
    conv_p = xbc_p[:, -(CONV_W - 1):]
    conv_s = jnp.concatenate([state_conv[0], xbc_s], axis=1)[:, -(CONV_W - 1):]
    kv_p, kv_s = split(kv)
    kvw = ATT_KV_HEADS * ATT_HEAD_DIM
    hshape = (ATT_KV_HEADS, ATT_HEAD_DIM)
    k_p = kv_p[:, -WINDOW:, :kvw].reshape(bp, WINDOW, *hshape)
    v_p = kv_p[:, -WINDOW:, kvw:].reshape(bp, WINDOW, *hshape)
    k_s = jnp.concatenate([cache_swa_k[0], kv_s[..., :kvw].reshape(bs, ls, *hshape)], axis=1)[:, -WINDOW:]
    v_s = jnp.concatenate([cache_swa_v[0], kv_s[..., kvw:].reshape(bs, ls, *hshape)], axis=1)[:, -WINDOW:]
    x, h = peer(x, ht, 0, norm_mix[1])

    h_p, h_s = split(h)
    prev_p = jnp.concatenate([jnp.zeros((bp, 1, d), f32), h_p[:, :-1]], axis=1)
    prev_s = jnp.concatenate([state_shift[0][:, None], h_s[:, :-1]], axis=1)
    def lora_pair(w1, w2):
        pad = LORA_PAD - w1.shape[1]
        return jnp.pad(w1, ((0, 0), (0, pad))), jnp.pad(w2, ((0, pad), (0, 0)))

    pairs = [lora_pair(rwkv_w1[0], rwkv_w2[0]), lora_pair(rwkv_a1[0], rwkv_a2[0]), lora_pair(rwkv_g1[0], rwkv_g2[0])]
    lora1 = jnp.stack([p[0] for p in pairs]).astype(bf16)
    lora2 = jnp.stack([p[1] for p in pairs]).astype(bf16)
    vecs = jnp.stack([rwkv_w0[0], rwkv_a0[0], rwkv_k_k[0], rwkv_k_a[0]])
    pre_args = (rwkv_mu[0], rwkv_w_rkv[0].astype(bf16), lora1, lora2, vecs)
    *parts_p, g_p = rwkv_pre(h, prev_p.reshape(tp, d), *pre_args, row0=0)
    *parts_s, g_s = rwkv_pre(h, prev_s.reshape(ts, d), *pre_args, row0=tp)

    def chain_table(p):
        return jnp.tile(p.reshape(RWKV_HEADS, RWKV_HEAD).T, (1, LANE // RWKV_HEADS))

    tab = jnp.stack([chain_table(rwkv_r_k[0].reshape(-1)), chain_table(rwkv_ln_w[0]), chain_table(rwkv_ln_b[0])])
    y_p, wkv_p = _rwkv_core([t.reshape(bp, lp, d) for t in parts_p], zero_state(state_wkv), tab)
    y_s, wkv_s = _rwkv_core([t.reshape(bs, ls, d) for t in parts_s], state_wkv[0], tab)
    x, ht = gated_out_proj(y_p, g_p, y_s, g_s, rwkv_w_o[0].astype(bf16), x, norm_ffn[1])
    shift_p, shift_s = h_p[:, -1], h_s[:, -1]
    _, y = peer(x, ht, 1, norm_final)
    y_p, y_s = split(y)
    return (y_p, y_s, ssm_p[None], conv_p[None], k_p[None], v_p[None], wkv_p[None], shift_p[None],
            ssm_s[None], conv_s[None], k_s[None], v_s[None], wkv_s[None], shift_s[None])
```

```python
import functools
import math

import jax
import jax.numpy as jnp
from jax import lax
from jax.experimental import pallas as pl
from jax.experimental.pallas import tpu as pltpu

f32 = jnp.float32
bf16 = jnp.bfloat16

D_MODEL = 1024
PAST_LEN = 16384
SSM_HEAD_DIM = 64
SSM_HEADS = 16
SSM_INNER = 1024
SSM_GROUPS = 2
SSM_STATE = 128
CONV_W = 4
CONV_DIM = 1536
SSD_CHUNK = 128
ATT_HEAD_DIM = 64
ATT_HEADS = 16
ATT_KV_HEADS = 4
ATT_GQA = 4
WINDOW = 128
ATT_BLOCK = 128
REL_BUCKETS = 32
REL_MAX_DIST = 128
RWKV_HEAD = 64
RWKV_HEADS = 16
PEER_HEADS = 8
N_KEYS = 128
N_EXPERTS = N_KEYS * N_KEYS
PEER_TOPK = 16
PEER_HALF = 128
EPS = 1e-5
GN_EPS = 64e-5

LANE = 128
SUBLANE = 8
HALF_LANE = LANE // 2
VMEM_LIMIT = 56 * 2 ** 20
HIGHEST = lax.Precision.HIGHEST
NT_DIMS = (((1,), (1,)), ((), ()))
TN_DIMS = (((0,), (0,)), ((), ()))

_CAND_COUNTS = [PEER_TOPK // (a + 1) for a in range(PEER_TOPK)]
_N_CAND = sum(_CAND_COUNTS)
_N_CAND_PAD = -(-_N_CAND // 8) * 8
_BIG = 1e9


def _cparams(*sem):
    return pltpu.CompilerParams(dimension_semantics=sem, vmem_limit_bytes=VMEM_LIMIT)


def _rms(x, g):
    return x * lax.rsqrt(jnp.mean(x * x, axis=-1, keepdims=True) + EPS) * g


def _silu(x):
    return x * jax.nn.sigmoid(x)


BF16_ROWS = 2 * SUBLANE


def _dup_bf16(v):
    u = pltpu.bitcast(v.astype(bf16).astype(f32), jnp.int32)
    return u | lax.shift_right_logical(u, jnp.int32(16))


def _row_as_bf16_tile(ref, h, i):
    row = ref[h, pl.ds(i, 1), :]
    tile = pltpu.bitcast(jnp.broadcast_to(row, (SUBLANE, row.shape[1])), bf16)
    return pltpu.repeat(tile, N_KEYS // BF16_ROWS, axis=0)


def _peer_select_kernel(ht_ref, wqt_ref, sk_ref, pos_ref, b0_ref, r1_ref, e0_ref, e1_ref,
                        qt_ref, s_ref, vals_ref, idx_ref, cs_ref):
    tm = ht_ref.shape[1]
    qt_ref[...] = jnp.dot(wqt_ref[...], ht_ref[...], preferred_element_type=f32).astype(bf16)
    row = lax.broadcasted_iota(jnp.int32, (N_KEYS, tm), 0).astype(f32)
    pos = pos_ref[...]
    neg_inf = f32(-jnp.inf)

    def head_compute(h, exact):
        bad = jnp.zeros((1, tm), f32)
        for c in (0, 1):
            off = pl.multiple_of((h * 2 + c) * PEER_HALF, PEER_HALF)
            s_ref[c] = jnp.dot(sk_ref[h, c], qt_ref[pl.ds(off, PEER_HALF), :], preferred_element_type=f32)

        bad_tiles = []
        for lt in range(tm // LANE):
            ls = slice(lt * LANE, (lt + 1) * LANE)
            row_t = row[:, :LANE]

            def remove_max(k, s, c, lt=lt, row_t=row_t):
                m = jnp.max(s, axis=0, keepdims=True)
                vals_ref[c, lt, pl.ds(k, 1), :] = m
                if exact:
                    idx = jnp.min(jnp.where(s == m, row_t, f32(N_KEYS)), axis=0, keepdims=True)
                    idx_ref[c, lt, pl.ds(k, 1), :] = idx
                    hit = row_t == idx
                else:
                    hit = s == m
                return jnp.where(hit, neg_inf, s), hit

            def extract(k, st, remove_max=remove_max):
                sa, sb, rank = st
                sa, _ = remove_max(k, sa, 0)
                sb, hit = remove_max(k, sb, 1)
                return sa, sb, jnp.where(hit, jnp.asarray(k, f32), rank)

            sa_fin, sb_fin, rank = lax.fori_loop(
                0, PEER_TOPK, extract, (s_ref[0, :, ls], s_ref[1, :, ls], jnp.full((N_KEYS, LANE), f32(PEER_TOPK))))
            r1_ref[h, :, ls] = rank.astype(bf16)
            if not exact:
                bad_t = jnp.zeros((1, LANE), f32)
                for s_fin in (sa_fin, sb_fin):
                    removed = jnp.sum(jnp.where(s_fin == neg_inf, f32(1.0), f32(0.0)), axis=0, keepdims=True)
                    bad_t = bad_t + jnp.abs(removed - f32(PEER_TOPK))
                bad_tiles.append(bad_t)
        if not exact:
            bad = bad + jnp.concatenate(bad_tiles, axis=1)

        def lane_tiles(ref, c):
            return jnp.concatenate([ref[c, lt] for lt in range(tm // LANE)], axis=1)

        v0 = lane_tiles(vals_ref, 0)
        v1 = lane_tiles(vals_ref, 1)
        r = 0
        for a in range(PEER_TOPK):
            nb = _CAND_COUNTS[a]
            cs_ref[r:r + nb, :] = v0[a:a + 1, :] + v1[0:nb, :]
            r += nb
        cs_ref[_N_CAND:_N_CAND_PAD, :] = jnp.full((_N_CAND_PAD - _N_CAND, tm), neg_inf)
        m1 = v0[0:1, :] + v1[0:1, :]

        cs = cs_ref[...]
        if exact:
            def extract2(k, st):
                cs, sel, z = st
                m = jnp.max(cs, axis=0, keepdims=True)
                p = jnp.min(jnp.where(cs == m, pos, f32(_BIG)), axis=0, keepdims=True)
                hit = pos == p
                return jnp.where(hit, neg_inf, cs), jnp.where(hit, f32(1.0), sel), z + jnp.exp(m - m1)

            _, sel, z = lax.fori_loop(0, PEER_TOPK, extract2,
                                      (cs, jnp.zeros((_N_CAND_PAD, tm), f32), jnp.zeros((1, tm), f32)))
        else:
            larger = jnp.zeros((_N_CAND_PAD, tm), f32)
            for r in range(_N_CAND):
                larger = larger + jnp.where(cs[r:r + 1, :] > cs, f32(1.0), f32(0.0))
            sel = jnp.where((larger < f32(PEER_TOPK)) & (cs > neg_inf), f32(1.0), f32(0.0))
            z = jnp.sum(sel * jnp.exp(cs - m1), axis=0, keepdims=True)
            bad = bad + jnp.abs(jnp.sum(sel, axis=0, keepdims=True) - f32(PEER_TOPK))
        s0 = s_ref[0]
        idx0 = lane_tiles(idx_ref, 0)
        bound0 = jnp.zeros((N_KEYS, tm), f32)
        r = 0
        for a in range(PEER_TOPK):
            nb = _CAND_COUNTS[a]
            cnt = jnp.sum(sel[r:r + nb, :], axis=0, keepdims=True)
            is_a = (row == idx0[a:a + 1, :]) if exact else (s0 == v0[a:a + 1, :])
            bound0 = jnp.where(is_a, cnt, bound0)
            r += nb
        b0_ref[h] = _dup_bf16(bound0)
        e0_ref[h] = _dup_bf16(jnp.exp(s0 - v0[0:1, :]) / z)
        e1_ref[h] = jnp.exp(s_ref[1] - v1[0:1, :]).astype(bf16)
        return bad

    def head_body(h, carry):
        bad = head_compute(h, False)

        @pl.when(jnp.max(bad) > 0.0)
        def _():
            head_compute(h, True)

        return carry

    lax.fori_loop(0, PEER_HEADS, head_body, 0)


def _peer_main_kernel(ht_ref, u_ref, v_ref, b0_ref, r1_ref, e0_ref, e1_ref, x_ref, gn_ref, o_ref, hn_ref,
                      acc_ref, *, nblk, npieces):
    j = pl.program_id(1)

    @pl.when(j == 0)
    def _():
        acc_ref[...] = jnp.zeros_like(acc_ref)

    ht = ht_ref[...]
    bpp = nblk // npieces
    pc = bpp * N_KEYS
    for p in range(npieces):
        at = jnp.dot(u_ref[p * pc:(p + 1) * pc, :], ht, preferred_element_type=f32)
        ws = []
        for b in range(bpp):
            i = j * nblk + p * bpp + b
            a = at[b * N_KEYS:(b + 1) * N_KEYS, :]
            g = jnp.zeros(a.shape, bf16)
            for h in range(PEER_HEADS):
                bnd = _row_as_bf16_tile(b0_ref, h, i)
                e0 = _row_as_bf16_tile(e0_ref, h, i)
                g = g + jnp.where(r1_ref[h] < bnd, e1_ref[h] * e0, bf16(0.0))
            gelu = 0.5 * a * (1.0 + lax.erf(a * f32(math.sqrt(0.5))))
            ws.append(g * gelu.astype(bf16))
        wt = jnp.concatenate(ws, axis=0) if bpp > 1 else ws[0]
        acc_ref[...] += lax.dot_general(wt, v_ref[p * pc:(p + 1) * pc, :], TN_DIMS,
                                        preferred_element_type=f32)

    @pl.when(j == pl.num_programs(1) - 1)
    def _():
        xn = x_ref[...] + acc_ref[...]
        o_ref[...] = xn
        hn_ref[...] = _rms(xn, gn_ref[...])


def peer_ffn_residual(x, ht, wqt, sk, u, v, g_next, tm_sel=256, tm=256, ce=2048, npieces=8):
    t = x.shape[0]
    d = D_MODEL
    pos_list = []
    for a in range(PEER_TOPK):
        pos_list += [a * PEER_TOPK + b for b in range(_CAND_COUNTS[a])]
    pos_list += [_BIG] * (_N_CAND_PAD - _N_CAND)
    pos = jnp.broadcast_to(jnp.asarray(pos_list, f32)[:, None], (_N_CAND_PAD, tm_sel))
    sel_shape = jax.ShapeDtypeStruct((PEER_HEADS, N_KEYS, t), jnp.int32)
    sel_shape16 = jax.ShapeDtypeStruct((PEER_HEADS, N_KEYS, t), bf16)
    sel_spec = pl.BlockSpec((PEER_HEADS, N_KEYS, tm_sel), lambda i: (0, 0, i))
    b0, r1, e0, e1 = pl.pallas_call(
        _peer_select_kernel,
        grid=(t // tm_sel,),
        in_specs=[pl.BlockSpec((d, tm_sel), lambda i: (0, i)),
                  pl.BlockSpec((PEER_HEADS * 2 * PEER_HALF, d), lambda i: (0, 0)),
                  pl.BlockSpec((PEER_HEADS, 2, N_KEYS, PEER_HALF), lambda i: (0, 0, 0, 0)),
                  pl.BlockSpec((_N_CAND_PAD, tm_sel), lambda i: (0, 0))],
        out_specs=[sel_spec] * 4,
        out_shape=[sel_shape, sel_shape16, sel_shape, sel_shape16],
        scratch_shapes=[pltpu.VMEM((PEER_HEADS * 2 * PEER_HALF, tm_sel), bf16),
                        pltpu.VMEM((2, N_KEYS, tm_sel), f32),
                        pltpu.VMEM((2, tm_sel // LANE, PEER_TOPK, LANE), f32),
                        pltpu.VMEM((2, tm_sel // LANE, PEER_TOPK, LANE), f32),
                        pltpu.VMEM((_N_CAND_PAD, tm_sel), f32)],
        compiler_params=_cparams("parallel"),
        name="peer_select",
    )(ht, wqt, sk, pos)

    nblk = ce // N_KEYS
    sel_spec2 = pl.BlockSpec((PEER_HEADS, N_KEYS, tm), lambda i, j: (0, 0, i))
    return pl.pallas_call(
        functools.partial(_peer_main_kernel, nblk=nblk, npieces=npieces),
        grid=(t // tm, N_EXPERTS // ce),
        in_specs=[pl.BlockSpec((d, tm), lambda i, j: (0, i)),
                  pl.BlockSpec((ce, d), lambda i, j: (j, 0)),
                  pl.BlockSpec((ce, d), lambda i, j: (j, 0)),
                  sel_spec2, sel_spec2, sel_spec2, sel_spec2,
                  pl.BlockSpec((tm, d), lambda i, j: (i, 0)), pl.BlockSpec((1, d), lambda i, j: (0, 0))],
        out_specs=[pl.BlockSpec((tm, d), lambda i, j: (i, 0))] * 2,
        out_shape=[jax.ShapeDtypeStruct((t, d), f32)] * 2,
        scratch_shapes=[pltpu.VMEM((tm, d), f32)],
        compiler_params=_cparams("parallel", "arbitrary"),
        name="peer_main",
    )(ht, u, v, b0, r1, e0, e1, x, g_next.reshape(1, d))


IN_WIDTHS = (SSM_INNER, CONV_DIM, ATT_HEADS * ATT_HEAD_DIM, 2 * ATT_KV_HEADS * ATT_HEAD_DIM, LANE)


def _in_proj_kernel(x_ref, g_ref, w_ref, *out_refs):
    h = _rms(x_ref[...], g_ref[...])
    y = jnp.dot(h.astype(bf16), w_ref[...], preferred_element_type=f32)
    o = 0
    for ref in out_refs:
        w = ref.shape[1]
        ref[...] = y[:, o:o + w]
        o += w


def in_proj(x, g, w_in, tm=256):
    t, d = x.shape
    o1 = SSM_INNER
    o2 = o1 + CONV_DIM
    o3 = o2 + SSM_HEADS
    w = jnp.concatenate([w_in[:, :o2], w_in[:, o3:], w_in[:, o2:o3],
                         jnp.zeros((d, LANE - SSM_HEADS), w_in.dtype)], axis=1).astype(bf16)
    n = w.shape[1]
    return pl.pallas_call(
        _in_proj_kernel,
        grid=(t // tm,),
        in_specs=[pl.BlockSpec((tm, d), lambda i: (i, 0)), pl.BlockSpec((1, d), lambda i: (0, 0)),
                  pl.BlockSpec((d, n), lambda i: (0, 0))],
        out_specs=[pl.BlockSpec((tm, wd), lambda i: (i, 0)) for wd in IN_WIDTHS],
        out_shape=[jax.ShapeDtypeStruct((t, wd), f32) for wd in IN_WIDTHS],
        compiler_params=_cparams("parallel"),
        name="in_proj",
    )(x, g.reshape(1, d), w)


N_PAIRS = SSM_HEADS // 2


def _ssd_kernel(*refs, lin):
    (z_ref, xbc_ref, dt_ref, cs_ref, h0_ref, cw_ref, cb_ref, dtb_ref, aneg_ref, dsk_ref, gn_ref, selh_ref) = refs[:12]
    refs = refs[12:]
    y_ref, hfin_ref, xe_ref, hp_ref, ys_ref = refs[:5]
    lc = SSD_CHUNK
    c = pl.program_id(1)

    @pl.when(c == 0)
    def _():
        xe_ref[0:SUBLANE, :] = cs_ref[0]
        hp_ref[...] = h0_ref[0]

    if lin == lc:
        xe_ref[SUBLANE:SUBLANE + lc, :] = xbc_ref[...]
        z = z_ref[...]
        dt_raw = dt_ref[...]
    else:
        zpad_ref, dtpad_ref = refs[5:7]
        xe_ref[SUBLANE:SUBLANE + lc, :] = jnp.zeros((lc, CONV_DIM), f32)
        xe_ref[SUBLANE:SUBLANE + lin, :] = xbc_ref[...]
        zpad_ref[...] = jnp.zeros_like(zpad_ref)
        zpad_ref[0:lin, :] = z_ref[...]
        dtpad_ref[...] = jnp.zeros_like(dtpad_ref)
        dtpad_ref[0:lin, :] = dt_ref[...]
        z = zpad_ref[...]
        dt_raw = dtpad_ref[...]

    conv = cb_ref[...]
    for j in range(CONV_W):
        o = SUBLANE - (CONV_W - 1) + j
        conv = conv + xe_ref[o:o + lc, :] * cw_ref[j:j + 1, :]
    if lin == lc:
        xe_ref[0:SUBLANE, :] = xe_ref[lc:lc + SUBLANE, :]
    xc = _silu(conv)
    xs = xc[:, :SSM_INNER]
    bm = xc[:, SSM_INNER:SSM_INNER + SSM_GROUPS * SSM_STATE]
    cm = xc[:, SSM_INNER + SSM_GROUPS * SSM_STATE:]

    row = lax.broadcasted_iota(jnp.int32, (lc, LANE), 0)
    col = lax.broadcasted_iota(jnp.int32, (lc, LANE), 1)
    causal = row >= col
    lane_lo = col < HALF_LANE
    neg_inf = f32(-jnp.inf)

    x = dt_raw + dtb_ref[...]
    dt = jnp.maximum(x, 0.0) + jnp.log(1.0 + jnp.exp(-jnp.abs(x)))
    if lin != lc:
        dt = jnp.where(row < lin, dt, 0.0)
    la = dt * aneg_ref[...]
    acs = jnp.dot(causal.astype(f32), la, precision=HIGHEST, preferred_element_type=f32)
    acs_t = acs.T
    selh = selh_ref[...]
    dt_exp = jnp.dot(dt, selh, precision=HIGHEST, preferred_element_type=f32)
    acs_exp = jnp.dot(acs, selh, precision=HIGHEST, preferred_element_type=f32)
    alast = acs[lc - 1:lc, :]
    xdt = xs * dt_exp
    eacs = jnp.exp(acs_exp)
    xdt_end = (xdt * jnp.exp(acs_exp[lc - 1:lc, :] - acs_exp)).astype(bf16)
    dsk = dsk_ref[...]

    for g in range(SSM_GROUPS):
        cmg = cm[:, g * SSM_STATE:(g + 1) * SSM_STATE].astype(bf16)
        bmg = bm[:, g * SSM_STATE:(g + 1) * SSM_STATE].astype(bf16)
        cb = lax.dot_general(cmg, bmg, NT_DIMS, preferred_element_type=f32)
        for jj in range(N_PAIRS // SSM_GROUPS):
            j = g * (N_PAIRS // SSM_GROUPS) + jj
            sl = slice(j * LANE, (j + 1) * LANE)
            xdt_pair = xdt[:, sl]
            ydiag = jnp.zeros((lc, LANE), f32)
            for half in (0, 1):
                h = 2 * j + half
                seg = acs[:, h:h + 1] - acs_t[h:h + 1, :]
                dec = jnp.exp(jnp.where(causal, seg, neg_inf))
                m = (cb * dec).astype(bf16)
                keep = lane_lo if half == 0 else jnp.logical_not(lane_lo)
                xd = jnp.where(keep, xdt_pair, 0.0).astype(bf16)
                ydiag = ydiag + jnp.dot(m, xd, preferred_element_type=f32)
            hpj = hp_ref[j]
            yoff = lax.dot_general(cmg, hpj.astype(bf16), NT_DIMS, preferred_element_type=f32) * eacs[:, sl]
            s_new = lax.dot_general(xdt_end[:, sl], bmg, TN_DIMS, preferred_element_type=f32)
            dl = jnp.where(row < SSM_HEAD_DIM, alast[:, 2 * j:2 * j + 1], alast[:, 2 * j + 1:2 * j + 2])
            hp_ref[j] = hpj * jnp.exp(dl) + s_new
            ys_ref[:, sl] = ydiag + yoff + dsk[:, sl] * xs[:, sl]

    y = ys_ref[...] * _silu(z)
    gs = SSM_INNER // SSM_GROUPS
    gn = gn_ref[...]
    for g in range(SSM_GROUPS):
        yg = y[:, g * gs:(g + 1) * gs]
        yg = yg * lax.rsqrt(jnp.mean(yg * yg, axis=-1, keepdims=True) + EPS) * gn[:, g * gs:(g + 1) * gs]
        y_ref[:, g * gs:(g + 1) * gs] = yg[0:lin, :]

    @pl.when(c == pl.num_programs(1) - 1)
    def _():
        hfin_ref[0] = hp_ref[...]


def ssd_mixer(z, xbc, dt, conv_state, ssm_state, params, *, batch, seq, row0):
    conv_w, conv_b, dt_bias, a_neg, d_skip, gnorm, selh = params
    lc = SSD_CHUNK
    lin = min(seq, lc)
    nc = seq // lin
    blk0 = row0 // lin
    cs = jnp.pad(conv_state, ((0, 0), (SUBLANE - (CONV_W - 1), 0), (0, 0)))
    h0 = ssm_state.reshape(batch, N_PAIRS, LANE, SSM_STATE)

    def rows(w):
        return pl.BlockSpec((lin, w), lambda b, c: (blk0 + b * nc + c, 0))

    def const(shape):
        return pl.BlockSpec(shape, lambda b, c: (0,) * len(shape))

    in_specs = [rows(SSM_INNER), rows(CONV_DIM), rows(LANE),
                pl.BlockSpec((1, SUBLANE, CONV_DIM), lambda b, c: (b, 0, 0)),
                pl.BlockSpec((1, N_PAIRS, LANE, SSM_STATE), lambda b, c: (b, 0, 0, 0)),
                const((CONV_W, CONV_DIM)), const((1, CONV_DIM)), const((1, LANE)), const((1, LANE)),
                const((1, SSM_INNER)), const((1, SSM_INNER)), const((LANE, SSM_INNER))]
    args = [z, xbc, dt, cs, h0, conv_w, conv_b, dt_bias, a_neg, d_skip, gnorm, selh]
    scratch = [pltpu.VMEM((lc + 2 * SUBLANE, CONV_DIM), f32), pltpu.VMEM((N_PAIRS, LANE, SSM_STATE), f32),
               pltpu.VMEM((lc, SSM_INNER), f32)]
    if lin != lc:
        scratch += [pltpu.VMEM((lc, SSM_INNER), f32), pltpu.VMEM((lc, LANE), f32)]
    y, h_fin = pl.pallas_call(
        functools.partial(_ssd_kernel, lin=lin),
        grid=(batch, nc),
        in_specs=in_specs,
        out_specs=[pl.BlockSpec((lin, SSM_INNER), lambda b, c: (b * nc + c, 0)),
                   pl.BlockSpec((1, N_PAIRS, LANE, SSM_STATE), lambda b, c: (b, 0, 0, 0))],
        out_shape=[jax.ShapeDtypeStruct((batch * seq, SSM_INNER), f32),
                   jax.ShapeDtypeStruct((batch, N_PAIRS, LANE, SSM_STATE), f32)],
        scratch_shapes=scratch,
        compiler_params=_cparams("parallel", "arbitrary"),
        name="ssd_mixer",
    )(*args)
    return y, h_fin.reshape(batch, SSM_HEADS, SSM_HEAD_DIM, SSM_STATE)


def _swa_kernel(*refs, lq, masked_first):
    q_ref, kp_ref, vp_ref, kc_ref, vc_ref, bias_ref, sink_ref = refs[:7]
    refs = refs[7:]
    o_ref = refs[0]
    n = pl.program_id(1)
    q = q_ref[...]
    if lq == ATT_BLOCK:
        kc = kc_ref[...]
        vc = vc_ref[...]
    else:
        kpad_ref, vpad_ref = refs[1:3]
        kpad_ref[...] = jnp.zeros_like(kpad_ref)
        vpad_ref[...] = jnp.zeros_like(vpad_ref)
        kpad_ref[0:lq, :] = kc_ref[...]
        vpad_ref[0:lq, :] = vc_ref[...]
        kc = kpad_ref[...]
        vc = vpad_ref[...]
    kp = kp_ref[...]
    vp = vp_ref[...]
    lane_lo = lax.broadcasted_iota(jnp.int32, (ATT_BLOCK, LANE), 1) < HALF_LANE
    lane_lo_q = lax.broadcasted_iota(jnp.int32, (lq, LANE), 1) < HALF_LANE
    neg_inf = f32(-jnp.inf)
    scale = f32(ATT_HEAD_DIM ** -0.5)

    for g in range(ATT_KV_HEADS):
        sl = slice((g // 2) * LANE, (g // 2 + 1) * LANE)
        odd = g % 2 == 1

        def kpad(k):
            pair = k[:, sl]
            if odd:
                pair = pltpu.roll(pair, HALF_LANE, 1)
            return jnp.where(lane_lo, pair, 0.0).astype(bf16)

        def vdup(v):
            pair = v[:, sl]
            rolled = pltpu.roll(pair, HALF_LANE, 1)
            return (jnp.where(lane_lo, rolled, pair) if odd else jnp.where(lane_lo, pair, rolled)).astype(bf16)

        qp0 = q[:, (2 * g) * LANE:(2 * g + 1) * LANE]
        qp1 = q[:, (2 * g + 1) * LANE:(2 * g + 2) * LANE]
        qg = jnp.concatenate([qp0, pltpu.roll(qp0, HALF_LANE, 1), qp1, pltpu.roll(qp1, HALF_LANE, 1)],
                             axis=0).astype(bf16)
        bias = bias_ref[g]
        sp = lax.dot_general(qg, kpad(kp), NT_DIMS, preferred_element_type=f32) * scale + bias[:, :WINDOW]
        sc = lax.dot_general(qg, kpad(kc), NT_DIMS, preferred_element_type=f32) * scale + bias[:, WINDOW:]
        if masked_first:
            sp = jnp.where(n > 0, sp, neg_inf)
        sink = sink_ref[g][:, 0:1]
        m = jnp.maximum(jnp.maximum(jnp.max(sp, axis=-1, keepdims=True), jnp.max(sc, axis=-1, keepdims=True)), sink)
        pp = jnp.exp(sp - m)
        pc = jnp.exp(sc - m)
        denom = jnp.sum(pp, axis=-1, keepdims=True) + jnp.sum(pc, axis=-1, keepdims=True) + jnp.exp(sink - m)
        og = (jnp.dot(pp.astype(bf16), vdup(vp), preferred_element_type=f32)
              + jnp.dot(pc.astype(bf16), vdup(vc), preferred_element_type=f32)) / denom
        o_ref[:, (2 * g) * LANE:(2 * g + 1) * LANE] = jnp.where(lane_lo_q, og[0:lq], og[lq:2 * lq])
        o_ref[:, (2 * g + 1) * LANE:(2 * g + 2) * LANE] = jnp.where(lane_lo_q, og[2 * lq:3 * lq], og[3 * lq:4 * lq])


def _rel_bucket(dist):
    exact = REL_BUCKETS // 2
    d = jnp.maximum(dist, 0)
    large = exact + (jnp.log(jnp.maximum(d, 1).astype(f32) / exact)
                     / math.log(REL_MAX_DIST / exact) * (REL_BUCKETS - exact)).astype(jnp.int32)
    large = jnp.minimum(large, REL_BUCKETS - 1)
    return jnp.where(d < exact, d, large)


def _swa_tables(rel_bias, sinks, lq):
    qi = jnp.arange(lq)[:, None]
    kj = jnp.arange(2 * ATT_BLOCK)[None, :]
    dist = qi + WINDOW - kj
    band = (dist >= 0) & (dist <= WINDOW)
    onehot = (_rel_bucket(dist)[..., None] == jnp.arange(REL_BUCKETS)).astype(f32)
    bias = jnp.einsum('qkb,bh->qkh', onehot, rel_bias, precision=HIGHEST)
    bias = jnp.where(band[..., None], bias, -jnp.inf)
    bias = jnp.transpose(bias, (2, 0, 1)).reshape(ATT_KV_HEADS, ATT_GQA * lq, 2 * ATT_BLOCK)
    sink = jnp.broadcast_to(sinks.reshape(ATT_KV_HEADS, ATT_GQA, 1, 1), (ATT_KV_HEADS, ATT_GQA, lq, LANE))
    return bias, sink.reshape(ATT_KV_HEADS, ATT_GQA * lq, LANE)


def swa_mixer(q, kv, k_prev, v_prev, rel_bias, sinks, *, batch, seq, row0):
    lq = min(seq, ATT_BLOCK)
    nb = seq // lq
    blk0 = row0 // lq
    kvw = ATT_KV_HEADS * ATT_HEAD_DIM
    bias, sink = _swa_tables(rel_bias, sinks, lq)
    cur_k = pl.BlockSpec((lq, kvw), lambda b, n: (blk0 + b * nb + n, 0))
    cur_v = pl.BlockSpec((lq, kvw), lambda b, n: (blk0 + b * nb + n, 1))
    if k_prev is None:
        prev_k = pl.BlockSpec((WINDOW, kvw), lambda b, n: (blk0 + b * nb + jnp.maximum(n - 1, 0), 0))
        prev_v = pl.BlockSpec((WINDOW, kvw), lambda b, n: (blk0 + b * nb + jnp.maximum(n - 1, 0), 1))
        kp_arr, vp_arr = kv, kv
    else:
        prev_k = pl.BlockSpec((WINDOW, kvw), lambda b, n: (b, 0))
        prev_v = prev_k
        kp_arr = k_prev.reshape(batch * WINDOW, kvw)
        vp_arr = v_prev.reshape(batch * WINDOW, kvw)
    rows = pl.BlockSpec((lq, ATT_HEADS * ATT_HEAD_DIM), lambda b, n: (blk0 + b * nb + n, 0))
    in_specs = [rows, prev_k, prev_v, cur_k, cur_v,
                pl.BlockSpec(bias.shape, lambda b, n: (0, 0, 0)), pl.BlockSpec(sink.shape, lambda b, n: (0, 0, 0))]
    args = [q, kp_arr, vp_arr, kv, kv, bias, sink]
    scratch = [] if lq == ATT_BLOCK else [pltpu.VMEM((ATT_BLOCK, kvw), f32), pltpu.VMEM((ATT_BLOCK, kvw), f32)]
    return pl.pallas_call(
        functools.partial(_swa_kernel, lq=lq, masked_first=k_prev is None),
        grid=(batch, nb),
        in_specs=in_specs,
        out_specs=pl.BlockSpec((lq, ATT_HEADS * ATT_HEAD_DIM), lambda b, n: (b * nb + n, 0)),
        out_shape=jax.ShapeDtypeStruct((batch * seq, ATT_HEADS * ATT_HEAD_DIM), f32),
        scratch_shapes=scratch,
        compiler_params=_cparams("parallel", "arbitrary"),
        name="swa_mixer",
    )(*args)


def _out_proj_kernel(yap_ref, ybp_ref, yas_ref, ybs_ref, w_ref, x_ref, g_ref, o_ref, ht_ref, *, n_prompt):
    ka = yap_ref.shape[1]

    def body(ya_ref, yb_ref):
        acc = jnp.dot(ya_ref[...].astype(bf16), w_ref[0:ka, :], preferred_element_type=f32)
        acc = acc + jnp.dot(yb_ref[...].astype(bf16), w_ref[ka:, :], preferred_element_type=f32)
        xn = x_ref[...] + acc
        o_ref[...] = xn
        ht_ref[...] = _rms(xn, g_ref[...]).T.astype(bf16)

    @pl.when(pl.program_id(0) < n_prompt)
    def _():
        body(yap_ref, ybp_ref)

    @pl.when(pl.program_id(0) >= n_prompt)
    def _():
        body(yas_ref, ybs_ref)


def out_proj(ya_p, yb_p, ya_s, yb_s, w, x, g, tm=512):
    t, d = x.shape
    ka, kb = ya_p.shape[1], yb_p.shape[1]
    n_p = ya_p.shape[0] // tm
    n_s = ya_s.shape[0] // tm

    def p_rows(k):
        return pl.BlockSpec((tm, k), lambda i: (jnp.minimum(i, n_p - 1), 0))

    def s_rows(k):
        return pl.BlockSpec((tm, k), lambda i: (jnp.maximum(i - n_p, 0), 0))

    return pl.pallas_call(
        functools.partial(_out_proj_kernel, n_prompt=n_p),
        grid=(n_p + n_s,),
        in_specs=[p_rows(ka), p_rows(kb), s_rows(ka), s_rows(kb),
                  pl.BlockSpec((ka + kb, d), lambda i: (0, 0)), pl.BlockSpec((tm, d), lambda i: (i, 0)),
                  pl.BlockSpec((1, d), lambda i: (0, 0))],
        out_specs=[pl.BlockSpec((tm, d), lambda i: (i, 0)), pl.BlockSpec((d, tm), lambda i: (0, i))],
        out_shape=[jax.ShapeDtypeStruct((t, d), f32), jax.ShapeDtypeStruct((d, t), bf16)],
        compiler_params=_cparams("arbitrary"),
        name="out_proj",
    )(ya_p, yb_p, ya_s, yb_s, w, x, g.reshape(1, d))


def _gated_out_kernel(yp_ref, gp_ref, ys_ref, gs_ref, w_ref, x_ref, gn_ref, o_ref, ht_ref, *, n_prompt):
    def body(y_ref, g_ref):
        a = (y_ref[...] * g_ref[...]).astype(bf16)
        xn = x_ref[...] + jnp.dot(a, w_ref[...], preferred_element_type=f32)
        o_ref[...] = xn
        ht_ref[...] = _rms(xn, gn_ref[...]).T.astype(bf16)

    @pl.when(pl.program_id(0) < n_prompt)
    def _():
        body(yp_ref, gp_ref)

    @pl.when(pl.program_id(0) >= n_prompt)
    def _():
        body(ys_ref, gs_ref)


def gated_out_proj(y_p, g_p, y_s, g_s, w, x, gn, tm=512):
    t, d = x.shape
    n_p = y_p.shape[0] // tm
    n_s = y_s.shape[0] // tm
    rows = pl.BlockSpec((tm, d), lambda i: (i, 0))
    p_rows = pl.BlockSpec((tm, d), lambda i: (jnp.minimum(i, n_p - 1), 0))
    s_rows = pl.BlockSpec((tm, d), lambda i: (jnp.maximum(i - n_p, 0), 0))
    return pl.pallas_call(
        functools.partial(_gated_out_kernel, n_prompt=n_p),
        grid=(n_p + n_s,),
        in_specs=[p_rows, p_rows, s_rows, s_rows,
                  pl.BlockSpec((d, d), lambda i: (0, 0)), rows, pl.BlockSpec((1, d), lambda i: (0, 0))],
        out_specs=[rows, pl.BlockSpec((d, tm), lambda i: (0, i))],
        out_shape=[jax.ShapeDtypeStruct((t, d), f32), jax.ShapeDtypeStruct((d, t), bf16)],
        compiler_params=_cparams("arbitrary"),
        name="rwkv_out",
    )(y_p, g_p, y_s, g_s, w, x, gn.reshape(1, d))


LORA_PAD = 2 * LANE


def _softplus(x):
    return jnp.maximum(x, 0.0) + jnp.log(1.0 + jnp.exp(-jnp.abs(x)))


def _rwkv_pre_kernel(h_ref, p_ref, mu_ref, wrkv_ref, l1_ref, l2_ref, vec_ref,
                     r_ref, d_ref, k_ref, v_ref, kk_ref, a_ref, g_ref):
    h = h_ref[...]
    xx = p_ref[...] - h

    def mix(j):
        return (h + xx * mu_ref[j:j + 1, :]).astype(bf16)

    def mm(a, w):
        return jnp.dot(a, w, preferred_element_type=f32)

    r_ref[...] = mm(mix(0), wrkv_ref[0])
    k = mm(mix(2), wrkv_ref[1])
    v_ref[...] = mm(mix(3), wrkv_ref[2])
    wl = vec_ref[0:1, :] + mm(jnp.tanh(mm(mix(1), l1_ref[0])).astype(bf16), l2_ref[0])
    al = vec_ref[1:2, :] + mm(mm(mix(4), l1_ref[1]).astype(bf16), l2_ref[1])
    g_ref[...] = mm(jax.nn.sigmoid(mm(mix(5), l1_ref[2])).astype(bf16), l2_ref[2])
    w = -_softplus(-wl) - 0.5
    d_ref[...] = jnp.exp(-jnp.exp(w))
    a = jax.nn.sigmoid(al)
    a_ref[...] = a
    kk_ref[...] = k * vec_ref[2:3, :]
    k_ref[...] = k * (1.0 + (a - 1.0) * vec_ref[3:4, :])


def rwkv_pre(h, prev, mu, w_rkv, lora1, lora2, vecs, *, row0, tm=256):
    t, d = prev.shape
    blk0 = row0 // tm
    rows = pl.BlockSpec((tm, d), lambda i: (i, 0))

    def const(a):
        return pl.BlockSpec(a.shape, lambda i: (0,) * a.ndim)

    return pl.pallas_call(
        _rwkv_pre_kernel,
        grid=(t // tm,),
        in_specs=[pl.BlockSpec((tm, d), lambda i: (blk0 + i, 0)), rows,
                  const(mu), const(w_rkv), const(lora1), const(lora2), const(vecs)],
        out_specs=[rows] * 7,
        out_shape=[jax.ShapeDtypeStruct((t, d), f32)] * 7,
        compiler_params=_cparams("parallel"),
        name="rwkv_pre",
    )(h, prev, mu, w_rkv, lora1, lora2, vecs)


RWKV_VGROUP = 8


RWKV_TB = 16
N_SEQ_TILE = LANE // RWKV_HEADS


def _swap_list_and_lane(arrs):
    n = len(arrs)
    lane = lax.broadcasted_iota(jnp.int32, arrs[0].shape, 1)
    s = n // 2
    while s >= 1:
        upper = (lane & s) != 0
        new = list(arrs)
        for i in range(n):
            if i & s == 0:
                a, b = arrs[i], arrs[i | s]
                new[i] = jnp.where(upper, pltpu.roll(b, s, 1), a)
                new[i | s] = jnp.where(upper, b, pltpu.roll(a, LANE - s, 1))
        arrs = new
        s //= 2
    return arrs


def _natural_to_chain(x_ref, dst_ref, q):
    x2 = x_ref[...].reshape(N_SEQ_TILE * RWKV_TB, D_MODEL)
    xt = x2.T
    arrs = _swap_list_and_lane([xt[h * RWKV_HEAD:(h + 1) * RWKV_HEAD, :] for h in range(RWKV_HEADS)])
    for t in range(RWKV_TB):
        dst_ref[q, t] = arrs[t]


def _chain_to_natural(src_ref, y_ref):
    arrs = _swap_list_and_lane([src_ref[t] for t in range(RWKV_TB)])
    xt = jnp.concatenate(arrs, axis=0)
    y_ref[...] = xt.T.reshape(N_SEQ_TILE, RWKV_TB, D_MODEL)


def _rwkv_scan_kernel(r_ref, d_ref, k_ref, v_ref, kk_ref, a_ref, s0_ref, tab_ref, y_ref, sfin_ref,
                      s_ref, q_ref, ys_ref, *, natural):
    tb = q_ref.shape[1]

    @pl.when(pl.program_id(1) == 0)
    def _():
        s_ref[...] = s0_ref[...]

    for q, ref in enumerate((r_ref, d_ref, k_ref, v_ref, kk_ref, a_ref)):
        if natural:
            _natural_to_chain(ref, q_ref, q)
        else:
            q_ref[q] = ref[...]
    r_q, d_q, k_q, v_q, kk_q, a_q = [q_ref.at[q] for q in range(6)]

    def step(t, carry):
        kkr = kk_q[t]
        nrm = jnp.sqrt(jnp.sum(kkr * kkr, axis=0, keepdims=True))
        kk = kkr / jnp.maximum(nrm, 1e-12)
        d = d_q[t]
        kv = k_q[t]
        r = r_q[t]
        b = kk * a_q[t]

        def vgroup(g, c2):
            v0 = pl.multiple_of(g * RWKV_VGROUP, RWKV_VGROUP)
            vrows = v_q[t, pl.ds(v0, RWKV_VGROUP), :]
            ys = []
            for vi in range(RWKV_VGROUP):
                sv = s_ref[v0 + vi]
                sa = -jnp.sum(sv * kk, axis=0, keepdims=True)
                sn = sv * d + sa * b + vrows[vi:vi + 1, :] * kv
                s_ref[v0 + vi] = sn
                ys.append(jnp.sum(sn * r, axis=0, keepdims=True))
            ys_ref[t, pl.ds(v0, RWKV_VGROUP), :] = jnp.concatenate(ys, axis=0)
            return c2

        lax.fori_loop(0, RWKV_HEAD // RWKV_VGROUP, vgroup, 0)
        y = ys_ref[t]
        mean = jnp.mean(y, axis=0, keepdims=True)
        yc = y - mean
        var = jnp.mean(yc * yc, axis=0, keepdims=True)
        bonus = jnp.sum(r * kv * tab_ref[0], axis=0, keepdims=True)
        ys_ref[t] = yc * lax.rsqrt(var + GN_EPS) * tab_ref[1] + tab_ref[2] + bonus * v_q[t]
        return carry

    lax.fori_loop(0, tb, step, 0)
    if natural:
        _chain_to_natural(ys_ref, y_ref)
    else:
        y_ref[...] = ys_ref[...]

    @pl.when(pl.program_id(1) == pl.num_programs(1) - 1)
    def _():
        sfin_ref[...] = s_ref[...]


def rwkv_scan(parts, s0, tab, *, natural):
    hd = RWKV_HEAD
    c = s0.shape[-1]
    if natural:
        nb, L, dm = parts[0].shape
        tb = RWKV_TB
        seq_spec = pl.BlockSpec((N_SEQ_TILE, tb, dm), lambda i, j: (i, j, 0))
        y_shape = jax.ShapeDtypeStruct((nb, L, dm), f32)
    else:
        L = parts[0].shape[0]
        tb = min(L, RWKV_TB)
        seq_spec = pl.BlockSpec((tb, hd, LANE), lambda i, j: (j, 0, i))
        y_shape = jax.ShapeDtypeStruct((L, hd, c), f32)
    st_spec = pl.BlockSpec((hd, hd, LANE), lambda i, j: (0, 0, i))
    return pl.pallas_call(
        functools.partial(_rwkv_scan_kernel, natural=natural),
        grid=(c // LANE, L // tb),
        in_specs=[seq_spec] * 6 + [st_spec, pl.BlockSpec(tab.shape, lambda i, j: (0, 0, 0))],
        out_specs=[seq_spec, st_spec],
        out_shape=[y_shape, jax.ShapeDtypeStruct((hd, hd, c), f32)],
        scratch_shapes=[pltpu.VMEM((hd, hd, LANE), f32), pltpu.VMEM((6, tb, hd, LANE), f32),
                        pltpu.VMEM((tb, hd, LANE), f32)],
        compiler_params=_cparams("parallel", "arbitrary"),
        name="rwkv_scan",
    )(*parts, s0, tab)


def _rwkv_core(parts, wkv, tab):
    b, L, _ = parts[0].shape
    s0 = jnp.transpose(wkv, (2, 3, 0, 1)).reshape(RWKV_HEAD, RWKV_HEAD, b * RWKV_HEADS)
    if b % N_SEQ_TILE == 0 and L % RWKV_TB == 0:
        y, s_fin = rwkv_scan(parts, s0, tab, natural=True)
        y = y.reshape(b * L, D_MODEL)
    else:
        def to_chain(t):
            t = jnp.transpose(t.reshape(b, L, RWKV_HEADS, RWKV_HEAD), (1, 3, 0, 2))
            return t.reshape(L, RWKV_HEAD, b * RWKV_HEADS)

        y, s_fin = rwkv_scan([to_chain(t) for t in parts], s0, tab, natural=False)
        y = jnp.transpose(y.reshape(L, RWKV_HEAD, b, RWKV_HEADS), (2, 0, 3, 1)).reshape(b * L, D_MODEL)
    s_fin = jnp.transpose(s_fin.reshape(RWKV_HEAD, RWKV_HEAD, b, RWKV_HEADS), (2, 3, 0, 1))
    return y, s_fin


def kernel(x_prompt, x_sample, state_ssm, state_conv, cache_swa_k, cache_swa_v, state_wkv, state_shift, rel_bias, norm_mix, norm_ffn, norm_final, mix_w_in, ssd_conv_w, ssd_conv_b, ssd_dt_bias, ssd_a_log, ssd_d_skip, ssd_gnorm, attn_sinks, mix_w_out, rwkv_mu, rwkv_w0, rwkv_w1, rwkv_w2, rwkv_a0, rwkv_a1, rwkv_a2, rwkv_g1, rwkv_g2, rwkv_k_k, rwkv_k_a, rwkv_r_k, rwkv_w_rkv, rwkv_w_o, rwkv_ln_w, rwkv_ln_b, peer_w_q, peer_sub_keys, peer_u, peer_v):
    bp, lp, d = x_prompt.shape
    bs, ls, _ = x_sample.shape
    tp = bp * lp
    ts = bs * ls
    tt = tp + ts
    x = jnp.concatenate([x_prompt.reshape(tp, d), x_sample.reshape(ts, d)], axis=0)

    def split(t):
        return t[:tp].reshape(bp, lp, -1), t[tp:].reshape(bs, ls, -1)

    def zero_state(a):
        return jnp.zeros((bp,) + a.shape[2:], a.dtype)

    def peer(x, ht, layer, g_next):
        wqt = peer_w_q[layer].T.astype(bf16)
        return peer_ffn_residual(x, ht, wqt, peer_sub_keys[layer].astype(bf16),
                                 peer_u[layer].astype(bf16), peer_v[layer].astype(bf16), g_next)

    z, xbc, q, kv, dt = in_proj(x, norm_mix[0], mix_w_in[0])
    pad16 = (0, LANE - SSM_HEADS)
    selh = (jnp.arange(LANE)[:, None] == jnp.arange(SSM_INNER)[None, :] // SSM_HEAD_DIM).astype(f32)
    ssd_params = (ssd_conv_w[0], ssd_conv_b[0].reshape(1, CONV_DIM),
                  jnp.pad(ssd_dt_bias[0], pad16).reshape(1, LANE),
                  jnp.pad(-jnp.exp(ssd_a_log[0]), pad16).reshape(1, LANE),
                  jnp.repeat(ssd_d_skip[0], SSM_HEAD_DIM).reshape(1, SSM_INNER),
                  ssd_gnorm[0].reshape(1, SSM_INNER), selh)
    y_ssd_p, ssm_p = ssd_mixer(z, xbc, dt, zero_state(state_conv), zero_state(state_ssm), ssd_params,
                               batch=bp, seq=lp, row0=0)
    y_ssd_s, ssm_s = ssd_mixer(z, xbc, dt, state_conv[0], state_ssm[0], ssd_params, batch=bs, seq=ls, row0=tp)
    o_att_p = swa_mixer(q, kv, None, None, rel_bias, attn_sinks[0], batch=bp, seq=lp, row0=0)
    o_att_s = swa_mixer(q, kv, cache_swa_k[0], cache_swa_v[0], rel_bias, attn_sinks[0], batch=bs, seq=ls, row0=tp)
    x, ht = out_proj(y_ssd_p, o_att_p, y_ssd_s, o_att_s, mix_w_out[0].astype(bf16), x, norm_ffn[0])
    xbc_p, xbc_s = split(xbc)
    conv_p = xbc_p[:, -(CONV_W - 1):]
    conv_s = jnp.concatenate([state_conv[0], xbc_s], axis=1)[:, -(CONV_W - 1):]
    kv_p, kv_s = split(kv)
    kvw = ATT_KV_HEADS * ATT_HEAD_DIM
    hshape = (ATT_KV_HEADS, ATT_HEAD_DIM)
    k_p = kv_p[:, -WINDOW:, :kvw].reshape(bp, WINDOW, *hshape)
    v_p = kv_p[:, -WINDOW:, kvw:].reshape(bp, WINDOW, *hshape)
    k_s = jnp.concatenate([cache_swa_k[0], kv_s[..., :kvw].reshape(bs, ls, *hshape)], axis=1)[:, -WINDOW:]
    v_s = jnp.concatenate([cache_swa_v[0], kv_s[..., kvw:].reshape(bs, ls, *hshape)], axis=1)[:, -WINDOW:]
    x, h = peer(x, ht, 0, norm_mix[1])

    h_p, h_s = split(h)
    prev_p = jnp.concatenate([jnp.zeros((bp, 1, d), f32), h_p[:, :-1]], axis=1)
    prev_s = jnp.concatenate([state_shift[0][:, None], h_s[:, :-1]], axis=1)
    def lora_pair(w1, w2):
        pad = LORA_PAD - w1.shape[1]
        return jnp.pad(w1, ((0, 0), (0, pad))), jnp.pad(w2, ((0, pad), (0, 0)))

    pairs = [lora_pair(rwkv_w1[0], rwkv_w2[0]), lora_pair(rwkv_a1[0], rwkv_a2[0]), lora_pair(rwkv_g1[0], rwkv_g2[0])]
    lora1 = jnp.stack([p[0] for p in pairs]).astype(bf16)
    lora2 = jnp.stack([p[1] for p in pairs]).astype(bf16)
    vecs = jnp.stack([rwkv_w0[0], rwkv_a0[0], rwkv_k_k[0], rwkv_k_a[0]])
    pre_args = (rwkv_mu[0], rwkv_w_rkv[0].astype(bf16), lora1, lora2, vecs)
    *parts_p, g_p = rwkv_pre(h, prev_p.reshape(tp, d), *pre_args, row0=0)
    *parts_s, g_s = rwkv_pre(h, prev_s.reshape(ts, d), *pre_args, row0=tp)

    def chain_table(p):
        return jnp.tile(p.reshape(RWKV_HEADS, RWKV_HEAD).T, (1, LANE // RWKV_HEADS))

    tab = jnp.stack([chain_table(rwkv_r_k[0].reshape(-1)), chain_table(rwkv_ln_w[0]), chain_table(rwkv_ln_b[0])])
    y_p, wkv_p = _rwkv_core([t.reshape(bp, lp, d) for t in parts_p], zero_state(state_wkv), tab)
    y_s, wkv_s = _rwkv_core([t.reshape(bs, ls, d) for t in parts_s], state_wkv[0], tab)
    x, ht = gated_out_proj(y_p, g_p, y_s, g_s, rwkv_w_o[0].astype(bf16), x, norm_ffn[1])
    shift_p, shift_s = h_p[:, -1], h_s[:, -1]
    _, y = peer(x, ht, 1, norm_final)
    y_p, y_s = split(y)
    return (y_p, y_s, ssm_p[None], conv_p[None], k_p[None], v_p[None], wkv_p[None], shift_p[None],
            ssm_s[None], conv_s[None], k_s[None], v_s[None], wkv_s[None], shift_s[None])
```

```python
import functools
import math

import jax
import jax.numpy as jnp
from jax import lax
from jax.experimental import pallas as pl
from jax.experimental.pallas import tpu as pltpu

f32 = jnp.float32
bf16 = jnp.bfloat16

D_MODEL = 1024
PAST_LEN = 16384
SSM_HEAD_DIM = 64
SSM_HEADS = 16
SSM_INNER = 1024
SSM_GROUPS = 2
SSM_STATE = 128
CONV_W = 4
CONV_DIM = 1536
SSD_CHUNK = 128
ATT_HEAD_DIM = 64
ATT_HEADS = 16
ATT_KV_HEADS = 4
ATT_GQA = 4
WINDOW = 128
ATT_BLOCK = 128
REL_BUCKETS = 32
REL_MAX_DIST = 128
RWKV_HEAD = 64
RWKV_HEADS = 16
PEER_HEADS = 8
N_KEYS = 128
N_EXPERTS = N_KEYS * N_KEYS
PEER_TOPK = 16
PEER_HALF = 128
EPS = 1e-5
GN_EPS = 64e-5

LANE = 128
SUBLANE = 8
HALF_LANE = LANE // 2
VMEM_LIMIT = 56 * 2 ** 20
HIGHEST = lax.Precision.HIGHEST
NT_DIMS = (((1,), (1,)), ((), ()))
TN_DIMS = (((0,), (0,)), ((), ()))

_CAND_COUNTS = [PEER_TOPK // (a + 1) for a in range(PEER_TOPK)]
_N_CAND = sum(_CAND_COUNTS)
_N_CAND_PAD = -(-_N_CAND // 8) * 8
_BIG = 1e9


def _cparams(*sem):
    return pltpu.CompilerParams(dimension_semantics=sem, vmem_limit_bytes=VMEM_LIMIT)


def _rms(x, g):
    return x * lax.rsqrt(jnp.mean(x * x, axis=-1, keepdims=True) + EPS) * g


def _silu(x):
    return x * jax.nn.sigmoid(x)


BF16_ROWS = 2 * SUBLANE


def _dup_bf16(v):
    u = pltpu.bitcast(v.astype(bf16).astype(f32), jnp.int32)
    return u | lax.shift_right_logical(u, jnp.int32(16))


GATE_ROWS = N_KEYS


def _row_as_bf16_tile(ref, h, i, rows):
    row = ref[h, pl.ds(i, 1), :]
    tile = pltpu.bitcast(jnp.broadcast_to(row, (SUBLANE, row.shape[1])), bf16)
    return pltpu.repeat(tile, rows // BF16_ROWS, axis=0)


def _peer_select_kernel(ht_ref, wqt_ref, sk_ref, pos_ref, b0_ref, r1_ref, e0_ref, e1_ref,
                        qt_ref, s_ref, vals_ref, idx_ref, cs_ref):
    tm = ht_ref.shape[1]
    qt_ref[...] = jnp.dot(wqt_ref[...], ht_ref[...], preferred_element_type=f32).astype(bf16)
    row = lax.broadcasted_iota(jnp.int32, (N_KEYS, tm), 0).astype(f32)
    pos = pos_ref[...]
    neg_inf = f32(-jnp.inf)

    def head_compute(h, exact):
        bad = jnp.zeros((1, tm), f32)
        for c in (0, 1):
            off = pl.multiple_of((h * 2 + c) * PEER_HALF, PEER_HALF)
            s_ref[c] = jnp.dot(sk_ref[h, c], qt_ref[pl.ds(off, PEER_HALF), :], preferred_element_type=f32)

        bad_tiles = []
        for lt in range(tm // LANE):
            ls = slice(lt * LANE, (lt + 1) * LANE)
            row_t = row[:, :LANE]

            def remove_max(k, s, c, lt=lt, row_t=row_t):
                m = jnp.max(s, axis=0, keepdims=True)
                vals_ref[c, lt, pl.ds(k, 1), :] = m
                if exact:
                    idx = jnp.min(jnp.where(s == m, row_t, f32(N_KEYS)), axis=0, keepdims=True)
                    idx_ref[c, lt, pl.ds(k, 1), :] = idx
                    hit = row_t == idx
                else:
                    hit = s == m
                return jnp.where(hit, neg_inf, s), hit

            def extract(k, st, remove_max=remove_max):
                sa, sb, rank = st
                sa, _ = remove_max(k, sa, 0)
                sb, hit = remove_max(k, sb, 1)
                return sa, sb, jnp.where(hit, jnp.asarray(k, f32), rank)

            sa_fin, sb_fin, rank = lax.fori_loop(
                0, PEER_TOPK, extract, (s_ref[0, :, ls], s_ref[1, :, ls], jnp.full((N_KEYS, LANE), f32(PEER_TOPK))))
            r1_ref[h, :, ls] = rank.astype(bf16)
            if not exact:
                bad_t = jnp.zeros((1, LANE), f32)
                for s_fin in (sa_fin, sb_fin):
                    removed = jnp.sum(jnp.where(s_fin == neg_inf, f32(1.0), f32(0.0)), axis=0, keepdims=True)
                    bad_t = bad_t + jnp.abs(removed - f32(PEER_TOPK))
                bad_tiles.append(bad_t)
        if not exact:
            bad = bad + jnp.concatenate(bad_tiles, axis=1)

        def lane_tiles(ref, c):
            return jnp.concatenate([ref[c, lt] for lt in range(tm // LANE)], axis=1)

        v0 = lane_tiles(vals_ref, 0)
        v1 = lane_tiles(vals_ref, 1)
        r = 0
        for a in range(PEER_TOPK):
            nb = _CAND_COUNTS[a]
            cs_ref[r:r + nb, :] = v0[a:a + 1, :] + v1[0:nb, :]
            r += nb
        cs_ref[_N_CAND:_N_CAND_PAD, :] = jnp.full((_N_CAND_PAD - _N_CAND, tm), neg_inf)
        m1 = v0[0:1, :] + v1[0:1, :]

        cs = cs_ref[...]
        if exact:
            def extract2(k, st):
                cs, sel, z = st
                m = jnp.max(cs, axis=0, keepdims=True)
                p = jnp.min(jnp.where(cs == m, pos, f32(_BIG)), axis=0, keepdims=True)
                hit = pos == p
                return jnp.where(hit, neg_inf, cs), jnp.where(hit, f32(1.0), sel), z + jnp.exp(m - m1)

            _, sel, z = lax.fori_loop(0, PEER_TOPK, extract2,
                                      (cs, jnp.zeros((_N_CAND_PAD, tm), f32), jnp.zeros((1, tm), f32)))
        else:
            larger = jnp.zeros((_N_CAND_PAD, tm), f32)
            for r in range(_N_CAND):
                larger = larger + jnp.where(cs[r:r + 1, :] > cs, f32(1.0), f32(0.0))
            sel = jnp.where((larger < f32(PEER_TOPK)) & (cs > neg_inf), f32(1.0), f32(0.0))
            z = jnp.sum(sel * jnp.exp(cs - m1), axis=0, keepdims=True)
            bad = bad + jnp.abs(jnp.sum(sel, axis=0, keepdims=True) - f32(PEER_TOPK))
        s0 = s_ref[0]
        idx0 = lane_tiles(idx_ref, 0)
        bound0 = jnp.zeros((N_KEYS, tm), f32)
        r = 0
        for a in range(PEER_TOPK):
            nb = _CAND_COUNTS[a]
            cnt = jnp.sum(sel[r:r + nb, :], axis=0, keepdims=True)
            is_a = (row == idx0[a:a + 1, :]) if exact else (s0 == v0[a:a + 1, :])
            bound0 = jnp.where(is_a, cnt, bound0)
            r += nb
        b0_ref[h] = _dup_bf16(bound0)
        e0_ref[h] = _dup_bf16(jnp.exp(s0 - v0[0:1, :]) / z)
        e1_ref[h] = jnp.exp(s_ref[1] - v1[0:1, :]).astype(bf16)
        return bad

    def head_body(h, carry):
        bad = head_compute(h, False)

        @pl.when(jnp.max(bad) > 0.0)
        def _():
            head_compute(h, True)

        return carry

    lax.fori_loop(0, PEER_HEADS, head_body, 0)


def _peer_main_kernel(ht_ref, u_ref, v_ref, b0_ref, r1_ref, e0_ref, e1_ref, x_ref, gn_ref, o_ref, hn_ref,
                      acc_ref, *, nblk, npieces):
    j = pl.program_id(1)

    @pl.when(j == 0)
    def _():
        acc_ref[...] = jnp.zeros_like(acc_ref)

    ht = ht_ref[...]
    bpp = nblk // npieces
    pc = bpp * N_KEYS
    for p in range(npieces):
        at = jnp.dot(u_ref[p * pc:(p + 1) * pc, :], ht, preferred_element_type=f32)
        ws = []
        for b in range(bpp):
            i = j * nblk + p * bpp + b
            rows_h = [(_row_as_bf16_tile(b0_ref, h, i, GATE_ROWS), _row_as_bf16_tile(e0_ref, h, i, GATE_ROWS))
                      for h in range(PEER_HEADS)]
            for sb in range(N_KEYS // GATE_ROWS):
                js = slice(sb * GATE_ROWS, (sb + 1) * GATE_ROWS)
                a = at[b * N_KEYS + sb * GATE_ROWS:b * N_KEYS + (sb + 1) * GATE_ROWS, :]
                g = jnp.zeros(a.shape, bf16)
                for h in range(PEER_HEADS):
                    bnd, e0 = rows_h[h]
                    g = g + jnp.where(r1_ref[h, js, :] < bnd, e1_ref[h, js, :] * e0, bf16(0.0))
                gelu = 0.5 * a * (1.0 + lax.erf(a * f32(math.sqrt(0.5))))
                ws.append(g * gelu.astype(bf16))
        wt = jnp.concatenate(ws, axis=0) if bpp > 1 else ws[0]
        acc_ref[...] += lax.dot_general(wt, v_ref[p * pc:(p + 1) * pc, :], TN_DIMS,
                                        preferred_element_type=f32)

    @pl.when(j == pl.num_programs(1) - 1)
    def _():
        xn = x_ref[...] + acc_ref[...]
        o_ref[...] = xn
        hn_ref[...] = _rms(xn, gn_ref[...])


def peer_ffn_residual(x, ht, wqt, sk, u, v, g_next, tm_sel=256, tm=256, ce=4096, npieces=16):
    t = x.shape[0]
    d = D_MODEL
    pos_list = []
    for a in range(PEER_TOPK):
        pos_list += [a * PEER_TOPK + b for b in range(_CAND_COUNTS[a])]
    pos_list += [_BIG] * (_N_CAND_PAD - _N_CAND)
    pos = jnp.broadcast_to(jnp.asarray(pos_list, f32)[:, None], (_N_CAND_PAD, tm_sel))
    sel_shape = jax.ShapeDtypeStruct((PEER_HEADS, N_KEYS, t), jnp.int32)
    sel_shape16 = jax.ShapeDtypeStruct((PEER_HEADS, N_KEYS, t), bf16)
    sel_spec = pl.BlockSpec((PEER_HEADS, N_KEYS, tm_sel), lambda i: (0, 0, i))
    b0, r1, e0, e1 = pl.pallas_call(
        _peer_select_kernel,
        grid=(t // tm_sel,),
        in_specs=[pl.BlockSpec((d, tm_sel), lambda i: (0, i)),
                  pl.BlockSpec((PEER_HEADS * 2 * PEER_HALF, d), lambda i: (0, 0)),
                  pl.BlockSpec((PEER_HEADS, 2, N_KEYS, PEER_HALF), lambda i: (0, 0, 0, 0)),
                  pl.BlockSpec((_N_CAND_PAD, tm_sel), lambda i: (0, 0))],
        out_specs=[sel_spec] * 4,
        out_shape=[sel_shape, sel_shape16, sel_shape, sel_shape16],
        scratch_shapes=[pltpu.VMEM((PEER_HEADS * 2 * PEER_HALF, tm_sel), bf16),
                        pltpu.VMEM((2, N_KEYS, tm_sel), f32),
                        pltpu.VMEM((2, tm_sel // LANE, PEER_TOPK, LANE), f32),
                        pltpu.VMEM((2, tm_sel // LANE, PEER_TOPK, LANE), f32),
                        pltpu.VMEM((_N_CAND_PAD, tm_sel), f32)],
        compiler_params=_cparams("parallel"),
        name="peer_select",
    )(ht, wqt, sk, pos)

    nblk = ce // N_KEYS
    sel_spec2 = pl.BlockSpec((PEER_HEADS, N_KEYS, tm), lambda i, j: (0, 0, i))
    return pl.pallas_call(
        functools.partial(_peer_main_kernel, nblk=nblk, npieces=npieces),
        grid=(t // tm, N_EXPERTS // ce),
        in_specs=[pl.BlockSpec((d, tm), lambda i, j: (0, i)),
                  pl.BlockSpec((ce, d), lambda i, j: (j, 0)),
                  pl.BlockSpec((ce, d), lambda i, j: (j, 0)),
                  sel_spec2, sel_spec2, sel_spec2, sel_spec2,
                  pl.BlockSpec((tm, d), lambda i, j: (i, 0)), pl.BlockSpec((1, d), lambda i, j: (0, 0))],
        out_specs=[pl.BlockSpec((tm, d), lambda i, j: (i, 0))] * 2,
        out_shape=[jax.ShapeDtypeStruct((t, d), f32)] * 2,
        scratch_shapes=[pltpu.VMEM((tm, d), f32)],
        compiler_params=_cparams("parallel", "arbitrary"),
        name="peer_main",
    )(ht, u, v, b0, r1, e0, e1, x, g_next.reshape(1, d))


IN_WIDTHS = (SSM_INNER, CONV_DIM, ATT_HEADS * ATT_HEAD_DIM, 2 * ATT_KV_HEADS * ATT_HEAD_DIM, LANE)


def _in_proj_kernel(x_ref, g_ref, w_ref, *out_refs):
    h = _rms(x_ref[...], g_ref[...])
    y = jnp.dot(h.astype(bf16), w_ref[...], preferred_element_type=f32)
    o = 0
    for ref in out_refs:
        w = ref.shape[1]
        ref[...] = y[:, o:o + w]
        o += w


def in_proj(x, g, w_in, tm=256):
    t, d = x.shape
    o1 = SSM_INNER
    o2 = o1 + CONV_DIM
    o3 = o2 + SSM_HEADS
    w = jnp.concatenate([w_in[:, :o2], w_in[:, o3:], w_in[:, o2:o3],
                         jnp.zeros((d, LANE - SSM_HEADS), w_in.dtype)], axis=1).astype(bf16)
    n = w.shape[1]
    return pl.pallas_call(
        _in_proj_kernel,
        grid=(t // tm,),
        in_specs=[pl.BlockSpec((tm, d), lambda i: (i, 0)), pl.BlockSpec((1, d), lambda i: (0, 0)),
                  pl.BlockSpec((d, n), lambda i: (0, 0))],
        out_specs=[pl.BlockSpec((tm, wd), lambda i: (i, 0)) for wd in IN_WIDTHS],
        out_shape=[jax.ShapeDtypeStruct((t, wd), f32) for wd in IN_WIDTHS],
        compiler_params=_cparams("parallel"),
        name="in_proj",
    )(x, g.reshape(1, d), w)


N_PAIRS = SSM_HEADS // 2


def _ssd_kernel(*refs, lin):
    (z_ref, xbc_ref, dt_ref, cs_ref, h0_ref, cw_ref, cb_ref, dtb_ref, aneg_ref, dsk_ref, gn_ref, selh_ref) = refs[:12]
    refs = refs[12:]
    y_ref, hfin_ref, xe_ref, hp_ref, ys_ref = refs[:5]
    lc = SSD_CHUNK
    c = pl.program_id(1)

    @pl.when(c == 0)
    def _():
        xe_ref[0:SUBLANE, :] = cs_ref[0]
        hp_ref[...] = h0_ref[0]

    if lin == lc:
        xe_ref[SUBLANE:SUBLANE + lc, :] = xbc_ref[...]
        z = z_ref[...]
        dt_raw = dt_ref[...]
    else:
        zpad_ref, dtpad_ref = refs[5:7]
        xe_ref[SUBLANE:SUBLANE + lc, :] = jnp.zeros((lc, CONV_DIM), f32)
        xe_ref[SUBLANE:SUBLANE + lin, :] = xbc_ref[...]
        zpad_ref[...] = jnp.zeros_like(zpad_ref)
        zpad_ref[0:lin, :] = z_ref[...]
        dtpad_ref[...] = jnp.zeros_like(dtpad_ref)
        dtpad_ref[0:lin, :] = dt_ref[...]
        z = zpad_ref[...]
        dt_raw = dtpad_ref[...]

    conv = cb_ref[...]
    for j in range(CONV_W):
        o = SUBLANE - (CONV_W - 1) + j
        conv = conv + xe_ref[o:o + lc, :] * cw_ref[j:j + 1, :]
    if lin == lc:
        xe_ref[0:SUBLANE, :] = xe_ref[lc:lc + SUBLANE, :]
    xc = _silu(conv)
    xs = xc[:, :SSM_INNER]
    bm = xc[:, SSM_INNER:SSM_INNER + SSM_GROUPS * SSM_STATE]
    cm = xc[:, SSM_INNER + SSM_GROUPS * SSM_STATE:]

    row = lax.broadcasted_iota(jnp.int32, (lc, LANE), 0)
    col = lax.broadcasted_iota(jnp.int32, (lc, LANE), 1)
    causal = row >= col
    lane_lo = col < HALF_LANE
    neg_inf = f32(-jnp.inf)

    x = dt_raw + dtb_ref[...]
    dt = jnp.maximum(x, 0.0) + jnp.log(1.0 + jnp.exp(-jnp.abs(x)))
    if lin != lc:
        dt = jnp.where(row < lin, dt, 0.0)
    la = dt * aneg_ref[...]
    acs = jnp.dot(causal.astype(f32), la, precision=HIGHEST, preferred_element_type=f32)
    acs_t = acs.T
    selh = selh_ref[...]
    dt_exp = jnp.dot(dt, selh, precision=HIGHEST, preferred_element_type=f32)
    acs_exp = jnp.dot(acs, selh, precision=HIGHEST, preferred_element_type=f32)
    alast = acs[lc - 1:lc, :]
    xdt = xs * dt_exp
    eacs = jnp.exp(acs_exp)
    xdt_end = (xdt * jnp.exp(acs_exp[lc - 1:lc, :] - acs_exp)).astype(bf16)
    dsk = dsk_ref[...]

    for g in range(SSM_GROUPS):
        cmg = cm[:, g * SSM_STATE:(g + 1) * SSM_STATE].astype(bf16)
        bmg = bm[:, g * SSM_STATE:(g + 1) * SSM_STATE].astype(bf16)
        cb = lax.dot_general(cmg, bmg, NT_DIMS, preferred_element_type=f32)
        for jj in range(N_PAIRS // SSM_GROUPS):
            j = g * (N_PAIRS // SSM_GROUPS) + jj
            sl = slice(j * LANE, (j + 1) * LANE)
            xdt_pair = xdt[:, sl]
            ydiag = jnp.zeros((lc, LANE), f32)
            for half in (0, 1):
                h = 2 * j + half
                seg = acs[:, h:h + 1] - acs_t[h:h + 1, :]
                dec = jnp.exp(jnp.where(causal, seg, neg_inf))
                m = (cb * dec).astype(bf16)
                keep = lane_lo if half == 0 else jnp.logical_not(lane_lo)
                xd = jnp.where(keep, xdt_pair, 0.0).astype(bf16)
                ydiag = ydiag + jnp.dot(m, xd, preferred_element_type=f32)
            hpj = hp_ref[j]
            yoff = lax.dot_general(cmg, hpj.astype(bf16), NT_DIMS, preferred_element_type=f32) * eacs[:, sl]
            s_new = lax.dot_general(xdt_end[:, sl], bmg, TN_DIMS, preferred_element_type=f32)
            dl = jnp.where(row < SSM_HEAD_DIM, alast[:, 2 * j:2 * j + 1], alast[:, 2 * j + 1:2 * j + 2])
            hp_ref[j] = hpj * jnp.exp(dl) + s_new
            ys_ref[:, sl] = ydiag + yoff + dsk[:, sl] * xs[:, sl]

    y = ys_ref[...] * _silu(z)
    gs = SSM_INNER // SSM_GROUPS
    gn = gn_ref[...]
    for g in range(SSM_GROUPS):
        yg = y[:, g * gs:(g + 1) * gs]
        yg = yg * lax.rsqrt(jnp.mean(yg * yg, axis=-1, keepdims=True) + EPS) * gn[:, g * gs:(g + 1) * gs]
        y_ref[:, g * gs:(g + 1) * gs] = yg[0:lin, :]

    @pl.when(c == pl.num_programs(1) - 1)
    def _():
        hfin_ref[0] = hp_ref[...]


def ssd_mixer(z, xbc, dt, conv_state, ssm_state, params, *, batch, seq, row0):
    conv_w, conv_b, dt_bias, a_neg, d_skip, gnorm, selh = params
    lc = SSD_CHUNK
    lin = min(seq, lc)
    nc = seq // lin
    blk0 = row0 // lin
    cs = jnp.pad(conv_state, ((0, 0), (SUBLANE - (CONV_W - 1), 0), (0, 0)))
    h0 = ssm_state.reshape(batch, N_PAIRS, LANE, SSM_STATE)

    def rows(w):
        return pl.BlockSpec((lin, w), lambda b, c: (blk0 + b * nc + c, 0))

    def const(shape):
        return pl.BlockSpec(shape, lambda b, c: (0,) * len(shape))

    in_specs = [rows(SSM_INNER), rows(CONV_DIM), rows(LANE),
                pl.BlockSpec((1, SUBLANE, CONV_DIM), lambda b, c: (b, 0, 0)),
                pl.BlockSpec((1, N_PAIRS, LANE, SSM_STATE), lambda b, c: (b, 0, 0, 0)),
                const((CONV_W, CONV_DIM)), const((1, CONV_DIM)), const((1, LANE)), const((1, LANE)),
                const((1, SSM_INNER)), const((1, SSM_INNER)), const((LANE, SSM_INNER))]
    args = [z, xbc, dt, cs, h0, conv_w, conv_b, dt_bias, a_neg, d_skip, gnorm, selh]
    scratch = [pltpu.VMEM((lc + 2 * SUBLANE, CONV_DIM), f32), pltpu.VMEM((N_PAIRS, LANE, SSM_STATE), f32),
               pltpu.VMEM((lc, SSM_INNER), f32)]
    if lin != lc:
        scratch += [pltpu.VMEM((lc, SSM_INNER), f32), pltpu.VMEM((lc, LANE), f32)]
    y, h_fin = pl.pallas_call(
        functools.partial(_ssd_kernel, lin=lin),
        grid=(batch, nc),
        in_specs=in_specs,
        out_specs=[pl.BlockSpec((lin, SSM_INNER), lambda b, c: (b * nc + c, 0)),
                   pl.BlockSpec((1, N_PAIRS, LANE, SSM_STATE), lambda b, c: (b, 0, 0, 0))],
        out_shape=[jax.ShapeDtypeStruct((batch * seq, SSM_INNER), f32),
                   jax.ShapeDtypeStruct((batch, N_PAIRS, LANE, SSM_STATE), f32)],
        scratch_shapes=scratch,
        compiler_params=_cparams("parallel", "arbitrary"),
        name="ssd_mixer",
    )(*args)
    return y, h_fin.reshape(batch, SSM_HEADS, SSM_HEAD_DIM, SSM_STATE)


def _swa_kernel(*refs, lq, masked_first):
    q_ref, kp_ref, vp_ref, kc_ref, vc_ref, bias_ref, sink_ref = refs[:7]
    refs = refs[7:]
    o_ref = refs[0]
    n = pl.program_id(1)
    q = q_ref[...]
    if lq == ATT_BLOCK:
        kc = kc_ref[...]
        vc = vc_ref[...]
    else:
        kpad_ref, vpad_ref = refs[1:3]
        kpad_ref[...] = jnp.zeros_like(kpad_ref)
        vpad_ref[...] = jnp.zeros_like(vpad_ref)
        kpad_ref[0:lq, :] = kc_ref[...]
        vpad_ref[0:lq, :] = vc_ref[...]
        kc = kpad_ref[...]
        vc = vpad_ref[...]
    kp = kp_ref[...]
    vp = vp_ref[...]
    lane_lo = lax.broadcasted_iota(jnp.int32, (ATT_BLOCK, LANE), 1) < HALF_LANE
    lane_lo_q = lax.broadcasted_iota(jnp.int32, (lq, LANE), 1) < HALF_LANE
    neg_inf = f32(-jnp.inf)
    scale = f32(ATT_HEAD_DIM ** -0.5)

    for g in range(ATT_KV_HEADS):
        sl = slice((g // 2) * LANE, (g // 2 + 1) * LANE)
        odd = g % 2 == 1

        def kpad(k):
            pair = k[:, sl]
            if odd:
                pair = pltpu.roll(pair, HALF_LANE, 1)
            return jnp.where(lane_lo, pair, 0.0).astype(bf16)

        def vdup(v):
            pair = v[:, sl]
            rolled = pltpu.roll(pair, HALF_LANE, 1)
            return (jnp.where(lane_lo, rolled, pair) if odd else jnp.where(lane_lo, pair, rolled)).astype(bf16)

        qp0 = q[:, (2 * g) * LANE:(2 * g + 1) * LANE]
        qp1 = q[:, (2 * g + 1) * LANE:(2 * g + 2) * LANE]
        qg = jnp.concatenate([qp0, pltpu.roll(qp0, HALF_LANE, 1), qp1, pltpu.roll(qp1, HALF_LANE, 1)],
                             axis=0).astype(bf16)
        bias = bias_ref[g]
        sp = lax.dot_general(qg, kpad(kp), NT_DIMS, preferred_element_type=f32) * scale + bias[:, :WINDOW]
        sc = lax.dot_general(qg, kpad(kc), NT_DIMS, preferred_element_type=f32) * scale + bias[:, WINDOW:]
        if masked_first:
            sp = jnp.where(n > 0, sp, neg_inf)
        sink = sink_ref[g][:, 0:1]
        m = jnp.maximum(jnp.maximum(jnp.max(sp, axis=-1, keepdims=True), jnp.max(sc, axis=-1, keepdims=True)), sink)
        pp = jnp.exp(sp - m)
        pc = jnp.exp(sc - m)
        denom = jnp.sum(pp, axis=-1, keepdims=True) + jnp.sum(pc, axis=-1, keepdims=True) + jnp.exp(sink - m)
        og = (jnp.dot(pp.astype(bf16), vdup(vp), preferred_element_type=f32)
              + jnp.dot(pc.astype(bf16), vdup(vc), preferred_element_type=f32)) / denom
        o_ref[:, (2 * g) * LANE:(2 * g + 1) * LANE] = jnp.where(lane_lo_q, og[0:lq], og[lq:2 * lq])
        o_ref[:, (2 * g + 1) * LANE:(2 * g + 2) * LANE] = jnp.where(lane_lo_q, og[2 * lq:3 * lq], og[3 * lq:4 * lq])


def _rel_bucket(dist):
    exact = REL_BUCKETS // 2
    d = jnp.maximum(dist, 0)
    large = exact + (jnp.log(jnp.maximum(d, 1).astype(f32) / exact)
                     / math.log(REL_MAX_DIST / exact) * (REL_BUCKETS - exact)).astype(jnp.int32)
    large = jnp.minimum(large, REL_BUCKETS - 1)
    return jnp.where(d < exact, d, large)


def _swa_tables(rel_bias, sinks, lq):
    qi = jnp.arange(lq)[:, None]
    kj = jnp.arange(2 * ATT_BLOCK)[None, :]
    dist = qi + WINDOW - kj
    band = (dist >= 0) & (dist <= WINDOW)
    onehot = (_rel_bucket(dist)[..., None] == jnp.arange(REL_BUCKETS)).astype(f32)
    bias = jnp.einsum('qkb,bh->qkh', onehot, rel_bias, precision=HIGHEST)
    bias = jnp.where(band[..., None], bias, -jnp.inf)
    bias = jnp.transpose(bias, (2, 0, 1)).reshape(ATT_KV_HEADS, ATT_GQA * lq, 2 * ATT_BLOCK)
    sink = jnp.broadcast_to(sinks.reshape(ATT_KV_HEADS, ATT_GQA, 1, 1), (ATT_KV_HEADS, ATT_GQA, lq, LANE))
    return bias, sink.reshape(ATT_KV_HEADS, ATT_GQA * lq, LANE)


def swa_mixer(q, kv, k_prev, v_prev, rel_bias, sinks, *, batch, seq, row0):
    lq = min(seq, ATT_BLOCK)
    nb = seq // lq
    blk0 = row0 // lq
    kvw = ATT_KV_HEADS * ATT_HEAD_DIM
    bias, sink = _swa_tables(rel_bias, sinks, lq)
    cur_k = pl.BlockSpec((lq, kvw), lambda b, n: (blk0 + b * nb + n, 0))
    cur_v = pl.BlockSpec((lq, kvw), lambda b, n: (blk0 + b * nb + n, 1))
    if k_prev is None:
        prev_k = pl.BlockSpec((WINDOW, kvw), lambda b, n: (blk0 + b * nb + jnp.maximum(n - 1, 0), 0))
        prev_v = pl.BlockSpec((WINDOW, kvw), lambda b, n: (blk0 + b * nb + jnp.maximum(n - 1, 0), 1))
        kp_arr, vp_arr = kv, kv
    else:
        prev_k = pl.BlockSpec((WINDOW, kvw), lambda b, n: (b, 0))
        prev_v = prev_k
        kp_arr = k_prev.reshape(batch * WINDOW, kvw)
        vp_arr = v_prev.reshape(batch * WINDOW, kvw)
    rows = pl.BlockSpec((lq, ATT_HEADS * ATT_HEAD_DIM), lambda b, n: (blk0 + b * nb + n, 0))
    in_specs = [rows, prev_k, prev_v, cur_k, cur_v,
                pl.BlockSpec(bias.shape, lambda b, n: (0, 0, 0)), pl.BlockSpec(sink.shape, lambda b, n: (0, 0, 0))]
    args = [q, kp_arr, vp_arr, kv, kv, bias, sink]
    scratch = [] if lq == ATT_BLOCK else [pltpu.VMEM((ATT_BLOCK, kvw), f32), pltpu.VMEM((ATT_BLOCK, kvw), f32)]
    return pl.pallas_call(
        functools.partial(_swa_kernel, lq=lq, masked_first=k_prev is None),
        grid=(batch, nb),
        in_specs=in_specs,
        out_specs=pl.BlockSpec((lq, ATT_HEADS * ATT_HEAD_DIM), lambda b, n: (b * nb + n, 0)),
        out_shape=jax.ShapeDtypeStruct((batch * seq, ATT_HEADS * ATT_HEAD_DIM), f32),
        scratch_shapes=scratch,
        compiler_params=_cparams("parallel", "arbitrary"),
        name="swa_mixer",
    )(*args)


def _out_proj_kernel(yap_ref, ybp_ref, yas_ref, ybs_ref, w_ref, x_ref, g_ref, o_ref, ht_ref, *, n_prompt):
    ka = yap_ref.shape[1]

    def body(ya_ref, yb_ref):
        acc = jnp.dot(ya_ref[...].astype(bf16), w_ref[0:ka, :], preferred_element_type=f32)
        acc = acc + jnp.dot(yb_ref[...].astype(bf16), w_ref[ka:, :], preferred_element_type=f32)
        xn = x_ref[...] + acc
        o_ref[...] = xn
        ht_ref[...] = _rms(xn, g_ref[...]).T.astype(bf16)

    @pl.when(pl.program_id(0) < n_prompt)
    def _():
        body(yap_ref, ybp_ref)

    @pl.when(pl.program_id(0) >= n_prompt)
    def _():
        body(yas_ref, ybs_ref)


def out_proj(ya_p, yb_p, ya_s, yb_s, w, x, g, tm=512):
    t, d = x.shape
    ka, kb = ya_p.shape[1], yb_p.shape[1]
    n_p = ya_p.shape[0] // tm
    n_s = ya_s.shape[0] // tm

    def p_rows(k):
        return pl.BlockSpec((tm, k), lambda i: (jnp.minimum(i, n_p - 1), 0))

    def s_rows(k):
        return pl.BlockSpec((tm, k), lambda i: (jnp.maximum(i - n_p, 0), 0))

    return pl.pallas_call(
        functools.partial(_out_proj_kernel, n_prompt=n_p),
        grid=(n_p + n_s,),
        in_specs=[p_rows(ka), p_rows(kb), s_rows(ka), s_rows(kb),
                  pl.BlockSpec((ka + kb, d), lambda i: (0, 0)), pl.BlockSpec((tm, d), lambda i: (i, 0)),
                  pl.BlockSpec((1, d), lambda i: (0, 0))],
        out_specs=[pl.BlockSpec((tm, d), lambda i: (i, 0)), pl.BlockSpec((d, tm), lambda i: (0, i))],
        out_shape=[jax.ShapeDtypeStruct((t, d), f32), jax.ShapeDtypeStruct((d, t), bf16)],
        compiler_params=_cparams("arbitrary"),
        name="out_proj",
    )(ya_p, yb_p, ya_s, yb_s, w, x, g.reshape(1, d))


def _gated_out_kernel(yp_ref, gp_ref, ys_ref, gs_ref, w_ref, x_ref, gn_ref, o_ref, ht_ref, *, n_prompt):
    def body(y_ref, g_ref):
        a = (y_ref[...] * g_ref[...]).astype(bf16)
        xn = x_ref[...] + jnp.dot(a, w_ref[...], preferred_element_type=f32)
        o_ref[...] = xn
        ht_ref[...] = _rms(xn, gn_ref[...]).T.astype(bf16)

    @pl.when(pl.program_id(0) < n_prompt)
    def _():
        body(yp_ref, gp_ref)

    @pl.when(pl.program_id(0) >= n_prompt)
    def _():
        body(ys_ref, gs_ref)


def gated_out_proj(y_p, g_p, y_s, g_s, w, x, gn, tm=512):
    t, d = x.shape
    n_p = y_p.shape[0] // tm
    n_s = y_s.shape[0] // tm
    rows = pl.BlockSpec((tm, d), lambda i: (i, 0))
    p_rows = pl.BlockSpec((tm, d), lambda i: (jnp.minimum(i, n_p - 1), 0))
    s_rows = pl.BlockSpec((tm, d), lambda i: (jnp.maximum(i - n_p, 0), 0))
    return pl.pallas_call(
        functools.partial(_gated_out_kernel, n_prompt=n_p),
        grid=(n_p + n_s,),
        in_specs=[p_rows, p_rows, s_rows, s_rows,
                  pl.BlockSpec((d, d), lambda i: (0, 0)), rows, pl.BlockSpec((1, d), lambda i: (0, 0))],
        out_specs=[rows, pl.BlockSpec((d, tm), lambda i: (0, i))],
        out_shape=[jax.ShapeDtypeStruct((t, d), f32), jax.ShapeDtypeStruct((d, t), bf16)],
        compiler_params=_cparams("arbitrary"),
        name="rwkv_out",
    )(y_p, g_p, y_s, g_s, w, x, gn.reshape(1, d))


LORA_PAD = 2 * LANE


def _softplus(x):
    return jnp.maximum(x, 0.0) + jnp.log(1.0 + jnp.exp(-jnp.abs(x)))


def _rwkv_pre_kernel(h_ref, p_ref, mu_ref, wrkv_ref, l1_ref, l2_ref, vec_ref,
                     r_ref, d_ref, k_ref, v_ref, a_ref, g_ref):
    h = h_ref[...]
    xx = p_ref[...] - h

    def mix(j):
        return (h + xx * mu_ref[j:j + 1, :]).astype(bf16)

    def mm(a, w):
        return jnp.dot(a, w, preferred_element_type=f32)

    r_ref[...] = mm(mix(0), wrkv_ref[0])
    k_ref[...] = mm(mix(2), wrkv_ref[1])
    v_ref[...] = mm(mix(3), wrkv_ref[2])
    wl = vec_ref[0:1, :] + mm(jnp.tanh(mm(mix(1), l1_ref[0])).astype(bf16), l2_ref[0])
    al = vec_ref[1:2, :] + mm(mm(mix(4), l1_ref[1]).astype(bf16), l2_ref[1])
    g_ref[...] = mm(jax.nn.sigmoid(mm(mix(5), l1_ref[2])).astype(bf16), l2_ref[2])
    w = -_softplus(-wl) - 0.5
    d_ref[...] = jnp.exp(-jnp.exp(w))
    a_ref[...] = jax.nn.sigmoid(al)


def rwkv_pre(h, prev, mu, w_rkv, lora1, lora2, vecs, *, row0, tm=256):
    t, d = prev.shape
    blk0 = row0 // tm
    rows = pl.BlockSpec((tm, d), lambda i: (i, 0))

    def const(a):
        return pl.BlockSpec(a.shape, lambda i: (0,) * a.ndim)

    return pl.pallas_call(
        _rwkv_pre_kernel,
        grid=(t // tm,),
        in_specs=[pl.BlockSpec((tm, d), lambda i: (blk0 + i, 0)), rows,
                  const(mu), const(w_rkv), const(lora1), const(lora2), const(vecs)],
        out_specs=[rows] * 6,
        out_shape=[jax.ShapeDtypeStruct((t, d), f32)] * 6,
        compiler_params=_cparams("parallel"),
        name="rwkv_pre",
    )(h, prev, mu, w_rkv, lora1, lora2, vecs)


RWKV_VGROUP = 8


RWKV_TB = 16
N_SEQ_TILE = LANE // RWKV_HEADS


def _swap_list_and_lane(arrs):
    n = len(arrs)
    lane = lax.broadcasted_iota(jnp.int32, arrs[0].shape, 1)
    s = n // 2
    while s >= 1:
        upper = (lane & s) != 0
        new = list(arrs)
        for i in range(n):
            if i & s == 0:
                a, b = arrs[i], arrs[i | s]
                new[i] = jnp.where(upper, pltpu.roll(b, s, 1), a)
                new[i | s] = jnp.where(upper, b, pltpu.roll(a, LANE - s, 1))
        arrs = new
        s //= 2
    return arrs


def _natural_to_chain(x_ref, dst_ref, q):
    x2 = x_ref[...].reshape(N_SEQ_TILE * RWKV_TB, D_MODEL)
    xt = x2.T
    arrs = _swap_list_and_lane([xt[h * RWKV_HEAD:(h + 1) * RWKV_HEAD, :] for h in range(RWKV_HEADS)])
    for t in range(RWKV_TB):
        dst_ref[q, t] = arrs[t]


def _chain_to_natural(src_ref, y_ref):
    arrs = _swap_list_and_lane([src_ref[t] for t in range(RWKV_TB)])
    xt = jnp.concatenate(arrs, axis=0)
    y_ref[...] = xt.T.reshape(N_SEQ_TILE, RWKV_TB, D_MODEL)


N_SCAN_IN = 5


def _rwkv_scan_kernel(r_ref, d_ref, k_ref, v_ref, a_ref, s0_ref, tab_ref, y_ref, sfin_ref,
                      s_ref, q_ref, ys_ref, *, natural):
    tb = q_ref.shape[1]

    @pl.when(pl.program_id(1) == 0)
    def _():
        s_ref[...] = s0_ref[...]

    for q, ref in enumerate((r_ref, d_ref, k_ref, v_ref, a_ref)):
        if natural:
            _natural_to_chain(ref, q_ref, q)
        else:
            q_ref[q] = ref[...]
    r_q, d_q, k_q, v_q, a_q = [q_ref.at[q] for q in range(N_SCAN_IN)]

    def step(t, carry):
        k_raw = k_q[t]
        a = a_q[t]
        kkr = k_raw * tab_ref[3]
        nrm = jnp.sqrt(jnp.sum(kkr * kkr, axis=0, keepdims=True))
        kk = kkr / jnp.maximum(nrm, 1e-12)
        d = d_q[t]
        kv = k_raw * (1.0 + (a - 1.0) * tab_ref[4])
        r = r_q[t]
        b = kk * a

        def vgroup(g, c2):
            v0 = pl.multiple_of(g * RWKV_VGROUP, RWKV_VGROUP)
            vrows = v_q[t, pl.ds(v0, RWKV_VGROUP), :]
            ys = []
            for vi in range(RWKV_VGROUP):
                sv = s_ref[v0 + vi]
                sa = -jnp.sum(sv * kk, axis=0, keepdims=True)
                sn = sv * d + sa * b + vrows[vi:vi + 1, :] * kv
                s_ref[v0 + vi] = sn
                ys.append(jnp.sum(sn * r, axis=0, keepdims=True))
            ys_ref[t, pl.ds(v0, RWKV_VGROUP), :] = jnp.concatenate(ys, axis=0)
            return c2

        lax.fori_loop(0, RWKV_HEAD // RWKV_VGROUP, vgroup, 0)
        y = ys_ref[t]
        mean = jnp.mean(y, axis=0, keepdims=True)
        yc = y - mean
        var = jnp.mean(yc * yc, axis=0, keepdims=True)
        bonus = jnp.sum(r * kv * tab_ref[0], axis=0, keepdims=True)
        ys_ref[t] = yc * lax.rsqrt(var + GN_EPS) * tab_ref[1] + tab_ref[2] + bonus * v_q[t]
        return carry

    lax.fori_loop(0, tb, step, 0)
    if natural:
        _chain_to_natural(ys_ref, y_ref)
    else:
        y_ref[...] = ys_ref[...]

    @pl.when(pl.program_id(1) == pl.num_programs(1) - 1)
    def _():
        sfin_ref[...] = s_ref[...]


def rwkv_scan(parts, s0, tab, *, natural):
    hd = RWKV_HEAD
    c = s0.shape[-1]
    if natural:
        nb, L, dm = parts[0].shape
        tb = RWKV_TB
        seq_spec = pl.BlockSpec((N_SEQ_TILE, tb, dm), lambda i, j: (i, j, 0))
        y_shape = jax.ShapeDtypeStruct((nb, L, dm), f32)
    else:
        L = parts[0].shape[0]
        tb = min(L, RWKV_TB)
        seq_spec = pl.BlockSpec((tb, hd, LANE), lambda i, j: (j, 0, i))
        y_shape = jax.ShapeDtypeStruct((L, hd, c), f32)
    st_spec = pl.BlockSpec((hd, hd, LANE), lambda i, j: (0, 0, i))
    return pl.pallas_call(
        functools.partial(_rwkv_scan_kernel, natural=natural),
        grid=(c // LANE, L // tb),
        in_specs=[seq_spec] * N_SCAN_IN + [st_spec, pl.BlockSpec(tab.shape, lambda i, j: (0, 0, 0))],
        out_specs=[seq_spec, st_spec],
        out_shape=[y_shape, jax.ShapeDtypeStruct((hd, hd, c), f32)],
        scratch_shapes=[pltpu.VMEM((hd, hd, LANE), f32), pltpu.VMEM((N_SCAN_IN, tb, hd, LANE), f32),
                        pltpu.VMEM((tb, hd, LANE), f32)],
        compiler_params=_cparams("parallel", "arbitrary"),
        name="rwkv_scan",
    )(*parts, s0, tab)


def _rwkv_core(parts, wkv, tab):
    b, L, _ = parts[0].shape
    s0 = jnp.transpose(wkv, (2, 3, 0, 1)).reshape(RWKV_HEAD, RWKV_HEAD, b * RWKV_HEADS)
    if b % N_SEQ_TILE == 0 and L % RWKV_TB == 0:
        y, s_fin = rwkv_scan(parts, s0, tab, natural=True)
        y = y.reshape(b * L, D_MODEL)
    else:
        def to_chain(t):
            t = jnp.transpose(t.reshape(b, L, RWKV_HEADS, RWKV_HEAD), (1, 3, 0, 2))
            return t.reshape(L, RWKV_HEAD, b * RWKV_HEADS)

        y, s_fin = rwkv_scan([to_chain(t) for t in parts], s0, tab, natural=False)
        y = jnp.transpose(y.reshape(L, RWKV_HEAD, b, RWKV_HEADS), (2, 0, 3, 1)).reshape(b * L, D_MODEL)
    s_fin = jnp.transpose(s_fin.reshape(RWKV_HEAD, RWKV_HEAD, b, RWKV_HEADS), (2, 3, 0, 1))
    return y, s_fin


def kernel(x_prompt, x_sample, state_ssm, state_conv, cache_swa_k, cache_swa_v, state_wkv, state_shift, rel_bias, norm_mix, norm_ffn, norm_final, mix_w_in, ssd_conv_w, ssd_conv_b, ssd_dt_bias, ssd_a_log, ssd_d_skip, ssd_gnorm, attn_sinks, mix_w_out, rwkv_mu, rwkv_w0, rwkv_w1, rwkv_w2, rwkv_a0, rwkv_a1, rwkv_a2, rwkv_g1, rwkv_g2, rwkv_k_k, rwkv_k_a, rwkv_r_k, rwkv_w_rkv, rwkv_w_o, rwkv_ln_w, rwkv_ln_b, peer_w_q, peer_sub_keys, peer_u, peer_v):
    bp, lp, d = x_prompt.shape
    bs, ls, _ = x_sample.shape
    tp = bp * lp
    ts = bs * ls
    tt = tp + ts
    x = jnp.concatenate([x_prompt.reshape(tp, d), x_sample.reshape(ts, d)], axis=0)

    def split(t):
        return t[:tp].reshape(bp, lp, -1), t[tp:].reshape(bs, ls, -1)

    def zero_state(a):
        return jnp.zeros((bp,) + a.shape[2:], a.dtype)

    def peer(x, ht, layer, g_next):
        wqt = peer_w_q[layer].T.astype(bf16)
        return peer_ffn_residual(x, ht, wqt, peer_sub_keys[layer].astype(bf16),
                                 peer_u[layer].astype(bf16), peer_v[layer].astype(bf16), g_next)

    z, xbc, q, kv, dt = in_proj(x, norm_mix[0], mix_w_in[0])
    pad16 = (0, LANE - SSM_HEADS)
    selh = (jnp.arange(LANE)[:, None] == jnp.arange(SSM_INNER)[None, :] // SSM_HEAD_DIM).astype(f32)
    ssd_params = (ssd_conv_w[0], ssd_conv_b[0].reshape(1, CONV_DIM),
                  jnp.pad(ssd_dt_bias[0], pad16).reshape(1, LANE),
                  jnp.pad(-jnp.exp(ssd_a_log[0]), pad16).reshape(1, LANE),
                  jnp.repeat(ssd_d_skip[0], SSM_HEAD_DIM).reshape(1, SSM_INNER),
                  ssd_gnorm[0].reshape(1, SSM_INNER), selh)
    y_ssd_p, ssm_p = ssd_mixer(z, xbc, dt, zero_state(state_conv), zero_state(state_ssm), ssd_params,
                               batch=bp, seq=lp, row0=0)
    y_ssd_s, ssm_s = ssd_mixer(z, xbc, dt, state_conv[0], state_ssm[0], ssd_params, batch=bs, seq=ls, row0=tp)
    o_att_p = swa_mixer(q, kv, None, None, rel_bias, attn_sinks[0], batch=bp, seq=lp, row0=0)
    o_att_s = swa_mixer(q, kv, cache_swa_k[0], cache_swa_v[0], rel_bias, attn_sinks[0], batch=bs, seq=ls, row0=tp)
    x, ht = out_proj(y_ssd_p, o_att_p, y_ssd_s, o_att_s, mix_w_out[0].astype(bf16), x, norm_ffn[0])
    xbc_p, xbc_s = split(xbc)
    conv_p = xbc_p[:, -(CONV_W - 1):]
    conv_s = jnp.concatenate([state_conv[0], xbc_s], axis=1)[:, -(CONV_W - 1):]
    kv_p, kv_s = split(kv)
    kvw = ATT_KV_HEADS * ATT_HEAD_DIM
    hshape = (ATT_KV_HEADS, ATT_HEAD_DIM)
    k_p = kv_p[:, -WINDOW:, :kvw].reshape(bp, WINDOW, *hshape)
    v_p = kv_p[:, -WINDOW:, kvw:].reshape(bp, WINDOW, *hshape)
    k_s = jnp.concatenate([cache_swa_k[0], kv_s[..., :kvw].reshape(bs, ls, *hshape)], axis=1)[:, -WINDOW:]
    v_s = jnp.concatenate([cache_swa_v[0], kv_s[..., kvw:].reshape(bs, ls, *hshape)], axis=1)[:, -WINDOW:]
    x, h = peer(x, ht, 0, norm_mix[1])

    h_p, h_s = split(h)
    prev_p = jnp.concatenate([jnp.zeros((bp, 1, d), f32), h_p[:, :-1]], axis=1)
    prev_s = jnp.concatenate([state_shift[0][:, None], h_s[:, :-1]], axis=1)
    def lora_pair(w1, w2):
        pad = LORA_PAD - w1.shape[1]
        return jnp.pad(w1, ((0, 0), (0, pad))), jnp.pad(w2, ((0, pad), (0, 0)))

    pairs = [lora_pair(rwkv_w1[0], rwkv_w2[0]), lora_pair(rwkv_a1[0], rwkv_a2[0]), lora_pair(rwkv_g1[0], rwkv_g2[0])]
    lora1 = jnp.stack([p[0] for p in pairs]).astype(bf16)
    lora2 = jnp.stack([p[1] for p in pairs]).astype(bf16)
    vecs = jnp.stack([rwkv_w0[0], rwkv_a0[0]])
    pre_args = (rwkv_mu[0], rwkv_w_rkv[0].astype(bf16), lora1, lora2, vecs)
    *parts_p, g_p = rwkv_pre(h, prev_p.reshape(tp, d), *pre_args, row0=0)
    *parts_s, g_s = rwkv_pre(h, prev_s.reshape(ts, d), *pre_args, row0=tp)

    def chain_table(p):
        return jnp.tile(p.reshape(RWKV_HEADS, RWKV_HEAD).T, (1, LANE // RWKV_HEADS))

    tab = jnp.stack([chain_table(p) for p in (rwkv_r_k[0].reshape(-1), rwkv_ln_w[0], rwkv_ln_b[0],
                                              rwkv_k_k[0], rwkv_k_a[0])])
    y_p, wkv_p = _rwkv_core([t.reshape(bp, lp, d) for t in parts_p], zero_state(state_wkv), tab)
    y_s, wkv_s = _rwkv_core([t.reshape(bs, ls, d) for t in parts_s], state_wkv[0], tab)
    x, ht = gated_out_proj(y_p, g_p, y_s, g_s, rwkv_w_o[0].astype(bf16), x, norm_ffn[1])
    shift_p, shift_s = h_p[:, -1], h_s[:, -1]
    _, y = peer(x, ht, 1, norm_final)
    y_p, y_s = split(y)
    return (y_p, y_s, ssm_p[None], conv_p[None], k_p[None], v_p[None], wkv_p[None], shift_p[None],
            ssm_s[None], conv_s[None], k_s[None], v_s[None], wkv_s[None], shift_s[None])
```

```python
import functools
import math

import jax
import jax.numpy as jnp
from jax import lax
from jax.experimental import pallas as pl
from jax.experimental.pallas import tpu as pltpu

f32 = jnp.float32
bf16 = jnp.bfloat16

D_MODEL = 1024
PAST_LEN = 16384
SSM_HEAD_DIM = 64
SSM_HEADS = 16
SSM_INNER = 1024
SSM_GROUPS = 2
SSM_STATE = 128
CONV_W = 4
CONV_DIM = 1536
SSD_CHUNK = 128
ATT_HEAD_DIM = 64
ATT_HEADS = 16
ATT_KV_HEADS = 4
ATT_GQA = 4
WINDOW = 128
ATT_BLOCK = 128
REL_BUCKETS = 32
REL_MAX_DIST = 128
RWKV_HEAD = 64
RWKV_HEADS = 16
PEER_HEADS = 8
N_KEYS = 128
N_EXPERTS = N_KEYS * N_KEYS
PEER_TOPK = 16
PEER_HALF = 128
EPS = 1e-5
GN_EPS = 64e-5

LANE = 128
SUBLANE = 8
HALF_LANE = LANE // 2
VMEM_LIMIT = 56 * 2 ** 20
HIGHEST = lax.Precision.HIGHEST
NT_DIMS = (((1,), (1,)), ((), ()))
TN_DIMS = (((0,), (0,)), ((), ()))

_CAND_COUNTS = [PEER_TOPK // (a + 1) for a in range(PEER_TOPK)]
_N_CAND = sum(_CAND_COUNTS)
_N_CAND_PAD = -(-_N_CAND // 8) * 8
_BIG = 1e9


def _cparams(*sem):
    return pltpu.CompilerParams(dimension_semantics=sem, vmem_limit_bytes=VMEM_LIMIT)


def _rms(x, g):
    return x * lax.rsqrt(jnp.mean(x * x, axis=-1, keepdims=True) + EPS) * g


def _silu(x):
    return x * jax.nn.sigmoid(x)


BF16_ROWS = 2 * SUBLANE


def _dup_bf16(v):
    u = pltpu.bitcast(v.astype(bf16).astype(f32), jnp.int32)
    return u | lax.shift_right_logical(u, jnp.int32(16))


GATE_ROWS = N_KEYS


def _row_as_bf16_tile(ref, h, i, rows):
    row = ref[h, pl.ds(i, 1), :]
    tile = pltpu.bitcast(jnp.broadcast_to(row, (SUBLANE, row.shape[1])), bf16)
    return pltpu.repeat(tile, rows // BF16_ROWS, axis=0)


def _peer_select_kernel(ht_ref, wqt_ref, sk_ref, pos_ref, b0_ref, r1_ref, e0_ref, e1_ref,
                        qt_ref, s_ref, vals_ref, idx_ref, cs_ref):
    tm = ht_ref.shape[1]
    qt_ref[...] = jnp.dot(wqt_ref[...], ht_ref[...], preferred_element_type=f32).astype(bf16)
    row = lax.broadcasted_iota(jnp.int32, (N_KEYS, tm), 0).astype(f32)
    pos = pos_ref[...]
    neg_inf = f32(-jnp.inf)

    def head_compute(h, exact):
        bad = jnp.zeros((1, tm), f32)
        for c in (0, 1):
            off = pl.multiple_of((h * 2 + c) * PEER_HALF, PEER_HALF)
            s_ref[c] = jnp.dot(sk_ref[h, c], qt_ref[pl.ds(off, PEER_HALF), :], preferred_element_type=f32)

        bad_tiles = []
        for lt in range(tm // LANE):
            ls = slice(lt * LANE, (lt + 1) * LANE)
            row_t = row[:, :LANE]

            def remove_max(k, s, c, lt=lt, row_t=row_t):
                m = jnp.max(s, axis=0, keepdims=True)
                vals_ref[c, lt, pl.ds(k, 1), :] = m
                if exact:
                    idx = jnp.min(jnp.where(s == m, row_t, f32(N_KEYS)), axis=0, keepdims=True)
                    idx_ref[c, lt, pl.ds(k, 1), :] = idx
                    hit = row_t == idx
                else:
                    hit = s == m
                return jnp.where(hit, neg_inf, s), hit

            def extract(k, st, remove_max=remove_max):
                sa, sb, rank = st
                sa, _ = remove_max(k, sa, 0)
                sb, hit = remove_max(k, sb, 1)
                return sa, sb, jnp.where(hit, jnp.asarray(k, f32), rank)

            sa_fin, sb_fin, rank = lax.fori_loop(
                0, PEER_TOPK, extract, (s_ref[0, :, ls], s_ref[1, :, ls], jnp.full((N_KEYS, LANE), f32(PEER_TOPK))))
            r1_ref[h, :, ls] = rank.astype(bf16)
            if not exact:
                bad_t = jnp.zeros((1, LANE), f32)
                for s_fin in (sa_fin, sb_fin):
                    removed = jnp.sum(jnp.where(s_fin == neg_inf, f32(1.0), f32(0.0)), axis=0, keepdims=True)
                    bad_t = bad_t + jnp.abs(removed - f32(PEER_TOPK))
                bad_tiles.append(bad_t)
        if not exact:
            bad = bad + jnp.concatenate(bad_tiles, axis=1)

        def lane_tiles(ref, c):
            return jnp.concatenate([ref[c, lt] for lt in range(tm // LANE)], axis=1)

        v0 = lane_tiles(vals_ref, 0)
        v1 = lane_tiles(vals_ref, 1)
        r = 0
        for a in range(PEER_TOPK):
            nb = _CAND_COUNTS[a]
            cs_ref[r:r + nb, :] = v0[a:a + 1, :] + v1[0:nb, :]
            r += nb
        cs_ref[_N_CAND:_N_CAND_PAD, :] = jnp.full((_N_CAND_PAD - _N_CAND, tm), neg_inf)
        m1 = v0[0:1, :] + v1[0:1, :]

        cs = cs_ref[...]
        if exact:
            def extract2(k, st):
                cs, sel, z = st
                m = jnp.max(cs, axis=0, keepdims=True)
                p = jnp.min(jnp.where(cs == m, pos, f32(_BIG)), axis=0, keepdims=True)
                hit = pos == p
                return jnp.where(hit, neg_inf, cs), jnp.where(hit, f32(1.0), sel), z + jnp.exp(m - m1)

            _, sel, z = lax.fori_loop(0, PEER_TOPK, extract2,
                                      (cs, jnp.zeros((_N_CAND_PAD, tm), f32), jnp.zeros((1, tm), f32)))
        else:
            larger = jnp.zeros((_N_CAND_PAD, tm), f32)
            for r in range(_N_CAND):
                larger = larger + jnp.where(cs[r:r + 1, :] > cs, f32(1.0), f32(0.0))
            sel = jnp.where((larger < f32(PEER_TOPK)) & (cs > neg_inf), f32(1.0), f32(0.0))
            z = jnp.sum(sel * jnp.exp(cs - m1), axis=0, keepdims=True)
            bad = bad + jnp.abs(jnp.sum(sel, axis=0, keepdims=True) - f32(PEER_TOPK))
        s0 = s_ref[0]
        idx0 = lane_tiles(idx_ref, 0)
        bound0 = jnp.zeros((N_KEYS, tm), f32)
        r = 0
        for a in range(PEER_TOPK):
            nb = _CAND_COUNTS[a]
            cnt = jnp.sum(sel[r:r + nb, :], axis=0, keepdims=True)
            is_a = (row == idx0[a:a + 1, :]) if exact else (s0 == v0[a:a + 1, :])
            bound0 = jnp.where(is_a, cnt, bound0)
            r += nb
        b0_ref[h] = _dup_bf16(bound0)
        e0_ref[h] = _dup_bf16(jnp.exp(s0 - v0[0:1, :]) / z)
        e1_ref[h] = jnp.exp(s_ref[1] - v1[0:1, :]).astype(bf16)
        return bad

    def head_body(h, carry):
        bad = head_compute(h, False)

        @pl.when(jnp.max(bad) > 0.0)
        def _():
            head_compute(h, True)

        return carry

    lax.fori_loop(0, PEER_HEADS, head_body, 0)


def _peer_main_kernel(ht_ref, u_ref, v_ref, b0_ref, r1_ref, e0_ref, e1_ref, x_ref, gn_ref, o_ref, hn_ref,
                      acc_ref, *, nblk, npieces):
    j = pl.program_id(1)

    @pl.when(j == 0)
    def _():
        acc_ref[...] = jnp.zeros_like(acc_ref)

    ht = ht_ref[...]
    bpp = nblk // npieces
    pc = bpp * N_KEYS
    for p in range(npieces):
        at = jnp.dot(u_ref[p * pc:(p + 1) * pc, :], ht, preferred_element_type=f32)
        ws = []
        for b in range(bpp):
            i = j * nblk + p * bpp + b
            rows_h = [(_row_as_bf16_tile(b0_ref, h, i, GATE_ROWS), _row_as_bf16_tile(e0_ref, h, i, GATE_ROWS))
                      for h in range(PEER_HEADS)]
            for sb in range(N_KEYS // GATE_ROWS):
                js = slice(sb * GATE_ROWS, (sb + 1) * GATE_ROWS)
                a = at[b * N_KEYS + sb * GATE_ROWS:b * N_KEYS + (sb + 1) * GATE_ROWS, :]
                g = jnp.zeros(a.shape, bf16)
                for h in range(PEER_HEADS):
                    bnd, e0 = rows_h[h]
                    g = g + jnp.where(r1_ref[h, js, :] < bnd, e1_ref[h, js, :] * e0, bf16(0.0))
                gelu = 0.5 * a * (1.0 + lax.erf(a * f32(math.sqrt(0.5))))
                ws.append(g * gelu.astype(bf16))
        wt = jnp.concatenate(ws, axis=0) if bpp > 1 else ws[0]
        acc_ref[...] += lax.dot_general(wt, v_ref[p * pc:(p + 1) * pc, :], TN_DIMS,
                                        preferred_element_type=f32)

    @pl.when(j == pl.num_programs(1) - 1)
    def _():
        xn = x_ref[...] + acc_ref[...]
        o_ref[...] = xn
        hn_ref[...] = _rms(xn, gn_ref[...])


def peer_ffn_residual(x, ht, wqt, sk, u, v, g_next, tm_sel=256, tm=256, ce=4096, npieces=16):
    t = x.shape[0]
    d = D_MODEL
    pos_list = []
    for a in range(PEER_TOPK):
        pos_list += [a * PEER_TOPK + b for b in range(_CAND_COUNTS[a])]
    pos_list += [_BIG] * (_N_CAND_PAD - _N_CAND)
    pos = jnp.broadcast_to(jnp.asarray(pos_list, f32)[:, None], (_N_CAND_PAD, tm_sel))
    sel_shape = jax.ShapeDtypeStruct((PEER_HEADS, N_KEYS, t), jnp.int32)
    sel_shape16 = jax.ShapeDtypeStruct((PEER_HEADS, N_KEYS, t), bf16)
    sel_spec = pl.BlockSpec((PEER_HEADS, N_KEYS, tm_sel), lambda i: (0, 0, i))
    b0, r1, e0, e1 = pl.pallas_call(
        _peer_select_kernel,
        grid=(t // tm_sel,),
        in_specs=[pl.BlockSpec((d, tm_sel), lambda i: (0, i)),
                  pl.BlockSpec((PEER_HEADS * 2 * PEER_HALF, d), lambda i: (0, 0)),
                  pl.BlockSpec((PEER_HEADS, 2, N_KEYS, PEER_HALF), lambda i: (0, 0, 0, 0)),
                  pl.BlockSpec((_N_CAND_PAD, tm_sel), lambda i: (0, 0))],
        out_specs=[sel_spec] * 4,
        out_shape=[sel_shape, sel_shape16, sel_shape, sel_shape16],
        scratch_shapes=[pltpu.VMEM((PEER_HEADS * 2 * PEER_HALF, tm_sel), bf16),
                        pltpu.VMEM((2, N_KEYS, tm_sel), f32),
                        pltpu.VMEM((2, tm_sel // LANE, PEER_TOPK, LANE), f32),
                        pltpu.VMEM((2, tm_sel // LANE, PEER_TOPK, LANE), f32),
                        pltpu.VMEM((_N_CAND_PAD, tm_sel), f32)],
        compiler_params=_cparams("parallel"),
        name="peer_select",
    )(ht, wqt, sk, pos)

    nblk = ce // N_KEYS
    sel_spec2 = pl.BlockSpec((PEER_HEADS, N_KEYS, tm), lambda i, j: (0, 0, i))
    return pl.pallas_call(
        functools.partial(_peer_main_kernel, nblk=nblk, npieces=npieces),
        grid=(t // tm, N_EXPERTS // ce),
        in_specs=[pl.BlockSpec((d, tm), lambda i, j: (0, i)),
                  pl.BlockSpec((ce, d), lambda i, j: (j, 0)),
                  pl.BlockSpec((ce, d), lambda i, j: (j, 0)),
                  sel_spec2, sel_spec2, sel_spec2, sel_spec2,
                  pl.BlockSpec((tm, d), lambda i, j: (i, 0)), pl.BlockSpec((1, d), lambda i, j: (0, 0))],
        out_specs=[pl.BlockSpec((tm, d), lambda i, j: (i, 0))] * 2,
        out_shape=[jax.ShapeDtypeStruct((t, d), f32)] * 2,
        scratch_shapes=[pltpu.VMEM((tm, d), f32)],
        compiler_params=_cparams("parallel", "arbitrary"),
        name="peer_main",
    )(ht, u, v, b0, r1, e0, e1, x, g_next.reshape(1, d))


IN_WIDTHS = (SSM_INNER, CONV_DIM, ATT_HEADS * ATT_HEAD_DIM, 2 * ATT_KV_HEADS * ATT_HEAD_DIM, LANE)


def _in_proj_kernel(x_ref, g_ref, w_ref, *out_refs):
    h = _rms(x_ref[...], g_ref[...])
    y = jnp.dot(h.astype(bf16), w_ref[...], preferred_element_type=f32)
    o = 0
    for ref in out_refs:
        w = ref.shape[1]
        ref[...] = y[:, o:o + w]
        o += w


def in_proj(x, g, w_in, tm=256):
    t, d = x.shape
    o1 = SSM_INNER
    o2 = o1 + CONV_DIM
    o3 = o2 + SSM_HEADS
    w = jnp.concatenate([w_in[:, :o2], w_in[:, o3:], w_in[:, o2:o3],
                         jnp.zeros((d, LANE - SSM_HEADS), w_in.dtype)], axis=1).astype(bf16)
    n = w.shape[1]
    return pl.pallas_call(
        _in_proj_kernel,
        grid=(t // tm,),
        in_specs=[pl.BlockSpec((tm, d), lambda i: (i, 0)), pl.BlockSpec((1, d), lambda i: (0, 0)),
                  pl.BlockSpec((d, n), lambda i: (0, 0))],
        out_specs=[pl.BlockSpec((tm, wd), lambda i: (i, 0)) for wd in IN_WIDTHS],
        out_shape=[jax.ShapeDtypeStruct((t, wd), f32) for wd in IN_WIDTHS],
        compiler_params=_cparams("parallel"),
        name="in_proj",
    )(x, g.reshape(1, d), w)


N_PAIRS = SSM_HEADS // 2
SSD_SHORT_CHUNK = 32


def _ssd_kernel(*refs, lin, lc):
    (z_ref, xbc_ref, dt_ref, cs_ref, h0_ref, cw_ref, cb_ref, dtb_ref, aneg_ref, dsk_ref, gn_ref, selh_ref) = refs[:12]
    refs = refs[12:]
    y_ref, hfin_ref, tail_ref, xe_ref, hp_ref, ys_ref = refs[:6]
    c = pl.program_id(1)

    @pl.when(c == 0)
    def _():
        xe_ref[0:SUBLANE, :] = cs_ref[0]
        hp_ref[...] = h0_ref[0]

    if lin == lc:
        xe_ref[SUBLANE:SUBLANE + lc, :] = xbc_ref[...]
        z = z_ref[...]
        dt_raw = dt_ref[...]
    else:
        zpad_ref, dtpad_ref = refs[6:8]
        xe_ref[SUBLANE:SUBLANE + lc, :] = jnp.zeros((lc, CONV_DIM), f32)
        xe_ref[SUBLANE:SUBLANE + lin, :] = xbc_ref[...]
        zpad_ref[...] = jnp.zeros_like(zpad_ref)
        zpad_ref[0:lin, :] = z_ref[...]
        dtpad_ref[...] = jnp.zeros_like(dtpad_ref)
        dtpad_ref[0:lin, :] = dt_ref[...]
        z = zpad_ref[...]
        dt_raw = dtpad_ref[...]

    conv = cb_ref[...]
    for j in range(CONV_W):
        o = SUBLANE - (CONV_W - 1) + j
        conv = conv + xe_ref[o:o + lc, :] * cw_ref[j:j + 1, :]
    if lin == lc:
        xe_ref[0:SUBLANE, :] = xe_ref[lc:lc + SUBLANE, :]
    xc = _silu(conv)
    xs = xc[:, :SSM_INNER]
    bm = xc[:, SSM_INNER:SSM_INNER + SSM_GROUPS * SSM_STATE]
    cm = xc[:, SSM_INNER + SSM_GROUPS * SSM_STATE:]

    row = lax.broadcasted_iota(jnp.int32, (lc, LANE), 0)
    col = lax.broadcasted_iota(jnp.int32, (lc, LANE), 1)
    causal = lax.broadcasted_iota(jnp.int32, (lc, lc), 0) >= lax.broadcasted_iota(jnp.int32, (lc, lc), 1)
    pair_row = lax.broadcasted_iota(jnp.int32, (LANE, SSM_STATE), 0)
    lane_lo = col < HALF_LANE
    neg_inf = f32(-jnp.inf)

    x = dt_raw + dtb_ref[...]
    dt = jnp.maximum(x, 0.0) + jnp.log(1.0 + jnp.exp(-jnp.abs(x)))
    if lin != lc:
        dt = jnp.where(row < lin, dt, 0.0)
    la = dt * aneg_ref[...]
    acs = jnp.dot(causal.astype(f32), la, precision=HIGHEST, preferred_element_type=f32)
    acs_sq = acs if lc == LANE else jnp.concatenate([acs, jnp.zeros((LANE - lc, LANE), f32)], axis=0)
    acs_t = acs_sq.T[:, :lc]
    selh = selh_ref[...]
    dt_exp = jnp.dot(dt, selh, precision=HIGHEST, preferred_element_type=f32)
    acs_exp = jnp.dot(acs, selh, precision=HIGHEST, preferred_element_type=f32)
    alast = acs[lc - 1:lc, :]
    xdt = xs * dt_exp
    eacs = jnp.exp(acs_exp)
    xdt_end = (xdt * jnp.exp(acs_exp[lc - 1:lc, :] - acs_exp)).astype(bf16)
    dsk = dsk_ref[...]

    for g in range(SSM_GROUPS):
        cmg = cm[:, g * SSM_STATE:(g + 1) * SSM_STATE].astype(bf16)
        bmg = bm[:, g * SSM_STATE:(g + 1) * SSM_STATE].astype(bf16)
        cb = lax.dot_general(cmg, bmg, NT_DIMS, preferred_element_type=f32)
        for jj in range(N_PAIRS // SSM_GROUPS):
            j = g * (N_PAIRS // SSM_GROUPS) + jj
            sl = slice(j * LANE, (j + 1) * LANE)
            xdt_pair = xdt[:, sl]
            ydiag = jnp.zeros((lc, LANE), f32)
            for half in (0, 1):
                h = 2 * j + half
                seg = acs[:, h:h + 1] - acs_t[h:h + 1, :]
                dec = jnp.exp(jnp.where(causal, seg, neg_inf))
                m = (cb * dec).astype(bf16)
                keep = lane_lo if half == 0 else jnp.logical_not(lane_lo)
                xd = jnp.where(keep, xdt_pair, 0.0).astype(bf16)
                ydiag = ydiag + jnp.dot(m, xd, preferred_element_type=f32)
            hpj = hp_ref[j]
            yoff = lax.dot_general(cmg, hpj.astype(bf16), NT_DIMS, preferred_element_type=f32) * eacs[:, sl]
            s_new = lax.dot_general(xdt_end[:, sl], bmg, TN_DIMS, preferred_element_type=f32)
            dl = jnp.where(pair_row < SSM_HEAD_DIM, alast[:, 2 * j:2 * j + 1], alast[:, 2 * j + 1:2 * j + 2])
            hp_ref[j] = hpj * jnp.exp(dl) + s_new
            ys_ref[:, sl] = ydiag + yoff + dsk[:, sl] * xs[:, sl]

    y = ys_ref[...] * _silu(z)
    gs = SSM_INNER // SSM_GROUPS
    gn = gn_ref[...]
    for g in range(SSM_GROUPS):
        yg = y[:, g * gs:(g + 1) * gs]
        yg = yg * lax.rsqrt(jnp.mean(yg * yg, axis=-1, keepdims=True) + EPS) * gn[:, g * gs:(g + 1) * gs]
        y_ref[:, g * gs:(g + 1) * gs] = yg[0:lin, :]

    @pl.when(c == pl.num_programs(1) - 1)
    def _():
        hfin_ref[0] = hp_ref[...]
        tail_ref[0] = xe_ref[lin:lin + SUBLANE, :]


def ssd_mixer(z, xbc, dt, conv_state, ssm_state, params, *, batch, seq, row0):
    conv_w, conv_b, dt_bias, a_neg, d_skip, gnorm, selh = params
    lin = min(seq, SSD_CHUNK)
    lc = SSD_CHUNK if lin == SSD_CHUNK else SSD_SHORT_CHUNK
    nc = seq // lin
    blk0 = row0 // lin
    cs = jnp.pad(conv_state, ((0, 0), (SUBLANE - (CONV_W - 1), 0), (0, 0)))
    h0 = ssm_state.reshape(batch, N_PAIRS, LANE, SSM_STATE)

    def rows(w):
        return pl.BlockSpec((lin, w), lambda b, c: (blk0 + b * nc + c, 0))

    def const(shape):
        return pl.BlockSpec(shape, lambda b, c: (0,) * len(shape))

    in_specs = [rows(SSM_INNER), rows(CONV_DIM), rows(LANE),
                pl.BlockSpec((1, SUBLANE, CONV_DIM), lambda b, c: (b, 0, 0)),
                pl.BlockSpec((1, N_PAIRS, LANE, SSM_STATE), lambda b, c: (b, 0, 0, 0)),
                const((CONV_W, CONV_DIM)), const((1, CONV_DIM)), const((1, LANE)), const((1, LANE)),
                const((1, SSM_INNER)), const((1, SSM_INNER)), const((LANE, SSM_INNER))]
    args = [z, xbc, dt, cs, h0, conv_w, conv_b, dt_bias, a_neg, d_skip, gnorm, selh]
    scratch = [pltpu.VMEM((lc + 2 * SUBLANE, CONV_DIM), f32), pltpu.VMEM((N_PAIRS, LANE, SSM_STATE), f32),
               pltpu.VMEM((lc, SSM_INNER), f32)]
    if lin != lc:
        scratch += [pltpu.VMEM((lc, SSM_INNER), f32), pltpu.VMEM((lc, LANE), f32)]
    y, h_fin, tail = pl.pallas_call(
        functools.partial(_ssd_kernel, lin=lin, lc=lc),
        grid=(batch, nc),
        in_specs=in_specs,
        out_specs=[pl.BlockSpec((lin, SSM_INNER), lambda b, c: (b * nc + c, 0)),
                   pl.BlockSpec((1, N_PAIRS, LANE, SSM_STATE), lambda b, c: (b, 0, 0, 0)),
                   pl.BlockSpec((1, SUBLANE, CONV_DIM), lambda b, c: (b, 0, 0))],
        out_shape=[jax.ShapeDtypeStruct((batch * seq, SSM_INNER), f32),
                   jax.ShapeDtypeStruct((batch, N_PAIRS, LANE, SSM_STATE), f32),
                   jax.ShapeDtypeStruct((batch, SUBLANE, CONV_DIM), f32)],
        scratch_shapes=scratch,
        compiler_params=_cparams("parallel", "arbitrary"),
        name="ssd_mixer",
    )(*args)
    new_conv = tail[:, SUBLANE - (CONV_W - 1):]
    return y, h_fin.reshape(batch, SSM_HEADS, SSM_HEAD_DIM, SSM_STATE), new_conv


def _swa_kernel(*refs, lq, masked_first):
    q_ref, kp_ref, vp_ref, kc_ref, vc_ref, bias_ref, sink_ref = refs[:7]
    refs = refs[7:]
    o_ref, nk_ref, nv_ref = refs[:3]
    n = pl.program_id(1)
    q = q_ref[...]
    for new_ref, prev_ref, cur_ref in ((nk_ref, kp_ref, kc_ref), (nv_ref, vp_ref, vc_ref)):
        if lq < WINDOW:
            new_ref[0:WINDOW - lq, :] = prev_ref[lq:WINDOW, :]
        new_ref[WINDOW - lq:WINDOW, :] = cur_ref[...]
    if lq == ATT_BLOCK:
        kc = kc_ref[...]
        vc = vc_ref[...]
    else:
        kpad_ref, vpad_ref = refs[3:5]
        kpad_ref[...] = jnp.zeros_like(kpad_ref)
        vpad_ref[...] = jnp.zeros_like(vpad_ref)
        kpad_ref[0:lq, :] = kc_ref[...]
        vpad_ref[0:lq, :] = vc_ref[...]
        kc = kpad_ref[...]
        vc = vpad_ref[...]
    kp = kp_ref[...]
    vp = vp_ref[...]
    lane_lo = lax.broadcasted_iota(jnp.int32, (ATT_BLOCK, LANE), 1) < HALF_LANE
    lane_lo_q = lax.broadcasted_iota(jnp.int32, (lq, LANE), 1) < HALF_LANE
    neg_inf = f32(-jnp.inf)
    scale = f32(ATT_HEAD_DIM ** -0.5)

    for g in range(ATT_KV_HEADS):
        sl = slice((g // 2) * LANE, (g // 2 + 1) * LANE)
        odd = g % 2 == 1

        def kpad(k):
            pair = k[:, sl]
            if odd:
                pair = pltpu.roll(pair, HALF_LANE, 1)
            return jnp.where(lane_lo, pair, 0.0).astype(bf16)

        def vdup(v):
            pair = v[:, sl]
            rolled = pltpu.roll(pair, HALF_LANE, 1)
            return (jnp.where(lane_lo, rolled, pair) if odd else jnp.where(lane_lo, pair, rolled)).astype(bf16)

        qp0 = q[:, (2 * g) * LANE:(2 * g + 1) * LANE]
        qp1 = q[:, (2 * g + 1) * LANE:(2 * g + 2) * LANE]
        qg = jnp.concatenate([qp0, pltpu.roll(qp0, HALF_LANE, 1), qp1, pltpu.roll(qp1, HALF_LANE, 1)],
                             axis=0).astype(bf16)
        bias = bias_ref[g]
        sp = lax.dot_general(qg, kpad(kp), NT_DIMS, preferred_element_type=f32) * scale + bias[:, :WINDOW]
        sc = lax.dot_general(qg, kpad(kc), NT_DIMS, preferred_element_type=f32) * scale + bias[:, WINDOW:]
        if masked_first:
            sp = jnp.where(n > 0, sp, neg_inf)
        sink = sink_ref[g][:, 0:1]
        m = jnp.maximum(jnp.maximum(jnp.max(sp, axis=-1, keepdims=True), jnp.max(sc, axis=-1, keepdims=True)), sink)
        pp = jnp.exp(sp - m)
        pc = jnp.exp(sc - m)
        denom = jnp.sum(pp, axis=-1, keepdims=True) + jnp.sum(pc, axis=-1, keepdims=True) + jnp.exp(sink - m)
        og = (jnp.dot(pp.astype(bf16), vdup(vp), preferred_element_type=f32)
              + jnp.dot(pc.astype(bf16), vdup(vc), preferred_element_type=f32)) / denom
        o_ref[:, (2 * g) * LANE:(2 * g + 1) * LANE] = jnp.where(lane_lo_q, og[0:lq], og[lq:2 * lq])
        o_ref[:, (2 * g + 1) * LANE:(2 * g + 2) * LANE] = jnp.where(lane_lo_q, og[2 * lq:3 * lq], og[3 * lq:4 * lq])


def _rel_bucket(dist):
    exact = REL_BUCKETS // 2
    d = jnp.maximum(dist, 0)
    large = exact + (jnp.log(jnp.maximum(d, 1).astype(f32) / exact)
                     / math.log(REL_MAX_DIST / exact) * (REL_BUCKETS - exact)).astype(jnp.int32)
    large = jnp.minimum(large, REL_BUCKETS - 1)
    return jnp.where(d < exact, d, large)


def _swa_tables(rel_bias, sinks, lq):
    qi = jnp.arange(lq)[:, None]
    kj = jnp.arange(2 * ATT_BLOCK)[None, :]
    dist = qi + WINDOW - kj
    band = (dist >= 0) & (dist <= WINDOW)
    onehot = (_rel_bucket(dist)[..., None] == jnp.arange(REL_BUCKETS)).astype(f32)
    bias = jnp.einsum('qkb,bh->qkh', onehot, rel_bias, precision=HIGHEST)
    bias = jnp.where(band[..., None], bias, -jnp.inf)
    bias = jnp.transpose(bias, (2, 0, 1)).reshape(ATT_KV_HEADS, ATT_GQA * lq, 2 * ATT_BLOCK)
    sink = jnp.broadcast_to(sinks.reshape(ATT_KV_HEADS, ATT_GQA, 1, 1), (ATT_KV_HEADS, ATT_GQA, lq, LANE))
    return bias, sink.reshape(ATT_KV_HEADS, ATT_GQA * lq, LANE)


def swa_mixer(q, kv, k_prev, v_prev, rel_bias, sinks, *, batch, seq, row0):
    lq = min(seq, ATT_BLOCK)
    nb = seq // lq
    blk0 = row0 // lq
    kvw = ATT_KV_HEADS * ATT_HEAD_DIM
    bias, sink = _swa_tables(rel_bias, sinks, lq)
    cur_k = pl.BlockSpec((lq, kvw), lambda b, n: (blk0 + b * nb + n, 0))
    cur_v = pl.BlockSpec((lq, kvw), lambda b, n: (blk0 + b * nb + n, 1))
    if k_prev is None:
        prev_k = pl.BlockSpec((WINDOW, kvw), lambda b, n: (blk0 + b * nb + jnp.maximum(n - 1, 0), 0))
        prev_v = pl.BlockSpec((WINDOW, kvw), lambda b, n: (blk0 + b * nb + jnp.maximum(n - 1, 0), 1))
        kp_arr, vp_arr = kv, kv
    else:
        prev_k = pl.BlockSpec((WINDOW, kvw), lambda b, n: (b, 0))
        prev_v = prev_k
        kp_arr = k_prev.reshape(batch * WINDOW, kvw)
        vp_arr = v_prev.reshape(batch * WINDOW, kvw)
    rows = pl.BlockSpec((lq, ATT_HEADS * ATT_HEAD_DIM), lambda b, n: (blk0 + b * nb + n, 0))
    in_specs = [rows, prev_k, prev_v, cur_k, cur_v,
                pl.BlockSpec(bias.shape, lambda b, n: (0, 0, 0)), pl.BlockSpec(sink.shape, lambda b, n: (0, 0, 0))]
    args = [q, kp_arr, vp_arr, kv, kv, bias, sink]
    scratch = [] if lq == ATT_BLOCK else [pltpu.VMEM((ATT_BLOCK, kvw), f32), pltpu.VMEM((ATT_BLOCK, kvw), f32)]
    cache_spec = pl.BlockSpec((WINDOW, kvw), lambda b, n: (b, 0))
    cache_shape = jax.ShapeDtypeStruct((batch * WINDOW, kvw), f32)
    o, new_k, new_v = pl.pallas_call(
        functools.partial(_swa_kernel, lq=lq, masked_first=k_prev is None),
        grid=(batch, nb),
        in_specs=in_specs,
        out_specs=[pl.BlockSpec((lq, ATT_HEADS * ATT_HEAD_DIM), lambda b, n: (b * nb + n, 0)), cache_spec, cache_spec],
        out_shape=[jax.ShapeDtypeStruct((batch * seq, ATT_HEADS * ATT_HEAD_DIM), f32), cache_shape, cache_shape],
        scratch_shapes=scratch,
        compiler_params=_cparams("parallel", "arbitrary"),
        name="swa_mixer",
    )(*args)
    cshape = (batch, WINDOW, ATT_KV_HEADS, ATT_HEAD_DIM)
    return o, new_k.reshape(cshape), new_v.reshape(cshape)


def _out_proj_kernel(yap_ref, ybp_ref, yas_ref, ybs_ref, w_ref, x_ref, g_ref, o_ref, ht_ref, *, n_prompt):
    ka = yap_ref.shape[1]

    def body(ya_ref, yb_ref):
        acc = jnp.dot(ya_ref[...].astype(bf16), w_ref[0:ka, :], preferred_element_type=f32)
        acc = acc + jnp.dot(yb_ref[...].astype(bf16), w_ref[ka:, :], preferred_element_type=f32)
        xn = x_ref[...] + acc
        o_ref[...] = xn
        ht_ref[...] = _rms(xn, g_ref[...]).T.astype(bf16)

    @pl.when(pl.program_id(0) < n_prompt)
    def _():
        body(yap_ref, ybp_ref)

    @pl.when(pl.program_id(0) >= n_prompt)
    def _():
        body(yas_ref, ybs_ref)


def out_proj(ya_p, yb_p, ya_s, yb_s, w, x, g, tm=512):
    t, d = x.shape
    ka, kb = ya_p.shape[1], yb_p.shape[1]
    n_p = ya_p.shape[0] // tm
    n_s = ya_s.shape[0] // tm

    def p_rows(k):
        return pl.BlockSpec((tm, k), lambda i: (jnp.minimum(i, n_p - 1), 0))

    def s_rows(k):
        return pl.BlockSpec((tm, k), lambda i: (jnp.maximum(i - n_p, 0), 0))

    return pl.pallas_call(
        functools.partial(_out_proj_kernel, n_prompt=n_p),
        grid=(n_p + n_s,),
        in_specs=[p_rows(ka), p_rows(kb), s_rows(ka), s_rows(kb),
                  pl.BlockSpec((ka + kb, d), lambda i: (0, 0)), pl.BlockSpec((tm, d), lambda i: (i, 0)),
                  pl.BlockSpec((1, d), lambda i: (0, 0))],
        out_specs=[pl.BlockSpec((tm, d), lambda i: (i, 0)), pl.BlockSpec((d, tm), lambda i: (0, i))],
        out_shape=[jax.ShapeDtypeStruct((t, d), f32), jax.ShapeDtypeStruct((d, t), bf16)],
        compiler_params=_cparams("arbitrary"),
        name="out_proj",
    )(ya_p, yb_p, ya_s, yb_s, w, x, g.reshape(1, d))


def _gated_out_kernel(yp_ref, gp_ref, ys_ref, gs_ref, w_ref, x_ref, gn_ref, o_ref, ht_ref, *, n_prompt):
    def body(y_ref, g_ref):
        a = (y_ref[...] * g_ref[...]).astype(bf16)
        xn = x_ref[...] + jnp.dot(a, w_ref[...], preferred_element_type=f32)
        o_ref[...] = xn
        ht_ref[...] = _rms(xn, gn_ref[...]).T.astype(bf16)

    @pl.when(pl.program_id(0) < n_prompt)
    def _():
        body(yp_ref, gp_ref)

    @pl.when(pl.program_id(0) >= n_prompt)
    def _():
        body(ys_ref, gs_ref)


def gated_out_proj(y_p, g_p, y_s, g_s, w, x, gn, tm=512):
    t, d = x.shape
    n_p = y_p.shape[0] // tm
    n_s = y_s.shape[0] // tm
    rows = pl.BlockSpec((tm, d), lambda i: (i, 0))
    p_rows = pl.BlockSpec((tm, d), lambda i: (jnp.minimum(i, n_p - 1), 0))
    s_rows = pl.BlockSpec((tm, d), lambda i: (jnp.maximum(i - n_p, 0), 0))
    return pl.pallas_call(
        functools.partial(_gated_out_kernel, n_prompt=n_p),
        grid=(n_p + n_s,),
        in_specs=[p_rows, p_rows, s_rows, s_rows,
                  pl.BlockSpec((d, d), lambda i: (0, 0)), rows, pl.BlockSpec((1, d), lambda i: (0, 0))],
        out_specs=[rows, pl.BlockSpec((d, tm), lambda i: (0, i))],
        out_shape=[jax.ShapeDtypeStruct((t, d), f32), jax.ShapeDtypeStruct((d, t), bf16)],
        compiler_params=_cparams("arbitrary"),
        name="rwkv_out",
    )(y_p, g_p, y_s, g_s, w, x, gn.reshape(1, d))


LORA_PAD = 2 * LANE


def _softplus(x):
    return jnp.maximum(x, 0.0) + jnp.log(1.0 + jnp.exp(-jnp.abs(x)))


def _rwkv_pre_kernel(h_ref, p_ref, mu_ref, wrkv_ref, l1_ref, l2_ref, vec_ref,
                     r_ref, d_ref, k_ref, v_ref, a_ref, g_ref):
    h = h_ref[...]
    xx = p_ref[...] - h

    def mix(j):
        return (h + xx * mu_ref[j:j + 1, :]).astype(bf16)

    def mm(a, w):
        return jnp.dot(a, w, preferred_element_type=f32)

    r_ref[...] = mm(mix(0), wrkv_ref[0])
    k_ref[...] = mm(mix(2), wrkv_ref[1])
    v_ref[...] = mm(mix(3), wrkv_ref[2])
    wl = vec_ref[0:1, :] + mm(jnp.tanh(mm(mix(1), l1_ref[0])).astype(bf16), l2_ref[0])
    al = vec_ref[1:2, :] + mm(mm(mix(4), l1_ref[1]).astype(bf16), l2_ref[1])
    g_ref[...] = mm(jax.nn.sigmoid(mm(mix(5), l1_ref[2])).astype(bf16), l2_ref[2])
    w = -_softplus(-wl) - 0.5
    d_ref[...] = jnp.exp(-jnp.exp(w))
    a_ref[...] = jax.nn.sigmoid(al)


def rwkv_pre(h, prev, mu, w_rkv, lora1, lora2, vecs, *, row0, tm=256):
    t, d = prev.shape
    blk0 = row0 // tm
    rows = pl.BlockSpec((tm, d), lambda i: (i, 0))

    def const(a):
        return pl.BlockSpec(a.shape, lambda i: (0,) * a.ndim)

    return pl.pallas_call(
        _rwkv_pre_kernel,
        grid=(t // tm,),
        in_specs=[pl.BlockSpec((tm, d), lambda i: (blk0 + i, 0)), rows,
                  const(mu), const(w_rkv), const(lora1), const(lora2), const(vecs)],
        out_specs=[rows] * 6,
        out_shape=[jax.ShapeDtypeStruct((t, d), f32)] * 6,
        compiler_params=_cparams("parallel"),
        name="rwkv_pre",
    )(h, prev, mu, w_rkv, lora1, lora2, vecs)


RWKV_VGROUP = 8


RWKV_TB = 16
N_SEQ_TILE = LANE // RWKV_HEADS


def _swap_list_and_lane(arrs):
    n = len(arrs)
    lane = lax.broadcasted_iota(jnp.int32, arrs[0].shape, 1)
    s = n // 2
    while s >= 1:
        upper = (lane & s) != 0
        new = list(arrs)
        for i in range(n):
            if i & s == 0:
                a, b = arrs[i], arrs[i | s]
                new[i] = jnp.where(upper, pltpu.roll(b, s, 1), a)
                new[i | s] = jnp.where(upper, b, pltpu.roll(a, LANE - s, 1))
        arrs = new
        s //= 2
    return arrs


def _natural_to_chain(x_ref, dst_ref, q):
    x2 = x_ref[...].reshape(N_SEQ_TILE * RWKV_TB, D_MODEL)
    xt = x2.T
    arrs = _swap_list_and_lane([xt[h * RWKV_HEAD:(h + 1) * RWKV_HEAD, :] for h in range(RWKV_HEADS)])
    for t in range(RWKV_TB):
        dst_ref[q, t] = arrs[t]


def _chain_to_natural(src_ref, y_ref):
    arrs = _swap_list_and_lane([src_ref[t] for t in range(RWKV_TB)])
    xt = jnp.concatenate(arrs, axis=0)
    y_ref[...] = xt.T.reshape(N_SEQ_TILE, RWKV_TB, D_MODEL)


N_SCAN_IN = 5


def _rwkv_scan_kernel(r_ref, d_ref, k_ref, v_ref, a_ref, s0_ref, tab_ref, y_ref, sfin_ref,
                      s_ref, q_ref, ys_ref, *, natural):
    tb = q_ref.shape[1]

    @pl.when(pl.program_id(1) == 0)
    def _():
        s_ref[...] = s0_ref[...]

    for q, ref in enumerate((r_ref, d_ref, k_ref, v_ref, a_ref)):
        if natural:
            _natural_to_chain(ref, q_ref, q)
        else:
            q_ref[q] = ref[...]
    r_q, d_q, k_q, v_q, a_q = [q_ref.at[q] for q in range(N_SCAN_IN)]

    def step(t, carry):
        k_raw = k_q[t]
        a = a_q[t]
        kkr = k_raw * tab_ref[3]
        nrm = jnp.sqrt(jnp.sum(kkr * kkr, axis=0, keepdims=True))
        kk = kkr / jnp.maximum(nrm, 1e-12)
        d = d_q[t]
        kv = k_raw * (1.0 + (a - 1.0) * tab_ref[4])
        r = r_q[t]
        b = kk * a

        def vgroup(g, c2):
            v0 = pl.multiple_of(g * RWKV_VGROUP, RWKV_VGROUP)
            vrows = v_q[t, pl.ds(v0, RWKV_VGROUP), :]
            ys = []
            for vi in range(RWKV_VGROUP):
                sv = s_ref[v0 + vi]
                sa = -jnp.sum(sv * kk, axis=0, keepdims=True)
                sn = sv * d + sa * b + vrows[vi:vi + 1, :] * kv
                s_ref[v0 + vi] = sn
                ys.append(jnp.sum(sn * r, axis=0, keepdims=True))
            ys_ref[t, pl.ds(v0, RWKV_VGROUP), :] = jnp.concatenate(ys, axis=0)
            return c2

        lax.fori_loop(0, RWKV_HEAD // RWKV_VGROUP, vgroup, 0)
        y = ys_ref[t]
        mean = jnp.mean(y, axis=0, keepdims=True)
        yc = y - mean
        var = jnp.mean(yc * yc, axis=0, keepdims=True)
        bonus = jnp.sum(r * kv * tab_ref[0], axis=0, keepdims=True)
        ys_ref[t] = yc * lax.rsqrt(var + GN_EPS) * tab_ref[1] + tab_ref[2] + bonus * v_q[t]
        return carry

    lax.fori_loop(0, tb, step, 0)
    if natural:
        _chain_to_natural(ys_ref, y_ref)
    else:
        y_ref[...] = ys_ref[...]

    @pl.when(pl.program_id(1) == pl.num_programs(1) - 1)
    def _():
        sfin_ref[...] = s_ref[...]


def rwkv_scan(parts, s0, tab, *, natural):
    hd = RWKV_HEAD
    c = s0.shape[-1]
    if natural:
        nb, L, dm = parts[0].shape
        tb = RWKV_TB
        seq_spec = pl.BlockSpec((N_SEQ_TILE, tb, dm), lambda i, j: (i, j, 0))
        y_shape = jax.ShapeDtypeStruct((nb, L, dm), f32)
    else:
        L = parts[0].shape[0]
        tb = min(L, RWKV_TB)
        seq_spec = pl.BlockSpec((tb, hd, LANE), lambda i, j: (j, 0, i))
        y_shape = jax.ShapeDtypeStruct((L, hd, c), f32)
    st_spec = pl.BlockSpec((hd, hd, LANE), lambda i, j: (0, 0, i))
    return pl.pallas_call(
        functools.partial(_rwkv_scan_kernel, natural=natural),
        grid=(c // LANE, L // tb),
        in_specs=[seq_spec] * N_SCAN_IN + [st_spec, pl.BlockSpec(tab.shape, lambda i, j: (0, 0, 0))],
        out_specs=[seq_spec, st_spec],
        out_shape=[y_shape, jax.ShapeDtypeStruct((hd, hd, c), f32)],
        scratch_shapes=[pltpu.VMEM((hd, hd, LANE), f32), pltpu.VMEM((N_SCAN_IN, tb, hd, LANE), f32),
                        pltpu.VMEM((tb, hd, LANE), f32)],
        compiler_params=_cparams("parallel", "arbitrary"),
        name="rwkv_scan",
    )(*parts, s0, tab)


def _rwkv_core(parts, wkv, tab):
    b, L, _ = parts[0].shape
    s0 = jnp.transpose(wkv, (2, 3, 0, 1)).reshape(RWKV_HEAD, RWKV_HEAD, b * RWKV_HEADS)
    if b % N_SEQ_TILE == 0 and L % RWKV_TB == 0:
        y, s_fin = rwkv_scan(parts, s0, tab, natural=True)
        y = y.reshape(b * L, D_MODEL)
    else:
        def to_chain(t):
            t = jnp.transpose(t.reshape(b, L, RWKV_HEADS, RWKV_HEAD), (1, 3, 0, 2))
            return t.reshape(L, RWKV_HEAD, b * RWKV_HEADS)

        y, s_fin = rwkv_scan([to_chain(t) for t in parts], s0, tab, natural=False)
        y = jnp.transpose(y.reshape(L, RWKV_HEAD, b, RWKV_HEADS), (2, 0, 3, 1)).reshape(b * L, D_MODEL)
    s_fin = jnp.transpose(s_fin.reshape(RWKV_HEAD, RWKV_HEAD, b, RWKV_HEADS), (2, 3, 0, 1))
    return y, s_fin


def kernel(x_prompt, x_sample, state_ssm, state_conv, cache_swa_k, cache_swa_v, state_wkv, state_shift, rel_bias, norm_mix, norm_ffn, norm_final, mix_w_in, ssd_conv_w, ssd_conv_b, ssd_dt_bias, ssd_a_log, ssd_d_skip, ssd_gnorm, attn_sinks, mix_w_out, rwkv_mu, rwkv_w0, rwkv_w1, rwkv_w2, rwkv_a0, rwkv_a1, rwkv_a2, rwkv_g1, rwkv_g2, rwkv_k_k, rwkv_k_a, rwkv_r_k, rwkv_w_rkv, rwkv_w_o, rwkv_ln_w, rwkv_ln_b, peer_w_q, peer_sub_keys, peer_u, peer_v):
    bp, lp, d = x_prompt.shape
    bs, ls, _ = x_sample.shape
    tp = bp * lp
    ts = bs * ls
    tt = tp + ts
    x = jnp.concatenate([x_prompt.reshape(tp, d), x_sample.reshape(ts, d)], axis=0)

    def split(t):
        return t[:tp].reshape(bp, lp, -1), t[tp:].reshape(bs, ls, -1)

    def zero_state(a):
        return jnp.zeros((bp,) + a.shape[2:], a.dtype)

    def peer(x, ht, layer, g_next):
        wqt = peer_w_q[layer].T.astype(bf16)
        return peer_ffn_residual(x, ht, wqt, peer_sub_keys[layer].astype(bf16),
                                 peer_u[layer].astype(bf16), peer_v[layer].astype(bf16), g_next)

    z, xbc, q, kv, dt = in_proj(x, norm_mix[0], mix_w_in[0])
    pad16 = (0, LANE - SSM_HEADS)
    selh = (jnp.arange(LANE)[:, None] == jnp.arange(SSM_INNER)[None, :] // SSM_HEAD_DIM).astype(f32)
    ssd_params = (ssd_conv_w[0], ssd_conv_b[0].reshape(1, CONV_DIM),
                  jnp.pad(ssd_dt_bias[0], pad16).reshape(1, LANE),
                  jnp.pad(-jnp.exp(ssd_a_log[0]), pad16).reshape(1, LANE),
                  jnp.repeat(ssd_d_skip[0], SSM_HEAD_DIM).reshape(1, SSM_INNER),
                  ssd_gnorm[0].reshape(1, SSM_INNER), selh)
    y_ssd_p, ssm_p, conv_p = ssd_mixer(z, xbc, dt, zero_state(state_conv), zero_state(state_ssm), ssd_params,
                                       batch=bp, seq=lp, row0=0)
    y_ssd_s, ssm_s, conv_s = ssd_mixer(z, xbc, dt, state_conv[0], state_ssm[0], ssd_params,
                                       batch=bs, seq=ls, row0=tp)
    o_att_p, k_p, v_p = swa_mixer(q, kv, None, None, rel_bias, attn_sinks[0], batch=bp, seq=lp, row0=0)
    o_att_s, k_s, v_s = swa_mixer(q, kv, cache_swa_k[0], cache_swa_v[0], rel_bias, attn_sinks[0],
                                  batch=bs, seq=ls, row0=tp)
    x, ht = out_proj(y_ssd_p, o_att_p, y_ssd_s, o_att_s, mix_w_out[0].astype(bf16), x, norm_ffn[0])
    x, h = peer(x, ht, 0, norm_mix[1])

    h_p, h_s = split(h)
    prev_p = jnp.concatenate([jnp.zeros((bp, 1, d), f32), h_p[:, :-1]], axis=1)
    prev_s = jnp.concatenate([state_shift[0][:, None], h_s[:, :-1]], axis=1)
    def lora_pair(w1, w2):
        pad = LORA_PAD - w1.shape[1]
        return jnp.pad(w1, ((0, 0), (0, pad))), jnp.pad(w2, ((0, pad), (0, 0)))

    pairs = [lora_pair(rwkv_w1[0], rwkv_w2[0]), lora_pair(rwkv_a1[0], rwkv_a2[0]), lora_pair(rwkv_g1[0], rwkv_g2[0])]
    lora1 = jnp.stack([p[0] for p in pairs]).astype(bf16)
    lora2 = jnp.stack([p[1] for p in pairs]).astype(bf16)
    vecs = jnp.stack([rwkv_w0[0], rwkv_a0[0]])
    pre_args = (rwkv_mu[0], rwkv_w_rkv[0].astype(bf16), lora1, lora2, vecs)
    *parts_p, g_p = rwkv_pre(h, prev_p.reshape(tp, d), *pre_args, row0=0)
    *parts_s, g_s = rwkv_pre(h, prev_s.reshape(ts, d), *pre_args, row0=tp)

    def chain_table(p):
        return jnp.tile(p.reshape(RWKV_HEADS, RWKV_HEAD).T, (1, LANE // RWKV_HEADS))

    tab = jnp.stack([chain_table(p) for p in (rwkv_r_k[0].reshape(-1), rwkv_ln_w[0], rwkv_ln_b[0],
                                              rwkv_k_k[0], rwkv_k_a[0])])
    y_p, wkv_p = _rwkv_core([t.reshape(bp, lp, d) for t in parts_p], zero_state(state_wkv), tab)
    y_s, wkv_s = _rwkv_core([t.reshape(bs, ls, d) for t in parts_s], state_wkv[0], tab)
    x, ht = gated_out_proj(y_p, g_p, y_s, g_s, rwkv_w_o[0].astype(bf16), x, norm_ffn[1])
    shift_p, shift_s = h_p[:, -1], h_s[:, -1]
    _, y = peer(x, ht, 1, norm_final)
    y_p, y_s = split(y)
    return (y_p, y_s, ssm_p[None], conv_p[None], k_p[None], v_p[None], wkv_p[None], shift_p[None],
            ssm_s[None], conv_s[None], k_s[None], v_s[None], wkv_s[None], shift_s[None])
```

```python
import functools
import math

import jax
import jax.numpy as jnp
from jax import lax
from jax.experimental import pallas as pl
from jax.experimental.pallas import tpu as pltpu

f32 = jnp.float32
bf16 = jnp.bfloat16

D_MODEL = 1024
PAST_LEN = 16384
SSM_HEAD_DIM = 64
SSM_HEADS = 16
SSM_INNER = 1024
SSM_GROUPS = 2
SSM_STATE = 128
CONV_W = 4
CONV_DIM = 1536
SSD_CHUNK = 128
ATT_HEAD_DIM = 64
ATT_HEADS = 16
ATT_KV_HEADS = 4
ATT_GQA = 4
WINDOW = 128
ATT_BLOCK = 128
REL_BUCKETS = 32
REL_MAX_DIST = 128
RWKV_HEAD = 64
RWKV_HEADS = 16
PEER_HEADS = 8
N_KEYS = 128
N_EXPERTS = N_KEYS * N_KEYS
PEER_TOPK = 16
PEER_HALF = 128
EPS = 1e-5
GN_EPS = 64e-5

LANE = 128
SUBLANE = 8
HALF_LANE = LANE // 2
VMEM_LIMIT = 56 * 2 ** 20
HIGHEST = lax.Precision.HIGHEST
NT_DIMS = (((1,), (1,)), ((), ()))
TN_DIMS = (((0,), (0,)), ((), ()))

_CAND_COUNTS = [PEER_TOPK // (a + 1) for a in range(PEER_TOPK)]
_N_CAND = sum(_CAND_COUNTS)
_N_CAND_PAD = -(-_N_CAND // 8) * 8
_BIG = 1e9


def _cparams(*sem):
    return pltpu.CompilerParams(dimension_semantics=sem, vmem_limit_bytes=VMEM_LIMIT)


def _rms(x, g):
    return x * lax.rsqrt(jnp.mean(x * x, axis=-1, keepdims=True) + EPS) * g


def _silu(x):
    return x * jax.nn.sigmoid(x)


BF16_ROWS = 2 * SUBLANE


def _dup_bf16(v):
    u = pltpu.bitcast(v.astype(bf16).astype(f32), jnp.int32)
    return u | lax.shift_right_logical(u, jnp.int32(16))


def _row_as_bf16_tile(ref, h, i):
    row = ref[h, pl.ds(i, 1), :]
    tile = pltpu.bitcast(jnp.broadcast_to(row, (SUBLANE, row.shape[1])), bf16)
    return pltpu.repeat(tile, N_KEYS // BF16_ROWS, axis=0)


def _peer_select_kernel(ht_ref, wqt_ref, sk_ref, pos_ref, b0_ref, r1_ref, e0_ref, e1_ref,
                        qt_ref, s_ref, vals_ref, idx_ref, cs_ref):
    tm = ht_ref.shape[1]
    qt_ref[...] = jnp.dot(wqt_ref[...], ht_ref[...], preferred_element_type=f32).astype(bf16)
    row = lax.broadcasted_iota(jnp.int32, (N_KEYS, tm), 0).astype(f32)
    pos = pos_ref[...]
    neg_inf = f32(-jnp.inf)

    def head_compute(h, exact):
        bad = jnp.zeros((1, tm), f32)
        for c in (0, 1):
            off = pl.multiple_of((h * 2 + c) * PEER_HALF, PEER_HALF)
            s_ref[c] = jnp.dot(sk_ref[h, c], qt_ref[pl.ds(off, PEER_HALF), :], preferred_element_type=f32)

        bad_tiles = []
        for lt in range(tm // LANE):
            ls = slice(lt * LANE, (lt + 1) * LANE)
            row_t = row[:, :LANE]

            def remove_max(k, s, c, lt=lt, row_t=row_t):
                m = jnp.max(s, axis=0, keepdims=True)
                vals_ref[c, lt, pl.ds(k, 1), :] = m
                if exact:
                    idx = jnp.min(jnp.where(s == m, row_t, f32(N_KEYS)), axis=0, keepdims=True)
                    idx_ref[c, lt, pl.ds(k, 1), :] = idx
                    hit = row_t == idx
                else:
                    hit = s == m
                return jnp.where(hit, neg_inf, s), hit

            def extract(k, st, remove_max=remove_max):
                sa, sb, rank = st
                sa, _ = remove_max(k, sa, 0)
                sb, hit = remove_max(k, sb, 1)
                return sa, sb, jnp.where(hit, jnp.asarray(k, f32), rank)

            sa_fin, sb_fin, rank = lax.fori_loop(
                0, PEER_TOPK, extract, (s_ref[0, :, ls], s_ref[1, :, ls], jnp.full((N_KEYS, LANE), f32(PEER_TOPK))))
            r1_ref[h, :, ls] = rank.astype(bf16)
            if not exact:
                bad_t = jnp.zeros((1, LANE), f32)
                for s_fin in (sa_fin, sb_fin):
                    removed = jnp.sum(jnp.where(s_fin == neg_inf, f32(1.0), f32(0.0)), axis=0, keepdims=True)
                    bad_t = bad_t + jnp.abs(removed - f32(PEER_TOPK))
                bad_tiles.append(bad_t)
        if not exact:
            bad = bad + jnp.concatenate(bad_tiles, axis=1)

        def lane_tiles(ref, c):
            return jnp.concatenate([ref[c, lt] for lt in range(tm // LANE)], axis=1)

        v0 = lane_tiles(vals_ref, 0)
        v1 = lane_tiles(vals_ref, 1)
        r = 0
        for a in range(PEER_TOPK):
            nb = _CAND_COUNTS[a]
            cs_ref[r:r + nb, :] = v0[a:a + 1, :] + v1[0:nb, :]
            r += nb
        cs_ref[_N_CAND:_N_CAND_PAD, :] = jnp.full((_N_CAND_PAD - _N_CAND, tm), neg_inf)
        m1 = v0[0:1, :] + v1[0:1, :]

        cs = cs_ref[...]
        if exact:
            def extract2(k, st):
                cs, sel, z = st
                m = jnp.max(cs, axis=0, keepdims=True)
                p = jnp.min(jnp.where(cs == m, pos, f32(_BIG)), axis=0, keepdims=True)
                hit = pos == p
                return jnp.where(hit, neg_inf, cs), jnp.where(hit, f32(1.0), sel), z + jnp.exp(m - m1)

            _, sel, z = lax.fori_loop(0, PEER_TOPK, extract2,
                                      (cs, jnp.zeros((_N_CAND_PAD, tm), f32), jnp.zeros((1, tm), f32)))
        else:
            larger = jnp.zeros((_N_CAND_PAD, tm), f32)
            for r in range(_N_CAND):
                larger = larger + jnp.where(cs[r:r + 1, :] > cs, f32(1.0), f32(0.0))
            sel = jnp.where((larger < f32(PEER_TOPK)) & (cs > neg_inf), f32(1.0), f32(0.0))
            z = jnp.sum(sel * jnp.exp(cs - m1), axis=0, keepdims=True)
            bad = bad + jnp.abs(jnp.sum(sel, axis=0, keepdims=True) - f32(PEER_TOPK))
        s0 = s_ref[0]
        idx0 = lane_tiles(idx_ref, 0)
        bound0 = jnp.zeros((N_KEYS, tm), f32)
        r = 0
        for a in range(PEER_TOPK):
            nb = _CAND_COUNTS[a]
            cnt = jnp.sum(sel[r:r + nb, :], axis=0, keepdims=True)
            is_a = (row == idx0[a:a + 1, :]) if exact else (s0 == v0[a:a + 1, :])
            bound0 = jnp.where(is_a, cnt, bound0)
            r += nb
        b0_ref[h] = _dup_bf16(bound0)
        e0_ref[h] = _dup_bf16(jnp.exp(s0 - v0[0:1, :]) / z)
        e1_ref[h] = jnp.exp(s_ref[1] - v1[0:1, :]).astype(bf16)
        return bad

    def head_body(h, carry):
        bad = head_compute(h, False)

        @pl.when(jnp.max(bad) > 0.0)
        def _():
            head_compute(h, True)

        return carry

    lax.fori_loop(0, PEER_HEADS, head_body, 0)


def _peer_main_kernel(ht_ref, u_ref, v_ref, b0_ref, r1_ref, e0_ref, e1_ref, x_ref, gn_ref, o_ref, hn_ref,
                      acc_ref, *, nblk, npieces):
    j = pl.program_id(1)

    @pl.when(j == 0)
    def _():
        acc_ref[...] = jnp.zeros_like(acc_ref)

    ht = ht_ref[...]
    bpp = nblk // npieces
    pc = bpp * N_KEYS
    for p in range(npieces):
        at = jnp.dot(u_ref[p * pc:(p + 1) * pc, :], ht, preferred_element_type=f32)
        ws = []
        for b in range(bpp):
            i = j * nblk + p * bpp + b
            a = at[b * N_KEYS:(b + 1) * N_KEYS, :]
            g = jnp.zeros(a.shape, bf16)
            for h in range(PEER_HEADS):
                bnd = _row_as_bf16_tile(b0_ref, h, i)
                e0 = _row_as_bf16_tile(e0_ref, h, i)
                g = g + jnp.where(r1_ref[h] < bnd, e1_ref[h] * e0, bf16(0.0))
            gelu = 0.5 * a * (1.0 + lax.erf(a * f32(math.sqrt(0.5))))
            ws.append(g * gelu.astype(bf16))
        wt = jnp.concatenate(ws, axis=0) if bpp > 1 else ws[0]
        acc_ref[...] += lax.dot_general(wt, v_ref[p * pc:(p + 1) * pc, :], TN_DIMS,
                                        preferred_element_type=f32)

    @pl.when(j == pl.num_programs(1) - 1)
    def _():
        xn = x_ref[...] + acc_ref[...]
        o_ref[...] = xn
        hn_ref[...] = _rms(xn, gn_ref[...])


def peer_ffn_residual(x, ht, wqt, sk, u, v, g_next, tm_sel=256, tm=256, ce=4096, npieces=16):
    t = x.shape[0]
    d = D_MODEL
    pos_list = []
    for a in range(PEER_TOPK):
        pos_list += [a * PEER_TOPK + b for b in range(_CAND_COUNTS[a])]
    pos_list += [_BIG] * (_N_CAND_PAD - _N_CAND)
    pos = jnp.broadcast_to(jnp.asarray(pos_list, f32)[:, None], (_N_CAND_PAD, tm_sel))
    sel_shape = jax.ShapeDtypeStruct((PEER_HEADS, N_KEYS, t), jnp.int32)
    sel_shape16 = jax.ShapeDtypeStruct((PEER_HEADS, N_KEYS, t), bf16)
    sel_spec = pl.BlockSpec((PEER_HEADS, N_KEYS, tm_sel), lambda i: (0, 0, i))
    b0, r1, e0, e1 = pl.pallas_call(
        _peer_select_kernel,
        grid=(t // tm_sel,),
        in_specs=[pl.BlockSpec((d, tm_sel), lambda i: (0, i)),
                  pl.BlockSpec((PEER_HEADS * 2 * PEER_HALF, d), lambda i: (0, 0)),
                  pl.BlockSpec((PEER_HEADS, 2, N_KEYS, PEER_HALF), lambda i: (0, 0, 0, 0)),
                  pl.BlockSpec((_N_CAND_PAD, tm_sel), lambda i: (0, 0))],
        out_specs=[sel_spec] * 4,
        out_shape=[sel_shape, sel_shape16, sel_shape, sel_shape16],
        scratch_shapes=[pltpu.VMEM((PEER_HEADS * 2 * PEER_HALF, tm_sel), bf16),
                        pltpu.VMEM((2, N_KEYS, tm_sel), f32),
                        pltpu.VMEM((2, tm_sel // LANE, PEER_TOPK, LANE), f32),
                        pltpu.VMEM((2, tm_sel // LANE, PEER_TOPK, LANE), f32),
                        pltpu.VMEM((_N_CAND_PAD, tm_sel), f32)],
        compiler_params=_cparams("parallel"),
        name="peer_select",
    )(ht, wqt, sk, pos)

    nblk = ce // N_KEYS
    sel_spec2 = pl.BlockSpec((PEER_HEADS, N_KEYS, tm), lambda i, j: (0, 0, i))
    return pl.pallas_call(
        functools.partial(_peer_main_kernel, nblk=nblk, npieces=npieces),
        grid=(t // tm, N_EXPERTS // ce),
        in_specs=[pl.BlockSpec((d, tm), lambda i, j: (0, i)),
                  pl.BlockSpec((ce, d), lambda i, j: (j, 0)),
                  pl.BlockSpec((ce, d), lambda i, j: (j, 0)),
                  sel_spec2, sel_spec2, sel_spec2, sel_spec2,
                  pl.BlockSpec((tm, d), lambda i, j: (i, 0)), pl.BlockSpec((1, d), lambda i, j: (0, 0))],
        out_specs=[pl.BlockSpec((tm, d), lambda i, j: (i, 0))] * 2,
        out_shape=[jax.ShapeDtypeStruct((t, d), f32)] * 2,
        scratch_shapes=[pltpu.VMEM((tm, d), f32)],
        compiler_params=_cparams("parallel", "arbitrary"),
        name="peer_main",
    )(ht, u, v, b0, r1, e0, e1, x, g_next.reshape(1, d))


IN_WIDTHS = (SSM_INNER, CONV_DIM, ATT_HEADS * ATT_HEAD_DIM, 2 * ATT_KV_HEADS * ATT_HEAD_DIM, LANE)


def _in_proj_kernel(x_ref, g_ref, w_ref, *out_refs):
    h = _rms(x_ref[...], g_ref[...])
    y = jnp.dot(h.astype(bf16), w_ref[...], preferred_element_type=f32)
    o = 0
    for ref in out_refs:
        w = ref.shape[1]
        ref[...] = y[:, o:o + w]
        o += w


def in_proj(x, g, w_in, tm=256):
    t, d = x.shape
    o1 = SSM_INNER
    o2 = o1 + CONV_DIM
    o3 = o2 + SSM_HEADS
    w = jnp.concatenate([w_in[:, :o2], w_in[:, o3:], w_in[:, o2:o3],
                         jnp.zeros((d, LANE - SSM_HEADS), w_in.dtype)], axis=1).astype(bf16)
    n = w.shape[1]
    return pl.pallas_call(
        _in_proj_kernel,
        grid=(t // tm,),
        in_specs=[pl.BlockSpec((tm, d), lambda i: (i, 0)), pl.BlockSpec((1, d), lambda i: (0, 0)),
                  pl.BlockSpec((d, n), lambda i: (0, 0))],
        out_specs=[pl.BlockSpec((tm, wd), lambda i: (i, 0)) for wd in IN_WIDTHS],
        out_shape=[jax.ShapeDtypeStruct((t, wd), f32) for wd in IN_WIDTHS],
        compiler_params=_cparams("parallel"),
        name="in_proj",
    )(x, g.reshape(1, d), w)


N_PAIRS = SSM_HEADS // 2
SSD_SHORT_CHUNK = 32


def _ssd_kernel(*refs, lin, lc):
    (z_ref, xbc_ref, dt_ref, cs_ref, h0_ref, cw_ref, cb_ref, dtb_ref, aneg_ref, dsk_ref, gn_ref, selh_ref) = refs[:12]
    refs = refs[12:]
    y_ref, hfin_ref, tail_ref, xe_ref, hp_ref, ys_ref = refs[:6]
    c = pl.program_id(1)

    @pl.when(c == 0)
    def _():
        xe_ref[0:SUBLANE, :] = cs_ref[0]
        hp_ref[...] = h0_ref[0]

    if lin == lc:
        xe_ref[SUBLANE:SUBLANE + lc, :] = xbc_ref[...]
        z = z_ref[...]
        dt_raw = dt_ref[...]
    else:
        zpad_ref, dtpad_ref = refs[6:8]
        xe_ref[SUBLANE:SUBLANE + lc, :] = jnp.zeros((lc, CONV_DIM), f32)
        xe_ref[SUBLANE:SUBLANE + lin, :] = xbc_ref[...]
        zpad_ref[...] = jnp.zeros_like(zpad_ref)
        zpad_ref[0:lin, :] = z_ref[...]
        dtpad_ref[...] = jnp.zeros_like(dtpad_ref)
        dtpad_ref[0:lin, :] = dt_ref[...]
        z = zpad_ref[...]
        dt_raw = dtpad_ref[...]

    conv = cb_ref[...]
    for j in range(CONV_W):
        o = SUBLANE - (CONV_W - 1) + j
        conv = conv + xe_ref[o:o + lc, :] * cw_ref[j:j + 1, :]
    if lin == lc:
        xe_ref[0:SUBLANE, :] = xe_ref[lc:lc + SUBLANE, :]
    xc = _silu(conv)
    xs = xc[:, :SSM_INNER]
    bm = xc[:, SSM_INNER:SSM_INNER + SSM_GROUPS * SSM_STATE]
    cm = xc[:, SSM_INNER + SSM_GROUPS * SSM_STATE:]

    row = lax.broadcasted_iota(jnp.int32, (lc, LANE), 0)
    col = lax.broadcasted_iota(jnp.int32, (lc, LANE), 1)
    causal = lax.broadcasted_iota(jnp.int32, (lc, lc), 0) >= lax.broadcasted_iota(jnp.int32, (lc, lc), 1)
    pair_row = lax.broadcasted_iota(jnp.int32, (LANE, SSM_STATE), 0)
    lane_lo = col < HALF_LANE
    neg_inf = f32(-jnp.inf)

    x = dt_raw + dtb_ref[...]
    dt = jnp.maximum(x, 0.0) + jnp.log(1.0 + jnp.exp(-jnp.abs(x)))
    if lin != lc:
        dt = jnp.where(row < lin, dt, 0.0)
    la = dt * aneg_ref[...]
    acs = jnp.dot(causal.astype(f32), la, precision=HIGHEST, preferred_element_type=f32)
    acs_sq = acs if lc == LANE else jnp.concatenate([acs, jnp.zeros((LANE - lc, LANE), f32)], axis=0)
    acs_t = acs_sq.T[:, :lc]
    selh = selh_ref[...]
    dt_exp = jnp.dot(dt, selh, precision=HIGHEST, preferred_element_type=f32)
    acs_exp = jnp.dot(acs, selh, precision=HIGHEST, preferred_element_type=f32)
    alast = acs[lc - 1:lc, :]
    xdt = xs * dt_exp
    eacs = jnp.exp(acs_exp)
    xdt_end = (xdt * jnp.exp(acs_exp[lc - 1:lc, :] - acs_exp)).astype(bf16)
    dsk = dsk_ref[...]

    for g in range(SSM_GROUPS):
        cmg = cm[:, g * SSM_STATE:(g + 1) * SSM_STATE].astype(bf16)
        bmg = bm[:, g * SSM_STATE:(g + 1) * SSM_STATE].astype(bf16)
        cb = lax.dot_general(cmg, bmg, NT_DIMS, preferred_element_type=f32)
        for jj in range(N_PAIRS // SSM_GROUPS):
            j = g * (N_PAIRS // SSM_GROUPS) + jj
            sl = slice(j * LANE, (j + 1) * LANE)
            xdt_pair = xdt[:, sl]
            ydiag = jnp.zeros((lc, LANE), f32)
            for half in (0, 1):
                h = 2 * j + half
                seg = acs[:, h:h + 1] - acs_t[h:h + 1, :]
                dec = jnp.exp(jnp.where(causal, seg, neg_inf))
                m = (cb * dec).astype(bf16)
                keep = lane_lo if half == 0 else jnp.logical_not(lane_lo)
                xd = jnp.where(keep, xdt_pair, 0.0).astype(bf16)
                ydiag = ydiag + jnp.dot(m, xd, preferred_element_type=f32)
            hpj = hp_ref[j]
            yoff = lax.dot_general(cmg, hpj.astype(bf16), NT_DIMS, preferred_element_type=f32) * eacs[:, sl]
            s_new = lax.dot_general(xdt_end[:, sl], bmg, TN_DIMS, preferred_element_type=f32)
            dl = jnp.where(pair_row < SSM_HEAD_DIM, alast[:, 2 * j:2 * j + 1], alast[:, 2 * j + 1:2 * j + 2])
            hp_ref[j] = hpj * jnp.exp(dl) + s_new
            ys_ref[:, sl] = ydiag + yoff + dsk[:, sl] * xs[:, sl]

    y = ys_ref[...] * _silu(z)
    gs = SSM_INNER // SSM_GROUPS
    gn = gn_ref[...]
    for g in range(SSM_GROUPS):
        yg = y[:, g * gs:(g + 1) * gs]
        yg = yg * lax.rsqrt(jnp.mean(yg * yg, axis=-1, keepdims=True) + EPS) * gn[:, g * gs:(g + 1) * gs]
        y_ref[:, g * gs:(g + 1) * gs] = yg[0:lin, :]

    @pl.when(c == pl.num_programs(1) - 1)
    def _():
        hfin_ref[0] = hp_ref[...]
        tail_ref[0] = xe_ref[lin:lin + SUBLANE, :]


def ssd_mixer(z, xbc, dt, conv_state, ssm_state, params, *, batch, seq, row0):
    conv_w, conv_b, dt_bias, a_neg, d_skip, gnorm, selh = params
    lin = min(seq, SSD_CHUNK)
    lc = SSD_CHUNK if lin == SSD_CHUNK else SSD_SHORT_CHUNK
    nc = seq // lin
    blk0 = row0 // lin
    cs = jnp.pad(conv_state, ((0, 0), (SUBLANE - (CONV_W - 1), 0), (0, 0)))
    h0 = ssm_state.reshape(batch, N_PAIRS, LANE, SSM_STATE)

    def rows(w):
        return pl.BlockSpec((lin, w), lambda b, c: (blk0 + b * nc + c, 0))

    def const(shape):
        return pl.BlockSpec(shape, lambda b, c: (0,) * len(shape))

    in_specs = [rows(SSM_INNER), rows(CONV_DIM), rows(LANE),
                pl.BlockSpec((1, SUBLANE, CONV_DIM), lambda b, c: (b, 0, 0)),
                pl.BlockSpec((1, N_PAIRS, LANE, SSM_STATE), lambda b, c: (b, 0, 0, 0)),
                const((CONV_W, CONV_DIM)), const((1, CONV_DIM)), const((1, LANE)), const((1, LANE)),
                const((1, SSM_INNER)), const((1, SSM_INNER)), const((LANE, SSM_INNER))]
    args = [z, xbc, dt, cs, h0, conv_w, conv_b, dt_bias, a_neg, d_skip, gnorm, selh]
    scratch = [pltpu.VMEM((lc + 2 * SUBLANE, CONV_DIM), f32), pltpu.VMEM((N_PAIRS, LANE, SSM_STATE), f32),
               pltpu.VMEM((lc, SSM_INNER), f32)]
    if lin != lc:
        scratch += [pltpu.VMEM((lc, SSM_INNER), f32), pltpu.VMEM((lc, LANE), f32)]
    y, h_fin, tail = pl.pallas_call(
        functools.partial(_ssd_kernel, lin=lin, lc=lc),
        grid=(batch, nc),
        in_specs=in_specs,
        out_specs=[pl.BlockSpec((lin, SSM_INNER), lambda b, c: (b * nc + c, 0)),
                   pl.BlockSpec((1, N_PAIRS, LANE, SSM_STATE), lambda b, c: (b, 0, 0, 0)),
                   pl.BlockSpec((1, SUBLANE, CONV_DIM), lambda b, c: (b, 0, 0))],
        out_shape=[jax.ShapeDtypeStruct((batch * seq, SSM_INNER), f32),
                   jax.ShapeDtypeStruct((batch, N_PAIRS, LANE, SSM_STATE), f32),
                   jax.ShapeDtypeStruct((batch, SUBLANE, CONV_DIM), f32)],
        scratch_shapes=scratch,
        compiler_params=_cparams("parallel", "arbitrary"),
        name="ssd_mixer",
    )(*args)
    new_conv = tail[:, SUBLANE - (CONV_W - 1):]
    return y, h_fin.reshape(batch, SSM_HEADS, SSM_HEAD_DIM, SSM_STATE), new_conv


def _swa_kernel(*refs, lq, masked_first):
    q_ref, kp_ref, vp_ref, kc_ref, vc_ref, bias_ref, sink_ref = refs[:7]
    refs = refs[7:]
    o_ref, nk_ref, nv_ref = refs[:3]
    n = pl.program_id(1)
    q = q_ref[...]
    for new_ref, prev_ref, cur_ref in ((nk_ref, kp_ref, kc_ref), (nv_ref, vp_ref, vc_ref)):
        if lq < WINDOW:
            new_ref[0:WINDOW - lq, :] = prev_ref[lq:WINDOW, :]
        new_ref[WINDOW - lq:WINDOW, :] = cur_ref[...]
    if lq == ATT_BLOCK:
        kc = kc_ref[...]
        vc = vc_ref[...]
    else:
        kpad_ref, vpad_ref = refs[3:5]
        kpad_ref[...] = jnp.zeros_like(kpad_ref)
        vpad_ref[...] = jnp.zeros_like(vpad_ref)
        kpad_ref[0:lq, :] = kc_ref[...]
        vpad_ref[0:lq, :] = vc_ref[...]
        kc = kpad_ref[...]
        vc = vpad_ref[...]
    kp = kp_ref[...]
    vp = vp_ref[...]
    lane_lo = lax.broadcasted_iota(jnp.int32, (ATT_BLOCK, LANE), 1) < HALF_LANE
    lane_lo_q = lax.broadcasted_iota(jnp.int32, (lq, LANE), 1) < HALF_LANE
    neg_inf = f32(-jnp.inf)
    scale = f32(ATT_HEAD_DIM ** -0.5)

    for g in range(ATT_KV_HEADS):
        sl = slice((g // 2) * LANE, (g // 2 + 1) * LANE)
        odd = g % 2 == 1

        def kpad(k):
            pair = k[:, sl]
            if odd:
                pair = pltpu.roll(pair, HALF_LANE, 1)
            return jnp.where(lane_lo, pair, 0.0).astype(bf16)

        def vdup(v):
            pair = v[:, sl]
            rolled = pltpu.roll(pair, HALF_LANE, 1)
            return (jnp.where(lane_lo, rolled, pair) if odd else jnp.where(lane_lo, pair, rolled)).astype(bf16)

        qp0 = q[:, (2 * g) * LANE:(2 * g + 1) * LANE]
        qp1 = q[:, (2 * g + 1) * LANE:(2 * g + 2) * LANE]
        qg = jnp.concatenate([qp0, pltpu.roll(qp0, HALF_LANE, 1), qp1, pltpu.roll(qp1, HALF_LANE, 1)],
                             axis=0).astype(bf16)
        bias = bias_ref[g]
        sp = lax.dot_general(qg, kpad(kp), NT_DIMS, preferred_element_type=f32) * scale + bias[:, :WINDOW]
        sc = lax.dot_general(qg, kpad(kc), NT_DIMS, preferred_element_type=f32) * scale + bias[:, WINDOW:]
        if masked_first:
            sp = jnp.where(n > 0, sp, neg_inf)
        sink = sink_ref[g][:, 0:1]
        m = jnp.maximum(jnp.max(jnp.maximum(sp, sc), axis=-1, keepdims=True), sink)
        pp = jnp.exp(sp - m)
        pc = jnp.exp(sc - m)
        denom = jnp.sum(pp + pc, axis=-1, keepdims=True) + jnp.exp(sink - m)
        og = (jnp.dot(pp.astype(bf16), vdup(vp), preferred_element_type=f32)
              + jnp.dot(pc.astype(bf16), vdup(vc), preferred_element_type=f32)) / denom
        o_ref[:, (2 * g) * LANE:(2 * g + 1) * LANE] = jnp.where(lane_lo_q, og[0:lq], og[lq:2 * lq])
        o_ref[:, (2 * g + 1) * LANE:(2 * g + 2) * LANE] = jnp.where(lane_lo_q, og[2 * lq:3 * lq], og[3 * lq:4 * lq])


def _rel_bucket(dist):
    exact = REL_BUCKETS // 2
    d = jnp.maximum(dist, 0)
    large = exact + (jnp.log(jnp.maximum(d, 1).astype(f32) / exact)
                     / math.log(REL_MAX_DIST / exact) * (REL_BUCKETS - exact)).astype(jnp.int32)
    large = jnp.minimum(large, REL_BUCKETS - 1)
    return jnp.where(d < exact, d, large)


def _swa_tables(rel_bias, sinks, lq):
    qi = jnp.arange(lq)[:, None]
    kj = jnp.arange(2 * ATT_BLOCK)[None, :]
    dist = qi + WINDOW - kj
    band = (dist >= 0) & (dist <= WINDOW)
    onehot = (_rel_bucket(dist)[..., None] == jnp.arange(REL_BUCKETS)).astype(f32)
    bias = jnp.einsum('qkb,bh->qkh', onehot, rel_bias, precision=HIGHEST)
    bias = jnp.where(band[..., None], bias, -jnp.inf)
    bias = jnp.transpose(bias, (2, 0, 1)).reshape(ATT_KV_HEADS, ATT_GQA * lq, 2 * ATT_BLOCK)
    sink = jnp.broadcast_to(sinks.reshape(ATT_KV_HEADS, ATT_GQA, 1, 1), (ATT_KV_HEADS, ATT_GQA, lq, LANE))
    return bias, sink.reshape(ATT_KV_HEADS, ATT_GQA * lq, LANE)


def swa_mixer(q, kv, k_prev, v_prev, rel_bias, sinks, *, batch, seq, row0):
    lq = min(seq, ATT_BLOCK)
    nb = seq // lq
    blk0 = row0 // lq
    kvw = ATT_KV_HEADS * ATT_HEAD_DIM
    bias, sink = _swa_tables(rel_bias, sinks, lq)
    cur_k = pl.BlockSpec((lq, kvw), lambda b, n: (blk0 + b * nb + n, 0))
    cur_v = pl.BlockSpec((lq, kvw), lambda b, n: (blk0 + b * nb + n, 1))
    if k_prev is None:
        prev_k = pl.BlockSpec((WINDOW, kvw), lambda b, n: (blk0 + b * nb + jnp.maximum(n - 1, 0), 0))
        prev_v = pl.BlockSpec((WINDOW, kvw), lambda b, n: (blk0 + b * nb + jnp.maximum(n - 1, 0), 1))
        kp_arr, vp_arr = kv, kv
    else:
        prev_k = pl.BlockSpec((WINDOW, kvw), lambda b, n: (b, 0))
        prev_v = prev_k
        kp_arr = k_prev.reshape(batch * WINDOW, kvw)
        vp_arr = v_prev.reshape(batch * WINDOW, kvw)
    rows = pl.BlockSpec((lq, ATT_HEADS * ATT_HEAD_DIM), lambda b, n: (blk0 + b * nb + n, 0))
    in_specs = [rows, prev_k, prev_v, cur_k, cur_v,
                pl.BlockSpec(bias.shape, lambda b, n: (0, 0, 0)), pl.BlockSpec(sink.shape, lambda b, n: (0, 0, 0))]
    args = [q, kp_arr, vp_arr, kv, kv, bias, sink]
    scratch = [] if lq == ATT_BLOCK else [pltpu.VMEM((ATT_BLOCK, kvw), f32), pltpu.VMEM((ATT_BLOCK, kvw), f32)]
    cache_spec = pl.BlockSpec((WINDOW, kvw), lambda b, n: (b, 0))
    cache_shape = jax.ShapeDtypeStruct((batch * WINDOW, kvw), f32)
    o, new_k, new_v = pl.pallas_call(
        functools.partial(_swa_kernel, lq=lq, masked_first=k_prev is None),
        grid=(batch, nb),
        in_specs=in_specs,
        out_specs=[pl.BlockSpec((lq, ATT_HEADS * ATT_HEAD_DIM), lambda b, n: (b * nb + n, 0)), cache_spec, cache_spec],
        out_shape=[jax.ShapeDtypeStruct((batch * seq, ATT_HEADS * ATT_HEAD_DIM), f32), cache_shape, cache_shape],
        scratch_shapes=scratch,
        compiler_params=_cparams("parallel", "arbitrary"),
        name="swa_mixer",
    )(*args)
    cshape = (batch, WINDOW, ATT_KV_HEADS, ATT_HEAD_DIM)
    return o, new_k.reshape(cshape), new_v.reshape(cshape)


def _out_proj_kernel(yap_ref, ybp_ref, yas_ref, ybs_ref, w_ref, x_ref, g_ref, o_ref, ht_ref, *, n_prompt):
    ka = yap_ref.shape[1]

    def body(ya_ref, yb_ref):
        acc = jnp.dot(ya_ref[...].astype(bf16), w_ref[0:ka, :], preferred_element_type=f32)
        acc = acc + jnp.dot(yb_ref[...].astype(bf16), w_ref[ka:, :], preferred_element_type=f32)
        xn = x_ref[...] + acc
        o_ref[...] = xn
        ht_ref[...] = _rms(xn, g_ref[...]).T.astype(bf16)

    @pl.when(pl.program_id(0) < n_prompt)
    def _():
        body(yap_ref, ybp_ref)

    @pl.when(pl.program_id(0) >= n_prompt)
    def _():
        body(yas_ref, ybs_ref)


def out_proj(ya_p, yb_p, ya_s, yb_s, w, x, g, tm=512):
    t, d = x.shape
    ka, kb = ya_p.shape[1], yb_p.shape[1]
    n_p = ya_p.shape[0] // tm
    n_s = ya_s.shape[0] // tm

    def p_rows(k):
        return pl.BlockSpec((tm, k), lambda i: (jnp.minimum(i, n_p - 1), 0))

    def s_rows(k):
        return pl.BlockSpec((tm, k), lambda i: (jnp.maximum(i - n_p, 0), 0))

    return pl.pallas_call(
        functools.partial(_out_proj_kernel, n_prompt=n_p),
        grid=(n_p + n_s,),
        in_specs=[p_rows(ka), p_rows(kb), s_rows(ka), s_rows(kb),
                  pl.BlockSpec((ka + kb, d), lambda i: (0, 0)), pl.BlockSpec((tm, d), lambda i: (i, 0)),
                  pl.BlockSpec((1, d), lambda i: (0, 0))],
        out_specs=[pl.BlockSpec((tm, d), lambda i: (i, 0)), pl.BlockSpec((d, tm), lambda i: (0, i))],
        out_shape=[jax.ShapeDtypeStruct((t, d), f32), jax.ShapeDtypeStruct((d, t), bf16)],
        compiler_params=_cparams("arbitrary"),
        name="out_proj",
    )(ya_p, yb_p, ya_s, yb_s, w, x, g.reshape(1, d))


def _gated_out_kernel(yp_ref, gp_ref, ys_ref, gs_ref, w_ref, x_ref, gn_ref, o_ref, ht_ref, *, n_prompt):
    def body(y_ref, g_ref):
        a = (y_ref[...] * g_ref[...]).astype(bf16)
        xn = x_ref[...] + jnp.dot(a, w_ref[...], preferred_element_type=f32)
        o_ref[...] = xn
        ht_ref[...] = _rms(xn, gn_ref[...]).T.astype(bf16)

    @pl.when(pl.program_id(0) < n_prompt)
    def _():
        body(yp_ref, gp_ref)

    @pl.when(pl.program_id(0) >= n_prompt)
    def _():
        body(ys_ref, gs_ref)


def gated_out_proj(y_p, g_p, y_s, g_s, w, x, gn, tm=512):
    t, d = x.shape
    n_p = y_p.shape[0] // tm
    n_s = y_s.shape[0] // tm
    rows = pl.BlockSpec((tm, d), lambda i: (i, 0))
    p_rows = pl.BlockSpec((tm, d), lambda i: (jnp.minimum(i, n_p - 1), 0))
    s_rows = pl.BlockSpec((tm, d), lambda i: (jnp.maximum(i - n_p, 0), 0))
    return pl.pallas_call(
        functools.partial(_gated_out_kernel, n_prompt=n_p),
        grid=(n_p + n_s,),
        in_specs=[p_rows, p_rows, s_rows, s_rows,
                  pl.BlockSpec((d, d), lambda i: (0, 0)), rows, pl.BlockSpec((1, d), lambda i: (0, 0))],
        out_specs=[rows, pl.BlockSpec((d, tm), lambda i: (0, i))],
        out_shape=[jax.ShapeDtypeStruct((t, d), f32), jax.ShapeDtypeStruct((d, t), bf16)],
        compiler_params=_cparams("arbitrary"),
        name="rwkv_out",
    )(y_p, g_p, y_s, g_s, w, x, gn.reshape(1, d))


LORA_PAD = 2 * LANE


def _softplus(x):
    return jnp.maximum(x, 0.0) + jnp.log(1.0 + jnp.exp(-jnp.abs(x)))


def _rwkv_pre_kernel(h_ref, p_ref, mu_ref, wrkv_ref, l1_ref, l2_ref, vec_ref,
                     r_ref, d_ref, k_ref, v_ref, a_ref, g_ref):
    h = h_ref[...]
    xx = p_ref[...] - h

    def mix(j):
        return (h + xx * mu_ref[j:j + 1, :]).astype(bf16)

    def mm(a, w):
        return jnp.dot(a, w, preferred_element_type=f32)

    r_ref[...] = mm(mix(0), wrkv_ref[0])
    k_ref[...] = mm(mix(2), wrkv_ref[1])
    v_ref[...] = mm(mix(3), wrkv_ref[2])
    wl = vec_ref[0:1, :] + mm(jnp.tanh(mm(mix(1), l1_ref[0])).astype(bf16), l2_ref[0])
    al = vec_ref[1:2, :] + mm(mm(mix(4), l1_ref[1]).astype(bf16), l2_ref[1])
    g_ref[...] = mm(jax.nn.sigmoid(mm(mix(5), l1_ref[2])).astype(bf16), l2_ref[2])
    w = -_softplus(-wl) - 0.5
    d_ref[...] = jnp.exp(-jnp.exp(w))
    a_ref[...] = jax.nn.sigmoid(al)


def rwkv_pre(h, prev, mu, w_rkv, lora1, lora2, vecs, *, row0, tm=256):
    t, d = prev.shape
    blk0 = row0 // tm
    rows = pl.BlockSpec((tm, d), lambda i: (i, 0))

    def const(a):
        return pl.BlockSpec(a.shape, lambda i: (0,) * a.ndim)

    return pl.pallas_call(
        _rwkv_pre_kernel,
        grid=(t // tm,),
        in_specs=[pl.BlockSpec((tm, d), lambda i: (blk0 + i, 0)), rows,
                  const(mu), const(w_rkv), const(lora1), const(lora2), const(vecs)],
        out_specs=[rows] * 6,
        out_shape=[jax.ShapeDtypeStruct((t, d), f32)] * 6,
        compiler_params=_cparams("parallel"),
        name="rwkv_pre",
    )(h, prev, mu, w_rkv, lora1, lora2, vecs)


RWKV_VGROUP = 8


RWKV_TB = 16
N_SEQ_TILE = LANE // RWKV_HEADS


CHAIN_HEADS = tuple(2 * hi + lo for lo in range(2) for hi in range(RWKV_HEADS // 2))
N_COLS = D_MODEL // LANE


def _swap_sublane_with_column(cols):
    sub = lax.broadcasted_iota(jnp.int32, cols[0].shape, 0)
    s = N_COLS // 2
    while s >= 1:
        upper = (sub & s) != 0
        new = list(cols)
        for c in range(N_COLS):
            if c & s == 0:
                a, b = cols[c], cols[c | s]
                new[c] = jnp.where(upper, pltpu.roll(b, s, 0), a)
                new[c | s] = jnp.where(upper, b, pltpu.roll(a, LANE - s, 0))
        cols = new
        s //= 2
    return cols


def _swap_rowbit_with_lanehalf(col):
    a = jnp.concatenate([col[g * 16:g * 16 + SUBLANE] for g in range(N_SEQ_TILE)], axis=0)
    b = jnp.concatenate([col[g * 16 + SUBLANE:(g + 1) * 16] for g in range(N_SEQ_TILE)], axis=0)
    upper = lax.broadcasted_iota(jnp.int32, a.shape, 1) >= HALF_LANE
    moved = pltpu.roll(jnp.where(upper, a, b), HALF_LANE, 1)
    a = jnp.where(upper, moved, a)
    b = jnp.where(upper, b, moved)
    pieces = []
    for g in range(N_SEQ_TILE):
        pieces += [a[g * SUBLANE:(g + 1) * SUBLANE], b[g * SUBLANE:(g + 1) * SUBLANE]]
    return jnp.concatenate(pieces, axis=0)


def _natural_to_chain(x_ref, dst_ref, q):
    x2 = x_ref[...].reshape(N_SEQ_TILE * RWKV_TB, D_MODEL)
    cols = _swap_sublane_with_column([x2[:, c * LANE:(c + 1) * LANE] for c in range(N_COLS)])
    for c in range(N_COLS):
        xt = _swap_rowbit_with_lanehalf(cols[c]).T
        dst_ref[q, c] = xt[:RWKV_HEAD]
        dst_ref[q, RWKV_TB // 2 + c] = xt[RWKV_HEAD:]


def _chain_to_natural(src_ref, y_ref):
    cols = []
    for c in range(N_COLS):
        xt = jnp.concatenate([src_ref[c], src_ref[RWKV_TB // 2 + c]], axis=0)
        cols.append(_swap_rowbit_with_lanehalf(xt.T))
    y2 = jnp.concatenate(_swap_sublane_with_column(cols), axis=1)
    y_ref[...] = y2.reshape(N_SEQ_TILE, RWKV_TB, D_MODEL)


N_SCAN_IN = 5


def _rwkv_scan_kernel(r_ref, d_ref, k_ref, v_ref, a_ref, s0_ref, tab_ref, y_ref, sfin_ref,
                      s_ref, q_ref, ys_ref, *, natural):
    tb = q_ref.shape[1]

    @pl.when(pl.program_id(1) == 0)
    def _():
        s_ref[...] = s0_ref[...]

    for q, ref in enumerate((r_ref, d_ref, k_ref, v_ref, a_ref)):
        if natural:
            _natural_to_chain(ref, q_ref, q)
        else:
            q_ref[q] = ref[...]
    r_q, d_q, k_q, v_q, a_q = [q_ref.at[q] for q in range(N_SCAN_IN)]

    def step(t, carry):
        k_raw = k_q[t]
        a = a_q[t]
        kkr = k_raw * tab_ref[3]
        nrm = jnp.sqrt(jnp.sum(kkr * kkr, axis=0, keepdims=True))
        kk = kkr / jnp.maximum(nrm, 1e-12)
        d = d_q[t]
        kv = k_raw * (1.0 + (a - 1.0) * tab_ref[4])
        r = r_q[t]
        b = kk * a

        def vgroup(g, c2):
            v0 = pl.multiple_of(g * RWKV_VGROUP, RWKV_VGROUP)
            vrows = v_q[t, pl.ds(v0, RWKV_VGROUP), :]
            ys = []
            for vi in range(RWKV_VGROUP):
                sv = s_ref[v0 + vi]
                sa = -jnp.sum(sv * kk, axis=0, keepdims=True)
                sn = sv * d + sa * b + vrows[vi:vi + 1, :] * kv
                s_ref[v0 + vi] = sn
                ys.append(jnp.sum(sn * r, axis=0, keepdims=True))
            ys_ref[t, pl.ds(v0, RWKV_VGROUP), :] = jnp.concatenate(ys, axis=0)
            return c2

        lax.fori_loop(0, RWKV_HEAD // RWKV_VGROUP, vgroup, 0)
        y = ys_ref[t]
        mean = jnp.mean(y, axis=0, keepdims=True)
        yc = y - mean
        var = jnp.mean(yc * yc, axis=0, keepdims=True)
        bonus = jnp.sum(r * kv * tab_ref[0], axis=0, keepdims=True)
        ys_ref[t] = yc * lax.rsqrt(var + GN_EPS) * tab_ref[1] + tab_ref[2] + bonus * v_q[t]
        return carry

    lax.fori_loop(0, tb, step, 0)
    if natural:
        _chain_to_natural(ys_ref, y_ref)
    else:
        y_ref[...] = ys_ref[...]

    @pl.when(pl.program_id(1) == pl.num_programs(1) - 1)
    def _():
        sfin_ref[...] = s_ref[...]


def rwkv_scan(parts, s0, tab, *, natural):
    hd = RWKV_HEAD
    c = s0.shape[-1]
    if natural:
        nb, L, dm = parts[0].shape
        tb = RWKV_TB
        seq_spec = pl.BlockSpec((N_SEQ_TILE, tb, dm), lambda i, j: (i, j, 0))
        y_shape = jax.ShapeDtypeStruct((nb, L, dm), f32)
    else:
        L = parts[0].shape[0]
        tb = min(L, RWKV_TB)
        seq_spec = pl.BlockSpec((tb, hd, LANE), lambda i, j: (j, 0, i))
        y_shape = jax.ShapeDtypeStruct((L, hd, c), f32)
    st_spec = pl.BlockSpec((hd, hd, LANE), lambda i, j: (0, 0, i))
    return pl.pallas_call(
        functools.partial(_rwkv_scan_kernel, natural=natural),
        grid=(c // LANE, L // tb),
        in_specs=[seq_spec] * N_SCAN_IN + [st_spec, pl.BlockSpec(tab.shape, lambda i, j: (0, 0, 0))],
        out_specs=[seq_spec, st_spec],
        out_shape=[y_shape, jax.ShapeDtypeStruct((hd, hd, c), f32)],
        scratch_shapes=[pltpu.VMEM((hd, hd, LANE), f32), pltpu.VMEM((N_SCAN_IN, tb, hd, LANE), f32),
                        pltpu.VMEM((tb, hd, LANE), f32)],
        compiler_params=_cparams("parallel", "arbitrary"),
        name="rwkv_scan",
    )(*parts, s0, tab)


def _rwkv_core(parts, wkv, tab):
    b, L, _ = parts[0].shape
    order = jnp.asarray(CHAIN_HEADS)
    inverse = jnp.argsort(order)
    s0 = jnp.transpose(wkv[:, order], (2, 3, 0, 1)).reshape(RWKV_HEAD, RWKV_HEAD, b * RWKV_HEADS)
    if b % N_SEQ_TILE == 0 and L % RWKV_TB == 0:
        y, s_fin = rwkv_scan(parts, s0, tab, natural=True)
        y = y.reshape(b * L, D_MODEL)
    else:
        def to_chain(t):
            t = jnp.transpose(t.reshape(b, L, RWKV_HEADS, RWKV_HEAD)[:, :, order], (1, 3, 0, 2))
            return t.reshape(L, RWKV_HEAD, b * RWKV_HEADS)

        y, s_fin = rwkv_scan([to_chain(t) for t in parts], s0, tab, natural=False)
        y = jnp.transpose(y.reshape(L, RWKV_HEAD, b, RWKV_HEADS)[..., inverse], (2, 0, 3, 1)).reshape(b * L, D_MODEL)
    s_fin = jnp.transpose(s_fin.reshape(RWKV_HEAD, RWKV_HEAD, b, RWKV_HEADS), (2, 3, 0, 1))[:, inverse]
    return y, s_fin


def kernel(x_prompt, x_sample, state_ssm, state_conv, cache_swa_k, cache_swa_v, state_wkv, state_shift, rel_bias, norm_mix, norm_ffn, norm_final, mix_w_in, ssd_conv_w, ssd_conv_b, ssd_dt_bias, ssd_a_log, ssd_d_skip, ssd_gnorm, attn_sinks, mix_w_out, rwkv_mu, rwkv_w0, rwkv_w1, rwkv_w2, rwkv_a0, rwkv_a1, rwkv_a2, rwkv_g1, rwkv_g2, rwkv_k_k, rwkv_k_a, rwkv_r_k, rwkv_w_rkv, rwkv_w_o, rwkv_ln_w, rwkv_ln_b, peer_w_q, peer_sub_keys, peer_u, peer_v):
    bp, lp, d = x_prompt.shape
    bs, ls, _ = x_sample.shape
    tp = bp * lp
    ts = bs * ls
    x = jnp.concatenate([x_prompt.reshape(tp, d), x_sample.reshape(ts, d)], axis=0)

    def split(t):
        return t[:tp].reshape(bp, lp, -1), t[tp:].reshape(bs, ls, -1)

    def zero_state(a):
        return jnp.zeros((bp,) + a.shape[2:], a.dtype)

    def peer(x, ht, layer, g_next):
        wqt = peer_w_q[layer].T.astype(bf16)
        return peer_ffn_residual(x, ht, wqt, peer_sub_keys[layer].astype(bf16),
                                 peer_u[layer].astype(bf16), peer_v[layer].astype(bf16), g_next)

    z, xbc, q, kv, dt = in_proj(x, norm_mix[0], mix_w_in[0])
    pad16 = (0, LANE - SSM_HEADS)
    selh = (jnp.arange(LANE)[:, None] == jnp.arange(SSM_INNER)[None, :] // SSM_HEAD_DIM).astype(f32)
    ssd_params = (ssd_conv_w[0], ssd_conv_b[0].reshape(1, CONV_DIM),
                  jnp.pad(ssd_dt_bias[0], pad16).reshape(1, LANE),
                  jnp.pad(-jnp.exp(ssd_a_log[0]), pad16).reshape(1, LANE),
                  jnp.repeat(ssd_d_skip[0], SSM_HEAD_DIM).reshape(1, SSM_INNER),
                  ssd_gnorm[0].reshape(1, SSM_INNER), selh)
    y_ssd_p, ssm_p, conv_p = ssd_mixer(z, xbc, dt, zero_state(state_conv), zero_state(state_ssm), ssd_params,
                                       batch=bp, seq=lp, row0=0)
    y_ssd_s, ssm_s, conv_s = ssd_mixer(z, xbc, dt, state_conv[0], state_ssm[0], ssd_params,
                                       batch=bs, seq=ls, row0=tp)
    o_att_p, k_p, v_p = swa_mixer(q, kv, None, None, rel_bias, attn_sinks[0], batch=bp, seq=lp, row0=0)
    o_att_s, k_s, v_s = swa_mixer(q, kv, cache_swa_k[0], cache_swa_v[0], rel_bias, attn_sinks[0],
                                  batch=bs, seq=ls, row0=tp)
    x, ht = out_proj(y_ssd_p, o_att_p, y_ssd_s, o_att_s, mix_w_out[0].astype(bf16), x, norm_ffn[0])
    x, h = peer(x, ht, 0, norm_mix[1])

    h_p, h_s = split(h)
    prev_p = jnp.concatenate([jnp.zeros((bp, 1, d), f32), h_p[:, :-1]], axis=1)
    prev_s = jnp.concatenate([state_shift[0][:, None], h_s[:, :-1]], axis=1)
    def lora_pair(w1, w2):
        pad = LORA_PAD - w1.shape[1]
        return jnp.pad(w1, ((0, 0), (0, pad))), jnp.pad(w2, ((0, pad), (0, 0)))

    pairs = [lora_pair(rwkv_w1[0], rwkv_w2[0]), lora_pair(rwkv_a1[0], rwkv_a2[0]), lora_pair(rwkv_g1[0], rwkv_g2[0])]
    lora1 = jnp.stack([p[0] for p in pairs]).astype(bf16)
    lora2 = jnp.stack([p[1] for p in pairs]).astype(bf16)
    vecs = jnp.stack([rwkv_w0[0], rwkv_a0[0]])
    pre_args = (rwkv_mu[0], rwkv_w_rkv[0].astype(bf16), lora1, lora2, vecs)
    *parts_p, g_p = rwkv_pre(h, prev_p.reshape(tp, d), *pre_args, row0=0)
    *parts_s, g_s = rwkv_pre(h, prev_s.reshape(ts, d), *pre_args, row0=tp)

    def chain_table(p):
        return jnp.tile(p.reshape(RWKV_HEADS, RWKV_HEAD)[jnp.asarray(CHAIN_HEADS)].T, (1, LANE // RWKV_HEADS))

    tab = jnp.stack([chain_table(p) for p in (rwkv_r_k[0].reshape(-1), rwkv_ln_w[0], rwkv_ln_b[0],
                                              rwkv_k_k[0], rwkv_k_a[0])])
    y_p, wkv_p = _rwkv_core([t.reshape(bp, lp, d) for t in parts_p], zero_state(state_wkv), tab)
    y_s, wkv_s = _rwkv_core([t.reshape(bs, ls, d) for t in parts_s], state_wkv[0], tab)
    x, ht = gated_out_proj(y_p, g_p, y_s, g_s, rwkv_w_o[0].astype(bf16), x, norm_ffn[1])
    shift_p, shift_s = h_p[:, -1], h_s[:, -1]
    _, y = peer(x, ht, 1, norm_final)
    y_p, y_s = split(y)
    return (y_p, y_s, ssm_p[None], conv_p[None], k_p[None], v_p[None], wkv_p[None], shift_p[None],
            ssm_s[None], conv_s[None], k_s[None], v_s[None], wkv_s[None], shift_s[None])
```

```python
import functools
import math

import jax
import jax.numpy as jnp
from jax import lax
from jax.experimental import pallas as pl
from jax.experimental.pallas import tpu as pltpu

f32 = jnp.float32
bf16 = jnp.bfloat16

D_MODEL = 1024
PAST_LEN = 16384
SSM_HEAD_DIM = 64
SSM_HEADS = 16
SSM_INNER = 1024
SSM_GROUPS = 2
SSM_STATE = 128
CONV_W = 4
CONV_DIM = 1536
SSD_CHUNK = 128
ATT_HEAD_DIM = 64
ATT_HEADS = 16
ATT_KV_HEADS = 4
ATT_GQA = 4
WINDOW = 128
ATT_BLOCK = 128
REL_BUCKETS = 32
REL_MAX_DIST = 128
RWKV_HEAD = 64
RWKV_HEADS = 16
PEER_HEADS = 8
N_KEYS = 128
N_EXPERTS = N_KEYS * N_KEYS
PEER_TOPK = 16
PEER_HALF = 128
EPS = 1e-5
GN_EPS = 64e-5

LANE = 128
SUBLANE = 8
HALF_LANE = LANE // 2
VMEM_LIMIT = 56 * 2 ** 20
HIGHEST = lax.Precision.HIGHEST
NT_DIMS = (((1,), (1,)), ((), ()))
TN_DIMS = (((0,), (0,)), ((), ()))

_CAND_COUNTS = [PEER_TOPK // (a + 1) for a in range(PEER_TOPK)]
_N_CAND = sum(_CAND_COUNTS)
_N_CAND_PAD = -(-_N_CAND // 8) * 8
_BIG = 1e9


def _cparams(*sem):
    return pltpu.CompilerParams(dimension_semantics=sem, vmem_limit_bytes=VMEM_LIMIT)


def _rms(x, g):
    return x * lax.rsqrt(jnp.mean(x * x, axis=-1, keepdims=True) + EPS) * g


def _silu(x):
    return x * jax.nn.sigmoid(x)


BF16_ROWS = 2 * SUBLANE


def _dup_bf16(v):
    u = pltpu.bitcast(v.astype(bf16).astype(f32), jnp.int32)
    return u | lax.shift_right_logical(u, jnp.int32(16))


def _row_as_bf16_tile(ref, h, i):
    row = ref[h, pl.ds(i, 1), :]
    tile = pltpu.bitcast(jnp.broadcast_to(row, (SUBLANE, row.shape[1])), bf16)
    return jnp.concatenate([tile] * (N_KEYS // BF16_ROWS), axis=0)


def _peer_select_kernel(ht_ref, wqt_ref, sk_ref, pos_ref, b0_ref, r1_ref, e0_ref, e1_ref,
                        qt_ref, s_ref, vals_ref, idx_ref, cs_ref):
    tm = ht_ref.shape[1]
    qt_ref[...] = jnp.dot(wqt_ref[...], ht_ref[...], preferred_element_type=f32).astype(bf16)
    row = lax.broadcasted_iota(jnp.int32, (N_KEYS, tm), 0).astype(f32)
    pos = pos_ref[...]
    neg_inf = f32(-jnp.inf)

    def head_compute(h, exact):
        bad = jnp.zeros((1, tm), f32)
        for c in (0, 1):
            off = pl.multiple_of((h * 2 + c) * PEER_HALF, PEER_HALF)
            s_ref[c] = jnp.dot(sk_ref[h, c], qt_ref[pl.ds(off, PEER_HALF), :], preferred_element_type=f32)

        bad_tiles = []
        for lt in range(tm // LANE):
            ls = slice(lt * LANE, (lt + 1) * LANE)
            row_t = row[:, :LANE]

            def remove_max(k, s, c, lt=lt, row_t=row_t):
                m = jnp.max(s, axis=0, keepdims=True)
                vals_ref[c, lt, pl.ds(k, 1), :] = m
                if exact:
                    idx = jnp.min(jnp.where(s == m, row_t, f32(N_KEYS)), axis=0, keepdims=True)
                    idx_ref[c, lt, pl.ds(k, 1), :] = idx
                    hit = row_t == idx
                else:
                    hit = s == m
                return jnp.where(hit, neg_inf, s), hit

            def extract(k, st, remove_max=remove_max):
                sa, sb, rank = st
                sa, _ = remove_max(k, sa, 0)
                sb, hit = remove_max(k, sb, 1)
                return sa, sb, jnp.where(hit, jnp.asarray(k, f32), rank)

            sa_fin, sb_fin, rank = lax.fori_loop(
                0, PEER_TOPK, extract, (s_ref[0, :, ls], s_ref[1, :, ls], jnp.full((N_KEYS, LANE), f32(PEER_TOPK))))
            r1_ref[h, :, ls] = rank.astype(bf16)
            if not exact:
                bad_t = jnp.zeros((1, LANE), f32)
                for s_fin in (sa_fin, sb_fin):
                    removed = jnp.sum(jnp.where(s_fin == neg_inf, f32(1.0), f32(0.0)), axis=0, keepdims=True)
                    bad_t = bad_t + jnp.abs(removed - f32(PEER_TOPK))
                bad_tiles.append(bad_t)
        if not exact:
            bad = bad + jnp.concatenate(bad_tiles, axis=1)

        def lane_tiles(ref, c):
            return jnp.concatenate([ref[c, lt] for lt in range(tm // LANE)], axis=1)

        v0 = lane_tiles(vals_ref, 0)
        v1 = lane_tiles(vals_ref, 1)
        r = 0
        for a in range(PEER_TOPK):
            nb = _CAND_COUNTS[a]
            cs_ref[r:r + nb, :] = v0[a:a + 1, :] + v1[0:nb, :]
            r += nb
        cs_ref[_N_CAND:_N_CAND_PAD, :] = jnp.full((_N_CAND_PAD - _N_CAND, tm), neg_inf)
        m1 = v0[0:1, :] + v1[0:1, :]

        cs = cs_ref[...]
        if exact:
            def extract2(k, st):
                cs, sel, z = st
                m = jnp.max(cs, axis=0, keepdims=True)
                p = jnp.min(jnp.where(cs == m, pos, f32(_BIG)), axis=0, keepdims=True)
                hit = pos == p
                return jnp.where(hit, neg_inf, cs), jnp.where(hit, f32(1.0), sel), z + jnp.exp(m - m1)

            _, sel, z = lax.fori_loop(0, PEER_TOPK, extract2,
                                      (cs, jnp.zeros((_N_CAND_PAD, tm), f32), jnp.zeros((1, tm), f32)))
        else:
            larger = jnp.zeros((_N_CAND_PAD, tm), f32)
            for r in range(_N_CAND):
                larger = larger + jnp.where(cs[r:r + 1, :] > cs, f32(1.0), f32(0.0))
            sel = jnp.where((larger < f32(PEER_TOPK)) & (cs > neg_inf), f32(1.0), f32(0.0))
            z = jnp.sum(sel * jnp.exp(cs - m1), axis=0, keepdims=True)
            bad = bad + jnp.abs(jnp.sum(sel, axis=0, keepdims=True) - f32(PEER_TOPK))
        s0 = s_ref[0]
        idx0 = lane_tiles(idx_ref, 0)
        bound0 = jnp.zeros((N_KEYS, tm), f32)
        r = 0
        for a in range(PEER_TOPK):
            nb = _CAND_COUNTS[a]
            cnt = jnp.sum(sel[r:r + nb, :], axis=0, keepdims=True)
            is_a = (row == idx0[a:a + 1, :]) if exact else (s0 == v0[a:a + 1, :])
            bound0 = jnp.where(is_a, cnt, bound0)
            r += nb
        b0_ref[h] = _dup_bf16(bound0)
        e0_ref[h] = _dup_bf16(jnp.exp(s0 - v0[0:1, :]) / z)
        e1_ref[h] = jnp.exp(s_ref[1] - v1[0:1, :]).astype(bf16)
        return bad

    def head_body(h, carry):
        bad = head_compute(h, False)

        @pl.when(jnp.max(bad) > 0.0)
        def _():
            head_compute(h, True)

        return carry

    lax.fori_loop(0, PEER_HEADS, head_body, 0)


def _peer_main_kernel(ht_ref, u_ref, v_ref, b0_ref, r1_ref, e0_ref, e1_ref, x_ref, gn_ref, o_ref, hn_ref,
                      acc_ref, *, nblk, npieces):
    j = pl.program_id(1)

    @pl.when(j == 0)
    def _():
        acc_ref[...] = jnp.zeros_like(acc_ref)

    ht = ht_ref[...]
    bpp = nblk // npieces
    pc = bpp * N_KEYS
    for p in range(npieces):
        at = jnp.dot(u_ref[p * pc:(p + 1) * pc, :], ht, preferred_element_type=f32)
        ws = []
        for b in range(bpp):
            i = j * nblk + p * bpp + b
            a = at[b * N_KEYS:(b + 1) * N_KEYS, :]
            g = jnp.zeros(a.shape, bf16)
            for h in range(PEER_HEADS):
                bnd = _row_as_bf16_tile(b0_ref, h, i)
                e0 = _row_as_bf16_tile(e0_ref, h, i)
                g = g + jnp.where(r1_ref[h] < bnd, e1_ref[h] * e0, bf16(0.0))
            gelu = 0.5 * a * (1.0 + lax.erf(a * f32(math.sqrt(0.5))))
            ws.append(g * gelu.astype(bf16))
        wt = jnp.concatenate(ws, axis=0) if bpp > 1 else ws[0]
        acc_ref[...] += lax.dot_general(wt, v_ref[p * pc:(p + 1) * pc, :], TN_DIMS,
                                        preferred_element_type=f32)

    @pl.when(j == pl.num_programs(1) - 1)
    def _():
        xn = x_ref[...] + acc_ref[...]
        o_ref[...] = xn
        hn_ref[...] = _rms(xn, gn_ref[...])


def peer_ffn_residual(x, ht, wqt, sk, u, v, g_next, tm_sel=256, tm=256, ce=4096, npieces=16):
    t = x.shape[0]
    d = D_MODEL
    pos_list = []
    for a in range(PEER_TOPK):
        pos_list += [a * PEER_TOPK + b for b in range(_CAND_COUNTS[a])]
    pos_list += [_BIG] * (_N_CAND_PAD - _N_CAND)
    pos = jnp.broadcast_to(jnp.asarray(pos_list, f32)[:, None], (_N_CAND_PAD, tm_sel))
    sel_shape = jax.ShapeDtypeStruct((PEER_HEADS, N_KEYS, t), jnp.int32)
    sel_shape16 = jax.ShapeDtypeStruct((PEER_HEADS, N_KEYS, t), bf16)
    sel_spec = pl.BlockSpec((PEER_HEADS, N_KEYS, tm_sel), lambda i: (0, 0, i))
    b0, r1, e0, e1 = pl.pallas_call(
        _peer_select_kernel,
        grid=(t // tm_sel,),
        in_specs=[pl.BlockSpec((d, tm_sel), lambda i: (0, i)),
                  pl.BlockSpec((PEER_HEADS * 2 * PEER_HALF, d), lambda i: (0, 0)),
                  pl.BlockSpec((PEER_HEADS, 2, N_KEYS, PEER_HALF), lambda i: (0, 0, 0, 0)),
                  pl.BlockSpec((_N_CAND_PAD, tm_sel), lambda i: (0, 0))],
        out_specs=[sel_spec] * 4,
        out_shape=[sel_shape, sel_shape16, sel_shape, sel_shape16],
        scratch_shapes=[pltpu.VMEM((PEER_HEADS * 2 * PEER_HALF, tm_sel), bf16),
                        pltpu.VMEM((2, N_KEYS, tm_sel), f32),
                        pltpu.VMEM((2, tm_sel // LANE, PEER_TOPK, LANE), f32),
                        pltpu.VMEM((2, tm_sel // LANE, PEER_TOPK, LANE), f32),
                        pltpu.VMEM((_N_CAND_PAD, tm_sel), f32)],
        compiler_params=_cparams("parallel"),
        name="peer_select",
    )(ht, wqt, sk, pos)

    nblk = ce // N_KEYS
    sel_spec2 = pl.BlockSpec((PEER_HEADS, N_KEYS, tm), lambda i, j: (0, 0, i))
    return pl.pallas_call(
        functools.partial(_peer_main_kernel, nblk=nblk, npieces=npieces),
        grid=(t // tm, N_EXPERTS // ce),
        in_specs=[pl.BlockSpec((d, tm), lambda i, j: (0, i)),
                  pl.BlockSpec((ce, d), lambda i, j: (j, 0)),
                  pl.BlockSpec((ce, d), lambda i, j: (j, 0)),
                  sel_spec2, sel_spec2, sel_spec2, sel_spec2,
                  pl.BlockSpec((tm, d), lambda i, j: (i, 0)), pl.BlockSpec((1, d), lambda i, j: (0, 0))],
        out_specs=[pl.BlockSpec((tm, d), lambda i, j: (i, 0))] * 2,
        out_shape=[jax.ShapeDtypeStruct((t, d), f32)] * 2,
        scratch_shapes=[pltpu.VMEM((tm, d), f32)],
        compiler_params=_cparams("parallel", "arbitrary"),
        name="peer_main",
    )(ht, u, v, b0, r1, e0, e1, x, g_next.reshape(1, d))


IN_WIDTHS = (SSM_INNER, CONV_DIM, ATT_HEADS * ATT_HEAD_DIM, 2 * ATT_KV_HEADS * ATT_HEAD_DIM, LANE)


def _in_proj_kernel(x_ref, g_ref, w_ref, *out_refs):
    h = _rms(x_ref[...], g_ref[...])
    y = jnp.dot(h.astype(bf16), w_ref[...], preferred_element_type=f32)
    o = 0
    for ref in out_refs:
        w = ref.shape[1]
        ref[...] = y[:, o:o + w]
        o += w


def in_proj(x, g, w_in, tm=256):
    t, d = x.shape
    o1 = SSM_INNER
    o2 = o1 + CONV_DIM
    o3 = o2 + SSM_HEADS
    w = jnp.concatenate([w_in[:, :o2], w_in[:, o3:], w_in[:, o2:o3],
                         jnp.zeros((d, LANE - SSM_HEADS), w_in.dtype)], axis=1).astype(bf16)
    n = w.shape[1]
    return pl.pallas_call(
        _in_proj_kernel,
        grid=(t // tm,),
        in_specs=[pl.BlockSpec((tm, d), lambda i: (i, 0)), pl.BlockSpec((1, d), lambda i: (0, 0)),
                  pl.BlockSpec((d, n), lambda i: (0, 0))],
        out_specs=[pl.BlockSpec((tm, wd), lambda i: (i, 0)) for wd in IN_WIDTHS],
        out_shape=[jax.ShapeDtypeStruct((t, wd), f32) for wd in IN_WIDTHS],
        compiler_params=_cparams("parallel"),
        name="in_proj",
    )(x, g.reshape(1, d), w)


N_PAIRS = SSM_HEADS // 2
SSD_SHORT_CHUNK = 32


def _ssd_kernel(*refs, lin, lc):
    (z_ref, xbc_ref, dt_ref, cs_ref, h0_ref, cw_ref, cb_ref, dtb_ref, aneg_ref, dsk_ref, gn_ref, selh_ref) = refs[:12]
    refs = refs[12:]
    y_ref, hfin_ref, tail_ref, xe_ref, hp_ref, ys_ref = refs[:6]
    c = pl.program_id(1)

    @pl.when(c == 0)
    def _():
        xe_ref[0:SUBLANE, :] = cs_ref[0]
        hp_ref[...] = h0_ref[0]

    if lin == lc:
        xe_ref[SUBLANE:SUBLANE + lc, :] = xbc_ref[...]
        z = z_ref[...]
        dt_raw = dt_ref[...]
    else:
        zpad_ref, dtpad_ref = refs[6:8]
        xe_ref[SUBLANE:SUBLANE + lc, :] = jnp.zeros((lc, CONV_DIM), f32)
        xe_ref[SUBLANE:SUBLANE + lin, :] = xbc_ref[...]
        zpad_ref[...] = jnp.zeros_like(zpad_ref)
        zpad_ref[0:lin, :] = z_ref[...]
        dtpad_ref[...] = jnp.zeros_like(dtpad_ref)
        dtpad_ref[0:lin, :] = dt_ref[...]
        z = zpad_ref[...]
        dt_raw = dtpad_ref[...]

    conv = cb_ref[...]
    for j in range(CONV_W):
        o = SUBLANE - (CONV_W - 1) + j
        conv = conv + xe_ref[o:o + lc, :] * cw_ref[j:j + 1, :]
    if lin == lc:
        xe_ref[0:SUBLANE, :] = xe_ref[lc:lc + SUBLANE, :]
    xc = _silu(conv)
    xs = xc[:, :SSM_INNER]
    bm = xc[:, SSM_INNER:SSM_INNER + SSM_GROUPS * SSM_STATE]
    cm = xc[:, SSM_INNER + SSM_GROUPS * SSM_STATE:]

    row = lax.broadcasted_iota(jnp.int32, (lc, LANE), 0)
    col = lax.broadcasted_iota(jnp.int32, (lc, LANE), 1)
    causal = lax.broadcasted_iota(jnp.int32, (lc, lc), 0) >= lax.broadcasted_iota(jnp.int32, (lc, lc), 1)
    pair_row = lax.broadcasted_iota(jnp.int32, (LANE, SSM_STATE), 0)
    lane_lo = col < HALF_LANE
    neg_inf = f32(-jnp.inf)

    x = dt_raw + dtb_ref[...]
    dt = jnp.maximum(x, 0.0) + jnp.log(1.0 + jnp.exp(-jnp.abs(x)))
    if lin != lc:
        dt = jnp.where(row < lin, dt, 0.0)
    la = dt * aneg_ref[...]
    acs = jnp.dot(causal.astype(f32), la, precision=HIGHEST, preferred_element_type=f32)
    acs_sq = acs if lc == LANE else jnp.concatenate([acs, jnp.zeros((LANE - lc, LANE), f32)], axis=0)
    acs_t = acs_sq.T[:, :lc]
    selh = selh_ref[...]
    dt_exp = jnp.dot(dt, selh, precision=HIGHEST, preferred_element_type=f32)
    acs_exp = jnp.dot(acs, selh, precision=HIGHEST, preferred_element_type=f32)
    alast = acs[lc - 1:lc, :]
    xdt = xs * dt_exp
    eacs = jnp.exp(acs_exp)
    xdt_end = (xdt * jnp.exp(acs_exp[lc - 1:lc, :] - acs_exp)).astype(bf16)
    dsk = dsk_ref[...]

    for g in range(SSM_GROUPS):
        cmg = cm[:, g * SSM_STATE:(g + 1) * SSM_STATE].astype(bf16)
        bmg = bm[:, g * SSM_STATE:(g + 1) * SSM_STATE].astype(bf16)
        cb = lax.dot_general(cmg, bmg, NT_DIMS, preferred_element_type=f32)
        for jj in range(N_PAIRS // SSM_GROUPS):
            j = g * (N_PAIRS // SSM_GROUPS) + jj
            sl = slice(j * LANE, (j + 1) * LANE)
            xdt_pair = xdt[:, sl]
            ydiag = jnp.zeros((lc, LANE), f32)
            for half in (0, 1):
                h = 2 * j + half
                seg = acs[:, h:h + 1] - acs_t[h:h + 1, :]
                dec = jnp.exp(jnp.where(causal, seg, neg_inf))
                m = (cb * dec).astype(bf16)
                keep = lane_lo if half == 0 else jnp.logical_not(lane_lo)
                xd = jnp.where(keep, xdt_pair, 0.0).astype(bf16)
                ydiag = ydiag + jnp.dot(m, xd, preferred_element_type=f32)
            hpj = hp_ref[j]
            yoff = lax.dot_general(cmg, hpj.astype(bf16), NT_DIMS, preferred_element_type=f32) * eacs[:, sl]
            s_new = lax.dot_general(xdt_end[:, sl], bmg, TN_DIMS, preferred_element_type=f32)
            dl = jnp.where(pair_row < SSM_HEAD_DIM, alast[:, 2 * j:2 * j + 1], alast[:, 2 * j + 1:2 * j + 2])
            hp_ref[j] = hpj * jnp.exp(dl) + s_new
            ys_ref[:, sl] = ydiag + yoff + dsk[:, sl] * xs[:, sl]

    y = ys_ref[...] * _silu(z)
    gs = SSM_INNER // SSM_GROUPS
    gn = gn_ref[...]
    for g in range(SSM_GROUPS):
        yg = y[:, g * gs:(g + 1) * gs]
        yg = yg * lax.rsqrt(jnp.mean(yg * yg, axis=-1, keepdims=True) + EPS) * gn[:, g * gs:(g + 1) * gs]
        y_ref[:, g * gs:(g + 1) * gs] = yg[0:lin, :]

    @pl.when(c == pl.num_programs(1) - 1)
    def _():
        hfin_ref[0] = hp_ref[...]
        tail_ref[0] = xe_ref[lin:lin + SUBLANE, :]


def ssd_mixer(z, xbc, dt, conv_state, ssm_state, params, *, batch, seq, row0):
    conv_w, conv_b, dt_bias, a_neg, d_skip, gnorm, selh = params
    lin = min(seq, SSD_CHUNK)
    lc = SSD_CHUNK if lin == SSD_CHUNK else SSD_SHORT_CHUNK
    nc = seq // lin
    blk0 = row0 // lin
    cs = jnp.pad(conv_state, ((0, 0), (SUBLANE - (CONV_W - 1), 0), (0, 0)))
    h0 = ssm_state.reshape(batch, N_PAIRS, LANE, SSM_STATE)

    def rows(w):
        return pl.BlockSpec((lin, w), lambda b, c: (blk0 + b * nc + c, 0))

    def const(shape):
        return pl.BlockSpec(shape, lambda b, c: (0,) * len(shape))

    in_specs = [rows(SSM_INNER), rows(CONV_DIM), rows(LANE),
                pl.BlockSpec((1, SUBLANE, CONV_DIM), lambda b, c: (b, 0, 0)),
                pl.BlockSpec((1, N_PAIRS, LANE, SSM_STATE), lambda b, c: (b, 0, 0, 0)),
                const((CONV_W, CONV_DIM)), const((1, CONV_DIM)), const((1, LANE)), const((1, LANE)),
                const((1, SSM_INNER)), const((1, SSM_INNER)), const((LANE, SSM_INNER))]
    args = [z, xbc, dt, cs, h0, conv_w, conv_b, dt_bias, a_neg, d_skip, gnorm, selh]
    scratch = [pltpu.VMEM((lc + 2 * SUBLANE, CONV_DIM), f32), pltpu.VMEM((N_PAIRS, LANE, SSM_STATE), f32),
               pltpu.VMEM((lc, SSM_INNER), f32)]
    if lin != lc:
        scratch += [pltpu.VMEM((lc, SSM_INNER), f32), pltpu.VMEM((lc, LANE), f32)]
    y, h_fin, tail = pl.pallas_call(
        functools.partial(_ssd_kernel, lin=lin, lc=lc),
        grid=(batch, nc),
        in_specs=in_specs,
        out_specs=[pl.BlockSpec((lin, SSM_INNER), lambda b, c: (b * nc + c, 0)),
                   pl.BlockSpec((1, N_PAIRS, LANE, SSM_STATE), lambda b, c: (b, 0, 0, 0)),
                   pl.BlockSpec((1, SUBLANE, CONV_DIM), lambda b, c: (b, 0, 0))],
        out_shape=[jax.ShapeDtypeStruct((batch * seq, SSM_INNER), f32),
                   jax.ShapeDtypeStruct((batch, N_PAIRS, LANE, SSM_STATE), f32),
                   jax.ShapeDtypeStruct((batch, SUBLANE, CONV_DIM), f32)],
        scratch_shapes=scratch,
        compiler_params=_cparams("parallel", "arbitrary"),
        name="ssd_mixer",
    )(*args)
    new_conv = tail[:, SUBLANE - (CONV_W - 1):]
    return y, h_fin.reshape(batch, SSM_HEADS, SSM_HEAD_DIM, SSM_STATE), new_conv


def _swa_kernel(*refs, lq, masked_first):
    q_ref, kp_ref, vp_ref, kc_ref, vc_ref, bias_ref, sink_ref = refs[:7]
    refs = refs[7:]
    o_ref, nk_ref, nv_ref = refs[:3]
    n = pl.program_id(1)
    q = q_ref[...]
    for new_ref, prev_ref, cur_ref in ((nk_ref, kp_ref, kc_ref), (nv_ref, vp_ref, vc_ref)):
        if lq < WINDOW:
            new_ref[0:WINDOW - lq, :] = prev_ref[lq:WINDOW, :]
        new_ref[WINDOW - lq:WINDOW, :] = cur_ref[...]
    if lq == ATT_BLOCK:
        kc = kc_ref[...]
        vc = vc_ref[...]
    else:
        kpad_ref, vpad_ref = refs[3:5]
        kpad_ref[...] = jnp.zeros_like(kpad_ref)
        vpad_ref[...] = jnp.zeros_like(vpad_ref)
        kpad_ref[0:lq, :] = kc_ref[...]
        vpad_ref[0:lq, :] = vc_ref[...]
        kc = kpad_ref[...]
        vc = vpad_ref[...]
    kp = kp_ref[...]
    vp = vp_ref[...]
    lane_lo = lax.broadcasted_iota(jnp.int32, (ATT_BLOCK, LANE), 1) < HALF_LANE
    lane_lo_q = lax.broadcasted_iota(jnp.int32, (lq, LANE), 1) < HALF_LANE
    neg_inf = f32(-jnp.inf)
    scale = f32(ATT_HEAD_DIM ** -0.5)

    for g in range(ATT_KV_HEADS):
        sl = slice((g // 2) * LANE, (g // 2 + 1) * LANE)
        odd = g % 2 == 1

        def kpad(k):
            pair = k[:, sl]
            if odd:
                pair = pltpu.roll(pair, HALF_LANE, 1)
            return jnp.where(lane_lo, pair, 0.0).astype(bf16)

        def vdup(v):
            pair = v[:, sl]
            rolled = pltpu.roll(pair, HALF_LANE, 1)
            return (jnp.where(lane_lo, rolled, pair) if odd else jnp.where(lane_lo, pair, rolled)).astype(bf16)

        qp0 = q[:, (2 * g) * LANE:(2 * g + 1) * LANE]
        qp1 = q[:, (2 * g + 1) * LANE:(2 * g + 2) * LANE]
        qg = jnp.concatenate([qp0, pltpu.roll(qp0, HALF_LANE, 1), qp1, pltpu.roll(qp1, HALF_LANE, 1)],
                             axis=0).astype(bf16)
        bias = bias_ref[g]
        sp = lax.dot_general(qg, kpad(kp), NT_DIMS, preferred_element_type=f32) * scale + bias[:, :WINDOW]
        sc = lax.dot_general(qg, kpad(kc), NT_DIMS, preferred_element_type=f32) * scale + bias[:, WINDOW:]
        if masked_first:
            sp = jnp.where(n > 0, sp, neg_inf)
        sink = sink_ref[g][:, 0:1]
        m = jnp.maximum(jnp.max(jnp.maximum(sp, sc), axis=-1, keepdims=True), sink)
        pp = jnp.exp(sp - m)
        pc = jnp.exp(sc - m)
        denom = jnp.sum(pp + pc, axis=-1, keepdims=True) + jnp.exp(sink - m)
        og = (jnp.dot(pp.astype(bf16), vdup(vp), preferred_element_type=f32)
              + jnp.dot(pc.astype(bf16), vdup(vc), preferred_element_type=f32)) / denom
        o_ref[:, (2 * g) * LANE:(2 * g + 1) * LANE] = jnp.where(lane_lo_q, og[0:lq], og[lq:2 * lq])
        o_ref[:, (2 * g + 1) * LANE:(2 * g + 2) * LANE] = jnp.where(lane_lo_q, og[2 * lq:3 * lq], og[3 * lq:4 * lq])


def _rel_bucket(dist):
    exact = REL_BUCKETS // 2
    d = jnp.maximum(dist, 0)
    large = exact + (jnp.log(jnp.maximum(d, 1).astype(f32) / exact)
                     / math.log(REL_MAX_DIST / exact) * (REL_BUCKETS - exact)).astype(jnp.int32)
    large = jnp.minimum(large, REL_BUCKETS - 1)
    return jnp.where(d < exact, d, large)


def _swa_tables(rel_bias, sinks, lq):
    qi = jnp.arange(lq)[:, None]
    kj = jnp.arange(2 * ATT_BLOCK)[None, :]
    dist = qi + WINDOW - kj
    band = (dist >= 0) & (dist <= WINDOW)
    onehot = (_rel_bucket(dist)[..., None] == jnp.arange(REL_BUCKETS)).astype(f32)
    bias = jnp.einsum('qkb,bh->qkh', onehot, rel_bias, precision=HIGHEST)
    bias = jnp.where(band[..., None], bias, -jnp.inf)
    bias = jnp.transpose(bias, (2, 0, 1)).reshape(ATT_KV_HEADS, ATT_GQA * lq, 2 * ATT_BLOCK)
    sink = jnp.broadcast_to(sinks.reshape(ATT_KV_HEADS, ATT_GQA, 1, 1), (ATT_KV_HEADS, ATT_GQA, lq, LANE))
    return bias, sink.reshape(ATT_KV_HEADS, ATT_GQA * lq, LANE)


def swa_mixer(q, kv, k_prev, v_prev, rel_bias, sinks, *, batch, seq, row0):
    lq = min(seq, ATT_BLOCK)
    nb = seq // lq
    blk0 = row0 // lq
    kvw = ATT_KV_HEADS * ATT_HEAD_DIM
    bias, sink = _swa_tables(rel_bias, sinks, lq)
    cur_k = pl.BlockSpec((lq, kvw), lambda b, n: (blk0 + b * nb + n, 0))
    cur_v = pl.BlockSpec((lq, kvw), lambda b, n: (blk0 + b * nb + n, 1))
    if k_prev is None:
        prev_k = pl.BlockSpec((WINDOW, kvw), lambda b, n: (blk0 + b * nb + jnp.maximum(n - 1, 0), 0))
        prev_v = pl.BlockSpec((WINDOW, kvw), lambda b, n: (blk0 + b * nb + jnp.maximum(n - 1, 0), 1))
        kp_arr, vp_arr = kv, kv
    else:
        prev_k = pl.BlockSpec((WINDOW, kvw), lambda b, n: (b, 0))
        prev_v = prev_k
        kp_arr = k_prev.reshape(batch * WINDOW, kvw)
        vp_arr = v_prev.reshape(batch * WINDOW, kvw)
    rows = pl.BlockSpec((lq, ATT_HEADS * ATT_HEAD_DIM), lambda b, n: (blk0 + b * nb + n, 0))
    in_specs = [rows, prev_k, prev_v, cur_k, cur_v,
                pl.BlockSpec(bias.shape, lambda b, n: (0, 0, 0)), pl.BlockSpec(sink.shape, lambda b, n: (0, 0, 0))]
    args = [q, kp_arr, vp_arr, kv, kv, bias, sink]
    scratch = [] if lq == ATT_BLOCK else [pltpu.VMEM((ATT_BLOCK, kvw), f32), pltpu.VMEM((ATT_BLOCK, kvw), f32)]
    cache_spec = pl.BlockSpec((WINDOW, kvw), lambda b, n: (b, 0))
    cache_shape = jax.ShapeDtypeStruct((batch * WINDOW, kvw), f32)
    o, new_k, new_v = pl.pallas_call(
        functools.partial(_swa_kernel, lq=lq, masked_first=k_prev is None),
        grid=(batch, nb),
        in_specs=in_specs,
        out_specs=[pl.BlockSpec((lq, ATT_HEADS * ATT_HEAD_DIM), lambda b, n: (b * nb + n, 0)), cache_spec, cache_spec],
        out_shape=[jax.ShapeDtypeStruct((batch * seq, ATT_HEADS * ATT_HEAD_DIM), f32), cache_shape, cache_shape],
        scratch_shapes=scratch,
        compiler_params=_cparams("parallel", "arbitrary"),
        name="swa_mixer",
    )(*args)
    cshape = (batch, WINDOW, ATT_KV_HEADS, ATT_HEAD_DIM)
    return o, new_k.reshape(cshape), new_v.reshape(cshape)


def _out_proj_kernel(yap_ref, ybp_ref, yas_ref, ybs_ref, w_ref, x_ref, g_ref, o_ref, ht_ref, *, n_prompt):
    ka = yap_ref.shape[1]

    def body(ya_ref, yb_ref):
        acc = jnp.dot(ya_ref[...].astype(bf16), w_ref[0:ka, :], preferred_element_type=f32)
        acc = acc + jnp.dot(yb_ref[...].astype(bf16), w_ref[ka:, :], preferred_element_type=f32)
        xn = x_ref[...] + acc
        o_ref[...] = xn
        ht_ref[...] = _rms(xn, g_ref[...]).T.astype(bf16)

    @pl.when(pl.program_id(0) < n_prompt)
    def _():
        body(yap_ref, ybp_ref)

    @pl.when(pl.program_id(0) >= n_prompt)
    def _():
        body(yas_ref, ybs_ref)


def out_proj(ya_p, yb_p, ya_s, yb_s, w, x, g, tm=512):
    t, d = x.shape
    ka, kb = ya_p.shape[1], yb_p.shape[1]
    n_p = ya_p.shape[0] // tm
    n_s = ya_s.shape[0] // tm

    def p_rows(k):
        return pl.BlockSpec((tm, k), lambda i: (jnp.minimum(i, n_p - 1), 0))

    def s_rows(k):
        return pl.BlockSpec((tm, k), lambda i: (jnp.maximum(i - n_p, 0), 0))

    return pl.pallas_call(
        functools.partial(_out_proj_kernel, n_prompt=n_p),
        grid=(n_p + n_s,),
        in_specs=[p_rows(ka), p_rows(kb), s_rows(ka), s_rows(kb),
                  pl.BlockSpec((ka + kb, d), lambda i: (0, 0)), pl.BlockSpec((tm, d), lambda i: (i, 0)),
                  pl.BlockSpec((1, d), lambda i: (0, 0))],
        out_specs=[pl.BlockSpec((tm, d), lambda i: (i, 0)), pl.BlockSpec((d, tm), lambda i: (0, i))],
        out_shape=[jax.ShapeDtypeStruct((t, d), f32), jax.ShapeDtypeStruct((d, t), bf16)],
        compiler_params=_cparams("arbitrary"),
        name="out_proj",
    )(ya_p, yb_p, ya_s, yb_s, w, x, g.reshape(1, d))


def _gated_out_kernel(yp_ref, gp_ref, ys_ref, gs_ref, w_ref, x_ref, gn_ref, o_ref, ht_ref, *, n_prompt):
    def body(y_ref, g_ref):
        a = (y_ref[...] * g_ref[...]).astype(bf16)
        xn = x_ref[...] + jnp.dot(a, w_ref[...], preferred_element_type=f32)
        o_ref[...] = xn
        ht_ref[...] = _rms(xn, gn_ref[...]).T.astype(bf16)

    @pl.when(pl.program_id(0) < n_prompt)
    def _():
        body(yp_ref, gp_ref)

    @pl.when(pl.program_id(0) >= n_prompt)
    def _():
        body(ys_ref, gs_ref)


def gated_out_proj(y_p, g_p, y_s, g_s, w, x, gn, tm=512):
    t, d = x.shape
    n_p = y_p.shape[0] // tm
    n_s = y_s.shape[0] // tm
    rows = pl.BlockSpec((tm, d), lambda i: (i, 0))
    p_rows = pl.BlockSpec((tm, d), lambda i: (jnp.minimum(i, n_p - 1), 0))
    s_rows = pl.BlockSpec((tm, d), lambda i: (jnp.maximum(i - n_p, 0), 0))
    return pl.pallas_call(
        functools.partial(_gated_out_kernel, n_prompt=n_p),
        grid=(n_p + n_s,),
        in_specs=[p_rows, p_rows, s_rows, s_rows,
                  pl.BlockSpec((d, d), lambda i: (0, 0)), rows, pl.BlockSpec((1, d), lambda i: (0, 0))],
        out_specs=[rows, pl.BlockSpec((d, tm), lambda i: (0, i))],
        out_shape=[jax.ShapeDtypeStruct((t, d), f32), jax.ShapeDtypeStruct((d, t), bf16)],
        compiler_params=_cparams("arbitrary"),
        name="rwkv_out",
    )(y_p, g_p, y_s, g_s, w, x, gn.reshape(1, d))


LORA_PAD = 2 * LANE


def _softplus(x):
    return jnp.maximum(x, 0.0) + jnp.log(1.0 + jnp.exp(-jnp.abs(x)))


def _rwkv_pre_kernel(h_ref, p_ref, mu_ref, wrkv_ref, l1_ref, l2_ref, vec_ref,
                     r_ref, d_ref, k_ref, v_ref, a_ref, g_ref):
    h = h_ref[...]
    xx = p_ref[...] - h

    def mix(j):
        return (h + xx * mu_ref[j:j + 1, :]).astype(bf16)

    def mm(a, w):
        return jnp.dot(a, w, preferred_element_type=f32)

    r_ref[...] = mm(mix(0), wrkv_ref[0])
    k_ref[...] = mm(mix(2), wrkv_ref[1])
    v_ref[...] = mm(mix(3), wrkv_ref[2])
    wl = vec_ref[0:1, :] + mm(jnp.tanh(mm(mix(1), l1_ref[0])).astype(bf16), l2_ref[0])
    al = vec_ref[1:2, :] + mm(mm(mix(4), l1_ref[1]).astype(bf16), l2_ref[1])
    g_ref[...] = mm(jax.nn.sigmoid(mm(mix(5), l1_ref[2])).astype(bf16), l2_ref[2])
    w = -_softplus(-wl) - 0.5
    d_ref[...] = jnp.exp(-jnp.exp(w))
    a_ref[...] = jax.nn.sigmoid(al)


def rwkv_pre(h, prev, mu, w_rkv, lora1, lora2, vecs, *, row0, tm=256):
    t, d = prev.shape
    blk0 = row0 // tm
    rows = pl.BlockSpec((tm, d), lambda i: (i, 0))

    def const(a):
        return pl.BlockSpec(a.shape, lambda i: (0,) * a.ndim)

    return pl.pallas_call(
        _rwkv_pre_kernel,
        grid=(t // tm,),
        in_specs=[pl.BlockSpec((tm, d), lambda i: (blk0 + i, 0)), rows,
                  const(mu), const(w_rkv), const(lora1), const(lora2), const(vecs)],
        out_specs=[rows] * 6,
        out_shape=[jax.ShapeDtypeStruct((t, d), f32)] * 6,
        compiler_params=_cparams("parallel"),
        name="rwkv_pre",
    )(h, prev, mu, w_rkv, lora1, lora2, vecs)


RWKV_VGROUP = 8


RWKV_TB = 16
N_SEQ_TILE = LANE // RWKV_HEADS


N_COLS = D_MODEL // LANE


def _swap_sublane_with_column(cols):
    sub = lax.broadcasted_iota(jnp.int32, cols[0].shape, 0)
    s = N_COLS // 2
    while s >= 1:
        upper = (sub & s) != 0
        new = list(cols)
        for c in range(N_COLS):
            if c & s == 0:
                a, b = cols[c], cols[c | s]
                new[c] = jnp.where(upper, pltpu.roll(b, s, 0), a)
                new[c | s] = jnp.where(upper, b, pltpu.roll(a, LANE - s, 0))
        cols = new
        s //= 2
    return cols


def _swap_rowbit_with_lanehalf(col):
    a = jnp.concatenate([col[g * 16:g * 16 + SUBLANE] for g in range(N_SEQ_TILE)], axis=0)
    b = jnp.concatenate([col[g * 16 + SUBLANE:(g + 1) * 16] for g in range(N_SEQ_TILE)], axis=0)
    upper = lax.broadcasted_iota(jnp.int32, a.shape, 1) >= HALF_LANE
    moved = pltpu.roll(jnp.where(upper, a, b), HALF_LANE, 1)
    a = jnp.where(upper, moved, a)
    b = jnp.where(upper, b, moved)
    pieces = []
    for g in range(N_SEQ_TILE):
        pieces += [a[g * SUBLANE:(g + 1) * SUBLANE], b[g * SUBLANE:(g + 1) * SUBLANE]]
    return jnp.concatenate(pieces, axis=0)


def _natural_to_chain(x_ref, dst_ref, q):
    x2 = x_ref[...].reshape(N_SEQ_TILE * RWKV_TB, D_MODEL)
    cols = _swap_sublane_with_column([x2[:, c * LANE:(c + 1) * LANE] for c in range(N_COLS)])
    for c in range(N_COLS):
        xt = _swap_rowbit_with_lanehalf(cols[c]).T
        dst_ref[q, c] = xt[:RWKV_HEAD]
        dst_ref[q, RWKV_TB // 2 + c] = xt[RWKV_HEAD:]


def _chain_to_natural(src_ref, y_ref):
    cols = []
    for c in range(N_COLS):
        xt = jnp.concatenate([src_ref[c], src_ref[RWKV_TB // 2 + c]], axis=0)
        cols.append(_swap_rowbit_with_lanehalf(xt.T))
    y2 = jnp.concatenate(_swap_sublane_with_column(cols), axis=1)
    y_ref[...] = y2.reshape(N_SEQ_TILE, RWKV_TB, D_MODEL)


N_SCAN_IN = 5


def _rwkv_scan_kernel(r_ref, d_ref, k_ref, v_ref, a_ref, s0_ref, tab_ref, y_ref, sfin_ref,
                      s_ref, q_ref, ys_ref, *, natural):
    tb = q_ref.shape[1]

    @pl.when(pl.program_id(1) == 0)
    def _():
        s_ref[...] = s0_ref[...]

    for q, ref in enumerate((r_ref, d_ref, k_ref, v_ref, a_ref)):
        if natural:
            _natural_to_chain(ref, q_ref, q)
        else:
            q_ref[q] = ref[...]
    r_q, d_q, k_q, v_q, a_q = [q_ref.at[q] for q in range(N_SCAN_IN)]

    def step(t, carry):
        k_raw = k_q[t]
        a = a_q[t]
        kkr = k_raw * tab_ref[3]
        nrm = jnp.sqrt(jnp.sum(kkr * kkr, axis=0, keepdims=True))
        kk = kkr / jnp.maximum(nrm, 1e-12)
        d = d_q[t]
        kv = k_raw * (1.0 + (a - 1.0) * tab_ref[4])
        r = r_q[t]
        b = kk * a

        def vgroup(g, c2):
            v0 = pl.multiple_of(g * RWKV_VGROUP, RWKV_VGROUP)
            vrows = v_q[t, pl.ds(v0, RWKV_VGROUP), :]
            ys = []
            for vi in range(RWKV_VGROUP):
                sv = s_ref[v0 + vi]
                sa = -jnp.sum(sv * kk, axis=0, keepdims=True)
                sn = sv * d + sa * b + vrows[vi:vi + 1, :] * kv
                s_ref[v0 + vi] = sn
                ys.append(jnp.sum(sn * r, axis=0, keepdims=True))
            ys_ref[t, pl.ds(v0, RWKV_VGROUP), :] = jnp.concatenate(ys, axis=0)
            return c2

        lax.fori_loop(0, RWKV_HEAD // RWKV_VGROUP, vgroup, 0)
        y = ys_ref[t]
        mean = jnp.mean(y, axis=0, keepdims=True)
        yc = y - mean
        var = jnp.mean(yc * yc, axis=0, keepdims=True)
        bonus = jnp.sum(r * kv * tab_ref[0], axis=0, keepdims=True)
        ys_ref[t] = yc * lax.rsqrt(var + GN_EPS) * tab_ref[1] + tab_ref[2] + bonus * v_q[t]
        return carry

    lax.fori_loop(0, tb, step, 0)
    if natural:
        _chain_to_natural(ys_ref, y_ref)
    else:
        y_ref[...] = ys_ref[...]

    @pl.when(pl.program_id(1) == pl.num_programs(1) - 1)
    def _():
        sfin_ref[...] = s_ref[...]


def rwkv_scan(parts, s0, tab, *, natural):
    hd = RWKV_HEAD
    c = s0.shape[-1]
    if natural:
        nb, L, dm = parts[0].shape
        tb = RWKV_TB
        seq_spec = pl.BlockSpec((N_SEQ_TILE, tb, dm), lambda i, j: (i, j, 0))
        y_shape = jax.ShapeDtypeStruct((nb, L, dm), f32)
    else:
        L = parts[0].shape[0]
        tb = min(L, RWKV_TB)
        seq_spec = pl.BlockSpec((tb, hd, LANE), lambda i, j: (j, 0, i))
        y_shape = jax.ShapeDtypeStruct((L, hd, c), f32)
    st_spec = pl.BlockSpec((hd, hd, LANE), lambda i, j: (0, 0, i))
    return pl.pallas_call(
        functools.partial(_rwkv_scan_kernel, natural=natural),
        grid=(c // LANE, L // tb),
        in_specs=[seq_spec] * N_SCAN_IN + [st_spec, pl.BlockSpec(tab.shape, lambda i, j: (0, 0, 0))],
        out_specs=[seq_spec, st_spec],
        out_shape=[y_shape, jax.ShapeDtypeStruct((hd, hd, c), f32)],
        scratch_shapes=[pltpu.VMEM((hd, hd, LANE), f32), pltpu.VMEM((N_SCAN_IN, tb, hd, LANE), f32),
                        pltpu.VMEM((tb, hd, LANE), f32)],
        compiler_params=_cparams("parallel", "arbitrary"),
        name="rwkv_scan",
    )(*parts, s0, tab)


def _rwkv_core(parts, wkv, tab):
    b, L, _ = parts[0].shape
    hh = RWKV_HEADS // 2
    s0 = jnp.transpose(wkv.reshape(b, hh, 2, RWKV_HEAD, RWKV_HEAD), (3, 4, 0, 2, 1))
    s0 = s0.reshape(RWKV_HEAD, RWKV_HEAD, b * RWKV_HEADS)
    if b % N_SEQ_TILE == 0 and L % RWKV_TB == 0:
        y, s_fin = rwkv_scan(parts, s0, tab, natural=True)
        y = y.reshape(b * L, D_MODEL)
    else:
        def to_chain(t):
            t = jnp.transpose(t.reshape(b, L, hh, 2, RWKV_HEAD), (1, 4, 0, 3, 2))
            return t.reshape(L, RWKV_HEAD, b * RWKV_HEADS)

        y, s_fin = rwkv_scan([to_chain(t) for t in parts], s0, tab, natural=False)
        y = jnp.transpose(y.reshape(L, RWKV_HEAD, b, 2, hh), (2, 0, 4, 3, 1)).reshape(b * L, D_MODEL)
    s_fin = jnp.transpose(s_fin.reshape(RWKV_HEAD, RWKV_HEAD, b, 2, hh), (2, 4, 3, 0, 1))
    return y, s_fin.reshape(b, RWKV_HEADS, RWKV_HEAD, RWKV_HEAD)


def kernel(x_prompt, x_sample, state_ssm, state_conv, cache_swa_k, cache_swa_v, state_wkv, state_shift, rel_bias, norm_mix, norm_ffn, norm_final, mix_w_in, ssd_conv_w, ssd_conv_b, ssd_dt_bias, ssd_a_log, ssd_d_skip, ssd_gnorm, attn_sinks, mix_w_out, rwkv_mu, rwkv_w0, rwkv_w1, rwkv_w2, rwkv_a0, rwkv_a1, rwkv_a2, rwkv_g1, rwkv_g2, rwkv_k_k, rwkv_k_a, rwkv_r_k, rwkv_w_rkv, rwkv_w_o, rwkv_ln_w, rwkv_ln_b, peer_w_q, peer_sub_keys, peer_u, peer_v):
    bp, lp, d = x_prompt.shape
    bs, ls, _ = x_sample.shape
    tp = bp * lp
    ts = bs * ls
    x = jnp.concatenate([x_prompt.reshape(tp, d), x_sample.reshape(ts, d)], axis=0)

    def split(t):
        return t[:tp].reshape(bp, lp, -1), t[tp:].reshape(bs, ls, -1)

    def zero_state(a):
        return jnp.zeros((bp,) + a.shape[2:], a.dtype)

    def peer(x, ht, layer, g_next):
        wqt = peer_w_q[layer].T.astype(bf16)
        return peer_ffn_residual(x, ht, wqt, peer_sub_keys[layer].astype(bf16),
                                 peer_u[layer].astype(bf16), peer_v[layer].astype(bf16), g_next)

    z, xbc, q, kv, dt = in_proj(x, norm_mix[0], mix_w_in[0])
    pad16 = (0, LANE - SSM_HEADS)
    selh = (jnp.arange(LANE)[:, None] == jnp.arange(SSM_INNER)[None, :] // SSM_HEAD_DIM).astype(f32)
    ssd_params = (ssd_conv_w[0], ssd_conv_b[0].reshape(1, CONV_DIM),
                  jnp.pad(ssd_dt_bias[0], pad16).reshape(1, LANE),
                  jnp.pad(-jnp.exp(ssd_a_log[0]), pad16).reshape(1, LANE),
                  jnp.repeat(ssd_d_skip[0], SSM_HEAD_DIM).reshape(1, SSM_INNER),
                  ssd_gnorm[0].reshape(1, SSM_INNER), selh)
    y_ssd_p, ssm_p, conv_p = ssd_mixer(z, xbc, dt, zero_state(state_conv), zero_state(state_ssm), ssd_params,
                                       batch=bp, seq=lp, row0=0)
    y_ssd_s, ssm_s, conv_s = ssd_mixer(z, xbc, dt, state_conv[0], state_ssm[0], ssd_params,
                                       batch=bs, seq=ls, row0=tp)
    o_att_p, k_p, v_p = swa_mixer(q, kv, None, None, rel_bias, attn_sinks[0], batch=bp, seq=lp, row0=0)
    o_att_s, k_s, v_s = swa_mixer(q, kv, cache_swa_k[0], cache_swa_v[0], rel_bias, attn_sinks[0],
                                  batch=bs, seq=ls, row0=tp)
    x, ht = out_proj(y_ssd_p, o_att_p, y_ssd_s, o_att_s, mix_w_out[0].astype(bf16), x, norm_ffn[0])
    x, h = peer(x, ht, 0, norm_mix[1])

    h_p, h_s = split(h)
    prev_p = jnp.concatenate([jnp.zeros((bp, 1, d), f32), h_p[:, :-1]], axis=1)
    prev_s = jnp.concatenate([state_shift[0][:, None], h_s[:, :-1]], axis=1)
    def lora_pair(w1, w2):
        pad = LORA_PAD - w1.shape[1]
        return jnp.pad(w1, ((0, 0), (0, pad))), jnp.pad(w2, ((0, pad), (0, 0)))

    pairs = [lora_pair(rwkv_w1[0], rwkv_w2[0]), lora_pair(rwkv_a1[0], rwkv_a2[0]), lora_pair(rwkv_g1[0], rwkv_g2[0])]
    lora1 = jnp.stack([p[0] for p in pairs]).astype(bf16)
    lora2 = jnp.stack([p[1] for p in pairs]).astype(bf16)
    vecs = jnp.stack([rwkv_w0[0], rwkv_a0[0]])
    pre_args = (rwkv_mu[0], rwkv_w_rkv[0].astype(bf16), lora1, lora2, vecs)
    *parts_p, g_p = rwkv_pre(h, prev_p.reshape(tp, d), *pre_args, row0=0)
    *parts_s, g_s = rwkv_pre(h, prev_s.reshape(ts, d), *pre_args, row0=tp)

    def chain_table(p):
        t = jnp.transpose(p.reshape(RWKV_HEADS // 2, 2, RWKV_HEAD), (2, 1, 0)).reshape(RWKV_HEAD, RWKV_HEADS)
        return jnp.tile(t, (1, LANE // RWKV_HEADS))

    tab = jnp.stack([chain_table(p) for p in (rwkv_r_k[0].reshape(-1), rwkv_ln_w[0], rwkv_ln_b[0],
                                              rwkv_k_k[0], rwkv_k_a[0])])
    y_p, wkv_p = _rwkv_core([t.reshape(bp, lp, d) for t in parts_p], zero_state(state_wkv), tab)
    y_s, wkv_s = _rwkv_core([t.reshape(bs, ls, d) for t in parts_s], state_wkv[0], tab)
    x, ht = gated_out_proj(y_p, g_p, y_s, g_s, rwkv_w_o[0].astype(bf16), x, norm_ffn[1])
    shift_p, shift_s = h_p[:, -1], h_s[:, -1]
    _, y = peer(x, ht, 1, norm_final)
    y_p, y_s = split(y)
    return (y_p, y_s, ssm_p[None], conv_p[None], k_p[None], v_p[None], wkv_p[None], shift_p[None],
            ssm_s[None], conv_s[None], k_s[None], v_s[None], wkv_s[None], shift_s[None])
```

```python
import functools
import math

import jax
import jax.numpy as jnp
from jax import lax
from jax.experimental import pallas as pl
from jax.experimental.pallas import tpu as pltpu

f32 = jnp.float32
bf16 = jnp.bfloat16

D_MODEL = 1024
PAST_LEN = 16384
SSM_HEAD_DIM = 64
SSM_HEADS = 16
SSM_INNER = 1024
SSM_GROUPS = 2
SSM_STATE = 128
CONV_W = 4
CONV_DIM = 1536
SSD_CHUNK = 128
ATT_HEAD_DIM = 64
ATT_HEADS = 16
ATT_KV_HEADS = 4
ATT_GQA = 4
WINDOW = 128
ATT_BLOCK = 128
REL_BUCKETS = 32
REL_MAX_DIST = 128
RWKV_HEAD = 64
RWKV_HEADS = 16
PEER_HEADS = 8
N_KEYS = 128
N_EXPERTS = N_KEYS * N_KEYS
PEER_TOPK = 16
PEER_HALF = 128
EPS = 1e-5
GN_EPS = 64e-5

LANE = 128
SUBLANE = 8
HALF_LANE = LANE // 2
VMEM_LIMIT = 56 * 2 ** 20
HIGHEST = lax.Precision.HIGHEST
NT_DIMS = (((1,), (1,)), ((), ()))
TN_DIMS = (((0,), (0,)), ((), ()))

_CAND_COUNTS = [PEER_TOPK // (a + 1) for a in range(PEER_TOPK)]
_N_CAND = sum(_CAND_COUNTS)
_N_CAND_PAD = -(-_N_CAND // 8) * 8
_BIG = 1e9


def _cparams(*sem):
    return pltpu.CompilerParams(dimension_semantics=sem, vmem_limit_bytes=VMEM_LIMIT)


def _rms(x, g):
    return x * lax.rsqrt(jnp.mean(x * x, axis=-1, keepdims=True) + EPS) * g


def _silu(x):
    return x * jax.nn.sigmoid(x)


BF16_ROWS = 2 * SUBLANE


def _dup_bf16(v):
    u = pltpu.bitcast(v.astype(bf16).astype(f32), jnp.int32)
    return u | lax.shift_right_logical(u, jnp.int32(16))


def _row_as_bf16_tile(ref, h, i):
    row = ref[h, pl.ds(i, 1), :]
    tile = pltpu.bitcast(jnp.broadcast_to(row, (SUBLANE, row.shape[1])), bf16)
    return jnp.concatenate([tile] * (N_KEYS // BF16_ROWS), axis=0)


def _peer_select_kernel(ht_ref, wqt_ref, sk_ref, pos_ref, b0_ref, r1_ref, e0_ref, e1_ref,
                        qt_ref, s_ref, vals_ref, idx_ref, cs_ref):
    tm = ht_ref.shape[1]
    qt_ref[...] = jnp.dot(wqt_ref[...], ht_ref[...], preferred_element_type=f32).astype(bf16)
    row = lax.broadcasted_iota(jnp.int32, (N_KEYS, tm), 0).astype(f32)
    pos = pos_ref[...]
    neg_inf = f32(-jnp.inf)

    def head_compute(h, exact):
        bad = jnp.zeros((1, tm), f32)
        for c in (0, 1):
            off = pl.multiple_of((h * 2 + c) * PEER_HALF, PEER_HALF)
            s_ref[c] = jnp.dot(sk_ref[h, c], qt_ref[pl.ds(off, PEER_HALF), :], preferred_element_type=f32)

        bad_tiles = []
        for lt in range(tm // LANE):
            ls = slice(lt * LANE, (lt + 1) * LANE)
            row_t = row[:, :LANE]

            def remove_max(k, s, c, lt=lt, row_t=row_t):
                m = jnp.max(s, axis=0, keepdims=True)
                vals_ref[c, lt, pl.ds(k, 1), :] = m
                if exact:
                    idx = jnp.min(jnp.where(s == m, row_t, f32(N_KEYS)), axis=0, keepdims=True)
                    idx_ref[c, lt, pl.ds(k, 1), :] = idx
                    hit = row_t == idx
                else:
                    hit = s == m
                return jnp.where(hit, neg_inf, s), hit

            def extract(k, st, remove_max=remove_max):
                sa, sb, rank = st
                sa, _ = remove_max(k, sa, 0)
                sb, hit = remove_max(k, sb, 1)
                return sa, sb, jnp.where(hit, jnp.asarray(k, f32), rank)

            sa_fin, sb_fin, rank = lax.fori_loop(
                0, PEER_TOPK, extract, (s_ref[0, :, ls], s_ref[1, :, ls], jnp.full((N_KEYS, LANE), f32(PEER_TOPK))))
            r1_ref[h, :, ls] = rank.astype(bf16)
            if not exact:
                bad_t = jnp.zeros((1, LANE), f32)
                for s_fin in (sa_fin, sb_fin):
                    removed = jnp.sum(jnp.where(s_fin == neg_inf, f32(1.0), f32(0.0)), axis=0, keepdims=True)
                    bad_t = bad_t + jnp.abs(removed - f32(PEER_TOPK))
                bad_tiles.append(bad_t)
        if not exact:
            bad = bad + jnp.concatenate(bad_tiles, axis=1)

        def lane_tiles(ref, c):
            return jnp.concatenate([ref[c, lt] for lt in range(tm // LANE)], axis=1)

        v0 = lane_tiles(vals_ref, 0)
        v1 = lane_tiles(vals_ref, 1)
        r = 0
        for a in range(PEER_TOPK):
            nb = _CAND_COUNTS[a]
            cs_ref[r:r + nb, :] = v0[a:a + 1, :] + v1[0:nb, :]
            r += nb
        cs_ref[_N_CAND:_N_CAND_PAD, :] = jnp.full((_N_CAND_PAD - _N_CAND, tm), neg_inf)
        m1 = v0[0:1, :] + v1[0:1, :]

        cs = cs_ref[...]
        if exact:
            def extract2(k, st):
                cs, sel, z = st
                m = jnp.max(cs, axis=0, keepdims=True)
                p = jnp.min(jnp.where(cs == m, pos, f32(_BIG)), axis=0, keepdims=True)
                hit = pos == p
                return jnp.where(hit, neg_inf, cs), jnp.where(hit, f32(1.0), sel), z + jnp.exp(m - m1)

            _, sel, z = lax.fori_loop(0, PEER_TOPK, extract2,
                                      (cs, jnp.zeros((_N_CAND_PAD, tm), f32), jnp.zeros((1, tm), f32)))
        else:
            larger = jnp.zeros((_N_CAND_PAD, tm), f32)
            for r in range(_N_CAND):
                larger = larger + jnp.where(cs[r:r + 1, :] > cs, f32(1.0), f32(0.0))
            sel = jnp.where((larger < f32(PEER_TOPK)) & (cs > neg_inf), f32(1.0), f32(0.0))
            z = jnp.sum(sel * jnp.exp(cs - m1), axis=0, keepdims=True)
            bad = bad + jnp.abs(jnp.sum(sel, axis=0, keepdims=True) - f32(PEER_TOPK))
        s0 = s_ref[0]
        idx0 = lane_tiles(idx_ref, 0)
        bound0 = jnp.zeros((N_KEYS, tm), f32)
        r = 0
        for a in range(PEER_TOPK):
            nb = _CAND_COUNTS[a]
            cnt = jnp.sum(sel[r:r + nb, :], axis=0, keepdims=True)
            is_a = (row == idx0[a:a + 1, :]) if exact else (s0 == v0[a:a + 1, :])
            bound0 = jnp.where(is_a, cnt, bound0)
            r += nb
        b0_ref[h] = _dup_bf16(bound0)
        e0_ref[h] = _dup_bf16(jnp.exp(s0 - v0[0:1, :]) / z)
        e1_ref[h] = jnp.exp(s_ref[1] - v1[0:1, :]).astype(bf16)
        return bad

    def head_body(h, carry):
        bad = head_compute(h, False)

        @pl.when(jnp.max(bad) > 0.0)
        def _():
            head_compute(h, True)

        return carry

    lax.fori_loop(0, PEER_HEADS, head_body, 0)


def _peer_main_kernel(ht_ref, u_ref, v_ref, b0_ref, r1_ref, e0_ref, e1_ref, x_ref, gn_ref, o_ref, hn_ref,
                      acc_ref, *, nblk, npieces):
    j = pl.program_id(1)

    @pl.when(j == 0)
    def _():
        acc_ref[...] = jnp.zeros_like(acc_ref)

    ht = ht_ref[...]
    bpp = nblk // npieces
    pc = bpp * N_KEYS
    for p in range(npieces):
        at = jnp.dot(u_ref[p * pc:(p + 1) * pc, :], ht, preferred_element_type=f32)
        ws = []
        for b in range(bpp):
            i = j * nblk + p * bpp + b
            a = at[b * N_KEYS:(b + 1) * N_KEYS, :]
            g = jnp.zeros(a.shape, bf16)
            for h in range(PEER_HEADS):
                bnd = _row_as_bf16_tile(b0_ref, h, i)
                e0 = _row_as_bf16_tile(e0_ref, h, i)
                g = g + jnp.where(r1_ref[h] < bnd, e1_ref[h] * e0, bf16(0.0))
            gelu = 0.5 * a * (1.0 + lax.erf(a * f32(math.sqrt(0.5))))
            ws.append(g * gelu.astype(bf16))
        wt = jnp.concatenate(ws, axis=0) if bpp > 1 else ws[0]
        acc_ref[...] += lax.dot_general(wt, v_ref[p * pc:(p + 1) * pc, :], TN_DIMS,
                                        preferred_element_type=f32)

    @pl.when(j == pl.num_programs(1) - 1)
    def _():
        xn = x_ref[...] + acc_ref[...]
        o_ref[...] = xn
        hn_ref[...] = _rms(xn, gn_ref[...])


def peer_ffn_residual(x, ht, wqt, sk, u, v, g_next, tm_sel=256, tm=256, ce=4096, npieces=16):
    t = x.shape[0]
    d = D_MODEL
    pos_list = []
    for a in range(PEER_TOPK):
        pos_list += [a * PEER_TOPK + b for b in range(_CAND_COUNTS[a])]
    pos_list += [_BIG] * (_N_CAND_PAD - _N_CAND)
    pos = jnp.broadcast_to(jnp.asarray(pos_list, f32)[:, None], (_N_CAND_PAD, tm_sel))
    sel_shape = jax.ShapeDtypeStruct((PEER_HEADS, N_KEYS, t), jnp.int32)
    sel_shape16 = jax.ShapeDtypeStruct((PEER_HEADS, N_KEYS, t), bf16)
    sel_spec = pl.BlockSpec((PEER_HEADS, N_KEYS, tm_sel), lambda i: (0, 0, i))
    b0, r1, e0, e1 = pl.pallas_call(
        _peer_select_kernel,
        grid=(t // tm_sel,),
        in_specs=[pl.BlockSpec((d, tm_sel), lambda i: (0, i)),
                  pl.BlockSpec((PEER_HEADS * 2 * PEER_HALF, d), lambda i: (0, 0)),
                  pl.BlockSpec((PEER_HEADS, 2, N_KEYS, PEER_HALF), lambda i: (0, 0, 0, 0)),
                  pl.BlockSpec((_N_CAND_PAD, tm_sel), lambda i: (0, 0))],
        out_specs=[sel_spec] * 4,
        out_shape=[sel_shape, sel_shape16, sel_shape, sel_shape16],
        scratch_shapes=[pltpu.VMEM((PEER_HEADS * 2 * PEER_HALF, tm_sel), bf16),
                        pltpu.VMEM((2, N_KEYS, tm_sel), f32),
                        pltpu.VMEM((2, tm_sel // LANE, PEER_TOPK, LANE), f32),
                        pltpu.VMEM((2, tm_sel // LANE, PEER_TOPK, LANE), f32),
                        pltpu.VMEM((_N_CAND_PAD, tm_sel), f32)],
        compiler_params=_cparams("parallel"),
        name="peer_select",
    )(ht, wqt, sk, pos)

    nblk = ce // N_KEYS
    sel_spec2 = pl.BlockSpec((PEER_HEADS, N_KEYS, tm), lambda i, j: (0, 0, i))
    return pl.pallas_call(
        functools.partial(_peer_main_kernel, nblk=nblk, npieces=npieces),
        grid=(t // tm, N_EXPERTS // ce),
        in_specs=[pl.BlockSpec((d, tm), lambda i, j: (0, i)),
                  pl.BlockSpec((ce, d), lambda i, j: (j, 0)),
                  pl.BlockSpec((ce, d), lambda i, j: (j, 0)),
                  sel_spec2, sel_spec2, sel_spec2, sel_spec2,
                  pl.BlockSpec((tm, d), lambda i, j: (i, 0)), pl.BlockSpec((1, d), lambda i, j: (0, 0))],
        out_specs=[pl.BlockSpec((tm, d), lambda i, j: (i, 0))] * 2,
        out_shape=[jax.ShapeDtypeStruct((t, d), f32)] * 2,
        scratch_shapes=[pltpu.VMEM((tm, d), f32)],
        compiler_params=_cparams("parallel", "arbitrary"),
        name="peer_main",
    )(ht, u, v, b0, r1, e0, e1, x, g_next.reshape(1, d))


IN_WIDTHS = (SSM_INNER, CONV_DIM, ATT_HEADS * ATT_HEAD_DIM, 2 * ATT_KV_HEADS * ATT_HEAD_DIM, LANE)


def _in_proj_kernel(x_ref, g_ref, w_ref, *out_refs):
    h = _rms(x_ref[...], g_ref[...])
    y = jnp.dot(h.astype(bf16), w_ref[...], preferred_element_type=f32)
    o = 0
    for ref in out_refs:
        w = ref.shape[1]
        ref[...] = y[:, o:o + w]
        o += w


def in_proj(x, g, w_in, tm=256):
    t, d = x.shape
    o1 = SSM_INNER
    o2 = o1 + CONV_DIM
    o3 = o2 + SSM_HEADS
    w = jnp.concatenate([w_in[:, :o2], w_in[:, o3:], w_in[:, o2:o3],
                         jnp.zeros((d, LANE - SSM_HEADS), w_in.dtype)], axis=1).astype(bf16)
    n = w.shape[1]
    return pl.pallas_call(
        _in_proj_kernel,
        grid=(t // tm,),
        in_specs=[pl.BlockSpec((tm, d), lambda i: (i, 0)), pl.BlockSpec((1, d), lambda i: (0, 0)),
                  pl.BlockSpec((d, n), lambda i: (0, 0))],
        out_specs=[pl.BlockSpec((tm, wd), lambda i: (i, 0)) for wd in IN_WIDTHS],
        out_shape=[jax.ShapeDtypeStruct((t, wd), f32) for wd in IN_WIDTHS],
        compiler_params=_cparams("parallel"),
        name="in_proj",
    )(x, g.reshape(1, d), w)


N_PAIRS = SSM_HEADS // 2
SSD_SHORT_CHUNK = 32


def _ssd_kernel(*refs, lin, lc):
    (z_ref, xbc_ref, dt_ref, cs_ref, h0_ref, cw_ref, cb_ref, dtb_ref, aneg_ref, dsk_ref, gn_ref, selh_ref) = refs[:12]
    refs = refs[12:]
    y_ref, hfin_ref, tail_ref, xe_ref, hp_ref, ys_ref = refs[:6]
    c = pl.program_id(1)

    @pl.when(c == 0)
    def _():
        xe_ref[0:SUBLANE, :] = cs_ref[0]
        hp_ref[...] = h0_ref[0]

    if lin == lc:
        xe_ref[SUBLANE:SUBLANE + lc, :] = xbc_ref[...]
        z = z_ref[...]
        dt_raw = dt_ref[...]
    else:
        zpad_ref, dtpad_ref = refs[6:8]
        xe_ref[SUBLANE:SUBLANE + lc, :] = jnp.zeros((lc, CONV_DIM), f32)
        xe_ref[SUBLANE:SUBLANE + lin, :] = xbc_ref[...]
        zpad_ref[...] = jnp.zeros_like(zpad_ref)
        zpad_ref[0:lin, :] = z_ref[...]
        dtpad_ref[...] = jnp.zeros_like(dtpad_ref)
        dtpad_ref[0:lin, :] = dt_ref[...]
        z = zpad_ref[...]
        dt_raw = dtpad_ref[...]

    conv = cb_ref[...]
    for j in range(CONV_W):
        o = SUBLANE - (CONV_W - 1) + j
        conv = conv + xe_ref[o:o + lc, :] * cw_ref[j:j + 1, :]
    if lin == lc:
        xe_ref[0:SUBLANE, :] = xe_ref[lc:lc + SUBLANE, :]
    xc = _silu(conv)
    xs = xc[:, :SSM_INNER]
    bm = xc[:, SSM_INNER:SSM_INNER + SSM_GROUPS * SSM_STATE]
    cm = xc[:, SSM_INNER + SSM_GROUPS * SSM_STATE:]

    row = lax.broadcasted_iota(jnp.int32, (lc, LANE), 0)
    col = lax.broadcasted_iota(jnp.int32, (lc, LANE), 1)
    causal = lax.broadcasted_iota(jnp.int32, (lc, lc), 0) >= lax.broadcasted_iota(jnp.int32, (lc, lc), 1)
    pair_row = lax.broadcasted_iota(jnp.int32, (LANE, SSM_STATE), 0)
    lane_lo = col < HALF_LANE
    neg_inf = f32(-jnp.inf)

    x = dt_raw + dtb_ref[...]
    dt = jnp.maximum(x, 0.0) + jnp.log(1.0 + jnp.exp(-jnp.abs(x)))
    if lin != lc:
        dt = jnp.where(row < lin, dt, 0.0)
    la = dt * aneg_ref[...]
    acs = jnp.dot(causal.astype(f32), la, precision=HIGHEST, preferred_element_type=f32)
    acs_sq = acs if lc == LANE else jnp.concatenate([acs, jnp.zeros((LANE - lc, LANE), f32)], axis=0)
    acs_t = acs_sq.T[:, :lc]
    selh = selh_ref[...]
    dt_exp = jnp.dot(dt, selh, precision=HIGHEST, preferred_element_type=f32)
    acs_exp = jnp.dot(acs, selh, precision=HIGHEST, preferred_element_type=f32)
    alast = acs[lc - 1:lc, :]
    xdt = xs * dt_exp
    eacs = jnp.exp(acs_exp)
    xdt_end = (xdt * jnp.exp(acs_exp[lc - 1:lc, :] - acs_exp)).astype(bf16)
    dsk = dsk_ref[...]

    for g in range(SSM_GROUPS):
        cmg = cm[:, g * SSM_STATE:(g + 1) * SSM_STATE].astype(bf16)
        bmg = bm[:, g * SSM_STATE:(g + 1) * SSM_STATE].astype(bf16)
        cb = lax.dot_general(cmg, bmg, NT_DIMS, preferred_element_type=f32)
        for jj in range(N_PAIRS // SSM_GROUPS):
            j = g * (N_PAIRS // SSM_GROUPS) + jj
            sl = slice(j * LANE, (j + 1) * LANE)
            xdt_pair = xdt[:, sl]
            ydiag = jnp.zeros((lc, LANE), f32)
            for half in (0, 1):
                h = 2 * j + half
                seg = acs[:, h:h + 1] - acs_t[h:h + 1, :]
                dec = jnp.exp(jnp.where(causal, seg, neg_inf))
                m = (cb * dec).astype(bf16)
                keep = lane_lo if half == 0 else jnp.logical_not(lane_lo)
                xd = jnp.where(keep, xdt_pair, 0.0).astype(bf16)
                ydiag = ydiag + jnp.dot(m, xd, preferred_element_type=f32)
            hpj = hp_ref[j]
            yoff = lax.dot_general(cmg, hpj.astype(bf16), NT_DIMS, preferred_element_type=f32) * eacs[:, sl]
            s_new = lax.dot_general(xdt_end[:, sl], bmg, TN_DIMS, preferred_element_type=f32)
            dl = jnp.where(pair_row < SSM_HEAD_DIM, alast[:, 2 * j:2 * j + 1], alast[:, 2 * j + 1:2 * j + 2])
            hp_ref[j] = hpj * jnp.exp(dl) + s_new
            ys_ref[:, sl] = ydiag + yoff + dsk[:, sl] * xs[:, sl]

    y = ys_ref[...] * _silu(z)
    gs = SSM_INNER // SSM_GROUPS
    gn = gn_ref[...]
    for g in range(SSM_GROUPS):
        yg = y[:, g * gs:(g + 1) * gs]
        yg = yg * lax.rsqrt(jnp.mean(yg * yg, axis=-1, keepdims=True) + EPS) * gn[:, g * gs:(g + 1) * gs]
        y_ref[:, g * gs:(g + 1) * gs] = yg[0:lin, :]

    @pl.when(c == pl.num_programs(1) - 1)
    def _():
        hfin_ref[0] = hp_ref[...]
        tail_ref[0] = xe_ref[lin:lin + SUBLANE, :]


def ssd_mixer(z, xbc, dt, conv_state, ssm_state, params, *, batch, seq, row0):
    conv_w, conv_b, dt_bias, a_neg, d_skip, gnorm, selh = params
    lin = min(seq, SSD_CHUNK)
    lc = SSD_CHUNK if lin == SSD_CHUNK else SSD_SHORT_CHUNK
    nc = seq // lin
    blk0 = row0 // lin
    cs = jnp.pad(conv_state, ((0, 0), (SUBLANE - (CONV_W - 1), 0), (0, 0)))
    h0 = ssm_state.reshape(batch, N_PAIRS, LANE, SSM_STATE)

    def rows(w):
        return pl.BlockSpec((lin, w), lambda b, c: (blk0 + b * nc + c, 0))

    def const(shape):
        return pl.BlockSpec(shape, lambda b, c: (0,) * len(shape))

    in_specs = [rows(SSM_INNER), rows(CONV_DIM), rows(LANE),
                pl.BlockSpec((1, SUBLANE, CONV_DIM), lambda b, c: (b, 0, 0)),
                pl.BlockSpec((1, N_PAIRS, LANE, SSM_STATE), lambda b, c: (b, 0, 0, 0)),
                const((CONV_W, CONV_DIM)), const((1, CONV_DIM)), const((1, LANE)), const((1, LANE)),
                const((1, SSM_INNER)), const((1, SSM_INNER)), const((LANE, SSM_INNER))]
    args = [z, xbc, dt, cs, h0, conv_w, conv_b, dt_bias, a_neg, d_skip, gnorm, selh]
    scratch = [pltpu.VMEM((lc + 2 * SUBLANE, CONV_DIM), f32), pltpu.VMEM((N_PAIRS, LANE, SSM_STATE), f32),
               pltpu.VMEM((lc, SSM_INNER), f32)]
    if lin != lc:
        scratch += [pltpu.VMEM((lc, SSM_INNER), f32), pltpu.VMEM((lc, LANE), f32)]
    y, h_fin, tail = pl.pallas_call(
        functools.partial(_ssd_kernel, lin=lin, lc=lc),
        grid=(batch, nc),
        in_specs=in_specs,
        out_specs=[pl.BlockSpec((lin, SSM_INNER), lambda b, c: (b * nc + c, 0)),
                   pl.BlockSpec((1, N_PAIRS, LANE, SSM_STATE), lambda b, c: (b, 0, 0, 0)),
                   pl.BlockSpec((1, SUBLANE, CONV_DIM), lambda b, c: (b, 0, 0))],
        out_shape=[jax.ShapeDtypeStruct((batch * seq, SSM_INNER), f32),
                   jax.ShapeDtypeStruct((batch, N_PAIRS, LANE, SSM_STATE), f32),
                   jax.ShapeDtypeStruct((batch, SUBLANE, CONV_DIM), f32)],
        scratch_shapes=scratch,
        compiler_params=_cparams("parallel", "arbitrary"),
        name="ssd_mixer",
    )(*args)
    new_conv = tail[:, SUBLANE - (CONV_W - 1):]
    return y, h_fin.reshape(batch, SSM_HEADS, SSM_HEAD_DIM, SSM_STATE), new_conv


def _swa_kernel(*refs, lq, masked_first):
    q_ref, kp_ref, vp_ref, kc_ref, vc_ref, bias_ref, sink_ref = refs[:7]
    refs = refs[7:]
    o_ref, nk_ref, nv_ref = refs[:3]
    n = pl.program_id(1)
    q = q_ref[...]
    for new_ref, prev_ref, cur_ref in ((nk_ref, kp_ref, kc_ref), (nv_ref, vp_ref, vc_ref)):
        if lq < WINDOW:
            new_ref[0:WINDOW - lq, :] = prev_ref[lq:WINDOW, :]
        new_ref[WINDOW - lq:WINDOW, :] = cur_ref[...]
    if lq == ATT_BLOCK:
        kc = kc_ref[...]
        vc = vc_ref[...]
    else:
        kpad_ref, vpad_ref = refs[3:5]
        kpad_ref[...] = jnp.zeros_like(kpad_ref)
        vpad_ref[...] = jnp.zeros_like(vpad_ref)
        kpad_ref[0:lq, :] = kc_ref[...]
        vpad_ref[0:lq, :] = vc_ref[...]
        kc = kpad_ref[...]
        vc = vpad_ref[...]
    kp = kp_ref[...]
    vp = vp_ref[...]
    lane_lo = lax.broadcasted_iota(jnp.int32, (ATT_BLOCK, LANE), 1) < HALF_LANE
    lane_lo_q = lax.broadcasted_iota(jnp.int32, (lq, LANE), 1) < HALF_LANE
    neg_inf = f32(-jnp.inf)
    scale = f32(ATT_HEAD_DIM ** -0.5)

    for g in range(ATT_KV_HEADS):
        sl = slice((g // 2) * LANE, (g // 2 + 1) * LANE)
        odd = g % 2 == 1

        def kpad(k):
            pair = k[:, sl]
            if odd:
                pair = pltpu.roll(pair, HALF_LANE, 1)
            return jnp.where(lane_lo, pair, 0.0).astype(bf16)

        def vdup(v):
            pair = v[:, sl]
            rolled = pltpu.roll(pair, HALF_LANE, 1)
            return (jnp.where(lane_lo, rolled, pair) if odd else jnp.where(lane_lo, pair, rolled)).astype(bf16)

        qp0 = q[:, (2 * g) * LANE:(2 * g + 1) * LANE]
        qp1 = q[:, (2 * g + 1) * LANE:(2 * g + 2) * LANE]
        qg = jnp.concatenate([qp0, pltpu.roll(qp0, HALF_LANE, 1), qp1, pltpu.roll(qp1, HALF_LANE, 1)],
                             axis=0).astype(bf16)
        bias = bias_ref[g]
        sp = lax.dot_general(qg, kpad(kp), NT_DIMS, preferred_element_type=f32) * scale + bias[:, :WINDOW]
        sc = lax.dot_general(qg, kpad(kc), NT_DIMS, preferred_element_type=f32) * scale + bias[:, WINDOW:]
        if masked_first:
            sp = jnp.where(n > 0, sp, neg_inf)
        sink = sink_ref[g][:, 0:1]
        m = jnp.maximum(jnp.max(jnp.maximum(sp, sc), axis=-1, keepdims=True), sink)
        pp = jnp.exp(sp - m)
        pc = jnp.exp(sc - m)
        denom = jnp.sum(pp + pc, axis=-1, keepdims=True) + jnp.exp(sink - m)
        og = (jnp.dot(pp.astype(bf16), vdup(vp), preferred_element_type=f32)
              + jnp.dot(pc.astype(bf16), vdup(vc), preferred_element_type=f32)) / denom
        o_ref[:, (2 * g) * LANE:(2 * g + 1) * LANE] = jnp.where(lane_lo_q, og[0:lq], og[lq:2 * lq])
        o_ref[:, (2 * g + 1) * LANE:(2 * g + 2) * LANE] = jnp.where(lane_lo_q, og[2 * lq:3 * lq], og[3 * lq:4 * lq])


def _rel_bucket(dist):
    exact = REL_BUCKETS // 2
    d = jnp.maximum(dist, 0)
    large = exact + (jnp.log(jnp.maximum(d, 1).astype(f32) / exact)
                     / math.log(REL_MAX_DIST / exact) * (REL_BUCKETS - exact)).astype(jnp.int32)
    large = jnp.minimum(large, REL_BUCKETS - 1)
    return jnp.where(d < exact, d, large)


def _swa_tables(rel_bias, sinks, lq):
    qi = jnp.arange(lq)[:, None]
    kj = jnp.arange(2 * ATT_BLOCK)[None, :]
    dist = qi + WINDOW - kj
    band = (dist >= 0) & (dist <= WINDOW)
    onehot = (_rel_bucket(dist)[..., None] == jnp.arange(REL_BUCKETS)).astype(f32)
    bias = jnp.einsum('qkb,bh->qkh', onehot, rel_bias, precision=HIGHEST)
    bias = jnp.where(band[..., None], bias, -jnp.inf)
    bias = jnp.transpose(bias, (2, 0, 1)).reshape(ATT_KV_HEADS, ATT_GQA * lq, 2 * ATT_BLOCK)
    sink = jnp.broadcast_to(sinks.reshape(ATT_KV_HEADS, ATT_GQA, 1, 1), (ATT_KV_HEADS, ATT_GQA, lq, LANE))
    return bias, sink.reshape(ATT_KV_HEADS, ATT_GQA * lq, LANE)


def swa_mixer(q, kv, k_prev, v_prev, rel_bias, sinks, *, batch, seq, row0):
    lq = min(seq, ATT_BLOCK)
    nb = seq // lq
    blk0 = row0 // lq
    kvw = ATT_KV_HEADS * ATT_HEAD_DIM
    bias, sink = _swa_tables(rel_bias, sinks, lq)
    cur_k = pl.BlockSpec((lq, kvw), lambda b, n: (blk0 + b * nb + n, 0))
    cur_v = pl.BlockSpec((lq, kvw), lambda b, n: (blk0 + b * nb + n, 1))
    if k_prev is None:
        prev_k = pl.BlockSpec((WINDOW, kvw), lambda b, n: (blk0 + b * nb + jnp.maximum(n - 1, 0), 0))
        prev_v = pl.BlockSpec((WINDOW, kvw), lambda b, n: (blk0 + b * nb + jnp.maximum(n - 1, 0), 1))
        kp_arr, vp_arr = kv, kv
    else:
        prev_k = pl.BlockSpec((WINDOW, kvw), lambda b, n: (b, 0))
        prev_v = prev_k
        kp_arr = k_prev.reshape(batch * WINDOW, kvw)
        vp_arr = v_prev.reshape(batch * WINDOW, kvw)
    rows = pl.BlockSpec((lq, ATT_HEADS * ATT_HEAD_DIM), lambda b, n: (blk0 + b * nb + n, 0))
    in_specs = [rows, prev_k, prev_v, cur_k, cur_v,
                pl.BlockSpec(bias.shape, lambda b, n: (0, 0, 0)), pl.BlockSpec(sink.shape, lambda b, n: (0, 0, 0))]
    args = [q, kp_arr, vp_arr, kv, kv, bias, sink]
    scratch = [] if lq == ATT_BLOCK else [pltpu.VMEM((ATT_BLOCK, kvw), f32), pltpu.VMEM((ATT_BLOCK, kvw), f32)]
    cache_spec = pl.BlockSpec((WINDOW, kvw), lambda b, n: (b, 0))
    cache_shape = jax.ShapeDtypeStruct((batch * WINDOW, kvw), f32)
    o, new_k, new_v = pl.pallas_call(
        functools.partial(_swa_kernel, lq=lq, masked_first=k_prev is None),
        grid=(batch, nb),
        in_specs=in_specs,
        out_specs=[pl.BlockSpec((lq, ATT_HEADS * ATT_HEAD_DIM), lambda b, n: (b * nb + n, 0)), cache_spec, cache_spec],
        out_shape=[jax.ShapeDtypeStruct((batch * seq, ATT_HEADS * ATT_HEAD_DIM), f32), cache_shape, cache_shape],
        scratch_shapes=scratch,
        compiler_params=_cparams("parallel", "arbitrary"),
        name="swa_mixer",
    )(*args)
    cshape = (batch, WINDOW, ATT_KV_HEADS, ATT_HEAD_DIM)
    return o, new_k.reshape(cshape), new_v.reshape(cshape)


def _out_proj_kernel(yap_ref, ybp_ref, yas_ref, ybs_ref, w_ref, x_ref, g_ref, o_ref, ht_ref, *, n_prompt):
    ka = yap_ref.shape[1]

    def body(ya_ref, yb_ref):
        acc = jnp.dot(ya_ref[...].astype(bf16), w_ref[0:ka, :], preferred_element_type=f32)
        acc = acc + jnp.dot(yb_ref[...].astype(bf16), w_ref[ka:, :], preferred_element_type=f32)
        xn = x_ref[...] + acc
        o_ref[...] = xn
        ht_ref[...] = _rms(xn, g_ref[...]).T.astype(bf16)

    @pl.when(pl.program_id(0) < n_prompt)
    def _():
        body(yap_ref, ybp_ref)

    @pl.when(pl.program_id(0) >= n_prompt)
    def _():
        body(yas_ref, ybs_ref)


def out_proj(ya_p, yb_p, ya_s, yb_s, w, x, g, tm=512):
    t, d = x.shape
    ka, kb = ya_p.shape[1], yb_p.shape[1]
    n_p = ya_p.shape[0] // tm
    n_s = ya_s.shape[0] // tm

    def p_rows(k):
        return pl.BlockSpec((tm, k), lambda i: (jnp.minimum(i, n_p - 1), 0))

    def s_rows(k):
        return pl.BlockSpec((tm, k), lambda i: (jnp.maximum(i - n_p, 0), 0))

    return pl.pallas_call(
        functools.partial(_out_proj_kernel, n_prompt=n_p),
        grid=(n_p + n_s,),
        in_specs=[p_rows(ka), p_rows(kb), s_rows(ka), s_rows(kb),
                  pl.BlockSpec((ka + kb, d), lambda i: (0, 0)), pl.BlockSpec((tm, d), lambda i: (i, 0)),
                  pl.BlockSpec((1, d), lambda i: (0, 0))],
        out_specs=[pl.BlockSpec((tm, d), lambda i: (i, 0)), pl.BlockSpec((d, tm), lambda i: (0, i))],
        out_shape=[jax.ShapeDtypeStruct((t, d), f32), jax.ShapeDtypeStruct((d, t), bf16)],
        compiler_params=_cparams("arbitrary"),
        name="out_proj",
    )(ya_p, yb_p, ya_s, yb_s, w, x, g.reshape(1, d))


def _gated_out_kernel(yp_ref, gp_ref, ys_ref, gs_ref, w_ref, x_ref, gn_ref, o_ref, ht_ref, *, n_prompt):
    def body(y_ref, g_ref):
        a = (y_ref[...] * g_ref[...]).astype(bf16)
        xn = x_ref[...] + jnp.dot(a, w_ref[...], preferred_element_type=f32)
        o_ref[...] = xn
        ht_ref[...] = _rms(xn, gn_ref[...]).T.astype(bf16)

    @pl.when(pl.program_id(0) < n_prompt)
    def _():
        body(yp_ref, gp_ref)

    @pl.when(pl.program_id(0) >= n_prompt)
    def _():
        body(ys_ref, gs_ref)


def gated_out_proj(y_p, g_p, y_s, g_s, w, x, gn, tm=512):
    t, d = x.shape
    n_p = y_p.shape[0] // tm
    n_s = y_s.shape[0] // tm
    rows = pl.BlockSpec((tm, d), lambda i: (i, 0))
    p_rows = pl.BlockSpec((tm, d), lambda i: (jnp.minimum(i, n_p - 1), 0))
    s_rows = pl.BlockSpec((tm, d), lambda i: (jnp.maximum(i - n_p, 0), 0))
    return pl.pallas_call(
        functools.partial(_gated_out_kernel, n_prompt=n_p),
        grid=(n_p + n_s,),
        in_specs=[p_rows, p_rows, s_rows, s_rows,
                  pl.BlockSpec((d, d), lambda i: (0, 0)), rows, pl.BlockSpec((1, d), lambda i: (0, 0))],
        out_specs=[rows, pl.BlockSpec((d, tm), lambda i: (0, i))],
        out_shape=[jax.ShapeDtypeStruct((t, d), f32), jax.ShapeDtypeStruct((d, t), bf16)],
        compiler_params=_cparams("arbitrary"),
        name="rwkv_out",
    )(y_p, g_p, y_s, g_s, w, x, gn.reshape(1, d))


LORA_PAD = 2 * LANE


def _softplus(x):
    return jnp.maximum(x, 0.0) + jnp.log(1.0 + jnp.exp(-jnp.abs(x)))


def _rwkv_pre_kernel(h_ref, *refs, seq_tiles):
    h = h_ref[...]
    if seq_tiles is None:
        p_ref, mu_ref, wrkv_ref, l1_ref, l2_ref, vec_ref, r_ref, d_ref, k_ref, v_ref, a_ref, g_ref = refs
        prev = p_ref[...]
    else:
        mu_ref, wrkv_ref, l1_ref, l2_ref, vec_ref, r_ref, d_ref, k_ref, v_ref, a_ref, g_ref, buf_ref = refs
        tm = h.shape[0]

        @pl.when(pl.program_id(0) % seq_tiles == 0)
        def _():
            buf_ref[0:SUBLANE, :] = jnp.zeros((SUBLANE, h.shape[1]), f32)

        buf_ref[SUBLANE:SUBLANE + tm, :] = h
        prev = buf_ref[SUBLANE - 1:SUBLANE - 1 + tm, :]
        buf_ref[0:SUBLANE, :] = buf_ref[tm:tm + SUBLANE, :]
    xx = prev - h

    def mix(j):
        return (h + xx * mu_ref[j:j + 1, :]).astype(bf16)

    def mm(a, w):
        return jnp.dot(a, w, preferred_element_type=f32)

    r_ref[...] = mm(mix(0), wrkv_ref[0])
    k_ref[...] = mm(mix(2), wrkv_ref[1])
    v_ref[...] = mm(mix(3), wrkv_ref[2])
    wl = vec_ref[0:1, :] + mm(jnp.tanh(mm(mix(1), l1_ref[0])).astype(bf16), l2_ref[0])
    al = vec_ref[1:2, :] + mm(mm(mix(4), l1_ref[1]).astype(bf16), l2_ref[1])
    g_ref[...] = mm(jax.nn.sigmoid(mm(mix(5), l1_ref[2])).astype(bf16), l2_ref[2])
    w = -_softplus(-wl) - 0.5
    d_ref[...] = jnp.exp(-jnp.exp(w))
    a_ref[...] = jax.nn.sigmoid(al)


def rwkv_pre(h, prev, mu, w_rkv, lora1, lora2, vecs, *, row0, nrows, seq=None, tm=256):
    d = h.shape[1]
    blk0 = row0 // tm
    rows = pl.BlockSpec((tm, d), lambda i: (i, 0))

    def const(a):
        return pl.BlockSpec(a.shape, lambda i: (0,) * a.ndim)

    consts = (mu, w_rkv, lora1, lora2, vecs)
    in_specs = [pl.BlockSpec((tm, d), lambda i: (blk0 + i, 0))]
    args = [h]
    if prev is not None:
        in_specs.append(rows)
        args.append(prev)
    return pl.pallas_call(
        functools.partial(_rwkv_pre_kernel, seq_tiles=None if prev is not None else seq // tm),
        grid=(nrows // tm,),
        in_specs=in_specs + [const(a) for a in consts],
        out_specs=[rows] * 6,
        out_shape=[jax.ShapeDtypeStruct((nrows, d), f32)] * 6,
        scratch_shapes=[] if prev is not None else [pltpu.VMEM((tm + 2 * SUBLANE, d), f32)],
        compiler_params=_cparams("parallel" if prev is not None else "arbitrary"),
        name="rwkv_pre",
    )(*args, *consts)


RWKV_VGROUP = 8


RWKV_TB = 16
N_SEQ_TILE = LANE // RWKV_HEADS


N_COLS = D_MODEL // LANE


def _swap_sublane_with_column(cols):
    sub = lax.broadcasted_iota(jnp.int32, cols[0].shape, 0)
    s = N_COLS // 2
    while s >= 1:
        upper = (sub & s) != 0
        new = list(cols)
        for c in range(N_COLS):
            if c & s == 0:
                a, b = cols[c], cols[c | s]
                new[c] = jnp.where(upper, pltpu.roll(b, s, 0), a)
                new[c | s] = jnp.where(upper, b, pltpu.roll(a, LANE - s, 0))
        cols = new
        s //= 2
    return cols


def _swap_rowbit_with_lanehalf(col):
    a = jnp.concatenate([col[g * 16:g * 16 + SUBLANE] for g in range(N_SEQ_TILE)], axis=0)
    b = jnp.concatenate([col[g * 16 + SUBLANE:(g + 1) * 16] for g in range(N_SEQ_TILE)], axis=0)
    upper = lax.broadcasted_iota(jnp.int32, a.shape, 1) >= HALF_LANE
    moved = pltpu.roll(jnp.where(upper, a, b), HALF_LANE, 1)
    a = jnp.where(upper, moved, a)
    b = jnp.where(upper, b, moved)
    pieces = []
    for g in range(N_SEQ_TILE):
        pieces += [a[g * SUBLANE:(g + 1) * SUBLANE], b[g * SUBLANE:(g + 1) * SUBLANE]]
    return jnp.concatenate(pieces, axis=0)


def _natural_to_chain(x_ref, dst_ref, q):
    x2 = x_ref[...].reshape(N_SEQ_TILE * RWKV_TB, D_MODEL)
    cols = _swap_sublane_with_column([x2[:, c * LANE:(c + 1) * LANE] for c in range(N_COLS)])
    for c in range(N_COLS):
        xt = _swap_rowbit_with_lanehalf(cols[c]).T
        dst_ref[q, c] = xt[:RWKV_HEAD]
        dst_ref[q, RWKV_TB // 2 + c] = xt[RWKV_HEAD:]


def _chain_to_natural(src_ref, y_ref):
    cols = []
    for c in range(N_COLS):
        xt = jnp.concatenate([src_ref[c], src_ref[RWKV_TB // 2 + c]], axis=0)
        cols.append(_swap_rowbit_with_lanehalf(xt.T))
    y2 = jnp.concatenate(_swap_sublane_with_column(cols), axis=1)
    y_ref[...] = y2.reshape(N_SEQ_TILE, RWKV_TB, D_MODEL)


N_SCAN_IN = 5


def _rwkv_scan_kernel(r_ref, d_ref, k_ref, v_ref, a_ref, s0_ref, tab_ref, y_ref, sfin_ref,
                      s_ref, q_ref, ys_ref, *, natural):
    tb = q_ref.shape[1]

    @pl.when(pl.program_id(1) == 0)
    def _():
        s_ref[...] = s0_ref[...]

    for q, ref in enumerate((r_ref, d_ref, k_ref, v_ref, a_ref)):
        if natural:
            _natural_to_chain(ref, q_ref, q)
        else:
            q_ref[q] = ref[...]
    r_q, d_q, k_q, v_q, a_q = [q_ref.at[q] for q in range(N_SCAN_IN)]

    def step(t, carry):
        k_raw = k_q[t]
        a = a_q[t]
        kkr = k_raw * tab_ref[3]
        nrm = jnp.sqrt(jnp.sum(kkr * kkr, axis=0, keepdims=True))
        kk = kkr / jnp.maximum(nrm, 1e-12)
        d = d_q[t]
        kv = k_raw * (1.0 + (a - 1.0) * tab_ref[4])
        r = r_q[t]
        b = kk * a

        def vgroup(g, c2):
            v0 = pl.multiple_of(g * RWKV_VGROUP, RWKV_VGROUP)
            vrows = v_q[t, pl.ds(v0, RWKV_VGROUP), :]
            ys = []
            for vi in range(RWKV_VGROUP):
                sv = s_ref[v0 + vi]
                sa = -jnp.sum(sv * kk, axis=0, keepdims=True)
                sn = sv * d + sa * b + vrows[vi:vi + 1, :] * kv
                s_ref[v0 + vi] = sn
                ys.append(jnp.sum(sn * r, axis=0, keepdims=True))
            ys_ref[t, pl.ds(v0, RWKV_VGROUP), :] = jnp.concatenate(ys, axis=0)
            return c2

        lax.fori_loop(0, RWKV_HEAD // RWKV_VGROUP, vgroup, 0)
        y = ys_ref[t]
        mean = jnp.mean(y, axis=0, keepdims=True)
        yc = y - mean
        var = jnp.mean(yc * yc, axis=0, keepdims=True)
        bonus = jnp.sum(r * kv * tab_ref[0], axis=0, keepdims=True)
        ys_ref[t] = yc * lax.rsqrt(var + GN_EPS) * tab_ref[1] + tab_ref[2] + bonus * v_q[t]
        return carry

    lax.fori_loop(0, tb, step, 0)
    if natural:
        _chain_to_natural(ys_ref, y_ref)
    else:
        y_ref[...] = ys_ref[...]

    @pl.when(pl.program_id(1) == pl.num_programs(1) - 1)
    def _():
        sfin_ref[...] = s_ref[...]


def rwkv_scan(parts, s0, tab, *, natural):
    hd = RWKV_HEAD
    c = s0.shape[-1]
    if natural:
        nb, L, dm = parts[0].shape
        tb = RWKV_TB
        seq_spec = pl.BlockSpec((N_SEQ_TILE, tb, dm), lambda i, j: (i, j, 0))
        y_shape = jax.ShapeDtypeStruct((nb, L, dm), f32)
    else:
        L = parts[0].shape[0]
        tb = min(L, RWKV_TB)
        seq_spec = pl.BlockSpec((tb, hd, LANE), lambda i, j: (j, 0, i))
        y_shape = jax.ShapeDtypeStruct((L, hd, c), f32)
    st_spec = pl.BlockSpec((hd, hd, LANE), lambda i, j: (0, 0, i))
    return pl.pallas_call(
        functools.partial(_rwkv_scan_kernel, natural=natural),
        grid=(c // LANE, L // tb),
        in_specs=[seq_spec] * N_SCAN_IN + [st_spec, pl.BlockSpec(tab.shape, lambda i, j: (0, 0, 0))],
        out_specs=[seq_spec, st_spec],
        out_shape=[y_shape, jax.ShapeDtypeStruct((hd, hd, c), f32)],
        scratch_shapes=[pltpu.VMEM((hd, hd, LANE), f32), pltpu.VMEM((N_SCAN_IN, tb, hd, LANE), f32),
                        pltpu.VMEM((tb, hd, LANE), f32)],
        compiler_params=_cparams("parallel", "arbitrary"),
        name="rwkv_scan",
    )(*parts, s0, tab)


def _rwkv_core(parts, wkv, tab):
    b, L, _ = parts[0].shape
    hh = RWKV_HEADS // 2
    s0 = jnp.transpose(wkv.reshape(b, hh, 2, RWKV_HEAD, RWKV_HEAD), (3, 4, 0, 2, 1))
    s0 = s0.reshape(RWKV_HEAD, RWKV_HEAD, b * RWKV_HEADS)
    if b % N_SEQ_TILE == 0 and L % RWKV_TB == 0:
        y, s_fin = rwkv_scan(parts, s0, tab, natural=True)
        y = y.reshape(b * L, D_MODEL)
    else:
        def to_chain(t):
            t = jnp.transpose(t.reshape(b, L, hh, 2, RWKV_HEAD), (1, 4, 0, 3, 2))
            return t.reshape(L, RWKV_HEAD, b * RWKV_HEADS)

        y, s_fin = rwkv_scan([to_chain(t) for t in parts], s0, tab, natural=False)
        y = jnp.transpose(y.reshape(L, RWKV_HEAD, b, 2, hh), (2, 0, 4, 3, 1)).reshape(b * L, D_MODEL)
    s_fin = jnp.transpose(s_fin.reshape(RWKV_HEAD, RWKV_HEAD, b, 2, hh), (2, 4, 3, 0, 1))
    return y, s_fin.reshape(b, RWKV_HEADS, RWKV_HEAD, RWKV_HEAD)


def kernel(x_prompt, x_sample, state_ssm, state_conv, cache_swa_k, cache_swa_v, state_wkv, state_shift, rel_bias, norm_mix, norm_ffn, norm_final, mix_w_in, ssd_conv_w, ssd_conv_b, ssd_dt_bias, ssd_a_log, ssd_d_skip, ssd_gnorm, attn_sinks, mix_w_out, rwkv_mu, rwkv_w0, rwkv_w1, rwkv_w2, rwkv_a0, rwkv_a1, rwkv_a2, rwkv_g1, rwkv_g2, rwkv_k_k, rwkv_k_a, rwkv_r_k, rwkv_w_rkv, rwkv_w_o, rwkv_ln_w, rwkv_ln_b, peer_w_q, peer_sub_keys, peer_u, peer_v):
    bp, lp, d = x_prompt.shape
    bs, ls, _ = x_sample.shape
    tp = bp * lp
    ts = bs * ls
    x = jnp.concatenate([x_prompt.reshape(tp, d), x_sample.reshape(ts, d)], axis=0)

    def split(t):
        return t[:tp].reshape(bp, lp, -1), t[tp:].reshape(bs, ls, -1)

    def zero_state(a):
        return jnp.zeros((bp,) + a.shape[2:], a.dtype)

    def peer(x, ht, layer, g_next):
        wqt = peer_w_q[layer].T.astype(bf16)
        return peer_ffn_residual(x, ht, wqt, peer_sub_keys[layer].astype(bf16),
                                 peer_u[layer].astype(bf16), peer_v[layer].astype(bf16), g_next)

    z, xbc, q, kv, dt = in_proj(x, norm_mix[0], mix_w_in[0])
    pad16 = (0, LANE - SSM_HEADS)
    selh = (jnp.arange(LANE)[:, None] == jnp.arange(SSM_INNER)[None, :] // SSM_HEAD_DIM).astype(f32)
    ssd_params = (ssd_conv_w[0], ssd_conv_b[0].reshape(1, CONV_DIM),
                  jnp.pad(ssd_dt_bias[0], pad16).reshape(1, LANE),
                  jnp.pad(-jnp.exp(ssd_a_log[0]), pad16).reshape(1, LANE),
                  jnp.repeat(ssd_d_skip[0], SSM_HEAD_DIM).reshape(1, SSM_INNER),
                  ssd_gnorm[0].reshape(1, SSM_INNER), selh)
    y_ssd_p, ssm_p, conv_p = ssd_mixer(z, xbc, dt, zero_state(state_conv), zero_state(state_ssm), ssd_params,
                                       batch=bp, seq=lp, row0=0)
    y_ssd_s, ssm_s, conv_s = ssd_mixer(z, xbc, dt, state_conv[0], state_ssm[0], ssd_params,
                                       batch=bs, seq=ls, row0=tp)
    o_att_p, k_p, v_p = swa_mixer(q, kv, None, None, rel_bias, attn_sinks[0], batch=bp, seq=lp, row0=0)
    o_att_s, k_s, v_s = swa_mixer(q, kv, cache_swa_k[0], cache_swa_v[0], rel_bias, attn_sinks[0],
                                  batch=bs, seq=ls, row0=tp)
    x, ht = out_proj(y_ssd_p, o_att_p, y_ssd_s, o_att_s, mix_w_out[0].astype(bf16), x, norm_ffn[0])
    x, h = peer(x, ht, 0, norm_mix[1])

    h_p, h_s = split(h)
    prev_s = jnp.concatenate([state_shift[0][:, None], h_s[:, :-1]], axis=1)
    def lora_pair(w1, w2):
        pad = LORA_PAD - w1.shape[1]
        return jnp.pad(w1, ((0, 0), (0, pad))), jnp.pad(w2, ((0, pad), (0, 0)))

    pairs = [lora_pair(rwkv_w1[0], rwkv_w2[0]), lora_pair(rwkv_a1[0], rwkv_a2[0]), lora_pair(rwkv_g1[0], rwkv_g2[0])]
    lora1 = jnp.stack([p[0] for p in pairs]).astype(bf16)
    lora2 = jnp.stack([p[1] for p in pairs]).astype(bf16)
    vecs = jnp.stack([rwkv_w0[0], rwkv_a0[0]])
    pre_args = (rwkv_mu[0], rwkv_w_rkv[0].astype(bf16), lora1, lora2, vecs)
    *parts_p, g_p = rwkv_pre(h, None, *pre_args, row0=0, nrows=tp, seq=lp)
    *parts_s, g_s = rwkv_pre(h, prev_s.reshape(ts, d), *pre_args, row0=tp, nrows=ts)

    def chain_table(p):
        t = jnp.transpose(p.reshape(RWKV_HEADS // 2, 2, RWKV_HEAD), (2, 1, 0)).reshape(RWKV_HEAD, RWKV_HEADS)
        return jnp.tile(t, (1, LANE // RWKV_HEADS))

    tab = jnp.stack([chain_table(p) for p in (rwkv_r_k[0].reshape(-1), rwkv_ln_w[0], rwkv_ln_b[0],
                                              rwkv_k_k[0], rwkv_k_a[0])])
    y_p, wkv_p = _rwkv_core([t.reshape(bp, lp, d) for t in parts_p], zero_state(state_wkv), tab)
    y_s, wkv_s = _rwkv_core([t.reshape(bs, ls, d) for t in parts_s], state_wkv[0], tab)
    x, ht = gated_out_proj(y_p, g_p, y_s, g_s, rwkv_w_o[0].astype(bf16), x, norm_ffn[1])
    shift_p, shift_s = h_p[:, -1], h_s[:, -1]
    _, y = peer(x, ht, 1, norm_final)
    y_p, y_s = split(y)
    return (y_p, y_s, ssm_p[None], conv_p[None], k_p[None], v_p[None], wkv_p[None], shift_p[None],
            ssm_s[None], conv_s[None], k_s[None], v_s[None], wkv_s[None], shift_s[None])
```

```python
import functools
import math

import jax
import jax.numpy as jnp
from jax import lax
from jax.experimental import pallas as pl
from jax.experimental.pallas import tpu as pltpu

f32 = jnp.float32
bf16 = jnp.bfloat16

D_MODEL = 1024
PAST_LEN = 16384
SSM_HEAD_DIM = 64
SSM_HEADS = 16
SSM_INNER = 1024
SSM_GROUPS = 2
SSM_STATE = 128
CONV_W = 4
CONV_DIM = 1536
SSD_CHUNK = 128
ATT_HEAD_DIM = 64
ATT_HEADS = 16
ATT_KV_HEADS = 4
ATT_GQA = 4
WINDOW = 128
ATT_BLOCK = 128
REL_BUCKETS = 32
REL_MAX_DIST = 128
RWKV_HEAD = 64
RWKV_HEADS = 16
PEER_HEADS = 8
N_KEYS = 128
N_EXPERTS = N_KEYS * N_KEYS
PEER_TOPK = 16
PEER_HALF = 128
EPS = 1e-5
GN_EPS = 64e-5

LANE = 128
SUBLANE = 8
HALF_LANE = LANE // 2
VMEM_LIMIT = 56 * 2 ** 20
HIGHEST = lax.Precision.HIGHEST
NT_DIMS = (((1,), (1,)), ((), ()))
TN_DIMS = (((0,), (0,)), ((), ()))

_CAND_COUNTS = [PEER_TOPK // (a + 1) for a in range(PEER_TOPK)]
_N_CAND = sum(_CAND_COUNTS)
_N_CAND_PAD = -(-_N_CAND // 8) * 8
_BIG = 1e9


def _cparams(*sem):
    return pltpu.CompilerParams(dimension_semantics=sem, vmem_limit_bytes=VMEM_LIMIT)


def _rms(x, g):
    return x * lax.rsqrt(jnp.mean(x * x, axis=-1, keepdims=True) + EPS) * g


def _silu(x):
    return x * jax.nn.sigmoid(x)


BF16_ROWS = 2 * SUBLANE


def _dup_bf16(v):
    u = pltpu.bitcast(v.astype(bf16).astype(f32), jnp.int32)
    return u | lax.shift_right_logical(u, jnp.int32(16))


def _row_as_bf16_tile(ref, h, i):
    row = ref[h, pl.ds(i, 1), :]
    tile = pltpu.bitcast(jnp.broadcast_to(row, (SUBLANE, row.shape[1])), bf16)
    return jnp.concatenate([tile] * (N_KEYS // BF16_ROWS), axis=0)


def _peer_select_kernel(ht_ref, wqt_ref, sk_ref, pos_ref, b0_ref, r1_ref, e0_ref, e1_ref,
                        qt_ref, s_ref, vals_ref, idx_ref, cs_ref):
    tm = ht_ref.shape[1]
    qt_ref[...] = jnp.dot(wqt_ref[...], ht_ref[...], preferred_element_type=f32).astype(bf16)
    row = lax.broadcasted_iota(jnp.int32, (N_KEYS, tm), 0).astype(f32)
    pos = pos_ref[...]
    neg_inf = f32(-jnp.inf)

    def head_compute(h, exact):
        bad = jnp.zeros((1, tm), f32)
        for c in (0, 1):
            off = pl.multiple_of((h * 2 + c) * PEER_HALF, PEER_HALF)
            s_ref[c] = jnp.dot(sk_ref[h, c], qt_ref[pl.ds(off, PEER_HALF), :], preferred_element_type=f32)

        bad_tiles = []
        for lt in range(tm // LANE):
            ls = slice(lt * LANE, (lt + 1) * LANE)
            row_t = row[:, :LANE]

            def remove_max(k, s, c, lt=lt, row_t=row_t):
                m = jnp.max(s, axis=0, keepdims=True)
                vals_ref[c, lt, pl.ds(k, 1), :] = m
                if exact:
                    idx = jnp.min(jnp.where(s == m, row_t, f32(N_KEYS)), axis=0, keepdims=True)
                    idx_ref[c, lt, pl.ds(k, 1), :] = idx
                    hit = row_t == idx
                else:
                    hit = s == m
                return jnp.where(hit, neg_inf, s), hit

            def extract(k, st, remove_max=remove_max):
                sa, sb, rank = st
                sa, _ = remove_max(k, sa, 0)
                sb, hit = remove_max(k, sb, 1)
                return sa, sb, jnp.where(hit, jnp.asarray(k, f32), rank)

            sa_fin, sb_fin, rank = lax.fori_loop(
                0, PEER_TOPK, extract, (s_ref[0, :, ls], s_ref[1, :, ls], jnp.full((N_KEYS, LANE), f32(PEER_TOPK))))
            r1_ref[h, :, ls] = rank.astype(bf16)
            if not exact:
                bad_t = jnp.zeros((1, LANE), f32)
                for s_fin in (sa_fin, sb_fin):
                    removed = jnp.sum(jnp.where(s_fin == neg_inf, f32(1.0), f32(0.0)), axis=0, keepdims=True)
                    bad_t = bad_t + jnp.abs(removed - f32(PEER_TOPK))
                bad_tiles.append(bad_t)
        if not exact:
            bad = bad + jnp.concatenate(bad_tiles, axis=1)

        def lane_tiles(ref, c):
            return jnp.concatenate([ref[c, lt] for lt in range(tm // LANE)], axis=1)

        v0 = lane_tiles(vals_ref, 0)
        v1 = lane_tiles(vals_ref, 1)
        r = 0
        for a in range(PEER_TOPK):
            nb = _CAND_COUNTS[a]
            cs_ref[r:r + nb, :] = v0[a:a + 1, :] + v1[0:nb, :]
            r += nb
        cs_ref[_N_CAND:_N_CAND_PAD, :] = jnp.full((_N_CAND_PAD - _N_CAND, tm), neg_inf)
        m1 = v0[0:1, :] + v1[0:1, :]

        cs = cs_ref[...]
        if exact:
            def extract2(k, st):
                cs, sel, z = st
                m = jnp.max(cs, axis=0, keepdims=True)
                p = jnp.min(jnp.where(cs == m, pos, f32(_BIG)), axis=0, keepdims=True)
                hit = pos == p
                return jnp.where(hit, neg_inf, cs), jnp.where(hit, f32(1.0), sel), z + jnp.exp(m - m1)

            _, sel, z = lax.fori_loop(0, PEER_TOPK, extract2,
                                      (cs, jnp.zeros((_N_CAND_PAD, tm), f32), jnp.zeros((1, tm), f32)))
        else:
            larger = jnp.zeros((_N_CAND_PAD, tm), f32)
            for r in range(_N_CAND):
                larger = larger + jnp.where(cs[r:r + 1, :] > cs, f32(1.0), f32(0.0))
            sel = jnp.where((larger < f32(PEER_TOPK)) & (cs > neg_inf), f32(1.0), f32(0.0))
            z = jnp.sum(sel * jnp.exp(cs - m1), axis=0, keepdims=True)
            bad = bad + jnp.abs(jnp.sum(sel, axis=0, keepdims=True) - f32(PEER_TOPK))
        s0 = s_ref[0]
        idx0 = lane_tiles(idx_ref, 0)
        bound0 = jnp.zeros((N_KEYS, tm), f32)
        r = 0
        for a in range(PEER_TOPK):
            nb = _CAND_COUNTS[a]
            cnt = jnp.sum(sel[r:r + nb, :], axis=0, keepdims=True)
            is_a = (row == idx0[a:a + 1, :]) if exact else (s0 == v0[a:a + 1, :])
            bound0 = jnp.where(is_a, cnt, bound0)
            r += nb
        b0_ref[h] = _dup_bf16(bound0)
        e0_ref[h] = _dup_bf16(jnp.exp(s0 - v0[0:1, :]) / z)
        e1_ref[h] = jnp.exp(s_ref[1] - v1[0:1, :]).astype(bf16)
        return bad

    def head_body(h, carry):
        bad = head_compute(h, False)

        @pl.when(jnp.max(bad) > 0.0)
        def _():
            head_compute(h, True)

        return carry

    lax.fori_loop(0, PEER_HEADS, head_body, 0)


def _peer_main_kernel(ht_ref, u_ref, v_ref, b0_ref, r1_ref, e0_ref, e1_ref, x_ref, gn_ref, o_ref, hn_ref,
                      acc_ref, *, nblk, npieces):
    j = pl.program_id(1)

    @pl.when(j == 0)
    def _():
        acc_ref[...] = jnp.zeros_like(acc_ref)

    ht = ht_ref[...]
    bpp = nblk // npieces
    pc = bpp * N_KEYS
    for p in range(npieces):
        at = jnp.dot(u_ref[p * pc:(p + 1) * pc, :], ht, preferred_element_type=f32)
        ws = []
        for b in range(bpp):
            i = j * nblk + p * bpp + b
            a = at[b * N_KEYS:(b + 1) * N_KEYS, :]
            g = jnp.zeros(a.shape, bf16)
            for h in range(PEER_HEADS):
                bnd = _row_as_bf16_tile(b0_ref, h, i)
                e0 = _row_as_bf16_tile(e0_ref, h, i)
                g = g + jnp.where(r1_ref[h] < bnd, e1_ref[h] * e0, bf16(0.0))
            gelu = 0.5 * a * (1.0 + lax.erf(a * f32(math.sqrt(0.5))))
            ws.append(g * gelu.astype(bf16))
        wt = jnp.concatenate(ws, axis=0) if bpp > 1 else ws[0]
        acc_ref[...] += lax.dot_general(wt, v_ref[p * pc:(p + 1) * pc, :], TN_DIMS,
                                        preferred_element_type=f32)

    @pl.when(j == pl.num_programs(1) - 1)
    def _():
        xn = x_ref[...] + acc_ref[...]
        o_ref[...] = xn
        hn_ref[...] = _rms(xn, gn_ref[...])


def peer_ffn_residual(x, ht, wqt, sk, u, v, g_next, tm_sel=512, tm=256, ce=4096, npieces=16):
    t = x.shape[0]
    d = D_MODEL
    pos_list = []
    for a in range(PEER_TOPK):
        pos_list += [a * PEER_TOPK + b for b in range(_CAND_COUNTS[a])]
    pos_list += [_BIG] * (_N_CAND_PAD - _N_CAND)
    pos = jnp.broadcast_to(jnp.asarray(pos_list, f32)[:, None], (_N_CAND_PAD, tm_sel))
    sel_shape = jax.ShapeDtypeStruct((PEER_HEADS, N_KEYS, t), jnp.int32)
    sel_shape16 = jax.ShapeDtypeStruct((PEER_HEADS, N_KEYS, t), bf16)
    sel_spec = pl.BlockSpec((PEER_HEADS, N_KEYS, tm_sel), lambda i: (0, 0, i))
    b0, r1, e0, e1 = pl.pallas_call(
        _peer_select_kernel,
        grid=(t // tm_sel,),
        in_specs=[pl.BlockSpec((d, tm_sel), lambda i: (0, i)),
                  pl.BlockSpec((PEER_HEADS * 2 * PEER_HALF, d), lambda i: (0, 0)),
                  pl.BlockSpec((PEER_HEADS, 2, N_KEYS, PEER_HALF), lambda i: (0, 0, 0, 0)),
                  pl.BlockSpec((_N_CAND_PAD, tm_sel), lambda i: (0, 0))],
        out_specs=[sel_spec] * 4,
        out_shape=[sel_shape, sel_shape16, sel_shape, sel_shape16],
        scratch_shapes=[pltpu.VMEM((PEER_HEADS * 2 * PEER_HALF, tm_sel), bf16),
                        pltpu.VMEM((2, N_KEYS, tm_sel), f32),
                        pltpu.VMEM((2, tm_sel // LANE, PEER_TOPK, LANE), f32),
                        pltpu.VMEM((2, tm_sel // LANE, PEER_TOPK, LANE), f32),
                        pltpu.VMEM((_N_CAND_PAD, tm_sel), f32)],
        compiler_params=_cparams("parallel"),
        name="peer_select",
    )(ht, wqt, sk, pos)

    nblk = ce // N_KEYS
    sel_spec2 = pl.BlockSpec((PEER_HEADS, N_KEYS, tm), lambda i, j: (0, 0, i))
    return pl.pallas_call(
        functools.partial(_peer_main_kernel, nblk=nblk, npieces=npieces),
        grid=(t // tm, N_EXPERTS // ce),
        in_specs=[pl.BlockSpec((d, tm), lambda i, j: (0, i)),
                  pl.BlockSpec((ce, d), lambda i, j: (j, 0)),
                  pl.BlockSpec((ce, d), lambda i, j: (j, 0)),
                  sel_spec2, sel_spec2, sel_spec2, sel_spec2,
                  pl.BlockSpec((tm, d), lambda i, j: (i, 0)), pl.BlockSpec((1, d), lambda i, j: (0, 0))],
        out_specs=[pl.BlockSpec((tm, d), lambda i, j: (i, 0))] * 2,
        out_shape=[jax.ShapeDtypeStruct((t, d), f32)] * 2,
        scratch_shapes=[pltpu.VMEM((tm, d), f32)],
        compiler_params=_cparams("parallel", "arbitrary"),
        name="peer_main",
    )(ht, u, v, b0, r1, e0, e1, x, g_next.reshape(1, d))


IN_WIDTHS = (SSM_INNER, CONV_DIM, ATT_HEADS * ATT_HEAD_DIM, 2 * ATT_KV_HEADS * ATT_HEAD_DIM, LANE)


def _in_proj_kernel(x_ref, g_ref, w_ref, *out_refs):
    h = _rms(x_ref[...], g_ref[...])
    y = jnp.dot(h.astype(bf16), w_ref[...], preferred_element_type=f32)
    o = 0
    for ref in out_refs:
        w = ref.shape[1]
        ref[...] = y[:, o:o + w]
        o += w


def in_proj(x, g, w_in, tm=256):
    t, d = x.shape
    o1 = SSM_INNER
    o2 = o1 + CONV_DIM
    o3 = o2 + SSM_HEADS
    w = jnp.concatenate([w_in[:, :o2], w_in[:, o3:], w_in[:, o2:o3],
                         jnp.zeros((d, LANE - SSM_HEADS), w_in.dtype)], axis=1).astype(bf16)
    n = w.shape[1]
    return pl.pallas_call(
        _in_proj_kernel,
        grid=(t // tm,),
        in_specs=[pl.BlockSpec((tm, d), lambda i: (i, 0)), pl.BlockSpec((1, d), lambda i: (0, 0)),
                  pl.BlockSpec((d, n), lambda i: (0, 0))],
        out_specs=[pl.BlockSpec((tm, wd), lambda i: (i, 0)) for wd in IN_WIDTHS],
        out_shape=[jax.ShapeDtypeStruct((t, wd), f32) for wd in IN_WIDTHS],
        compiler_params=_cparams("parallel"),
        name="in_proj",
    )(x, g.reshape(1, d), w)


N_PAIRS = SSM_HEADS // 2
SSD_SHORT_CHUNK = 32


def _ssd_kernel(*refs, lin, lc):
    (z_ref, xbc_ref, dt_ref, cs_ref, h0_ref, cw_ref, cb_ref, dtb_ref, aneg_ref, dsk_ref, gn_ref, selh_ref) = refs[:12]
    refs = refs[12:]
    y_ref, hfin_ref, tail_ref, xe_ref, hp_ref, ys_ref = refs[:6]
    c = pl.program_id(1)

    @pl.when(c == 0)
    def _():
        xe_ref[0:SUBLANE, :] = cs_ref[0]
        hp_ref[...] = h0_ref[0]

    if lin == lc:
        xe_ref[SUBLANE:SUBLANE + lc, :] = xbc_ref[...]
        z = z_ref[...]
        dt_raw = dt_ref[...]
    else:
        zpad_ref, dtpad_ref = refs[6:8]
        xe_ref[SUBLANE:SUBLANE + lc, :] = jnp.zeros((lc, CONV_DIM), f32)
        xe_ref[SUBLANE:SUBLANE + lin, :] = xbc_ref[...]
        zpad_ref[...] = jnp.zeros_like(zpad_ref)
        zpad_ref[0:lin, :] = z_ref[...]
        dtpad_ref[...] = jnp.zeros_like(dtpad_ref)
        dtpad_ref[0:lin, :] = dt_ref[...]
        z = zpad_ref[...]
        dt_raw = dtpad_ref[...]

    conv = cb_ref[...]
    for j in range(CONV_W):
        o = SUBLANE - (CONV_W - 1) + j
        conv = conv + xe_ref[o:o + lc, :] * cw_ref[j:j + 1, :]
    if lin == lc:
        xe_ref[0:SUBLANE, :] = xe_ref[lc:lc + SUBLANE, :]
    xc = _silu(conv)
    xs = xc[:, :SSM_INNER]
    bm = xc[:, SSM_INNER:SSM_INNER + SSM_GROUPS * SSM_STATE]
    cm = xc[:, SSM_INNER + SSM_GROUPS * SSM_STATE:]

    row = lax.broadcasted_iota(jnp.int32, (lc, LANE), 0)
    col = lax.broadcasted_iota(jnp.int32, (lc, LANE), 1)
    causal = lax.broadcasted_iota(jnp.int32, (lc, lc), 0) >= lax.broadcasted_iota(jnp.int32, (lc, lc), 1)
    pair_row = lax.broadcasted_iota(jnp.int32, (LANE, SSM_STATE), 0)
    lane_lo = col < HALF_LANE
    neg_inf = f32(-jnp.inf)

    x = dt_raw + dtb_ref[...]
    dt = jnp.maximum(x, 0.0) + jnp.log(1.0 + jnp.exp(-jnp.abs(x)))
    if lin != lc:
        dt = jnp.where(row < lin, dt, 0.0)
    la = dt * aneg_ref[...]
    acs = jnp.dot(causal.astype(f32), la, precision=HIGHEST, preferred_element_type=f32)
    acs_sq = acs if lc == LANE else jnp.concatenate([acs, jnp.zeros((LANE - lc, LANE), f32)], axis=0)
    acs_t = acs_sq.T[:, :lc]
    selh = selh_ref[...]
    dt_exp = jnp.dot(dt, selh, precision=HIGHEST, preferred_element_type=f32)
    acs_exp = jnp.dot(acs, selh, precision=HIGHEST, preferred_element_type=f32)
    alast = acs[lc - 1:lc, :]
    xdt = xs * dt_exp
    eacs = jnp.exp(acs_exp)
    xdt_end = (xdt * jnp.exp(acs_exp[lc - 1:lc, :] - acs_exp)).astype(bf16)
    dsk = dsk_ref[...]

    for g in range(SSM_GROUPS):
        cmg = cm[:, g * SSM_STATE:(g + 1) * SSM_STATE].astype(bf16)
        bmg = bm[:, g * SSM_STATE:(g + 1) * SSM_STATE].astype(bf16)
        cb = lax.dot_general(cmg, bmg, NT_DIMS, preferred_element_type=f32)
        for jj in range(N_PAIRS // SSM_GROUPS):
            j = g * (N_PAIRS // SSM_GROUPS) + jj
            sl = slice(j * LANE, (j + 1) * LANE)
            xdt_pair = xdt[:, sl]
            ydiag = jnp.zeros((lc, LANE), f32)
            for half in (0, 1):
                h = 2 * j + half
                seg = acs[:, h:h + 1] - acs_t[h:h + 1, :]
                dec = jnp.exp(jnp.where(causal, seg, neg_inf))
                m = (cb * dec).astype(bf16)
                keep = lane_lo if half == 0 else jnp.logical_not(lane_lo)
                xd = jnp.where(keep, xdt_pair, 0.0).astype(bf16)
                ydiag = ydiag + jnp.dot(m, xd, preferred_element_type=f32)
            hpj = hp_ref[j]
            yoff = lax.dot_general(cmg, hpj.astype(bf16), NT_DIMS, preferred_element_type=f32) * eacs[:, sl]
            s_new = lax.dot_general(xdt_end[:, sl], bmg, TN_DIMS, preferred_element_type=f32)
            dl = jnp.where(pair_row < SSM_HEAD_DIM, alast[:, 2 * j:2 * j + 1], alast[:, 2 * j + 1:2 * j + 2])
            hp_ref[j] = hpj * jnp.exp(dl) + s_new
            ys_ref[:, sl] = ydiag + yoff + dsk[:, sl] * xs[:, sl]

    y = ys_ref[...] * _silu(z)
    gs = SSM_INNER // SSM_GROUPS
    gn = gn_ref[...]
    for g in range(SSM_GROUPS):
        yg = y[:, g * gs:(g + 1) * gs]
        yg = yg * lax.rsqrt(jnp.mean(yg * yg, axis=-1, keepdims=True) + EPS) * gn[:, g * gs:(g + 1) * gs]
        y_ref[:, g * gs:(g + 1) * gs] = yg[0:lin, :]

    @pl.when(c == pl.num_programs(1) - 1)
    def _():
        hfin_ref[0] = hp_ref[...]
        tail_ref[0] = xe_ref[lin:lin + SUBLANE, :]


def ssd_mixer(z, xbc, dt, conv_state, ssm_state, params, *, batch, seq, row0):
    conv_w, conv_b, dt_bias, a_neg, d_skip, gnorm, selh = params
    lin = min(seq, SSD_CHUNK)
    lc = SSD_CHUNK if lin == SSD_CHUNK else SSD_SHORT_CHUNK
    nc = seq // lin
    blk0 = row0 // lin
    cs = jnp.pad(conv_state, ((0, 0), (SUBLANE - (CONV_W - 1), 0), (0, 0)))
    h0 = ssm_state.reshape(batch, N_PAIRS, LANE, SSM_STATE)

    def rows(w):
        return pl.BlockSpec((lin, w), lambda b, c: (blk0 + b * nc + c, 0))

    def const(shape):
        return pl.BlockSpec(shape, lambda b, c: (0,) * len(shape))

    in_specs = [rows(SSM_INNER), rows(CONV_DIM), rows(LANE),
                pl.BlockSpec((1, SUBLANE, CONV_DIM), lambda b, c: (b, 0, 0)),
                pl.BlockSpec((1, N_PAIRS, LANE, SSM_STATE), lambda b, c: (b, 0, 0, 0)),
                const((CONV_W, CONV_DIM)), const((1, CONV_DIM)), const((1, LANE)), const((1, LANE)),
                const((1, SSM_INNER)), const((1, SSM_INNER)), const((LANE, SSM_INNER))]
    args = [z, xbc, dt, cs, h0, conv_w, conv_b, dt_bias, a_neg, d_skip, gnorm, selh]
    scratch = [pltpu.VMEM((lc + 2 * SUBLANE, CONV_DIM), f32), pltpu.VMEM((N_PAIRS, LANE, SSM_STATE), f32),
               pltpu.VMEM((lc, SSM_INNER), f32)]
    if lin != lc:
        scratch += [pltpu.VMEM((lc, SSM_INNER), f32), pltpu.VMEM((lc, LANE), f32)]
    y, h_fin, tail = pl.pallas_call(
        functools.partial(_ssd_kernel, lin=lin, lc=lc),
        grid=(batch, nc),
        in_specs=in_specs,
        out_specs=[pl.BlockSpec((lin, SSM_INNER), lambda b, c: (b * nc + c, 0)),
                   pl.BlockSpec((1, N_PAIRS, LANE, SSM_STATE), lambda b, c: (b, 0, 0, 0)),
                   pl.BlockSpec((1, SUBLANE, CONV_DIM), lambda b, c: (b, 0, 0))],
        out_shape=[jax.ShapeDtypeStruct((batch * seq, SSM_INNER), f32),
                   jax.ShapeDtypeStruct((batch, N_PAIRS, LANE, SSM_STATE), f32),
                   jax.ShapeDtypeStruct((batch, SUBLANE, CONV_DIM), f32)],
        scratch_shapes=scratch,
        compiler_params=_cparams("parallel", "arbitrary"),
        name="ssd_mixer",
    )(*args)
    new_conv = tail[:, SUBLANE - (CONV_W - 1):]
    return y, h_fin.reshape(batch, SSM_HEADS, SSM_HEAD_DIM, SSM_STATE), new_conv


def _swa_kernel(*refs, lq, masked_first):
    q_ref, kp_ref, vp_ref, kc_ref, vc_ref, bias_ref, sink_ref = refs[:7]
    refs = refs[7:]
    o_ref, nk_ref, nv_ref = refs[:3]
    n = pl.program_id(1)
    q = q_ref[...]
    for new_ref, prev_ref, cur_ref in ((nk_ref, kp_ref, kc_ref), (nv_ref, vp_ref, vc_ref)):
        if lq < WINDOW:
            new_ref[0:WINDOW - lq, :] = prev_ref[lq:WINDOW, :]
        new_ref[WINDOW - lq:WINDOW, :] = cur_ref[...]
    if lq == ATT_BLOCK:
        kc = kc_ref[...]
        vc = vc_ref[...]
    else:
        kpad_ref, vpad_ref = refs[3:5]
        kpad_ref[...] = jnp.zeros_like(kpad_ref)
        vpad_ref[...] = jnp.zeros_like(vpad_ref)
        kpad_ref[0:lq, :] = kc_ref[...]
        vpad_ref[0:lq, :] = vc_ref[...]
        kc = kpad_ref[...]
        vc = vpad_ref[...]
    kp = kp_ref[...]
    vp = vp_ref[...]
    lane_lo = lax.broadcasted_iota(jnp.int32, (ATT_BLOCK, LANE), 1) < HALF_LANE
    lane_lo_q = lax.broadcasted_iota(jnp.int32, (lq, LANE), 1) < HALF_LANE
    neg_inf = f32(-jnp.inf)
    scale = f32(ATT_HEAD_DIM ** -0.5)

    for g in range(ATT_KV_HEADS):
        sl = slice((g // 2) * LANE, (g // 2 + 1) * LANE)
        odd = g % 2 == 1

        def kpad(k):
            pair = k[:, sl]
            if odd:
                pair = pltpu.roll(pair, HALF_LANE, 1)
            return jnp.where(lane_lo, pair, 0.0).astype(bf16)

        def vdup(v):
            pair = v[:, sl]
            rolled = pltpu.roll(pair, HALF_LANE, 1)
            return (jnp.where(lane_lo, rolled, pair) if odd else jnp.where(lane_lo, pair, rolled)).astype(bf16)

        qp0 = q[:, (2 * g) * LANE:(2 * g + 1) * LANE]
        qp1 = q[:, (2 * g + 1) * LANE:(2 * g + 2) * LANE]
        qg = jnp.concatenate([qp0, pltpu.roll(qp0, HALF_LANE, 1), qp1, pltpu.roll(qp1, HALF_LANE, 1)],
                             axis=0).astype(bf16)
        bias = bias_ref[g]
        sp = lax.dot_general(qg, kpad(kp), NT_DIMS, preferred_element_type=f32) * scale + bias[:, :WINDOW]
        sc = lax.dot_general(qg, kpad(kc), NT_DIMS, preferred_element_type=f32) * scale + bias[:, WINDOW:]
        if masked_first:
            sp = jnp.where(n > 0, sp, neg_inf)
        sink = sink_ref[g][:, 0:1]
        m = jnp.maximum(jnp.max(jnp.maximum(sp, sc), axis=-1, keepdims=True), sink)
        pp = jnp.exp(sp - m)
        pc = jnp.exp(sc - m)
        denom = jnp.sum(pp + pc, axis=-1, keepdims=True) + jnp.exp(sink - m)
        og = (jnp.dot(pp.astype(bf16), vdup(vp), preferred_element_type=f32)
              + jnp.dot(pc.astype(bf16), vdup(vc), preferred_element_type=f32)) / denom
        o_ref[:, (2 * g) * LANE:(2 * g + 1) * LANE] = jnp.where(lane_lo_q, og[0:lq], og[lq:2 * lq])
        o_ref[:, (2 * g + 1) * LANE:(2 * g + 2) * LANE] = jnp.where(lane_lo_q, og[2 * lq:3 * lq], og[3 * lq:4 * lq])


def _rel_bucket(dist):
    exact = REL_BUCKETS // 2
    d = jnp.maximum(dist, 0)
    large = exact + (jnp.log(jnp.maximum(d, 1).astype(f32) / exact)
                     / math.log(REL_MAX_DIST / exact) * (REL_BUCKETS - exact)).astype(jnp.int32)
    large = jnp.minimum(large, REL_BUCKETS - 1)
    return jnp.where(d < exact, d, large)


def _swa_tables(rel_bias, sinks, lq):
    qi = jnp.arange(lq)[:, None]
    kj = jnp.arange(2 * ATT_BLOCK)[None, :]
    dist = qi + WINDOW - kj
    band = (dist >= 0) & (dist <= WINDOW)
    onehot = (_rel_bucket(dist)[..., None] == jnp.arange(REL_BUCKETS)).astype(f32)
    bias = jnp.einsum('qkb,bh->qkh', onehot, rel_bias, precision=HIGHEST)
    bias = jnp.where(band[..., None], bias, -jnp.inf)
    bias = jnp.transpose(bias, (2, 0, 1)).reshape(ATT_KV_HEADS, ATT_GQA * lq, 2 * ATT_BLOCK)
    sink = jnp.broadcast_to(sinks.reshape(ATT_KV_HEADS, ATT_GQA, 1, 1), (ATT_KV_HEADS, ATT_GQA, lq, LANE))
    return bias, sink.reshape(ATT_KV_HEADS, ATT_GQA * lq, LANE)


def swa_mixer(q, kv, k_prev, v_prev, rel_bias, sinks, *, batch, seq, row0):
    lq = min(seq, ATT_BLOCK)
    nb = seq // lq
    blk0 = row0 // lq
    kvw = ATT_KV_HEADS * ATT_HEAD_DIM
    bias, sink = _swa_tables(rel_bias, sinks, lq)
    cur_k = pl.BlockSpec((lq, kvw), lambda b, n: (blk0 + b * nb + n, 0))
    cur_v = pl.BlockSpec((lq, kvw), lambda b, n: (blk0 + b * nb + n, 1))
    if k_prev is None:
        prev_k = pl.BlockSpec((WINDOW, kvw), lambda b, n: (blk0 + b * nb + jnp.maximum(n - 1, 0), 0))
        prev_v = pl.BlockSpec((WINDOW, kvw), lambda b, n: (blk0 + b * nb + jnp.maximum(n - 1, 0), 1))
        kp_arr, vp_arr = kv, kv
    else:
        prev_k = pl.BlockSpec((WINDOW, kvw), lambda b, n: (b, 0))
        prev_v = prev_k
        kp_arr = k_prev.reshape(batch * WINDOW, kvw)
        vp_arr = v_prev.reshape(batch * WINDOW, kvw)
    rows = pl.BlockSpec((lq, ATT_HEADS * ATT_HEAD_DIM), lambda b, n: (blk0 + b * nb + n, 0))
    in_specs = [rows, prev_k, prev_v, cur_k, cur_v,
                pl.BlockSpec(bias.shape, lambda b, n: (0, 0, 0)), pl.BlockSpec(sink.shape, lambda b, n: (0, 0, 0))]
    args = [q, kp_arr, vp_arr, kv, kv, bias, sink]
    scratch = [] if lq == ATT_BLOCK else [pltpu.VMEM((ATT_BLOCK, kvw), f32), pltpu.VMEM((ATT_BLOCK, kvw), f32)]
    cache_spec = pl.BlockSpec((WINDOW, kvw), lambda b, n: (b, 0))
    cache_shape = jax.ShapeDtypeStruct((batch * WINDOW, kvw), f32)
    o, new_k, new_v = pl.pallas_call(
        functools.partial(_swa_kernel, lq=lq, masked_first=k_prev is None),
        grid=(batch, nb),
        in_specs=in_specs,
        out_specs=[pl.BlockSpec((lq, ATT_HEADS * ATT_HEAD_DIM), lambda b, n: (b * nb + n, 0)), cache_spec, cache_spec],
        out_shape=[jax.ShapeDtypeStruct((batch * seq, ATT_HEADS * ATT_HEAD_DIM), f32), cache_shape, cache_shape],
        scratch_shapes=scratch,
        compiler_params=_cparams("parallel", "arbitrary"),
        name="swa_mixer",
    )(*args)
    cshape = (batch, WINDOW, ATT_KV_HEADS, ATT_HEAD_DIM)
    return o, new_k.reshape(cshape), new_v.reshape(cshape)


def _out_proj_kernel(yap_ref, ybp_ref, yas_ref, ybs_ref, w_ref, x_ref, g_ref, o_ref, ht_ref, *, n_prompt):
    ka = yap_ref.shape[1]

    def body(ya_ref, yb_ref):
        acc = jnp.dot(ya_ref[...].astype(bf16), w_ref[0:ka, :], preferred_element_type=f32)
        acc = acc + jnp.dot(yb_ref[...].astype(bf16), w_ref[ka:, :], preferred_element_type=f32)
        xn = x_ref[...] + acc
        o_ref[...] = xn
        ht_ref[...] = _rms(xn, g_ref[...]).T.astype(bf16)

    @pl.when(pl.program_id(0) < n_prompt)
    def _():
        body(yap_ref, ybp_ref)

    @pl.when(pl.program_id(0) >= n_prompt)
    def _():
        body(yas_ref, ybs_ref)


def out_proj(ya_p, yb_p, ya_s, yb_s, w, x, g, tm=512):
    t, d = x.shape
    ka, kb = ya_p.shape[1], yb_p.shape[1]
    n_p = ya_p.shape[0] // tm
    n_s = ya_s.shape[0] // tm

    def p_rows(k):
        return pl.BlockSpec((tm, k), lambda i: (jnp.minimum(i, n_p - 1), 0))

    def s_rows(k):
        return pl.BlockSpec((tm, k), lambda i: (jnp.maximum(i - n_p, 0), 0))

    return pl.pallas_call(
        functools.partial(_out_proj_kernel, n_prompt=n_p),
        grid=(n_p + n_s,),
        in_specs=[p_rows(ka), p_rows(kb), s_rows(ka), s_rows(kb),
                  pl.BlockSpec((ka + kb, d), lambda i: (0, 0)), pl.BlockSpec((tm, d), lambda i: (i, 0)),
                  pl.BlockSpec((1, d), lambda i: (0, 0))],
        out_specs=[pl.BlockSpec((tm, d), lambda i: (i, 0)), pl.BlockSpec((d, tm), lambda i: (0, i))],
        out_shape=[jax.ShapeDtypeStruct((t, d), f32), jax.ShapeDtypeStruct((d, t), bf16)],
        compiler_params=_cparams("arbitrary"),
        name="out_proj",
    )(ya_p, yb_p, ya_s, yb_s, w, x, g.reshape(1, d))


def _gated_out_kernel(yp_ref, gp_ref, ys_ref, gs_ref, w_ref, x_ref, gn_ref, o_ref, ht_ref, *, n_prompt):
    def body(y_ref, g_ref):
        a = (y_ref[...] * g_ref[...]).astype(bf16)
        xn = x_ref[...] + jnp.dot(a, w_ref[...], preferred_element_type=f32)
        o_ref[...] = xn
        ht_ref[...] = _rms(xn, gn_ref[...]).T.astype(bf16)

    @pl.when(pl.program_id(0) < n_prompt)
    def _():
        body(yp_ref, gp_ref)

    @pl.when(pl.program_id(0) >= n_prompt)
    def _():
        body(ys_ref, gs_ref)


def gated_out_proj(y_p, g_p, y_s, g_s, w, x, gn, tm=512):
    t, d = x.shape
    n_p = y_p.shape[0] // tm
    n_s = y_s.shape[0] // tm
    rows = pl.BlockSpec((tm, d), lambda i: (i, 0))
    p_rows = pl.BlockSpec((tm, d), lambda i: (jnp.minimum(i, n_p - 1), 0))
    s_rows = pl.BlockSpec((tm, d), lambda i: (jnp.maximum(i - n_p, 0), 0))
    return pl.pallas_call(
        functools.partial(_gated_out_kernel, n_prompt=n_p),
        grid=(n_p + n_s,),
        in_specs=[p_rows, p_rows, s_rows, s_rows,
                  pl.BlockSpec((d, d), lambda i: (0, 0)), rows, pl.BlockSpec((1, d), lambda i: (0, 0))],
        out_specs=[rows, pl.BlockSpec((d, tm), lambda i: (0, i))],
        out_shape=[jax.ShapeDtypeStruct((t, d), f32), jax.ShapeDtypeStruct((d, t), bf16)],
        compiler_params=_cparams("arbitrary"),
        name="rwkv_out",
    )(y_p, g_p, y_s, g_s, w, x, gn.reshape(1, d))


LORA_PAD = 2 * LANE


def _softplus(x):
    return jnp.maximum(x, 0.0) + jnp.log(1.0 + jnp.exp(-jnp.abs(x)))


def _rwkv_pre_kernel(h_ref, p_ref, mu_ref, wrkv_ref, l1_ref, l2_ref, vec_ref,
                     r_ref, d_ref, k_ref, v_ref, a_ref, g_ref):
    h = h_ref[...]
    xx = p_ref[...] - h

    def mix(j):
        return (h + xx * mu_ref[j:j + 1, :]).astype(bf16)

    def mm(a, w):
        return jnp.dot(a, w, preferred_element_type=f32)

    r_ref[...] = mm(mix(0), wrkv_ref[0])
    k_ref[...] = mm(mix(2), wrkv_ref[1])
    v_ref[...] = mm(mix(3), wrkv_ref[2])
    wl = vec_ref[0:1, :] + mm(jnp.tanh(mm(mix(1), l1_ref[0])).astype(bf16), l2_ref[0])
    al = vec_ref[1:2, :] + mm(mm(mix(4), l1_ref[1]).astype(bf16), l2_ref[1])
    g_ref[...] = mm(jax.nn.sigmoid(mm(mix(5), l1_ref[2])).astype(bf16), l2_ref[2])
    w = -_softplus(-wl) - 0.5
    d_ref[...] = jnp.exp(-jnp.exp(w))
    a_ref[...] = jax.nn.sigmoid(al)


def rwkv_pre(h, prev, mu, w_rkv, lora1, lora2, vecs, *, row0, tm=256):
    t, d = prev.shape
    blk0 = row0 // tm
    rows = pl.BlockSpec((tm, d), lambda i: (i, 0))

    def const(a):
        return pl.BlockSpec(a.shape, lambda i: (0,) * a.ndim)

    return pl.pallas_call(
        _rwkv_pre_kernel,
        grid=(t // tm,),
        in_specs=[pl.BlockSpec((tm, d), lambda i: (blk0 + i, 0)), rows,
                  const(mu), const(w_rkv), const(lora1), const(lora2), const(vecs)],
        out_specs=[rows] * 6,
        out_shape=[jax.ShapeDtypeStruct((t, d), f32)] * 6,
        compiler_params=_cparams("parallel"),
        name="rwkv_pre",
    )(h, prev, mu, w_rkv, lora1, lora2, vecs)


RWKV_VGROUP = 8


RWKV_TB = 16
N_SEQ_TILE = LANE // RWKV_HEADS


N_COLS = D_MODEL // LANE


def _swap_sublane_with_column(cols):
    sub = lax.broadcasted_iota(jnp.int32, cols[0].shape, 0)
    s = N_COLS // 2
    while s >= 1:
        upper = (sub & s) != 0
        new = list(cols)
        for c in range(N_COLS):
            if c & s == 0:
                a, b = cols[c], cols[c | s]
                new[c] = jnp.where(upper, pltpu.roll(b, s, 0), a)
                new[c | s] = jnp.where(upper, b, pltpu.roll(a, LANE - s, 0))
        cols = new
        s //= 2
    return cols


def _swap_rowbit_with_lanehalf(col):
    a = jnp.concatenate([col[g * 16:g * 16 + SUBLANE] for g in range(N_SEQ_TILE)], axis=0)
    b = jnp.concatenate([col[g * 16 + SUBLANE:(g + 1) * 16] for g in range(N_SEQ_TILE)], axis=0)
    upper = lax.broadcasted_iota(jnp.int32, a.shape, 1) >= HALF_LANE
    moved = pltpu.roll(jnp.where(upper, a, b), HALF_LANE, 1)
    a = jnp.where(upper, moved, a)
    b = jnp.where(upper, b, moved)
    pieces = []
    for g in range(N_SEQ_TILE):
        pieces += [a[g * SUBLANE:(g + 1) * SUBLANE], b[g * SUBLANE:(g + 1) * SUBLANE]]
    return jnp.concatenate(pieces, axis=0)


def _natural_to_chain(x_ref, dst_ref, q):
    x2 = x_ref[...].reshape(N_SEQ_TILE * RWKV_TB, D_MODEL)
    cols = _swap_sublane_with_column([x2[:, c * LANE:(c + 1) * LANE] for c in range(N_COLS)])
    for c in range(N_COLS):
        xt = _swap_rowbit_with_lanehalf(cols[c]).T
        dst_ref[q, c] = xt[:RWKV_HEAD]
        dst_ref[q, RWKV_TB // 2 + c] = xt[RWKV_HEAD:]


def _chain_to_natural(src_ref, y_ref):
    cols = []
    for c in range(N_COLS):
        xt = jnp.concatenate([src_ref[c], src_ref[RWKV_TB // 2 + c]], axis=0)
        cols.append(_swap_rowbit_with_lanehalf(xt.T))
    y2 = jnp.concatenate(_swap_sublane_with_column(cols), axis=1)
    y_ref[...] = y2.reshape(N_SEQ_TILE, RWKV_TB, D_MODEL)


N_SCAN_IN = 5


def _rwkv_scan_kernel(r_ref, d_ref, k_ref, v_ref, a_ref, s0_ref, tab_ref, y_ref, sfin_ref,
                      s_ref, q_ref, ys_ref, *, natural):
    tb = q_ref.shape[1]

    @pl.when(pl.program_id(1) == 0)
    def _():
        s_ref[...] = s0_ref[...]

    for q, ref in enumerate((r_ref, d_ref, k_ref, v_ref, a_ref)):
        if natural:
            _natural_to_chain(ref, q_ref, q)
        else:
            q_ref[q] = ref[...]
    r_q, d_q, k_q, v_q, a_q = [q_ref.at[q] for q in range(N_SCAN_IN)]

    def step(t, carry):
        k_raw = k_q[t]
        a = a_q[t]
        kkr = k_raw * tab_ref[3]
        nrm = jnp.sqrt(jnp.sum(kkr * kkr, axis=0, keepdims=True))
        kk = kkr / jnp.maximum(nrm, 1e-12)
        d = d_q[t]
        kv = k_raw * (1.0 + (a - 1.0) * tab_ref[4])
        r = r_q[t]
        b = kk * a

        def vgroup(g, c2):
            v0 = pl.multiple_of(g * RWKV_VGROUP, RWKV_VGROUP)
            vrows = v_q[t, pl.ds(v0, RWKV_VGROUP), :]
            ys = []
            for vi in range(RWKV_VGROUP):
                sv = s_ref[v0 + vi]
                sa = -jnp.sum(sv * kk, axis=0, keepdims=True)
                sn = sv * d + sa * b + vrows[vi:vi + 1, :] * kv
                s_ref[v0 + vi] = sn
                ys.append(jnp.sum(sn * r, axis=0, keepdims=True))
            ys_ref[t, pl.ds(v0, RWKV_VGROUP), :] = jnp.concatenate(ys, axis=0)
            return c2

        lax.fori_loop(0, RWKV_HEAD // RWKV_VGROUP, vgroup, 0)
        y = ys_ref[t]
        mean = jnp.mean(y, axis=0, keepdims=True)
        yc = y - mean
        var = jnp.mean(yc * yc, axis=0, keepdims=True)
        bonus = jnp.sum(r * kv * tab_ref[0], axis=0, keepdims=True)
        ys_ref[t] = yc * lax.rsqrt(var + GN_EPS) * tab_ref[1] + tab_ref[2] + bonus * v_q[t]
        return carry

    lax.fori_loop(0, tb, step, 0)
    if natural:
        _chain_to_natural(ys_ref, y_ref)
    else:
        y_ref[...] = ys_ref[...]

    @pl.when(pl.program_id(1) == pl.num_programs(1) - 1)
    def _():
        sfin_ref[...] = s_ref[...]


def rwkv_scan(parts, s0, tab, *, natural):
    hd = RWKV_HEAD
    c = s0.shape[-1]
    if natural:
        nb, L, dm = parts[0].shape
        tb = RWKV_TB
        seq_spec = pl.BlockSpec((N_SEQ_TILE, tb, dm), lambda i, j: (i, j, 0))
        y_shape = jax.ShapeDtypeStruct((nb, L, dm), f32)
    else:
        L = parts[0].shape[0]
        tb = min(L, RWKV_TB)
        seq_spec = pl.BlockSpec((tb, hd, LANE), lambda i, j: (j, 0, i))
        y_shape = jax.ShapeDtypeStruct((L, hd, c), f32)
    st_spec = pl.BlockSpec((hd, hd, LANE), lambda i, j: (0, 0, i))
    return pl.pallas_call(
        functools.partial(_rwkv_scan_kernel, natural=natural),
        grid=(c // LANE, L // tb),
        in_specs=[seq_spec] * N_SCAN_IN + [st_spec, pl.BlockSpec(tab.shape, lambda i, j: (0, 0, 0))],
        out_specs=[seq_spec, st_spec],
        out_shape=[y_shape, jax.ShapeDtypeStruct((hd, hd, c), f32)],
        scratch_shapes=[pltpu.VMEM((hd, hd, LANE), f32), pltpu.VMEM((N_SCAN_IN, tb, hd, LANE), f32),
                        pltpu.VMEM((tb, hd, LANE), f32)],
        compiler_params=_cparams("parallel", "arbitrary"),
        name="rwkv_scan",
    )(*parts, s0, tab)


def _rwkv_core(parts, wkv, tab):
    b, L, _ = parts[0].shape
    hh = RWKV_HEADS // 2
    s0 = jnp.transpose(wkv.reshape(b, hh, 2, RWKV_HEAD, RWKV_HEAD), (3, 4, 0, 2, 1))
    s0 = s0.reshape(RWKV_HEAD, RWKV_HEAD, b * RWKV_HEADS)
    if b % N_SEQ_TILE == 0 and L % RWKV_TB == 0:
        y, s_fin = rwkv_scan(parts, s0, tab, natural=True)
        y = y.reshape(b * L, D_MODEL)
    else:
        def to_chain(t):
            t = jnp.transpose(t.reshape(b, L, hh, 2, RWKV_HEAD), (1, 4, 0, 3, 2))
            return t.reshape(L, RWKV_HEAD, b * RWKV_HEADS)

        y, s_fin = rwkv_scan([to_chain(t) for t in parts], s0, tab, natural=False)
        y = jnp.transpose(y.reshape(L, RWKV_HEAD, b, 2, hh), (2, 0, 4, 3, 1)).reshape(b * L, D_MODEL)
    s_fin = jnp.transpose(s_fin.reshape(RWKV_HEAD, RWKV_HEAD, b, 2, hh), (2, 4, 3, 0, 1))
    return y, s_fin.reshape(b, RWKV_HEADS, RWKV_HEAD, RWKV_HEAD)


def kernel(x_prompt, x_sample, state_ssm, state_conv, cache_swa_k, cache_swa_v, state_wkv, state_shift, rel_bias, norm_mix, norm_ffn, norm_final, mix_w_in, ssd_conv_w, ssd_conv_b, ssd_dt_bias, ssd_a_log, ssd_d_skip, ssd_gnorm, attn_sinks, mix_w_out, rwkv_mu, rwkv_w0, rwkv_w1, rwkv_w2, rwkv_a0, rwkv_a1, rwkv_a2, rwkv_g1, rwkv_g2, rwkv_k_k, rwkv_k_a, rwkv_r_k, rwkv_w_rkv, rwkv_w_o, rwkv_ln_w, rwkv_ln_b, peer_w_q, peer_sub_keys, peer_u, peer_v):
    bp, lp, d = x_prompt.shape
    bs, ls, _ = x_sample.shape
    tp = bp * lp
    ts = bs * ls
    x = jnp.concatenate([x_prompt.reshape(tp, d), x_sample.reshape(ts, d)], axis=0)

    def split(t):
        return t[:tp].reshape(bp, lp, -1), t[tp:].reshape(bs, ls, -1)

    def zero_state(a):
        return jnp.zeros((bp,) + a.shape[2:], a.dtype)

    def peer(x, ht, layer, g_next):
        wqt = peer_w_q[layer].T.astype(bf16)
        return peer_ffn_residual(x, ht, wqt, peer_sub_keys[layer].astype(bf16),
                                 peer_u[layer].astype(bf16), peer_v[layer].astype(bf16), g_next)

    z, xbc, q, kv, dt = in_proj(x, norm_mix[0], mix_w_in[0])
    pad16 = (0, LANE - SSM_HEADS)
    selh = (jnp.arange(LANE)[:, None] == jnp.arange(SSM_INNER)[None, :] // SSM_HEAD_DIM).astype(f32)
    ssd_params = (ssd_conv_w[0], ssd_conv_b[0].reshape(1, CONV_DIM),
                  jnp.pad(ssd_dt_bias[0], pad16).reshape(1, LANE),
                  jnp.pad(-jnp.exp(ssd_a_log[0]), pad16).reshape(1, LANE),
                  jnp.repeat(ssd_d_skip[0], SSM_HEAD_DIM).reshape(1, SSM_INNER),
                  ssd_gnorm[0].reshape(1, SSM_INNER), selh)
    y_ssd_p, ssm_p, conv_p = ssd_mixer(z, xbc, dt, zero_state(state_conv), zero_state(state_ssm), ssd_params,
                                       batch=bp, seq=lp, row0=0)
    y_ssd_s, ssm_s, conv_s = ssd_mixer(z, xbc, dt, state_conv[0], state_ssm[0], ssd_params,
                                       batch=bs, seq=ls, row0=tp)
    o_att_p, k_p, v_p = swa_mixer(q, kv, None, None, rel_bias, attn_sinks[0], batch=bp, seq=lp, row0=0)
    o_att_s, k_s, v_s = swa_mixer(q, kv, cache_swa_k[0], cache_swa_v[0], rel_bias, attn_sinks[0],
                                  batch=bs, seq=ls, row0=tp)
    x, ht = out_proj(y_ssd_p, o_att_p, y_ssd_s, o_att_s, mix_w_out[0].astype(bf16), x, norm_ffn[0])
    x, h = peer(x, ht, 0, norm_mix[1])

    h_p, h_s = split(h)
    prev_p = jnp.concatenate([jnp.zeros((bp, 1, d), f32), h_p[:, :-1]], axis=1)
    prev_s = jnp.concatenate([state_shift[0][:, None], h_s[:, :-1]], axis=1)
    def lora_pair(w1, w2):
        pad = LORA_PAD - w1.shape[1]
        return jnp.pad(w1, ((0, 0), (0, pad))), jnp.pad(w2, ((0, pad), (0, 0)))

    pairs = [lora_pair(rwkv_w1[0], rwkv_w2[0]), lora_pair(rwkv_a1[0], rwkv_a2[0]), lora_pair(rwkv_g1[0], rwkv_g2[0])]
    lora1 = jnp.stack([p[0] for p in pairs]).astype(bf16)
    lora2 = jnp.stack([p[1] for p in pairs]).astype(bf16)
    vecs = jnp.stack([rwkv_w0[0], rwkv_a0[0]])
    pre_args = (rwkv_mu[0], rwkv_w_rkv[0].astype(bf16), lora1, lora2, vecs)
    *parts_p, g_p = rwkv_pre(h, prev_p.reshape(tp, d), *pre_args, row0=0)
    *parts_s, g_s = rwkv_pre(h, prev_s.reshape(ts, d), *pre_args, row0=tp)

    def chain_table(p):
        t = jnp.transpose(p.reshape(RWKV_HEADS // 2, 2, RWKV_HEAD), (2, 1, 0)).reshape(RWKV_HEAD, RWKV_HEADS)
        return jnp.tile(t, (1, LANE // RWKV_HEADS))

    tab = jnp.stack([chain_table(p) for p in (rwkv_r_k[0].reshape(-1), rwkv_ln_w[0], rwkv_ln_b[0],
                                              rwkv_k_k[0], rwkv_k_a[0])])
    y_p, wkv_p = _rwkv_core([t.reshape(bp, lp, d) for t in parts_p], zero_state(state_wkv), tab)
    y_s, wkv_s = _rwkv_core([t.reshape(bs, ls, d) for t in parts_s], state_wkv[0], tab)
    x, ht = gated_out_proj(y_p, g_p, y_s, g_s, rwkv_w_o[0].astype(bf16), x, norm_ffn[1])
    shift_p, shift_s = h_p[:, -1], h_s[:, -1]
    _, y = peer(x, ht, 1, norm_final)
    y_p, y_s = split(y)
    return (y_p, y_s, ssm_p[None], conv_p[None], k_p[None], v_p[None], wkv_p[None], shift_p[None],
            ssm_s[None], conv_s[None], k_s[None], v_s[None], wkv_s[None], shift_s[None])
```
